```python
import math
import jax, jax.numpy as jnp
from jax import lax
import numpy as np

D_MODEL = 1024
BATCH = 8
SEQ = 2048
DEPTH = 4
DEC_BATCH = 128
DEC_SEQ = 8
PAST_LEN = 8192
PAGE_SIZE = 128

N_MIXERS = 3
N_ATTN_LAYERS = (DEPTH + 2) // 3
N_SSM_LAYERS = (DEPTH + 1) // 3
N_POOL_LAYERS = DEPTH // 3

HEAD_DIM = 64
N_HEADS = D_MODEL // HEAD_DIM
N_KV_HEADS = 4
GQA_GROUP = N_HEADS // N_KV_HEADS
WINDOW = 128
BLOCK_Q = 128
Q_DIM = N_HEADS * HEAD_DIM
KV_DIM = N_KV_HEADS * HEAD_DIM
QKV_DIM = Q_DIM + 2 * KV_DIM
REL_BUCKETS = 32
REL_MAX_DIST = 128

D_INNER = 2 * D_MODEL
SSM_HEAD_DIM = 64
SSM_HEADS = D_INNER // SSM_HEAD_DIM
SSM_GROUPS = 4
SSM_HEADS_PER_GROUP = SSM_HEADS // SSM_GROUPS
D_STATE = 128
CONV_WIDTH = 4
CONV_DIM = D_INNER + 2 * SSM_GROUPS * D_STATE
SSM_IN_DIM = D_INNER + CONV_DIM + SSM_HEADS
SSD_CHUNK = 128
RMS_EPS = 1e-5

POOL_WINDOWS = (2, 4, 8, 16)
POOL_GROUPS = len(POOL_WINDOWS)
POOL_GROUP_DIM = D_MODEL // POOL_GROUPS
POOL_STATE_LEN = max(POOL_WINDOWS) - 1

D_FF = -(-8 * D_MODEL // (3 * 256)) * 256

DEEPNORM_ALPHA = (2 * DEPTH) ** 0.25
DEEPNORM_BETA = (8 * DEPTH) ** -0.25
LN_EPS = 1e-5

kernel_name = 'hybrid_swa_ssd_pool_decoder_step'


def _layer_norm(x, g, b):
    xf = x.astype(jnp.float32)
    mu = xf.mean(-1, keepdims=True)
    var = jnp.square(xf - mu).mean(-1, keepdims=True)
    return (xf - mu) * lax.rsqrt(var + LN_EPS) * g + b


def _swiglu(x, wg, wu, wd):
    return (jax.nn.silu(x @ wg) * (x @ wu)) @ wd


def _t5_bucket(dist):
    n = jnp.maximum(dist, 0)
    max_exact = REL_BUCKETS // 2
    nf = jnp.maximum(n, 1).astype(jnp.float32)
    large = max_exact + (jnp.log(nf / max_exact) / math.log(REL_MAX_DIST / max_exact)
                         * (REL_BUCKETS - max_exact)).astype(jnp.int32)
    large = jnp.minimum(large, REL_BUCKETS - 1)
    return jnp.where(n < max_exact, n, large)


def _qkv(x, w_qkv, b_qkv):
    h = x @ w_qkv + b_qkv
    lead = x.shape[:-1]
    q = h[..., :Q_DIM].reshape(*lead, N_KV_HEADS, GQA_GROUP, HEAD_DIM)
    k = h[..., Q_DIM:Q_DIM + KV_DIM].reshape(*lead, N_KV_HEADS, HEAD_DIM)
    v = h[..., Q_DIM + KV_DIM:].reshape(*lead, N_KV_HEADS, HEAD_DIM)
    return q, k, v


def _sink_attention(q, k, v, dist, valid, rel_bias, sinks):
    s = jnp.einsum('...qkgd,...skd->...kgqs', q, k).astype(jnp.float32) * (HEAD_DIM ** -0.5)
    bias = rel_bias.astype(jnp.float32)[_t5_bucket(dist)]
    bias = jnp.moveaxis(bias, -1, 0).reshape(N_KV_HEADS, GQA_GROUP, *dist.shape)
    s = jnp.where(valid, s + bias, -jnp.inf)
    sink = sinks.astype(jnp.float32).reshape(N_KV_HEADS, GQA_GROUP, 1, 1)
    m = jnp.maximum(s.max(-1, keepdims=True), sink)
    p = jnp.exp(s - m)
    w = p / (p.sum(-1, keepdims=True) + jnp.exp(sink - m))
    return jnp.einsum('...kgqs,...skd->...qkgd', w, v.astype(jnp.float32))


def _swa_prompt(x, w_qkv, b_qkv, w_o, b_o, sinks, rel_bias):
    b, l, _ = x.shape
    nb = l // BLOCK_Q
    q, k, v = _qkv(x, w_qkv, b_qkv)
    qb = q.reshape(b, nb, BLOCK_Q, N_KV_HEADS, GQA_GROUP, HEAD_DIM)

    def band(t):
        tb = t.reshape(b, nb, BLOCK_Q, N_KV_HEADS, HEAD_DIM)
        prev = jnp.concatenate([jnp.zeros_like(tb[:, :1]), tb[:, :-1]], axis=1)
        return jnp.concatenate([prev, tb], axis=2)

    qi = jnp.arange(BLOCK_Q, dtype=jnp.int32)
    si = jnp.arange(2 * BLOCK_Q, dtype=jnp.int32)
    dist = qi[:, None] + BLOCK_Q - si[None, :]
    kpos = jnp.arange(nb, dtype=jnp.int32)[:, None] * BLOCK_Q + si[None, :] - BLOCK_Q
    valid = (dist >= 0) & (dist < WINDOW) & (kpos[:, None, :] >= 0)
    o = _sink_attention(qb, band(k), band(v), dist, valid[:, None, None], rel_bias, sinks)
    y = o.reshape(b, l, Q_DIM) @ w_o + b_o
    return y, k[:, -WINDOW:], v[:, -WINDOW:]


def _swa_sample(x, k_buf, v_buf, start, w_qkv, b_qkv, w_o, b_o, sinks, rel_bias):
    b, l, _ = x.shape
    q, k, v = _qkv(x, w_qkv, b_qkv)
    k_all = jnp.concatenate([k_buf.astype(k.dtype), k], axis=1)
    v_all = jnp.concatenate([v_buf.astype(v.dtype), v], axis=1)
    qpos = start + jnp.arange(l, dtype=jnp.int32)
    kpos = start - WINDOW + jnp.arange(WINDOW + l, dtype=jnp.int32)
    dist = qpos[:, None] - kpos[None, :]
    valid = (dist >= 0) & (dist < WINDOW)
    o = _sink_attention(q, k_all, v_all, dist, valid, rel_bias, sinks)
    y = o.reshape(b, l, Q_DIM) @ w_o + b_o
    return y, k_all[:, -WINDOW:], v_all[:, -WINDOW:]


def _ssd_scan(x, dt, a, bm, cm, h0, chunk):
    b, l = x.shape[:2]
    nc = l // chunk

    def to_chunks(t):
        return jnp.moveaxis(t.astype(jnp.float32).reshape(b, nc, chunk, *t.shape[2:]), 1, 0)

    causal = jnp.tril(jnp.ones((chunk, chunk), dtype=bool))[None, :, :, None, None]

    def step(h, inp):
        xc, dtc, bc, cc = inp
        acum = jnp.cumsum(dtc * a, axis=1)
        seg = acum[:, :, None] - acum[:, None, :]
        lmat = jnp.exp(jnp.where(causal, seg, -jnp.inf))
        cb = jnp.einsum('btgn,bsgn->btsg', cc, bc)
        w = cb[..., None] * lmat * dtc[:, None]
        y = jnp.einsum('btsgr,bsgrp->btgrp', w, xc)
        y = y + jnp.einsum('btgn,bgrpn->btgrp', cc, h) * jnp.exp(acum)[..., None]
        decay = jnp.exp(acum[:, -1:] - acum) * dtc
        h = h * jnp.exp(acum[:, -1])[..., None, None] + jnp.einsum('bsgn,bsgrp->bgrpn', bc, decay[..., None] * xc)
        return h, y

    h, ys = lax.scan(step, h0, (to_chunks(x), to_chunks(dt), to_chunks(bm), to_chunks(cm)))
    return jnp.moveaxis(ys, 0, 1).reshape(x.shape), h


def _mamba2(x, conv_state, ssm_state, w_in, conv_w, conv_b, dt_bias, a_log, d_skip, norm_w, w_out):
    b, l, _ = x.shape
    zxbcdt = x @ w_in
    z = zxbcdt[..., :D_INNER]
    xbc = zxbcdt[..., D_INNER:D_INNER + CONV_DIM]
    dt = zxbcdt[..., D_INNER + CONV_DIM:]
    xpad = jnp.concatenate([conv_state.astype(xbc.dtype), xbc], axis=1)
    conv = conv_b + sum(xpad[:, j:j + l] * conv_w[j] for j in range(CONV_WIDTH))
    xbc = jax.nn.silu(conv)
    gbn = SSM_GROUPS * D_STATE
    xs = xbc[..., :D_INNER].reshape(b, l, SSM_GROUPS, SSM_HEADS_PER_GROUP, SSM_HEAD_DIM)
    bm = xbc[..., D_INNER:D_INNER + gbn].reshape(b, l, SSM_GROUPS, D_STATE)
    cm = xbc[..., D_INNER + gbn:].reshape(b, l, SSM_GROUPS, D_STATE)
    dt = jax.nn.softplus(dt.astype(jnp.float32) + dt_bias).reshape(b, l, SSM_GROUPS, SSM_HEADS_PER_GROUP)
    a = -jnp.exp(a_log.astype(jnp.float32)).reshape(SSM_GROUPS, SSM_HEADS_PER_GROUP)
    h0 = ssm_state.astype(jnp.float32).reshape(b, SSM_GROUPS, SSM_HEADS_PER_GROUP, SSM_HEAD_DIM, D_STATE)
    y, h = _ssd_scan(xs, dt, a, bm, cm, h0, min(SSD_CHUNK, l))
    y = y + d_skip.reshape(SSM_GROUPS, SSM_HEADS_PER_GROUP, 1) * xs
    y = y.reshape(b, l, D_INNER) * jax.nn.silu(z.astype(jnp.float32))
    yg = y.reshape(b, l, SSM_GROUPS, D_INNER // SSM_GROUPS)
    yg = yg * lax.rsqrt(jnp.mean(yg * yg, -1, keepdims=True) + RMS_EPS)
    y = yg.reshape(b, l, D_INNER) * norm_w
    new_h = h.reshape(b, SSM_HEADS, SSM_HEAD_DIM, D_STATE)
    return y @ w_out, xpad[:, -(CONV_WIDTH - 1):], new_h


def _pool_mixer(x, prefix, start, pool_w, pool_scale):
    b, l, _ = x.shape
    xf = x.astype(jnp.float32)
    xp = jnp.concatenate([prefix.astype(jnp.float32), xf], axis=1)
    cs = jnp.concatenate([jnp.zeros((b, 1, D_MODEL), jnp.float32), jnp.cumsum(xp, axis=1)], axis=1)
    hi = cs[:, POOL_STATE_LEN + 1:]
    pos = start + jnp.arange(l, dtype=jnp.int32)
    outs = []
    for g, w in enumerate(POOL_WINDOWS):
        sl = slice(g * POOL_GROUP_DIM, (g + 1) * POOL_GROUP_DIM)
        lo = cs[:, POOL_STATE_LEN + 1 - w:POOL_STATE_LEN + 1 - w + l, sl]
        cnt = jnp.minimum(pos + 1, w).astype(jnp.float32)[None, :, None]
        diff = (hi[..., sl] - lo) / cnt - xf[..., sl]
        outs.append(diff @ pool_w[g])
    y = jnp.concatenate(outs, axis=-1) * pool_scale
    return y, xp[:, -POOL_STATE_LEN:]


def setup_inputs(seed: int = 0) -> dict:
    key = jax.random.key(seed)
    ks = iter(jax.random.split(key, 32))

    def nrm(shape, scale):
        return jax.random.normal(next(ks), shape, jnp.float32) * scale

    def unif(shape, lo, hi):
        return jax.random.uniform(next(ks), shape, jnp.float32, lo, hi)

    beta = DEEPNORM_BETA
    x_prompt = nrm((BATCH, SEQ, D_MODEL), 1.0)
    x_sample = nrm((DEC_BATCH, DEC_SEQ, D_MODEL), 1.0)
    cache_k = nrm((N_ATTN_LAYERS, DEC_BATCH, WINDOW, N_KV_HEADS, HEAD_DIM), 1.0)
    cache_v = nrm((N_ATTN_LAYERS, DEC_BATCH, WINDOW, N_KV_HEADS, HEAD_DIM), beta)
    state_conv = nrm((N_SSM_LAYERS, DEC_BATCH, CONV_WIDTH - 1, CONV_DIM), 1.0)
    state_ssm = nrm((N_SSM_LAYERS, DEC_BATCH, SSM_HEADS, SSM_HEAD_DIM, D_STATE), 0.1)
    state_pool = nrm((N_POOL_LAYERS, DEC_BATCH, POOL_STATE_LEN, D_MODEL), 1.0)
    rel_bias = nrm((REL_BUCKETS, N_HEADS), 0.5)
    w_qk = nrm((N_ATTN_LAYERS, D_MODEL, Q_DIM + KV_DIM), D_MODEL ** -0.5)
    w_v = nrm((N_ATTN_LAYERS, D_MODEL, KV_DIM), D_MODEL ** -0.5 * beta)
    attn_w_qkv = jnp.concatenate([w_qk, w_v], axis=-1)
    attn_b_qkv = nrm((N_ATTN_LAYERS, QKV_DIM), 0.02)
    attn_w_o = nrm((N_ATTN_LAYERS, Q_DIM, D_MODEL), Q_DIM ** -0.5 * beta)
    attn_b_o = nrm((N_ATTN_LAYERS, D_MODEL), 0.02)
    attn_sinks = nrm((N_ATTN_LAYERS, N_HEADS), 1.0)
    ssm_w_in = nrm((N_SSM_LAYERS, D_MODEL, SSM_IN_DIM), D_MODEL ** -0.5)
    ssm_conv_w = nrm((N_SSM_LAYERS, CONV_WIDTH, CONV_DIM), CONV_WIDTH ** -0.5)
    ssm_conv_b = nrm((N_SSM_LAYERS, CONV_DIM), 0.02)
    dt0 = jnp.exp(unif((N_SSM_LAYERS, SSM_HEADS), math.log(1e-3), math.log(1e-1)))
    ssm_dt_bias = dt0 + jnp.log(-jnp.expm1(-dt0))
    ssm_a_log = jnp.log(unif((N_SSM_LAYERS, SSM_HEADS), 1.0, 16.0))
    ssm_d = 1.0 + nrm((N_SSM_LAYERS, SSM_HEADS), 0.02)
    ssm_norm_w = 1.0 + nrm((N_SSM_LAYERS, D_INNER), 0.02)
    ssm_w_out = nrm((N_SSM_LAYERS, D_INNER, D_MODEL), D_INNER ** -0.5 * beta)
    pool_w = nrm((N_POOL_LAYERS, POOL_GROUPS, POOL_GROUP_DIM, POOL_GROUP_DIM), POOL_GROUP_DIM ** -0.5 * beta)
    pool_scale = 1.0 + nrm((N_POOL_LAYERS, D_MODEL), 0.02)
    ffn_w_gate = nrm((DEPTH, D_MODEL, D_FF), D_MODEL ** -0.5 * beta)
    ffn_w_up = nrm((DEPTH, D_MODEL, D_FF), D_MODEL ** -0.5 * beta)
    ffn_w_down = nrm((DEPTH, D_FF, D_MODEL), D_FF ** -0.5 * beta)
    ln_g = 1.0 + nrm((DEPTH, 2, D_MODEL), 0.02)
    ln_b = nrm((DEPTH, 2, D_MODEL), 0.02)
    return {'x_prompt': x_prompt, 'x_sample': x_sample, 'cache_k': cache_k, 'cache_v': cache_v,
            'state_conv': state_conv, 'state_ssm': state_ssm, 'state_pool': state_pool,
            'rel_bias': rel_bias, 'attn_w_qkv': attn_w_qkv, 'attn_b_qkv': attn_b_qkv,
            'attn_w_o': attn_w_o, 'attn_b_o': attn_b_o, 'attn_sinks': attn_sinks,
            'ssm_w_in': ssm_w_in, 'ssm_conv_w': ssm_conv_w, 'ssm_conv_b': ssm_conv_b,
            'ssm_dt_bias': ssm_dt_bias, 'ssm_a_log': ssm_a_log, 'ssm_d': ssm_d,
            'ssm_norm_w': ssm_norm_w, 'ssm_w_out': ssm_w_out, 'pool_w': pool_w,
            'pool_scale': pool_scale, 'ffn_w_gate': ffn_w_gate, 'ffn_w_up': ffn_w_up,
            'ffn_w_down': ffn_w_down, 'ln_g': ln_g, 'ln_b': ln_b}


def reference(x_prompt, x_sample, cache_k, cache_v, state_conv, state_ssm, state_pool, rel_bias,
              attn_w_qkv, attn_b_qkv, attn_w_o, attn_b_o, attn_sinks, ssm_w_in, ssm_conv_w,
              ssm_conv_b, ssm_dt_bias, ssm_a_log, ssm_d, ssm_norm_w, ssm_w_out, pool_w,
              pool_scale, ffn_w_gate, ffn_w_up, ffn_w_down, ln_g, ln_b):
    xp, xs = x_prompt, x_sample
    bp = xp.shape[0]
    nk_p, nv_p, nc_p, nh_p, npool_p = [], [], [], [], []
    nk_s, nv_s, nc_s, nh_s, npool_s = [], [], [], [], []
    for i in range(DEPTH):
        j = i // N_MIXERS
        kind = i % N_MIXERS
        if kind == 0:
            mp, kp, vp = _swa_prompt(xp, attn_w_qkv[j], attn_b_qkv[j], attn_w_o[j], attn_b_o[j],
                                     attn_sinks[j], rel_bias)
            ms, ks_, vs_ = _swa_sample(xs, cache_k[j], cache_v[j], PAST_LEN, attn_w_qkv[j], attn_b_qkv[j],
                                       attn_w_o[j], attn_b_o[j], attn_sinks[j], rel_bias)
            nk_p.append(kp); nv_p.append(vp); nk_s.append(ks_); nv_s.append(vs_)
        elif kind == 1:
            ssm_args = (ssm_w_in[j], ssm_conv_w[j], ssm_conv_b[j], ssm_dt_bias[j], ssm_a_log[j],
                        ssm_d[j], ssm_norm_w[j], ssm_w_out[j])
            conv0 = jnp.zeros((bp, CONV_WIDTH - 1, CONV_DIM), xp.dtype)
            h0 = jnp.zeros((bp, SSM_HEADS, SSM_HEAD_DIM, D_STATE), jnp.float32)
            mp, cp, hp = _mamba2(xp, conv0, h0, *ssm_args)
            ms, cs_, hs_ = _mamba2(xs, state_conv[j], state_ssm[j], *ssm_args)
            nc_p.append(cp); nh_p.append(hp); nc_s.append(cs_); nh_s.append(hs_)
        else:
            pool0 = jnp.zeros((bp, POOL_STATE_LEN, D_MODEL), jnp.float32)
            mp, pp = _pool_mixer(xp, pool0, 0, pool_w[j], pool_scale[j])
            ms, ps_ = _pool_mixer(xs, state_pool[j], PAST_LEN, pool_w[j], pool_scale[j])
            npool_p.append(pp); npool_s.append(ps_)
        xp = _layer_norm(DEEPNORM_ALPHA * xp + mp, ln_g[i, 0], ln_b[i, 0])
        xs = _layer_norm(DEEPNORM_ALPHA * xs + ms, ln_g[i, 0], ln_b[i, 0])
        xp = _layer_norm(DEEPNORM_ALPHA * xp + _swiglu(xp, ffn_w_gate[i], ffn_w_up[i], ffn_w_down[i]),
                         ln_g[i, 1], ln_b[i, 1])
        xs = _layer_norm(DEEPNORM_ALPHA * xs + _swiglu(xs, ffn_w_gate[i], ffn_w_up[i], ffn_w_down[i]),
                         ln_g[i, 1], ln_b[i, 1])
    return (xp, xs,
            jnp.stack(nk_p), jnp.stack(nv_p), jnp.stack(nc_p), jnp.stack(nh_p), jnp.stack(npool_p),
            jnp.stack(nk_s), jnp.stack(nv_s), jnp.stack(nc_s), jnp.stack(nh_s), jnp.stack(npool_s))
```

```python
import functools
import math

import jax
import jax.numpy as jnp
import numpy as np
from jax import lax
from jax.experimental import pallas as pl
from jax.experimental.pallas import tpu as pltpu

DEPTH = 4
N_MIXERS = 3
PAST_LEN = 8192
DEEPNORM_ALPHA = (2 * DEPTH) ** 0.25
LN_EPS = 1e-5

V7X_VMEM_BYTES = 64 * 1024 * 1024
LANES = 128
SUBLANES = 8
MXU_DIM = 256

BF16 = jnp.bfloat16
F32 = jnp.float32


def _vmem_limit(estimate_bytes):
    return int(min(V7X_VMEM_BYTES - 8 * 1024 * 1024, max(32 * 1024 * 1024, estimate_bytes * 3 // 2)))


def _layer_norm_rows(v, g, b):
    mu = jnp.mean(v, axis=-1, keepdims=True)
    d = v - mu
    var = jnp.mean(d * d, axis=-1, keepdims=True)
    return d * lax.rsqrt(var + LN_EPS) * g + b


def _resident(shape):
    nd = len(shape)
    return pl.BlockSpec(shape, lambda *_: (0,) * nd, pipeline_mode=pl.Buffered(1))


def _ffn_chunks(d_ff):
    step = 2 * MXU_DIM
    edges = list(range(0, d_ff, step)) + [d_ff]
    return tuple(zip(edges[:-1], edges[1:]))


def _ffn_kernel(x_ref, wg_ref, wu_ref, wd_ref, g_ref, b_ref, o_ref, *, chunks):
    x = x_ref[...]
    xb = x.astype(BF16)
    acc = jnp.zeros(x.shape, F32)
    for c0, c1 in chunks:
        gate = jnp.dot(xb, wg_ref[:, c0:c1], preferred_element_type=F32)
        up = jnp.dot(xb, wu_ref[:, c0:c1], preferred_element_type=F32)
        h = (gate * jax.nn.sigmoid(gate)) * up
        acc = acc + jnp.dot(h.astype(BF16), wd_ref[c0:c1, :], preferred_element_type=F32)
    o_ref[...] = _layer_norm_rows(DEEPNORM_ALPHA * x + acc, g_ref[...], b_ref[...])


def _ffn_ln(x, wg, wu, wd, g, b, *, tm=512):
    t, d = x.shape
    d_ff = wg.shape[1]
    tm = min(tm, t)
    assert t % tm == 0
    est = 3 * d * d_ff * 2 + 4 * tm * d * 4 + 4 * tm * d_ff * 4
    return pl.pallas_call(
        functools.partial(_ffn_kernel, chunks=_ffn_chunks(d_ff)),
        grid=(t // tm,),
        in_specs=[
            pl.BlockSpec((tm, d), lambda i: (i, 0)),
            _resident((d, d_ff)),
            _resident((d, d_ff)),
            _resident((d_ff, d)),
            _resident((1, d)),
            _resident((1, d)),
        ],
        out_specs=pl.BlockSpec((tm, d), lambda i: (i, 0)),
        out_shape=jax.ShapeDtypeStruct((t, d), F32),
        compiler_params=pltpu.CompilerParams(
            dimension_semantics=("arbitrary",), vmem_limit_bytes=_vmem_limit(est)),
        name="ffn_ln",
    )(x, wg, wu, wd, g, b)


HEAD_DIM = 64
N_HEADS = 16
N_KV_HEADS = 4
GQA_GROUP = N_HEADS // N_KV_HEADS
WINDOW = 128
REL_BUCKETS = 32
REL_MAX_DIST = 128
Q_DIM = N_HEADS * HEAD_DIM
KV_DIM = N_KV_HEADS * HEAD_DIM
NEG_INF = float("-inf")


def _t5_bucket_table(dist):
    n = np.maximum(dist, 0)
    max_exact = REL_BUCKETS // 2
    nf = np.maximum(n, 1).astype(np.float32)
    large = max_exact + (np.log(nf / np.float32(max_exact)) / np.float32(math.log(REL_MAX_DIST / max_exact))
                         * np.float32(REL_BUCKETS - max_exact)).astype(np.int32)
    large = np.minimum(large, REL_BUCKETS - 1)
    bucket = np.where(n < max_exact, n, large)
    valid = (dist >= 0) & (dist < WINDOW)
    return np.where(valid, bucket, -1).astype(np.int32)


def _bias_from_buckets(bucket, relb_ref, head):
    acc = jnp.full(bucket.shape, NEG_INF, F32)
    for bkt in range(REL_BUCKETS):
        acc = jnp.where(bucket == bkt, relb_ref[bkt, head], acc)
    return acc


def _attn_prompt_kernel(x_ref, wqkv_ref, bqkv_ref, wo_ref, bo_ref, g_ref, b_ref, bucket_ref,
                        relb_ref, sink_ref, o_ref, kout_ref, vout_ref,
                        kbuf, vbuf, obuf, bias_scr, *, tq):
    bi = pl.program_id(0)
    j = pl.program_id(1)
    nj = pl.num_programs(1)
    blk = WINDOW

    @pl.when((bi == 0) & (j == 0))
    def _build_bias():
        bucket = bucket_ref[...]
        for h in range(N_HEADS):
            bias_scr[h] = _bias_from_buckets(bucket, relb_ref, h)

    @pl.when(j == 0)
    def _no_past():
        kbuf[0:blk, :] = jnp.zeros((blk, KV_DIM), BF16)
        vbuf[0:blk, :] = jnp.zeros((blk, KV_DIM), BF16)

    x = x_ref[0]
    qkv = jnp.dot(x.astype(BF16), wqkv_ref[...], preferred_element_type=F32) + bqkv_ref[...]
    q = (qkv[:, :Q_DIM] * (HEAD_DIM ** -0.5)).astype(BF16)
    k = qkv[:, Q_DIM:Q_DIM + KV_DIM]
    v = qkv[:, Q_DIM + KV_DIM:]
    kbuf[blk:blk + tq, :] = k.astype(BF16)
    vbuf[blk:blk + tq, :] = v.astype(BF16)

    @pl.when(j == nj - 1)
    def _emit_cache():
        kout_ref[0] = k[tq - WINDOW:, :]
        vout_ref[0] = v[tq - WINDOW:, :]

    col = lax.broadcasted_iota(jnp.int32, (blk, 2 * blk), 1)
    for i in range(tq // blk):
        r0 = i * blk
        for kv in range(N_KV_HEADS):
            c0 = kv * HEAD_DIM
            kk = kbuf[r0:r0 + 2 * blk, c0:c0 + HEAD_DIM]
            vv = vbuf[r0:r0 + 2 * blk, c0:c0 + HEAD_DIM]
            for g in range(GQA_GROUP):
                h = kv * GQA_GROUP + g
                qh = q[r0:r0 + blk, h * HEAD_DIM:(h + 1) * HEAD_DIM]
                s = lax.dot_general(qh, kk, (((1,), (1,)), ((), ())), preferred_element_type=F32)
                s = s + bias_scr[h]
                if i == 0:
                    s = jnp.where((col < blk) & (j == 0), NEG_INF, s)
                sink = sink_ref[h]
                m = jnp.maximum(jnp.max(s, axis=-1, keepdims=True), sink)
                p = jnp.exp(s - m)
                den = jnp.sum(p, axis=-1, keepdims=True) + jnp.exp(sink - m)
                o = jnp.dot(p.astype(BF16), vv, preferred_element_type=F32) / den
                obuf[r0:r0 + blk, h * HEAD_DIM:(h + 1) * HEAD_DIM] = o.astype(BF16)

    y = jnp.dot(obuf[...], wo_ref[...], preferred_element_type=F32) + bo_ref[...]
    o_ref[0] = _layer_norm_rows(DEEPNORM_ALPHA * x + y, g_ref[...], b_ref[...])
    kbuf[0:blk, :] = kbuf[tq:tq + blk, :]
    vbuf[0:blk, :] = vbuf[tq:tq + blk, :]


def _attn_prompt(x, wqkv, bqkv, wo, bo, g, b, rel_bias, sinks, *, tq=512):
    bsz, l, d = x.shape
    tq = min(tq, l)
    assert l % tq == 0 and tq % WINDOW == 0
    qi = np.arange(WINDOW)[:, None]
    si = np.arange(2 * WINDOW)[None, :]
    bucket = jnp.asarray(_t5_bucket_table(qi + WINDOW - si))
    qkv_dim = wqkv.shape[1]
    smem = pl.BlockSpec(memory_space=pltpu.SMEM)
    est = (d * qkv_dim + Q_DIM * d) * 2 + 4 * tq * d * 4 + N_HEADS * WINDOW * 2 * WINDOW * 4 + 3 * tq * qkv_dim * 4
    return pl.pallas_call(
        functools.partial(_attn_prompt_kernel, tq=tq),
        grid=(bsz, l // tq),
        in_specs=[
            pl.BlockSpec((1, tq, d), lambda bi, j: (bi, j, 0)),
            _resident((d, qkv_dim)), _resident((1, qkv_dim)),
            _resident((Q_DIM, d)), _resident((1, d)),
            _resident((1, d)), _resident((1, d)),
            _resident((WINDOW, 2 * WINDOW)),
            smem, smem,
        ],
        out_specs=[
            pl.BlockSpec((1, tq, d), lambda bi, j: (bi, j, 0)),
            pl.BlockSpec((1, WINDOW, KV_DIM), lambda bi, j: (bi, 0, 0)),
            pl.BlockSpec((1, WINDOW, KV_DIM), lambda bi, j: (bi, 0, 0)),
        ],
        out_shape=[
            jax.ShapeDtypeStruct((bsz, l, d), F32),
            jax.ShapeDtypeStruct((bsz, WINDOW, KV_DIM), F32),
            jax.ShapeDtypeStruct((bsz, WINDOW, KV_DIM), F32),
        ],
        scratch_shapes=[
            pltpu.VMEM((WINDOW + tq, KV_DIM), BF16),
            pltpu.VMEM((WINDOW + tq, KV_DIM), BF16),
            pltpu.VMEM((tq, Q_DIM), BF16),
            pltpu.VMEM((N_HEADS, WINDOW, 2 * WINDOW), F32),
        ],
        compiler_params=pltpu.CompilerParams(
            dimension_semantics=("arbitrary", "arbitrary"), vmem_limit_bytes=_vmem_limit(est)),
        name="attn_prompt",
    )(x, wqkv, bqkv, wo, bo, g, b, bucket, rel_bias, sinks)


SAMPLE_KEYS = 2 * WINDOW


def _attn_sample_kernel(x_ref, ck_ref, cv_ref, wqkv_ref, bqkv_ref, wo_ref, bo_ref, g_ref, b_ref, bucket_ref,
                        relb_ref, sink_ref, o_ref, nk_ref, nv_ref,
                        kcat, vcat, qbuf, obuf, bias_scr, *, bb, l):
    rows_g = GQA_GROUP * l

    @pl.when(pl.program_id(0) == 0)
    def _init():
        bucket = bucket_ref[...]
        for h in range(N_HEADS):
            kv, g_ = divmod(h, GQA_GROUP)
            bias_scr[kv, g_ * l:(g_ + 1) * l, :] = _bias_from_buckets(bucket, relb_ref, h)
        kcat[...] = jnp.zeros(kcat.shape, F32)
        vcat[...] = jnp.zeros(vcat.shape, F32)

    x = x_ref[...]
    qkv = jnp.dot(x.astype(BF16), wqkv_ref[...], preferred_element_type=F32) + bqkv_ref[...]
    qbuf[...] = qkv[:, :Q_DIM] * (HEAD_DIM ** -0.5)
    k_new = qkv[:, Q_DIM:Q_DIM + KV_DIM]
    v_new = qkv[:, Q_DIM + KV_DIM:]

    sink_cols = []
    for kv in range(N_KV_HEADS):
        sink_cols.append(jnp.concatenate(
            [jnp.full((l, 1), sink_ref[kv * GQA_GROUP + g_], F32) for g_ in range(GQA_GROUP)], axis=0))

    for i in range(bb):
        r0 = i * l
        ck = ck_ref[i]
        cv = cv_ref[i]
        kn = k_new[r0:r0 + l, :]
        vn = v_new[r0:r0 + l, :]
        nk_ref[i, 0:WINDOW - l, :] = ck[l:, :]
        nk_ref[i, WINDOW - l:WINDOW, :] = kn
        nv_ref[i, 0:WINDOW - l, :] = cv[l:, :]
        nv_ref[i, WINDOW - l:WINDOW, :] = vn
        kcat[0:WINDOW, :] = ck
        kcat[WINDOW:WINDOW + l, :] = kn
        vcat[0:WINDOW, :] = cv
        vcat[WINDOW:WINDOW + l, :] = vn
        kall = kcat[...].astype(BF16)
        vall = vcat[...].astype(BF16)
        for kv in range(N_KV_HEADS):
            c0 = kv * HEAD_DIM
            qs = jnp.concatenate(
                [qbuf[r0:r0 + l, (kv * GQA_GROUP + g_) * HEAD_DIM:(kv * GQA_GROUP + g_ + 1) * HEAD_DIM]
                 for g_ in range(GQA_GROUP)], axis=0).astype(BF16)
            s = lax.dot_general(qs, kall[:, c0:c0 + HEAD_DIM], (((1,), (1,)), ((), ())),
                                preferred_element_type=F32)
            s = s + bias_scr[kv]
            sink = sink_cols[kv]
            m = jnp.maximum(jnp.max(s, axis=-1, keepdims=True), sink)
            p = jnp.exp(s - m)
            den = jnp.sum(p, axis=-1, keepdims=True) + jnp.exp(sink - m)
            o = jnp.dot(p.astype(BF16), vall[:, c0:c0 + HEAD_DIM], preferred_element_type=F32) / den
            for g_ in range(GQA_GROUP):
                h = kv * GQA_GROUP + g_
                obuf[r0:r0 + l, h * HEAD_DIM:(h + 1) * HEAD_DIM] = o[g_ * l:(g_ + 1) * l, :]

    y = jnp.dot(obuf[...].astype(BF16), wo_ref[...], preferred_element_type=F32) + bo_ref[...]
    o_ref[...] = _layer_norm_rows(DEEPNORM_ALPHA * x + y, g_ref[...], b_ref[...])


def _attn_sample(x, cache_k, cache_v, wqkv, bqkv, wo, bo, g, b, rel_bias, sinks, *, bb=16):
    bsz, l, d = x.shape
    assert bsz % bb == 0 and l == SUBLANES
    ti = np.arange(l)[:, None]
    ci = np.arange(SAMPLE_KEYS)[None, :]
    table = _t5_bucket_table(ti + WINDOW - ci)
    table = np.where(ci < WINDOW + l, table, -1).astype(np.int32)
    bucket = jnp.asarray(table)
    qkv_dim = wqkv.shape[1]
    rows = bb * l
    smem = pl.BlockSpec(memory_space=pltpu.SMEM)
    est = (d * qkv_dim + Q_DIM * d) * 2 + 8 * bb * WINDOW * KV_DIM * 4 + 8 * rows * d * 4
    return pl.pallas_call(
        functools.partial(_attn_sample_kernel, bb=bb, l=l),
        grid=(bsz // bb,),
        in_specs=[
            pl.BlockSpec((rows, d), lambda i: (i, 0)),
            pl.BlockSpec((bb, WINDOW, KV_DIM), lambda i: (i, 0, 0)),
            pl.BlockSpec((bb, WINDOW, KV_DIM), lambda i: (i, 0, 0)),
            _resident((d, qkv_dim)), _resident((1, qkv_dim)),
            _resident((Q_DIM, d)), _resident((1, d)),
            _resident((1, d)), _resident((1, d)),
            _resident((l, SAMPLE_KEYS)),
            smem, smem,
        ],
        out_specs=[
            pl.BlockSpec((rows, d), lambda i: (i, 0)),
            pl.BlockSpec((bb, WINDOW, KV_DIM), lambda i: (i, 0, 0)),
            pl.BlockSpec((bb, WINDOW, KV_DIM), lambda i: (i, 0, 0)),
        ],
        out_shape=[
            jax.ShapeDtypeStruct((bsz * l, d), F32),
            jax.ShapeDtypeStruct((bsz, WINDOW, KV_DIM), F32),
            jax.ShapeDtypeStruct((bsz, WINDOW, KV_DIM), F32),
        ],
        scratch_shapes=[
            pltpu.VMEM((SAMPLE_KEYS, KV_DIM), F32),
            pltpu.VMEM((SAMPLE_KEYS, KV_DIM), F32),
            pltpu.VMEM((rows, Q_DIM), F32),
            pltpu.VMEM((rows, Q_DIM), F32),
            pltpu.VMEM((N_KV_HEADS, GQA_GROUP * l, SAMPLE_KEYS), F32),
        ],
        compiler_params=pltpu.CompilerParams(
            dimension_semantics=("arbitrary",), vmem_limit_bytes=_vmem_limit(est)),
        name="attn_sample",
    )(x.reshape(bsz * l, d), cache_k, cache_v, wqkv, bqkv, wo, bo, g, b, bucket, rel_bias, sinks)


POOL_WINDOWS = (2, 4, 8, 16)
POOL_PAD = max(POOL_WINDOWS)
POOL_STATE_LEN = POOL_PAD - 1


def _pool_mix(window_sum, x, cnt_of, pw_ref, scale):
    gd = x.shape[-1] // len(POOL_WINDOWS)
    outs = []
    for g, w in enumerate(POOL_WINDOWS):
        diff = window_sum(g, w) / cnt_of(w) - x[:, g * gd:(g + 1) * gd]
        outs.append(jnp.dot(diff.astype(BF16), pw_ref[g], preferred_element_type=F32))
    return jnp.concatenate(outs, axis=-1) * scale


def _pool_prompt_kernel(x_ref, pw_ref, sc_ref, g_ref, b_ref, o_ref, st_ref, buf, *, tm):
    j = pl.program_id(1)
    nj = pl.num_programs(1)
    d = x_ref.shape[-1]
    gd = d // len(POOL_WINDOWS)

    @pl.when(j == 0)
    def _no_past():
        buf[0:POOL_PAD, :] = jnp.zeros((POOL_PAD, d), F32)

    x = x_ref[0]
    buf[POOL_PAD:POOL_PAD + tm, :] = x

    def window_sum(g, w):
        acc = x[:, g * gd:(g + 1) * gd]
        for s in range(1, w):
            acc = acc + buf[POOL_PAD - s:POOL_PAD - s + tm, g * gd:(g + 1) * gd]
        return acc

    pos1 = j * tm + lax.broadcasted_iota(jnp.int32, (tm, 1), 0) + 1

    def cnt_of(w):
        return jnp.minimum(pos1, w).astype(F32)

    y = _pool_mix(window_sum, x, cnt_of, pw_ref, sc_ref[...])
    o_ref[0] = _layer_norm_rows(DEEPNORM_ALPHA * x + y, g_ref[...], b_ref[...])

    @pl.when(j == nj - 1)
    def _emit_state():
        st_ref[0] = buf[tm + 1:tm + POOL_PAD, :]

    buf[0:POOL_PAD, :] = buf[tm:tm + POOL_PAD, :]


def _pool_prompt(x, pw, scale, g, b, *, tm=512):
    bsz, l, d = x.shape
    tm = min(tm, l)
    assert l % tm == 0 and tm >= POOL_PAD
    ng, gd = pw.shape[0], pw.shape[1]
    est = 6 * tm * d * 4 + ng * gd * gd * 2
    return pl.pallas_call(
        functools.partial(_pool_prompt_kernel, tm=tm),
        grid=(bsz, l // tm),
        in_specs=[
            pl.BlockSpec((1, tm, d), lambda bi, j: (bi, j, 0)),
            _resident((ng, gd, gd)), _resident((1, d)), _resident((1, d)), _resident((1, d)),
        ],
        out_specs=[
            pl.BlockSpec((1, tm, d), lambda bi, j: (bi, j, 0)),
            pl.BlockSpec((1, POOL_STATE_LEN, d), lambda bi, j: (bi, 0, 0)),
        ],
        out_shape=[
            jax.ShapeDtypeStruct((bsz, l, d), F32),
            jax.ShapeDtypeStruct((bsz, POOL_STATE_LEN, d), F32),
        ],
        scratch_shapes=[pltpu.VMEM((POOL_PAD + tm, d), F32)],
        compiler_params=pltpu.CompilerParams(
            dimension_semantics=("arbitrary", "arbitrary"), vmem_limit_bytes=_vmem_limit(est)),
        name="pool_prompt",
    )(x, pw, scale, g, b)


def _pool_sample_kernel(x_ref, st_ref, pw_ref, sc_ref, g_ref, b_ref, o_ref, nst_ref, buf, *, bb, l):
    d = x_ref.shape[-1]
    gd = d // len(POOL_WINDOWS)
    x3 = x_ref[...]
    buf[:, 1:POOL_PAD, :] = st_ref[...]
    buf[:, POOL_PAD:POOL_PAD + l, :] = x3
    x = x3.reshape(bb * l, d)

    def window_sum(g, w):
        acc = x3[:, :, g * gd:(g + 1) * gd]
        for s in range(1, w):
            acc = acc + buf[:, POOL_PAD - s:POOL_PAD - s + l, g * gd:(g + 1) * gd]
        return acc.reshape(bb * l, gd)

    y = _pool_mix(window_sum, x, lambda w: float(w), pw_ref, sc_ref[...])
    o_ref[...] = _layer_norm_rows(DEEPNORM_ALPHA * x + y, g_ref[...], b_ref[...]).reshape(bb, l, d)
    nst_ref[...] = buf[:, l + 1:l + POOL_PAD, :]


def _pool_sample(x, state, pw, scale, g, b, *, start, bb=16):
    bsz, l, d = x.shape
    assert bsz % bb == 0 and l == SUBLANES and start + 1 >= POOL_PAD
    ng, gd = pw.shape[0], pw.shape[1]
    est = 8 * bb * (POOL_PAD + l) * d * 4 + ng * gd * gd * 2
    return pl.pallas_call(
        functools.partial(_pool_sample_kernel, bb=bb, l=l),
        grid=(bsz // bb,),
        in_specs=[
            pl.BlockSpec((bb, l, d), lambda i: (i, 0, 0)),
            pl.BlockSpec((bb, POOL_STATE_LEN, d), lambda i: (i, 0, 0)),
            _resident((ng, gd, gd)), _resident((1, d)), _resident((1, d)), _resident((1, d)),
        ],
        out_specs=[
            pl.BlockSpec((bb, l, d), lambda i: (i, 0, 0)),
            pl.BlockSpec((bb, POOL_STATE_LEN, d), lambda i: (i, 0, 0)),
        ],
        out_shape=[
            jax.ShapeDtypeStruct((bsz, l, d), F32),
            jax.ShapeDtypeStruct((bsz, POOL_STATE_LEN, d), F32),
        ],
        scratch_shapes=[pltpu.VMEM((bb, POOL_PAD + l, d), F32)],
        compiler_params=pltpu.CompilerParams(
            dimension_semantics=("arbitrary",), vmem_limit_bytes=_vmem_limit(est)),
        name="pool_sample",
    )(x, state, pw, scale, g, b)


D_INNER = 2048
SSM_HEAD_DIM = 64
SSM_HEADS = D_INNER // SSM_HEAD_DIM
SSM_GROUPS = 4
SSM_HPG = SSM_HEADS // SSM_GROUPS
D_STATE = 128
CONV_WIDTH = 4
GBN = SSM_GROUPS * D_STATE
CONV_DIM = D_INNER + 2 * GBN
GROUP_INNER = D_INNER // SSM_GROUPS
RMS_EPS = 1e-5
CONV_PAD = SUBLANES
SSM_ROWS = 128


def _split_bf16(v, parts=3):
    out = []
    r = v
    for _ in range(parts):
        p = r.astype(BF16)
        out.append(p)
        r = r - p.astype(F32)
    return out


def _expand_heads(v, e):
    return sum(jnp.dot(p, e, preferred_element_type=F32) for p in _split_bf16(v))


def _ssm_front(x, xp_buf, conv_state, win_ref, cw_ref, cb_ref, dtb_ref, alog_ref, e_ref, tril_ref, ybuf, *, nseq, q):
    rows = nseq * q
    zxd = jnp.dot(x.astype(BF16), win_ref[...], preferred_element_type=F32)
    z = zxd[:, :D_INNER]
    xbc_pre = zxd[:, D_INNER:D_INNER + CONV_DIM]
    dt_pre = zxd[:, D_INNER + CONV_DIM:]

    if conv_state is not None:
        xp_buf[:, CONV_PAD - (CONV_WIDTH - 1):CONV_PAD, :] = conv_state
    xp_buf[:, CONV_PAD:CONV_PAD + q, :] = xbc_pre.reshape(nseq, q, CONV_DIM)
    conv = cb_ref[...].reshape(1, 1, CONV_DIM)
    for jj in range(CONV_WIDTH):
        off = CONV_PAD - (CONV_WIDTH - 1) + jj
        conv = conv + xp_buf[:, off:off + q, :] * cw_ref[jj:jj + 1, :].reshape(1, 1, CONV_DIM)
    conv = conv.reshape(rows, CONV_DIM)
    xbc = conv * jax.nn.sigmoid(conv)
    new_conv = xp_buf[:, q + CONV_PAD - (CONV_WIDTH - 1):q + CONV_PAD, :]

    xs = xbc[:, :D_INNER]
    dtv = dt_pre + dtb_ref[...]
    dt = jnp.maximum(dtv, 0.0) + jnp.log1p(jnp.exp(-jnp.abs(dtv)))
    a = -jnp.exp(alog_ref[...])
    da = dt * a
    tril = tril_ref[...]
    acum = sum(jnp.dot(tril, p, preferred_element_type=F32) for p in _split_bf16(da))
    causal = tril > 0.5
    acum_t = acum.T
    dt_t = dt.T

    for g in range(SSM_GROUPS):
        bg = xbc[:, D_INNER + g * D_STATE:D_INNER + (g + 1) * D_STATE].astype(BF16)
        cg = xbc[:, D_INNER + GBN + g * D_STATE:D_INNER + GBN + (g + 1) * D_STATE].astype(BF16)
        cb = lax.dot_general(cg, bg, (((1,), (1,)), ((), ())), preferred_element_type=F32)
        for r in range(SSM_HPG):
            h = g * SSM_HPG + r
            seg = acum[:, h:h + 1] - acum_t[h:h + 1, :]
            lm = jnp.exp(jnp.where(causal, seg, NEG_INF))
            w = (cb * lm * dt_t[h:h + 1, :]).astype(BF16)
            xh = xs[:, h * SSM_HEAD_DIM:(h + 1) * SSM_HEAD_DIM].astype(BF16)
            ybuf[:, h * SSM_HEAD_DIM:(h + 1) * SSM_HEAD_DIM] = jnp.dot(w, xh, preferred_element_type=F32)

    a3 = acum.reshape(nseq, q, LANES)
    alast = jnp.broadcast_to(a3[:, q - 1:q, :], (nseq, q, LANES)).reshape(rows, LANES)
    e = e_ref[...]
    exp_a = _expand_heads(jnp.exp(acum), e)
    dd = _expand_heads(jnp.exp(alast - acum) * dt, e)
    return dict(z=z, xs=xs, xbc=xbc, exp_a=exp_a, xdd=(xs * dd).astype(BF16), new_conv=new_conv)


def _ssm_finish(x, y, z, nw_ref, wout_ref, g_ref, b_ref):
    y = y * (z * jax.nn.sigmoid(z))
    parts = []
    for g in range(SSM_GROUPS):
        yg = y[:, g * GROUP_INNER:(g + 1) * GROUP_INNER]
        parts.append(yg * lax.rsqrt(jnp.mean(yg * yg, axis=-1, keepdims=True) + RMS_EPS))
    y = jnp.concatenate(parts, axis=-1) * nw_ref[...]
    out = jnp.dot(y.astype(BF16), wout_ref[...], preferred_element_type=F32)
    return _layer_norm_rows(DEEPNORM_ALPHA * x + out, g_ref[...], b_ref[...])


def _ssm_prompt_kernel(x_ref, win_ref, cw_ref, cb_ref, dtb_ref, alog_ref, dsk_ref, nw_ref, wout_ref, e_ref,
                       tril_ref, g_ref, b_ref, o_ref, cout_ref, sout_ref, xp_buf, ht, ybuf):
    j = pl.program_id(1)
    nj = pl.num_programs(1)
    q = SSM_ROWS

    @pl.when(j == 0)
    def _no_past():
        xp_buf[:, 0:CONV_PAD, :] = jnp.zeros((1, CONV_PAD, CONV_DIM), F32)
        ht[...] = jnp.zeros(ht.shape, F32)

    x = x_ref[0]
    f = _ssm_front(x, xp_buf, None, win_ref, cw_ref, cb_ref, dtb_ref, alog_ref, e_ref, tril_ref, ybuf, nseq=1, q=q)
    xbc, exp_a, xdd = f["xbc"], f["exp_a"], f["xdd"]
    decay = exp_a[q - 1:q, :]
    y_parts = []
    for g in range(SSM_GROUPS):
        sl = slice(g * GROUP_INNER, (g + 1) * GROUP_INNER)
        bg = xbc[:, D_INNER + g * D_STATE:D_INNER + (g + 1) * D_STATE]
        cg = xbc[:, D_INNER + GBN + g * D_STATE:D_INNER + GBN + (g + 1) * D_STATE].astype(BF16)
        hg = ht[:, sl]
        y_parts.append(jnp.dot(cg, hg.astype(BF16), preferred_element_type=F32))
        ht[:, sl] = hg * decay[:, sl] + jnp.dot(bg.T.astype(BF16), xdd[:, sl], preferred_element_type=F32)
    y = ybuf[...] + exp_a * jnp.concatenate(y_parts, axis=-1) + dsk_ref[...] * f["xs"]
    o_ref[0] = _ssm_finish(x, y, f["z"], nw_ref, wout_ref, g_ref, b_ref)
    xp_buf[:, 0:CONV_PAD, :] = xp_buf[:, q:q + CONV_PAD, :]

    @pl.when(j == nj - 1)
    def _emit_state():
        cout_ref[...] = f["new_conv"]
        sout_ref[0] = ht[...].T


def _ssm_consts(nseq, q):
    rows = nseq * q
    r = np.arange(rows)
    tril = ((r[:, None] >= r[None, :]) & (r[:, None] // q == r[None, :] // q)).astype(np.float32)
    e = np.zeros((LANES, D_INNER), np.float32)
    e[np.arange(D_INNER) // SSM_HEAD_DIM, np.arange(D_INNER)] = 1.0
    return jnp.asarray(tril, BF16), jnp.asarray(e, BF16)


def _ssm_weight_specs(d, win_cols):
    return [
        _resident((d, win_cols)), _resident((CONV_WIDTH, CONV_DIM)), _resident((1, CONV_DIM)),
        _resident((1, LANES)), _resident((1, LANES)),
    ]


def _ssm_prompt(x, w, g, b):
    bsz, l, d = x.shape
    q = SSM_ROWS
    assert l % q == 0
    tril, e = _ssm_consts(1, q)
    win_cols = w["win"].shape[1]
    est = (d * win_cols + D_INNER * d) * 2 + LANES * D_INNER * 4 + 12 * q * D_INNER * 4 + 3 * q * win_cols * 4
    return pl.pallas_call(
        _ssm_prompt_kernel,
        grid=(bsz, l // q),
        in_specs=[pl.BlockSpec((1, q, d), lambda bi, j: (bi, j, 0))] + _ssm_weight_specs(d, win_cols) + [
            _resident((1, D_INNER)), _resident((1, D_INNER)), _resident((D_INNER, d)),
            _resident((LANES, D_INNER)), _resident((q, q)), _resident((1, d)), _resident((1, d)),
        ],
        out_specs=[
            pl.BlockSpec((1, q, d), lambda bi, j: (bi, j, 0)),
            pl.BlockSpec((1, CONV_WIDTH - 1, CONV_DIM), lambda bi, j: (bi, 0, 0)),
            pl.BlockSpec((1, D_INNER, D_STATE), lambda bi, j: (bi, 0, 0)),
        ],
        out_shape=[
            jax.ShapeDtypeStruct((bsz, l, d), F32),
            jax.ShapeDtypeStruct((bsz, CONV_WIDTH - 1, CONV_DIM), F32),
            jax.ShapeDtypeStruct((bsz, D_INNER, D_STATE), F32),
        ],
        scratch_shapes=[
            pltpu.VMEM((1, CONV_PAD + q, CONV_DIM), F32),
            pltpu.VMEM((D_STATE, D_INNER), F32),
            pltpu.VMEM((q, D_INNER), F32),
        ],
        compiler_params=pltpu.CompilerParams(
            dimension_semantics=("arbitrary", "arbitrary"), vmem_limit_bytes=_vmem_limit(est)),
        name="ssm_prompt",
    )(x, w["win"], w["cw"], w["cb"], w["dtb"], w["alog"], w["dsk"], w["nw"], w["wout"], e, tril, g, b)


def _ssm_sample_front_kernel(x_ref, cst_ref, win_ref, cw_ref, cb_ref, dtb_ref, alog_ref, dsk_ref, e_ref, tril_ref,
                             y_ref, z_ref, ea_ref, xdd_ref, c_ref, bt_ref, cout_ref, xp_buf, ybuf, *, nseq, q):
    x = x_ref[...]
    f = _ssm_front(x, xp_buf, cst_ref[...], win_ref, cw_ref, cb_ref, dtb_ref, alog_ref, e_ref, tril_ref, ybuf,
                   nseq=nseq, q=q)
    xbc = f["xbc"]
    y_ref[...] = ybuf[...] + dsk_ref[...] * f["xs"]
    z_ref[...] = f["z"]
    ea_ref[...] = f["exp_a"]
    xdd_ref[...] = f["xdd"]
    c_ref[...] = xbc[:, D_INNER + GBN:]
    bt_ref[...] = jnp.concatenate(
        [xbc[:, D_INNER + g * D_STATE:D_INNER + (g + 1) * D_STATE].T for g in range(SSM_GROUPS)], axis=0).astype(BF16)
    cout_ref[...] = f["new_conv"]


def _ssm_sample_state_kernel(st_ref, y_ref, ea_ref, c_ref, xdd_ref, bt_ref, o_ref, nst_ref, *, bb, q, nseq):
    ea = ea_ref[...]
    xdd = xdd_ref[...]
    first_seq = (pl.program_id(0) % (nseq // bb)) * bb
    lane_seq = lax.broadcasted_iota(jnp.int32, (D_STATE, nseq * q), 1) // q
    for i in range(bb):
        r0 = i * q
        ht = st_ref[i].T
        decay = ea[r0 + q - 1:r0 + q, :]
        y_parts = []
        new_parts = []
        for g in range(SSM_GROUPS):
            sl = slice(g * GROUP_INNER, (g + 1) * GROUP_INNER)
            cg = c_ref[r0:r0 + q, g * D_STATE:(g + 1) * D_STATE].astype(BF16)
            hg = ht[:, sl]
            y_parts.append(jnp.dot(cg, hg.astype(BF16), preferred_element_type=F32))
            btg = bt_ref[g * D_STATE:(g + 1) * D_STATE, :]
            btg = jnp.where(lane_seq == first_seq + i, btg, jnp.zeros_like(btg))
            new_parts.append(hg * decay[:, sl] + jnp.dot(btg, xdd[:, sl], preferred_element_type=F32))
        o_ref[r0:r0 + q, :] = y_ref[r0:r0 + q, :] + ea[r0:r0 + q, :] * jnp.concatenate(y_parts, axis=-1)
        nst_ref[i] = jnp.concatenate(new_parts, axis=-1).T


def _ssm_sample_finish_kernel(x_ref, y_ref, z_ref, nw_ref, wout_ref, g_ref, b_ref, o_ref):
    o_ref[...] = _ssm_finish(x_ref[...], y_ref[...], z_ref[...], nw_ref, wout_ref, g_ref, b_ref)


def _ssm_sample(x, conv_state, ssm_state, w, g, b, *, bb_state=4):
    bsz, q, d = x.shape
    t = bsz * q
    nseq = SSM_ROWS // q
    rows = SSM_ROWS
    assert q == SUBLANES and bsz % nseq == 0 and nseq % bb_state == 0
    tril, e = _ssm_consts(nseq, q)
    win_cols = w["win"].shape[1]
    x2 = x.reshape(t, d)
    row_spec = lambda c: pl.BlockSpec((rows, c), lambda i: (i, 0))
    est = d * win_cols * 2 + LANES * D_INNER * 4 + 16 * rows * D_INNER * 4 + 3 * rows * win_cols * 4
    ydiag, z, exp_a, xdd, cmat, bt, new_conv = pl.pallas_call(
        functools.partial(_ssm_sample_front_kernel, nseq=nseq, q=q),
        grid=(t // rows,),
        in_specs=[row_spec(d), pl.BlockSpec((nseq, CONV_WIDTH - 1, CONV_DIM), lambda i: (i, 0, 0))]
        + _ssm_weight_specs(d, win_cols) + [_resident((1, D_INNER)), _resident((LANES, D_INNER)), _resident((rows, rows))],
        out_specs=[row_spec(D_INNER), row_spec(D_INNER), row_spec(D_INNER), row_spec(D_INNER), row_spec(GBN),
                   pl.BlockSpec((GBN, rows), lambda i: (i, 0)),
                   pl.BlockSpec((nseq, CONV_WIDTH - 1, CONV_DIM), lambda i: (i, 0, 0))],
        out_shape=[
            jax.ShapeDtypeStruct((t, D_INNER), F32), jax.ShapeDtypeStruct((t, D_INNER), F32),
            jax.ShapeDtypeStruct((t, D_INNER), F32), jax.ShapeDtypeStruct((t, D_INNER), BF16),
            jax.ShapeDtypeStruct((t, GBN), F32),
            jax.ShapeDtypeStruct((t // rows * GBN, rows), BF16),
            jax.ShapeDtypeStruct((bsz, CONV_WIDTH - 1, CONV_DIM), F32),
        ],
        scratch_shapes=[pltpu.VMEM((nseq, CONV_PAD + q, CONV_DIM), F32), pltpu.VMEM((rows, D_INNER), F32)],
        compiler_params=pltpu.CompilerParams(
            dimension_semantics=("arbitrary",), vmem_limit_bytes=_vmem_limit(est)),
        name="ssm_sample_front",
    )(x2, conv_state, w["win"], w["cw"], w["cb"], w["dtb"], w["alog"], w["dsk"], e, tril)

    srows = bb_state * q
    srow_spec = lambda c: pl.BlockSpec((srows, c), lambda i: (i, 0))
    st_spec = pl.BlockSpec((bb_state, D_INNER, D_STATE), lambda i: (i, 0, 0))
    est = 4 * bb_state * D_INNER * D_STATE * 4 + 8 * D_INNER * D_STATE * 4
    per_block = nseq // bb_state
    y, new_state = pl.pallas_call(
        functools.partial(_ssm_sample_state_kernel, bb=bb_state, q=q, nseq=nseq),
        grid=(bsz // bb_state,),
        in_specs=[st_spec, srow_spec(D_INNER), srow_spec(D_INNER), srow_spec(GBN),
                  pl.BlockSpec((rows, D_INNER), lambda i: (i // per_block, 0)),
                  pl.BlockSpec((GBN, rows), lambda i: (i // per_block, 0))],
        out_specs=[srow_spec(D_INNER), st_spec],
        out_shape=[jax.ShapeDtypeStruct((t, D_INNER), F32),
                   jax.ShapeDtypeStruct((bsz, D_INNER, D_STATE), F32)],
        compiler_params=pltpu.CompilerParams(
            dimension_semantics=("arbitrary",), vmem_limit_bytes=_vmem_limit(est)),
        name="ssm_sample_state",
    )(ssm_state, ydiag, exp_a, cmat, xdd, bt)

    est = D_INNER * d * 2 + 8 * rows * D_INNER * 4
    out = pl.pallas_call(
        _ssm_sample_finish_kernel,
        grid=(t // rows,),
        in_specs=[row_spec(d), row_spec(D_INNER), row_spec(D_INNER),
                  _resident((1, D_INNER)), _resident((D_INNER, d)), _resident((1, d)), _resident((1, d))],
        out_specs=row_spec(d),
        out_shape=jax.ShapeDtypeStruct((t, d), F32),
        compiler_params=pltpu.CompilerParams(
            dimension_semantics=("arbitrary",), vmem_limit_bytes=_vmem_limit(est)),
        name="ssm_sample_finish",
    )(x2, y, z, w["nw"], w["wout"], g, b)
    return out.reshape(bsz, q, d), new_conv, new_state


def _ssm_weights(w_in, conv_w, conv_b, dt_bias, a_log, d_skip, norm_w, w_out):
    d = w_in.shape[0]
    pad = LANES - SSM_HEADS
    win = jnp.concatenate([w_in, jnp.zeros((d, pad), w_in.dtype)], axis=1).astype(BF16)
    return dict(
        win=win, cw=conv_w, cb=conv_b.reshape(1, CONV_DIM),
        dtb=jnp.pad(dt_bias, (0, pad)).reshape(1, LANES),
        alog=jnp.pad(a_log, (0, pad)).reshape(1, LANES),
        dsk=jnp.repeat(d_skip, SSM_HEAD_DIM).reshape(1, D_INNER),
        nw=norm_w.reshape(1, D_INNER), wout=w_out.astype(BF16))


def kernel(x_prompt, x_sample, cache_k, cache_v, state_conv, state_ssm, state_pool, rel_bias, attn_w_qkv, attn_b_qkv, attn_w_o, attn_b_o, attn_sinks, ssm_w_in, ssm_conv_w, ssm_conv_b, ssm_dt_bias, ssm_a_log, ssm_d, ssm_norm_w, ssm_w_out, pool_w, pool_scale, ffn_w_gate, ffn_w_up, ffn_w_down, ln_g, ln_b):
    xp, xs = x_prompt, x_sample
    d = xp.shape[-1]
    nk_p, nv_p, nc_p, nh_p, npool_p = [], [], [], [], []
    nk_s, nv_s, nc_s, nh_s, npool_s = [], [], [], [], []
    for i in range(DEPTH):
        j = i // N_MIXERS
        kind = i % N_MIXERS
        g1 = ln_g[i, 0].reshape(1, d)
        b1 = ln_b[i, 0].reshape(1, d)
        if kind == 0:
            wqkv = attn_w_qkv[j].astype(BF16)
            bqkv = attn_b_qkv[j].reshape(1, -1)
            wo = attn_w_o[j].astype(BF16)
            bo = attn_b_o[j].reshape(1, d)
            xp, kp, vp = _attn_prompt(xp, wqkv, bqkv, wo, bo, g1, b1, rel_bias, attn_sinks[j])
            ck = cache_k[j].reshape(cache_k.shape[1], WINDOW, KV_DIM)
            cv = cache_v[j].reshape(cache_v.shape[1], WINDOW, KV_DIM)
            xs, ks_, vs_ = _attn_sample(xs, ck, cv, wqkv, bqkv, wo, bo, g1, b1, rel_bias, attn_sinks[j])
            xs = xs.reshape(x_sample.shape)
            kv_shape = (-1, WINDOW, N_KV_HEADS, HEAD_DIM)
            nk_p.append(kp.reshape(kv_shape)); nv_p.append(vp.reshape(kv_shape))
            nk_s.append(ks_.reshape(kv_shape)); nv_s.append(vs_.reshape(kv_shape))
        elif kind == 1:
            w = _ssm_weights(ssm_w_in[j], ssm_conv_w[j], ssm_conv_b[j], ssm_dt_bias[j], ssm_a_log[j],
                             ssm_d[j], ssm_norm_w[j], ssm_w_out[j])
            xp, cp, hp = _ssm_prompt(xp, w, g1, b1)
            xs, cs_, hs_ = _ssm_sample(xs, state_conv[j], state_ssm[j].reshape(-1, D_INNER, D_STATE), w, g1, b1)
            st_shape = (-1, SSM_HEADS, SSM_HEAD_DIM, D_STATE)
            nc_p.append(cp); nh_p.append(hp.reshape(st_shape))
            nc_s.append(cs_); nh_s.append(hs_.reshape(st_shape))
        else:
            pw = pool_w[j].astype(BF16)
            psc = pool_scale[j].reshape(1, d)
            xp, pp = _pool_prompt(xp, pw, psc, g1, b1)
            xs, ps_ = _pool_sample(xs, state_pool[j], pw, psc, g1, b1, start=PAST_LEN)
            npool_p.append(pp); npool_s.append(ps_)
        wg = ffn_w_gate[i].astype(BF16)
        wu = ffn_w_up[i].astype(BF16)
        wd = ffn_w_down[i].astype(BF16)
        g2 = ln_g[i, 1].reshape(1, d)
        b2 = ln_b[i, 1].reshape(1, d)
        xp = _ffn_ln(xp.reshape(-1, d), wg, wu, wd, g2, b2).reshape(xp.shape)
        xs = _ffn_ln(xs.reshape(-1, d), wg, wu, wd, g2, b2).reshape(xs.shape)
    return (xp, xs,
            jnp.stack(nk_p), jnp.stack(nv_p), jnp.stack(nc_p), jnp.stack(nh_p), jnp.stack(npool_p),
            jnp.stack(nk_s), jnp.stack(nv_s), jnp.stack(nc_s), jnp.stack(nh_s), jnp.stack(npool_s))
```

```python
import functools
import math

import jax
import jax.numpy as jnp
import numpy as np
from jax import lax
from jax.experimental import pallas as pl
from jax.experimental.pallas import tpu as pltpu

DEPTH = 4
N_MIXERS = 3
PAST_LEN = 8192
DEEPNORM_ALPHA = (2 * DEPTH) ** 0.25
LN_EPS = 1e-5

V7X_VMEM_BYTES = 64 * 1024 * 1024
LANES = 128
SUBLANES = 8
MXU_DIM = 256

BF16 = jnp.bfloat16
F32 = jnp.float32


def _vmem_limit(estimate_bytes):
    return int(min(V7X_VMEM_BYTES - 8 * 1024 * 1024, max(32 * 1024 * 1024, estimate_bytes * 3 // 2)))


def _layer_norm_rows(v, g, b):
    mu = jnp.mean(v, axis=-1, keepdims=True)
    d = v - mu
    var = jnp.mean(d * d, axis=-1, keepdims=True)
    return d * lax.rsqrt(var + LN_EPS) * g + b


def _resident(shape):
    nd = len(shape)
    return pl.BlockSpec(shape, lambda *_: (0,) * nd, pipeline_mode=pl.Buffered(1))


def _layer_resident(shape, layer):
    nd = len(shape)
    return pl.BlockSpec((1,) + tuple(shape), lambda *_: (layer,) + (0,) * nd, pipeline_mode=pl.Buffered(1))


def _ffn_chunks(d_ff):
    step = 2 * MXU_DIM
    edges = list(range(0, d_ff, step)) + [d_ff]
    return tuple(zip(edges[:-1], edges[1:]))


def _ffn_kernel(x_ref, wg_ref, wu_ref, wd_ref, g_ref, b_ref, o_ref, *, chunks):
    x = x_ref[...]
    xb = x.astype(BF16)
    acc = jnp.zeros(x.shape, F32)
    for c0, c1 in chunks:
        gate = jnp.dot(xb, wg_ref[0, :, c0:c1], preferred_element_type=F32)
        up = jnp.dot(xb, wu_ref[0, :, c0:c1], preferred_element_type=F32)
        h = (gate * jax.nn.sigmoid(gate)) * up
        acc = acc + jnp.dot(h.astype(BF16), wd_ref[0, c0:c1, :], preferred_element_type=F32)
    o_ref[...] = _layer_norm_rows(DEEPNORM_ALPHA * x + acc, g_ref[...], b_ref[...])


def _ffn_ln(x, wg, wu, wd, g, b, *, layer, tm=512):
    t, d = x.shape
    d_ff = wg.shape[-1]
    tm = min(tm, t)
    assert t % tm == 0
    est = 3 * d * d_ff * 2 + 4 * tm * d * 4 + 4 * tm * d_ff * 4
    return pl.pallas_call(
        functools.partial(_ffn_kernel, chunks=_ffn_chunks(d_ff)),
        grid=(t // tm,),
        in_specs=[
            pl.BlockSpec((tm, d), lambda i: (i, 0)),
            _layer_resident((d, d_ff), layer),
            _layer_resident((d, d_ff), layer),
            _layer_resident((d_ff, d), layer),
            _resident((1, d)),
            _resident((1, d)),
        ],
        out_specs=pl.BlockSpec((tm, d), lambda i: (i, 0)),
        out_shape=jax.ShapeDtypeStruct((t, d), F32),
        compiler_params=pltpu.CompilerParams(
            dimension_semantics=("arbitrary",), vmem_limit_bytes=_vmem_limit(est)),
        name="ffn_ln",
    )(x, wg, wu, wd, g, b)


HEAD_DIM = 64
N_HEADS = 16
N_KV_HEADS = 4
GQA_GROUP = N_HEADS // N_KV_HEADS
WINDOW = 128
REL_BUCKETS = 32
REL_MAX_DIST = 128
Q_DIM = N_HEADS * HEAD_DIM
KV_DIM = N_KV_HEADS * HEAD_DIM
NEG_INF = float("-inf")
SCORE_LOOKAHEAD = 4


def _t5_bucket_table(dist):
    n = np.maximum(dist, 0)
    max_exact = REL_BUCKETS // 2
    nf = np.maximum(n, 1).astype(np.float32)
    large = max_exact + (np.log(nf / np.float32(max_exact)) / np.float32(math.log(REL_MAX_DIST / max_exact))
                         * np.float32(REL_BUCKETS - max_exact)).astype(np.int32)
    large = np.minimum(large, REL_BUCKETS - 1)
    bucket = np.where(n < max_exact, n, large)
    valid = (dist >= 0) & (dist < WINDOW)
    return np.where(valid, bucket, -1).astype(np.int32)


def _bias_from_buckets(bucket, relb_ref, head):
    acc = jnp.full(bucket.shape, NEG_INF, F32)
    for bkt in range(REL_BUCKETS):
        acc = jnp.where(bucket == bkt, relb_ref[bkt, head], acc)
    return acc


def _attn_prompt_kernel(x_ref, wqkv_ref, bqkv_ref, wo_ref, bo_ref, g_ref, b_ref, bucket_ref,
                        relb_ref, sink_ref, o_ref, kout_ref, vout_ref,
                        ka_lo, ka_hi, kb_lo, kb_hi, vt, ot, bias_scr, sink_scr, *, tq):
    bi = pl.program_id(0)
    j = pl.program_id(1)
    nj = pl.num_programs(1)
    blk = WINDOW
    half_heads = GQA_GROUP // 2
    kbufs = (ka_lo, ka_hi, kb_lo, kb_hi)

    @pl.when((bi == 0) & (j == 0))
    def _build_tables():
        bucket = bucket_ref[...]
        lane = lax.broadcasted_iota(jnp.int32, (1, 2 * blk), 1)
        for kv in range(N_KV_HEADS):
            for half in range(2):
                ha = kv * GQA_GROUP + half
                hb = ha + half_heads
                pair = kv * 2 + half
                bias_scr[pair, :, 0:blk] = _bias_from_buckets(bucket, relb_ref, ha)
                bias_scr[pair, :, blk:2 * blk] = _bias_from_buckets(bucket, relb_ref, hb)
                sink_scr[pair] = jnp.where(lane < blk, sink_ref[ha], sink_ref[hb])

    @pl.when(j == 0)
    def _no_past():
        for buf in kbufs:
            buf[0:blk, :] = jnp.zeros((blk, KV_DIM), BF16)
        vt[:, 0:blk] = jnp.zeros((KV_DIM, blk), BF16)

    x = x_ref[0]
    qkv = jnp.dot(x.astype(BF16), wqkv_ref[0], preferred_element_type=F32) + bqkv_ref[...]
    q = (qkv[:, :Q_DIM] * (HEAD_DIM ** -0.5)).astype(BF16)
    k = qkv[:, Q_DIM:Q_DIM + KV_DIM]
    v = qkv[:, Q_DIM + KV_DIM:]
    k_sw = jnp.concatenate(
        [pltpu.roll(k[:, c * LANES:(c + 1) * LANES], HEAD_DIM, axis=1) for c in range(KV_DIM // LANES)], axis=1)
    lo = (lax.broadcasted_iota(jnp.int32, (1, KV_DIM), 1) % LANES) < HEAD_DIM
    ka_lo[blk:blk + tq, :] = jnp.where(lo, k, 0.0).astype(BF16)
    ka_hi[blk:blk + tq, :] = jnp.where(lo, 0.0, k).astype(BF16)
    kb_lo[blk:blk + tq, :] = jnp.where(lo, k_sw, 0.0).astype(BF16)
    kb_hi[blk:blk + tq, :] = jnp.where(lo, 0.0, k_sw).astype(BF16)
    vt[:, blk:blk + tq] = v.T.astype(BF16)

    @pl.when(j == nj - 1)
    def _emit_cache():
        kout_ref[0] = k[tq - WINDOW:, :]
        vout_ref[0] = v[tq - WINDOW:, :]

    ks = lax.broadcasted_iota(jnp.int32, (blk, 2 * blk), 0)
    qt = lax.broadcasted_iota(jnp.int32, (blk, 2 * blk), 1) % blk
    own = ks <= qt
    chains = [(i, kv, half) for i in range(tq // blk) for kv in range(N_KV_HEADS) for half in range(2)]

    def scores(i, kv, half):
        r0 = i * blk
        c0 = (kv // 2) * LANES
        in_lo = kv % 2 == 0
        if half == 0:
            kk = (ka_lo if in_lo else kb_lo)[r0:r0 + 2 * blk, c0:c0 + LANES]
        else:
            kk = (kb_hi if in_lo else ka_hi)[r0:r0 + 2 * blk, c0:c0 + LANES]
        qa = q[r0:r0 + blk, (2 * kv) * LANES:(2 * kv + 1) * LANES]
        qb = q[r0:r0 + blk, (2 * kv + 1) * LANES:(2 * kv + 2) * LANES]
        q_pair = jnp.concatenate([qa, qb], axis=0)
        return lax.dot_general(kk, q_pair, (((1,), (1,)), ((), ())), preferred_element_type=F32)

    def softmax(i, kv, half, s2):
        pair = kv * 2 + half
        s_prev = s2[0:blk, :]
        if i == 0:
            s_prev = jnp.where(j == 0, NEG_INF, s_prev)
        s = jnp.where(own, s2[blk:2 * blk, :], s_prev) + bias_scr[pair]
        sink = sink_scr[pair]
        m = jnp.maximum(jnp.max(s, axis=0, keepdims=True), sink)
        p = jnp.exp(s - m)
        inv = 1.0 / (jnp.sum(p, axis=0, keepdims=True) + jnp.exp(sink - m))
        p2 = jnp.concatenate([jnp.where(own, 0.0, p), jnp.where(own, p, 0.0)], axis=0).astype(BF16)
        return p2, inv

    def weighted_values(i, kv, half, p2, inv):
        r0 = i * blk
        ha = kv * GQA_GROUP + half
        hb = ha + half_heads
        v_t = vt[kv * HEAD_DIM:(kv + 1) * HEAD_DIM, r0:r0 + 2 * blk]
        o_t = jnp.dot(v_t, p2, preferred_element_type=F32) * inv
        ot[ha * HEAD_DIM:(ha + 1) * HEAD_DIM, r0:r0 + blk] = o_t[:, 0:blk]
        ot[hb * HEAD_DIM:(hb + 1) * HEAD_DIM, r0:r0 + blk] = o_t[:, blk:2 * blk]

    pending = [scores(*ch) for ch in chains[:SCORE_LOOKAHEAD]]
    probs = None
    for c, chain in enumerate(chains):
        s_cur = pending.pop(0)
        if c + SCORE_LOOKAHEAD < len(chains):
            pending.append(scores(*chains[c + SCORE_LOOKAHEAD]))
        new_probs = softmax(*chain, s_cur)
        if probs is not None:
            weighted_values(*chains[c - 1], *probs)
        probs = new_probs
    weighted_values(*chains[-1], *probs)

    o = ot[...].T.astype(BF16)
    y = jnp.dot(o, wo_ref[0], preferred_element_type=F32) + bo_ref[...]
    o_ref[0] = _layer_norm_rows(DEEPNORM_ALPHA * x + y, g_ref[...], b_ref[...])
    for buf in kbufs:
        buf[0:blk, :] = buf[tq:tq + blk, :]
    vt[:, 0:blk] = vt[:, tq:tq + blk]


def _attn_prompt(x, wqkv, bqkv, wo, bo, g, b, rel_bias, sinks, *, layer, tq=512):
    bsz, l, d = x.shape
    tq = min(tq, l)
    assert l % tq == 0 and tq % WINDOW == 0
    qi = np.arange(WINDOW)[None, :]
    si = np.arange(WINDOW)[:, None]
    bucket = jnp.asarray(_t5_bucket_table(np.where(si <= qi, qi - si, qi + WINDOW - si)))
    qkv_dim = wqkv.shape[-1]
    smem = pl.BlockSpec(memory_space=pltpu.SMEM)
    est = (d * qkv_dim + Q_DIM * d) * 2 + 5 * tq * d * 4 + N_HEADS * WINDOW * 2 * WINDOW * 4 + 3 * tq * qkv_dim * 4
    return pl.pallas_call(
        functools.partial(_attn_prompt_kernel, tq=tq),
        grid=(bsz, l // tq),
        in_specs=[
            pl.BlockSpec((1, tq, d), lambda bi, j: (bi, j, 0)),
            _layer_resident((d, qkv_dim), layer), _resident((1, qkv_dim)),
            _layer_resident((Q_DIM, d), layer), _resident((1, d)),
            _resident((1, d)), _resident((1, d)),
            _resident((WINDOW, WINDOW)),
            smem, smem,
        ],
        out_specs=[
            pl.BlockSpec((1, tq, d), lambda bi, j: (bi, j, 0)),
            pl.BlockSpec((1, WINDOW, KV_DIM), lambda bi, j: (bi, 0, 0)),
            pl.BlockSpec((1, WINDOW, KV_DIM), lambda bi, j: (bi, 0, 0)),
        ],
        out_shape=[
            jax.ShapeDtypeStruct((bsz, l, d), F32),
            jax.ShapeDtypeStruct((bsz, WINDOW, KV_DIM), F32),
            jax.ShapeDtypeStruct((bsz, WINDOW, KV_DIM), F32),
        ],
        scratch_shapes=[pltpu.VMEM((WINDOW + tq, KV_DIM), BF16)] * 4 + [
            pltpu.VMEM((KV_DIM, WINDOW + tq), BF16),
            pltpu.VMEM((Q_DIM, tq), F32),
            pltpu.VMEM((N_HEADS // 2, WINDOW, 2 * WINDOW), F32),
            pltpu.VMEM((N_HEADS // 2, 1, 2 * WINDOW), F32),
        ],
        compiler_params=pltpu.CompilerParams(
            dimension_semantics=("arbitrary", "arbitrary"), vmem_limit_bytes=_vmem_limit(est)),
        name="attn_prompt",
    )(x, wqkv, bqkv, wo, bo, g, b, bucket, rel_bias, sinks)


SAMPLE_KEYS = 2 * WINDOW


def _attn_sample_kernel(x_ref, ck_ref, cv_ref, wqkv_ref, bqkv_ref, wo_ref, bo_ref, g_ref, b_ref, bucket_ref,
                        relb_ref, sink_ref, o_ref, nk_ref, nv_ref,
                        kcat, vcat, qbuf, obuf, bias_scr, *, bb, l):
    rows_g = GQA_GROUP * l

    @pl.when(pl.program_id(0) == 0)
    def _init():
        bucket = bucket_ref[...]
        for h in range(N_HEADS):
            kv, g_ = divmod(h, GQA_GROUP)
            bias_scr[kv, g_ * l:(g_ + 1) * l, :] = _bias_from_buckets(bucket, relb_ref, h)
        kcat[...] = jnp.zeros(kcat.shape, F32)
        vcat[...] = jnp.zeros(vcat.shape, F32)

    x = x_ref[...]
    qkv = jnp.dot(x.astype(BF16), wqkv_ref[0], preferred_element_type=F32) + bqkv_ref[...]
    qbuf[...] = qkv[:, :Q_DIM] * (HEAD_DIM ** -0.5)
    k_new = qkv[:, Q_DIM:Q_DIM + KV_DIM]
    v_new = qkv[:, Q_DIM + KV_DIM:]

    sink_cols = []
    for kv in range(N_KV_HEADS):
        sink_cols.append(jnp.concatenate(
            [jnp.full((l, 1), sink_ref[kv * GQA_GROUP + g_], F32) for g_ in range(GQA_GROUP)], axis=0))

    for i in range(bb):
        r0 = i * l
        ck = ck_ref[i]
        cv = cv_ref[i]
        kn = k_new[r0:r0 + l, :]
        vn = v_new[r0:r0 + l, :]
        nk_ref[i, 0:WINDOW - l, :] = ck[l:, :]
        nk_ref[i, WINDOW - l:WINDOW, :] = kn
        nv_ref[i, 0:WINDOW - l, :] = cv[l:, :]
        nv_ref[i, WINDOW - l:WINDOW, :] = vn
        kcat[0:WINDOW, :] = ck
        kcat[WINDOW:WINDOW + l, :] = kn
        vcat[0:WINDOW, :] = cv
        vcat[WINDOW:WINDOW + l, :] = vn
        kall = kcat[...].astype(BF16)
        vall = vcat[...].astype(BF16)
        for kv in range(N_KV_HEADS):
            c0 = kv * HEAD_DIM
            qs = jnp.concatenate(
                [qbuf[r0:r0 + l, (kv * GQA_GROUP + g_) * HEAD_DIM:(kv * GQA_GROUP + g_ + 1) * HEAD_DIM]
                 for g_ in range(GQA_GROUP)], axis=0).astype(BF16)
            s = lax.dot_general(qs, kall[:, c0:c0 + HEAD_DIM], (((1,), (1,)), ((), ())),
                                preferred_element_type=F32)
            s = s + bias_scr[kv]
            sink = sink_cols[kv]
            m = jnp.maximum(jnp.max(s, axis=-1, keepdims=True), sink)
            p = jnp.exp(s - m)
            den = jnp.sum(p, axis=-1, keepdims=True) + jnp.exp(sink - m)
            o = jnp.dot(p.astype(BF16), vall[:, c0:c0 + HEAD_DIM], preferred_element_type=F32) / den
            for g_ in range(GQA_GROUP):
                h = kv * GQA_GROUP + g_
                obuf[r0:r0 + l, h * HEAD_DIM:(h + 1) * HEAD_DIM] = o[g_ * l:(g_ + 1) * l, :]

    y = jnp.dot(obuf[...].astype(BF16), wo_ref[0], preferred_element_type=F32) + bo_ref[...]
    o_ref[...] = _layer_norm_rows(DEEPNORM_ALPHA * x + y, g_ref[...], b_ref[...])


def _attn_sample(x, cache_k, cache_v, wqkv, bqkv, wo, bo, g, b, rel_bias, sinks, *, layer, bb=16):
    bsz, l, d = x.shape
    assert bsz % bb == 0 and l == SUBLANES
    ti = np.arange(l)[:, None]
    ci = np.arange(SAMPLE_KEYS)[None, :]
    table = _t5_bucket_table(ti + WINDOW - ci)
    table = np.where(ci < WINDOW + l, table, -1).astype(np.int32)
    bucket = jnp.asarray(table)
    qkv_dim = wqkv.shape[-1]
    rows = bb * l
    smem = pl.BlockSpec(memory_space=pltpu.SMEM)
    est = (d * qkv_dim + Q_DIM * d) * 2 + 8 * bb * WINDOW * KV_DIM * 4 + 8 * rows * d * 4
    return pl.pallas_call(
        functools.partial(_attn_sample_kernel, bb=bb, l=l),
        grid=(bsz // bb,),
        in_specs=[
            pl.BlockSpec((rows, d), lambda i: (i, 0)),
            pl.BlockSpec((bb, WINDOW, KV_DIM), lambda i: (i, 0, 0)),
            pl.BlockSpec((bb, WINDOW, KV_DIM), lambda i: (i, 0, 0)),
            _layer_resident((d, qkv_dim), layer), _resident((1, qkv_dim)),
            _layer_resident((Q_DIM, d), layer), _resident((1, d)),
            _resident((1, d)), _resident((1, d)),
            _resident((l, SAMPLE_KEYS)),
            smem, smem,
        ],
        out_specs=[
            pl.BlockSpec((rows, d), lambda i: (i, 0)),
            pl.BlockSpec((bb, WINDOW, KV_DIM), lambda i: (i, 0, 0)),
            pl.BlockSpec((bb, WINDOW, KV_DIM), lambda i: (i, 0, 0)),
        ],
        out_shape=[
            jax.ShapeDtypeStruct((bsz * l, d), F32),
            jax.ShapeDtypeStruct((bsz, WINDOW, KV_DIM), F32),
            jax.ShapeDtypeStruct((bsz, WINDOW, KV_DIM), F32),
        ],
        scratch_shapes=[
            pltpu.VMEM((SAMPLE_KEYS, KV_DIM), F32),
            pltpu.VMEM((SAMPLE_KEYS, KV_DIM), F32),
            pltpu.VMEM((rows, Q_DIM), F32),
            pltpu.VMEM((rows, Q_DIM), F32),
            pltpu.VMEM((N_KV_HEADS, GQA_GROUP * l, SAMPLE_KEYS), F32),
        ],
        compiler_params=pltpu.CompilerParams(
            dimension_semantics=("arbitrary",), vmem_limit_bytes=_vmem_limit(est)),
        name="attn_sample",
    )(x.reshape(bsz * l, d), cache_k, cache_v, wqkv, bqkv, wo, bo, g, b, bucket, rel_bias, sinks)


POOL_WINDOWS = (2, 4, 8, 16)
POOL_PAD = max(POOL_WINDOWS)
POOL_STATE_LEN = POOL_PAD - 1


def _pool_mix(window_sum, x, cnt_of, pw_ref, scale):
    gd = x.shape[-1] // len(POOL_WINDOWS)
    outs = []
    for g, w in enumerate(POOL_WINDOWS):
        diff = window_sum(g, w) / cnt_of(w) - x[:, g * gd:(g + 1) * gd]
        outs.append(jnp.dot(diff.astype(BF16), pw_ref[g], preferred_element_type=F32))
    return jnp.concatenate(outs, axis=-1) * scale


def _pool_prompt_kernel(x_ref, pw_ref, sc_ref, g_ref, b_ref, o_ref, st_ref, buf, *, tm):
    j = pl.program_id(1)
    nj = pl.num_programs(1)
    d = x_ref.shape[-1]
    gd = d // len(POOL_WINDOWS)

    @pl.when(j == 0)
    def _no_past():
        buf[0:POOL_PAD, :] = jnp.zeros((POOL_PAD, d), F32)

    x = x_ref[0]
    buf[POOL_PAD:POOL_PAD + tm, :] = x

    def window_sum(g, w):
        acc = x[:, g * gd:(g + 1) * gd]
        for s in range(1, w):
            acc = acc + buf[POOL_PAD - s:POOL_PAD - s + tm, g * gd:(g + 1) * gd]
        return acc

    pos1 = j * tm + lax.broadcasted_iota(jnp.int32, (tm, 1), 0) + 1

    def cnt_of(w):
        return jnp.minimum(pos1, w).astype(F32)

    y = _pool_mix(window_sum, x, cnt_of, pw_ref, sc_ref[...])
    o_ref[0] = _layer_norm_rows(DEEPNORM_ALPHA * x + y, g_ref[...], b_ref[...])

    @pl.when(j == nj - 1)
    def _emit_state():
        st_ref[0] = buf[tm + 1:tm + POOL_PAD, :]

    buf[0:POOL_PAD, :] = buf[tm:tm + POOL_PAD, :]


def _pool_prompt(x, pw, scale, g, b, *, tm=512):
    bsz, l, d = x.shape
    tm = min(tm, l)
    assert l % tm == 0 and tm >= POOL_PAD
    ng, gd = pw.shape[0], pw.shape[1]
    est = 6 * tm * d * 4 + ng * gd * gd * 2
    return pl.pallas_call(
        functools.partial(_pool_prompt_kernel, tm=tm),
        grid=(bsz, l // tm),
        in_specs=[
            pl.BlockSpec((1, tm, d), lambda bi, j: (bi, j, 0)),
            _resident((ng, gd, gd)), _resident((1, d)), _resident((1, d)), _resident((1, d)),
        ],
        out_specs=[
            pl.BlockSpec((1, tm, d), lambda bi, j: (bi, j, 0)),
            pl.BlockSpec((1, POOL_STATE_LEN, d), lambda bi, j: (bi, 0, 0)),
        ],
        out_shape=[
            jax.ShapeDtypeStruct((bsz, l, d), F32),
            jax.ShapeDtypeStruct((bsz, POOL_STATE_LEN, d), F32),
        ],
        scratch_shapes=[pltpu.VMEM((POOL_PAD + tm, d), F32)],
        compiler_params=pltpu.CompilerParams(
            dimension_semantics=("arbitrary", "arbitrary"), vmem_limit_bytes=_vmem_limit(est)),
        name="pool_prompt",
    )(x, pw, scale, g, b)


def _pool_sample_kernel(x_ref, st_ref, pw_ref, sc_ref, g_ref, b_ref, o_ref, nst_ref, buf, *, bb, l):
    d = x_ref.shape[-1]
    gd = d // len(POOL_WINDOWS)
    x3 = x_ref[...]
    buf[:, 1:POOL_PAD, :] = st_ref[...]
    buf[:, POOL_PAD:POOL_PAD + l, :] = x3
    x = x3.reshape(bb * l, d)

    def window_sum(g, w):
        acc = x3[:, :, g * gd:(g + 1) * gd]
        for s in range(1, w):
            acc = acc + buf[:, POOL_PAD - s:POOL_PAD - s + l, g * gd:(g + 1) * gd]
        return acc.reshape(bb * l, gd)

    y = _pool_mix(window_sum, x, lambda w: float(w), pw_ref, sc_ref[...])
    o_ref[...] = _layer_norm_rows(DEEPNORM_ALPHA * x + y, g_ref[...], b_ref[...]).reshape(bb, l, d)
    nst_ref[...] = buf[:, l + 1:l + POOL_PAD, :]


def _pool_sample(x, state, pw, scale, g, b, *, start, bb=16):
    bsz, l, d = x.shape
    assert bsz % bb == 0 and l == SUBLANES and start + 1 >= POOL_PAD
    ng, gd = pw.shape[0], pw.shape[1]
    est = 8 * bb * (POOL_PAD + l) * d * 4 + ng * gd * gd * 2
    return pl.pallas_call(
        functools.partial(_pool_sample_kernel, bb=bb, l=l),
        grid=(bsz // bb,),
        in_specs=[
            pl.BlockSpec((bb, l, d), lambda i: (i, 0, 0)),
            pl.BlockSpec((bb, POOL_STATE_LEN, d), lambda i: (i, 0, 0)),
            _resident((ng, gd, gd)), _resident((1, d)), _resident((1, d)), _resident((1, d)),
        ],
        out_specs=[
            pl.BlockSpec((bb, l, d), lambda i: (i, 0, 0)),
            pl.BlockSpec((bb, POOL_STATE_LEN, d), lambda i: (i, 0, 0)),
        ],
        out_shape=[
            jax.ShapeDtypeStruct((bsz, l, d), F32),
            jax.ShapeDtypeStruct((bsz, POOL_STATE_LEN, d), F32),
        ],
        scratch_shapes=[pltpu.VMEM((bb, POOL_PAD + l, d), F32)],
        compiler_params=pltpu.CompilerParams(
            dimension_semantics=("arbitrary",), vmem_limit_bytes=_vmem_limit(est)),
        name="pool_sample",
    )(x, state, pw, scale, g, b)


D_INNER = 2048
SSM_HEAD_DIM = 64
SSM_HEADS = D_INNER // SSM_HEAD_DIM
SSM_GROUPS = 4
SSM_HPG = SSM_HEADS // SSM_GROUPS
D_STATE = 128
CONV_WIDTH = 4
GBN = SSM_GROUPS * D_STATE
CONV_DIM = D_INNER + 2 * GBN
GROUP_INNER = D_INNER // SSM_GROUPS
RMS_EPS = 1e-5
CONV_PAD = SUBLANES
SSM_ROWS = 128


def _split_bf16(v, parts=3):
    out = []
    r = v
    for _ in range(parts):
        p = r.astype(BF16)
        out.append(p)
        r = r - p.astype(F32)
    return out


def _expand_heads(v, e):
    return sum(jnp.dot(p, e, preferred_element_type=F32) for p in _split_bf16(v))


def _ssm_front(x, xp_buf, conv_state, win_ref, cw_ref, cb_ref, dtb_ref, alog_ref, e_ref, tril_ref, ybuf, *, nseq, q):
    rows = nseq * q
    zxd = jnp.dot(x.astype(BF16), win_ref[...], preferred_element_type=F32)
    z = zxd[:, :D_INNER]
    xbc_pre = zxd[:, D_INNER:D_INNER + CONV_DIM]
    dt_pre = zxd[:, D_INNER + CONV_DIM:]

    if conv_state is not None:
        xp_buf[:, CONV_PAD - (CONV_WIDTH - 1):CONV_PAD, :] = conv_state
    xp_buf[:, CONV_PAD:CONV_PAD + q, :] = xbc_pre.reshape(nseq, q, CONV_DIM)
    conv = cb_ref[...].reshape(1, 1, CONV_DIM)
    for jj in range(CONV_WIDTH):
        off = CONV_PAD - (CONV_WIDTH - 1) + jj
        conv = conv + xp_buf[:, off:off + q, :] * cw_ref[jj:jj + 1, :].reshape(1, 1, CONV_DIM)
    conv = conv.reshape(rows, CONV_DIM)
    xbc = conv * jax.nn.sigmoid(conv)
    new_conv = xp_buf[:, q + CONV_PAD - (CONV_WIDTH - 1):q + CONV_PAD, :]

    xs = xbc[:, :D_INNER]
    dtv = dt_pre + dtb_ref[...]
    dt = jnp.maximum(dtv, 0.0) + jnp.log1p(jnp.exp(-jnp.abs(dtv)))
    a = -jnp.exp(alog_ref[...])
    da = dt * a
    tril = tril_ref[...]
    acum = sum(jnp.dot(tril, p, preferred_element_type=F32) for p in _split_bf16(da))
    causal = tril > 0.5
    acum_t = acum.T
    dt_t = dt.T

    for g in range(SSM_GROUPS):
        bg = xbc[:, D_INNER + g * D_STATE:D_INNER + (g + 1) * D_STATE].astype(BF16)
        cg = xbc[:, D_INNER + GBN + g * D_STATE:D_INNER + GBN + (g + 1) * D_STATE].astype(BF16)
        cb = lax.dot_general(cg, bg, (((1,), (1,)), ((), ())), preferred_element_type=F32)
        for r in range(SSM_HPG):
            h = g * SSM_HPG + r
            seg = acum[:, h:h + 1] - acum_t[h:h + 1, :]
            lm = jnp.exp(jnp.where(causal, seg, NEG_INF))
            w = (cb * lm * dt_t[h:h + 1, :]).astype(BF16)
            xh = xs[:, h * SSM_HEAD_DIM:(h + 1) * SSM_HEAD_DIM].astype(BF16)
            ybuf[:, h * SSM_HEAD_DIM:(h + 1) * SSM_HEAD_DIM] = jnp.dot(w, xh, preferred_element_type=F32)

    a3 = acum.reshape(nseq, q, LANES)
    alast = jnp.broadcast_to(a3[:, q - 1:q, :], (nseq, q, LANES)).reshape(rows, LANES)
    e = e_ref[...]
    exp_a = _expand_heads(jnp.exp(acum), e)
    dd = _expand_heads(jnp.exp(alast - acum) * dt, e)
    return dict(z=z, xs=xs, xbc=xbc, exp_a=exp_a, xdd=(xs * dd).astype(BF16), new_conv=new_conv)


def _ssm_finish(x, y, z, nw_ref, wout_ref, g_ref, b_ref):
    y = y * (z * jax.nn.sigmoid(z))
    parts = []
    for g in range(SSM_GROUPS):
        yg = y[:, g * GROUP_INNER:(g + 1) * GROUP_INNER]
        parts.append(yg * lax.rsqrt(jnp.mean(yg * yg, axis=-1, keepdims=True) + RMS_EPS))
    y = jnp.concatenate(parts, axis=-1) * nw_ref[...]
    out = jnp.dot(y.astype(BF16), wout_ref[...], preferred_element_type=F32)
    return _layer_norm_rows(DEEPNORM_ALPHA * x + out, g_ref[...], b_ref[...])


def _ssm_prompt_kernel(x_ref, win_ref, cw_ref, cb_ref, dtb_ref, alog_ref, dsk_ref, nw_ref, wout_ref, e_ref,
                       tril_ref, g_ref, b_ref, o_ref, cout_ref, sout_ref, xp_buf, ht, ybuf):
    j = pl.program_id(1)
    nj = pl.num_programs(1)
    q = SSM_ROWS

    @pl.when(j == 0)
    def _no_past():
        xp_buf[:, 0:CONV_PAD, :] = jnp.zeros((1, CONV_PAD, CONV_DIM), F32)
        ht[...] = jnp.zeros(ht.shape, F32)

    x = x_ref[0]
    f = _ssm_front(x, xp_buf, None, win_ref, cw_ref, cb_ref, dtb_ref, alog_ref, e_ref, tril_ref, ybuf, nseq=1, q=q)
    xbc, exp_a, xdd = f["xbc"], f["exp_a"], f["xdd"]
    decay = exp_a[q - 1:q, :]
    y_parts = []
    for g in range(SSM_GROUPS):
        sl = slice(g * GROUP_INNER, (g + 1) * GROUP_INNER)
        bg = xbc[:, D_INNER + g * D_STATE:D_INNER + (g + 1) * D_STATE]
        cg = xbc[:, D_INNER + GBN + g * D_STATE:D_INNER + GBN + (g + 1) * D_STATE].astype(BF16)
        hg = ht[:, sl]
        y_parts.append(jnp.dot(cg, hg.astype(BF16), preferred_element_type=F32))
        ht[:, sl] = hg * decay[:, sl] + jnp.dot(bg.T.astype(BF16), xdd[:, sl], preferred_element_type=F32)
    y = ybuf[...] + exp_a * jnp.concatenate(y_parts, axis=-1) + dsk_ref[...] * f["xs"]
    o_ref[0] = _ssm_finish(x, y, f["z"], nw_ref, wout_ref, g_ref, b_ref)
    xp_buf[:, 0:CONV_PAD, :] = xp_buf[:, q:q + CONV_PAD, :]

    @pl.when(j == nj - 1)
    def _emit_state():
        cout_ref[...] = f["new_conv"]
        sout_ref[0] = ht[...].T


def _ssm_consts(nseq, q):
    rows = nseq * q
    r = np.arange(rows)
    tril = ((r[:, None] >= r[None, :]) & (r[:, None] // q == r[None, :] // q)).astype(np.float32)
    e = np.zeros((LANES, D_INNER), np.float32)
    e[np.arange(D_INNER) // SSM_HEAD_DIM, np.arange(D_INNER)] = 1.0
    return jnp.asarray(tril, BF16), jnp.asarray(e, BF16)


def _ssm_weight_specs(d, win_cols):
    return [
        _resident((d, win_cols)), _resident((CONV_WIDTH, CONV_DIM)), _resident((1, CONV_DIM)),
        _resident((1, LANES)), _resident((1, LANES)),
    ]


def _ssm_prompt(x, w, g, b):
    bsz, l, d = x.shape
    q = SSM_ROWS
    assert l % q == 0
    tril, e = _ssm_consts(1, q)
    win_cols = w["win"].shape[1]
    est = (d * win_cols + D_INNER * d) * 2 + LANES * D_INNER * 4 + 12 * q * D_INNER * 4 + 3 * q * win_cols * 4
    return pl.pallas_call(
        _ssm_prompt_kernel,
        grid=(bsz, l // q),
        in_specs=[pl.BlockSpec((1, q, d), lambda bi, j: (bi, j, 0))] + _ssm_weight_specs(d, win_cols) + [
            _resident((1, D_INNER)), _resident((1, D_INNER)), _resident((D_INNER, d)),
            _resident((LANES, D_INNER)), _resident((q, q)), _resident((1, d)), _resident((1, d)),
        ],
        out_specs=[
            pl.BlockSpec((1, q, d), lambda bi, j: (bi, j, 0)),
            pl.BlockSpec((1, CONV_WIDTH - 1, CONV_DIM), lambda bi, j: (bi, 0, 0)),
            pl.BlockSpec((1, D_INNER, D_STATE), lambda bi, j: (bi, 0, 0)),
        ],
        out_shape=[
            jax.ShapeDtypeStruct((bsz, l, d), F32),
            jax.ShapeDtypeStruct((bsz, CONV_WIDTH - 1, CONV_DIM), F32),
            jax.ShapeDtypeStruct((bsz, D_INNER, D_STATE), F32),
        ],
        scratch_shapes=[
            pltpu.VMEM((1, CONV_PAD + q, CONV_DIM), F32),
            pltpu.VMEM((D_STATE, D_INNER), F32),
            pltpu.VMEM((q, D_INNER), F32),
        ],
        compiler_params=pltpu.CompilerParams(
            dimension_semantics=("arbitrary", "arbitrary"), vmem_limit_bytes=_vmem_limit(est)),
        name="ssm_prompt",
    )(x, w["win"], w["cw"], w["cb"], w["dtb"], w["alog"], w["dsk"], w["nw"], w["wout"], e, tril, g, b)


def _ssm_sample_front_kernel(x_ref, cst_ref, win_ref, cw_ref, cb_ref, dtb_ref, alog_ref, dsk_ref, e_ref, tril_ref,
                             y_ref, z_ref, ea_ref, xdd_ref, c_ref, bt_ref, cout_ref, xp_buf, ybuf, *, nseq, q):
    x = x_ref[...]
    f = _ssm_front(x, xp_buf, cst_ref[...], win_ref, cw_ref, cb_ref, dtb_ref, alog_ref, e_ref, tril_ref, ybuf,
                   nseq=nseq, q=q)
    xbc = f["xbc"]
    y_ref[...] = ybuf[...] + dsk_ref[...] * f["xs"]
    z_ref[...] = f["z"]
    ea_ref[...] = f["exp_a"]
    xdd_ref[...] = f["xdd"]
    c_ref[...] = xbc[:, D_INNER + GBN:]
    bt_ref[...] = jnp.concatenate(
        [xbc[:, D_INNER + g * D_STATE:D_INNER + (g + 1) * D_STATE].T for g in range(SSM_GROUPS)], axis=0).astype(BF16)
    cout_ref[...] = f["new_conv"]


def _ssm_sample_state_kernel(st_ref, y_ref, ea_ref, c_ref, xdd_ref, bt_ref, o_ref, nst_ref, *, bb, q, nseq):
    ea = ea_ref[...]
    xdd = xdd_ref[...]
    first_seq = (pl.program_id(0) % (nseq // bb)) * bb
    lane_seq = lax.broadcasted_iota(jnp.int32, (D_STATE, nseq * q), 1) // q
    for i in range(bb):
        r0 = i * q
        ht = st_ref[i].T
        decay = ea[r0 + q - 1:r0 + q, :]
        y_parts = []
        new_parts = []
        for g in range(SSM_GROUPS):
            sl = slice(g * GROUP_INNER, (g + 1) * GROUP_INNER)
            cg = c_ref[r0:r0 + q, g * D_STATE:(g + 1) * D_STATE].astype(BF16)
            hg = ht[:, sl]
            y_parts.append(jnp.dot(cg, hg.astype(BF16), preferred_element_type=F32))
            btg = bt_ref[g * D_STATE:(g + 1) * D_STATE, :]
            btg = jnp.where(lane_seq == first_seq + i, btg, jnp.zeros_like(btg))
            new_parts.append(hg * decay[:, sl] + jnp.dot(btg, xdd[:, sl], preferred_element_type=F32))
        o_ref[r0:r0 + q, :] = y_ref[r0:r0 + q, :] + ea[r0:r0 + q, :] * jnp.concatenate(y_parts, axis=-1)
        nst_ref[i] = jnp.concatenate(new_parts, axis=-1).T


def _ssm_sample_finish_kernel(x_ref, y_ref, z_ref, nw_ref, wout_ref, g_ref, b_ref, o_ref):
    o_ref[...] = _ssm_finish(x_ref[...], y_ref[...], z_ref[...], nw_ref, wout_ref, g_ref, b_ref)


def _ssm_sample(x, conv_state, ssm_state, w, g, b, *, bb_state=4):
    bsz, q, d = x.shape
    t = bsz * q
    nseq = SSM_ROWS // q
    rows = SSM_ROWS
    assert q == SUBLANES and bsz % nseq == 0 and nseq % bb_state == 0
    tril, e = _ssm_consts(nseq, q)
    win_cols = w["win"].shape[1]
    x2 = x.reshape(t, d)
    row_spec = lambda c: pl.BlockSpec((rows, c), lambda i: (i, 0))
    est = d * win_cols * 2 + LANES * D_INNER * 4 + 16 * rows * D_INNER * 4 + 3 * rows * win_cols * 4
    ydiag, z, exp_a, xdd, cmat, bt, new_conv = pl.pallas_call(
        functools.partial(_ssm_sample_front_kernel, nseq=nseq, q=q),
        grid=(t // rows,),
        in_specs=[row_spec(d), pl.BlockSpec((nseq, CONV_WIDTH - 1, CONV_DIM), lambda i: (i, 0, 0))]
        + _ssm_weight_specs(d, win_cols) + [_resident((1, D_INNER)), _resident((LANES, D_INNER)), _resident((rows, rows))],
        out_specs=[row_spec(D_INNER), row_spec(D_INNER), row_spec(D_INNER), row_spec(D_INNER), row_spec(GBN),
                   pl.BlockSpec((GBN, rows), lambda i: (i, 0)),
                   pl.BlockSpec((nseq, CONV_WIDTH - 1, CONV_DIM), lambda i: (i, 0, 0))],
        out_shape=[
            jax.ShapeDtypeStruct((t, D_INNER), F32), jax.ShapeDtypeStruct((t, D_INNER), F32),
            jax.ShapeDtypeStruct((t, D_INNER), F32), jax.ShapeDtypeStruct((t, D_INNER), BF16),
            jax.ShapeDtypeStruct((t, GBN), F32),
            jax.ShapeDtypeStruct((t // rows * GBN, rows), BF16),
            jax.ShapeDtypeStruct((bsz, CONV_WIDTH - 1, CONV_DIM), F32),
        ],
        scratch_shapes=[pltpu.VMEM((nseq, CONV_PAD + q, CONV_DIM), F32), pltpu.VMEM((rows, D_INNER), F32)],
        compiler_params=pltpu.CompilerParams(
            dimension_semantics=("arbitrary",), vmem_limit_bytes=_vmem_limit(est)),
        name="ssm_sample_front",
    )(x2, conv_state, w["win"], w["cw"], w["cb"], w["dtb"], w["alog"], w["dsk"], e, tril)

    srows = bb_state * q
    srow_spec = lambda c: pl.BlockSpec((srows, c), lambda i: (i, 0))
    st_spec = pl.BlockSpec((bb_state, D_INNER, D_STATE), lambda i: (i, 0, 0))
    est = 4 * bb_state * D_INNER * D_STATE * 4 + 8 * D_INNER * D_STATE * 4
    per_block = nseq // bb_state
    y, new_state = pl.pallas_call(
        functools.partial(_ssm_sample_state_kernel, bb=bb_state, q=q, nseq=nseq),
        grid=(bsz // bb_state,),
        in_specs=[st_spec, srow_spec(D_INNER), srow_spec(D_INNER), srow_spec(GBN),
                  pl.BlockSpec((rows, D_INNER), lambda i: (i // per_block, 0)),
                  pl.BlockSpec((GBN, rows), lambda i: (i // per_block, 0))],
        out_specs=[srow_spec(D_INNER), st_spec],
        out_shape=[jax.ShapeDtypeStruct((t, D_INNER), F32),
                   jax.ShapeDtypeStruct((bsz, D_INNER, D_STATE), F32)],
        compiler_params=pltpu.CompilerParams(
            dimension_semantics=("arbitrary",), vmem_limit_bytes=_vmem_limit(est)),
        name="ssm_sample_state",
    )(ssm_state, ydiag, exp_a, cmat, xdd, bt)

    est = D_INNER * d * 2 + 8 * rows * D_INNER * 4
    out = pl.pallas_call(
        _ssm_sample_finish_kernel,
        grid=(t // rows,),
        in_specs=[row_spec(d), row_spec(D_INNER), row_spec(D_INNER),
                  _resident((1, D_INNER)), _resident((D_INNER, d)), _resident((1, d)), _resident((1, d))],
        out_specs=row_spec(d),
        out_shape=jax.ShapeDtypeStruct((t, d), F32),
        compiler_params=pltpu.CompilerParams(
            dimension_semantics=("arbitrary",), vmem_limit_bytes=_vmem_limit(est)),
        name="ssm_sample_finish",
    )(x2, y, z, w["nw"], w["wout"], g, b)
    return out.reshape(bsz, q, d), new_conv, new_state


def _ssm_weights(w_in, conv_w, conv_b, dt_bias, a_log, d_skip, norm_w, w_out):
    d = w_in.shape[0]
    pad = LANES - SSM_HEADS
    win = jnp.concatenate([w_in, jnp.zeros((d, pad), w_in.dtype)], axis=1).astype(BF16)
    return dict(
        win=win, cw=conv_w, cb=conv_b.reshape(1, CONV_DIM),
        dtb=jnp.pad(dt_bias, (0, pad)).reshape(1, LANES),
        alog=jnp.pad(a_log, (0, pad)).reshape(1, LANES),
        dsk=jnp.repeat(d_skip, SSM_HEAD_DIM).reshape(1, D_INNER),
        nw=norm_w.reshape(1, D_INNER), wout=w_out.astype(BF16))


def kernel(x_prompt, x_sample, cache_k, cache_v, state_conv, state_ssm, state_pool, rel_bias, attn_w_qkv, attn_b_qkv, attn_w_o, attn_b_o, attn_sinks, ssm_w_in, ssm_conv_w, ssm_conv_b, ssm_dt_bias, ssm_a_log, ssm_d, ssm_norm_w, ssm_w_out, pool_w, pool_scale, ffn_w_gate, ffn_w_up, ffn_w_down, ln_g, ln_b):
    xp, xs = x_prompt, x_sample
    d = xp.shape[-1]
    wqkv, wo = attn_w_qkv.astype(BF16), attn_w_o.astype(BF16)
    wg, wu, wd = ffn_w_gate.astype(BF16), ffn_w_up.astype(BF16), ffn_w_down.astype(BF16)
    nk_p, nv_p, nc_p, nh_p, npool_p = [], [], [], [], []
    nk_s, nv_s, nc_s, nh_s, npool_s = [], [], [], [], []
    for i in range(DEPTH):
        j = i // N_MIXERS
        kind = i % N_MIXERS
        g1 = ln_g[i, 0].reshape(1, d)
        b1 = ln_b[i, 0].reshape(1, d)
        if kind == 0:
            bqkv = attn_b_qkv[j].reshape(1, -1)
            bo = attn_b_o[j].reshape(1, d)
            xp, kp, vp = _attn_prompt(xp, wqkv, bqkv, wo, bo, g1, b1, rel_bias, attn_sinks[j], layer=j)
            ck = cache_k[j].reshape(cache_k.shape[1], WINDOW, KV_DIM)
            cv = cache_v[j].reshape(cache_v.shape[1], WINDOW, KV_DIM)
            xs, ks_, vs_ = _attn_sample(xs, ck, cv, wqkv, bqkv, wo, bo, g1, b1, rel_bias, attn_sinks[j], layer=j)
            xs = xs.reshape(x_sample.shape)
            kv_shape = (-1, WINDOW, N_KV_HEADS, HEAD_DIM)
            nk_p.append(kp.reshape(kv_shape)); nv_p.append(vp.reshape(kv_shape))
            nk_s.append(ks_.reshape(kv_shape)); nv_s.append(vs_.reshape(kv_shape))
        elif kind == 1:
            w = _ssm_weights(ssm_w_in[j], ssm_conv_w[j], ssm_conv_b[j], ssm_dt_bias[j], ssm_a_log[j],
                             ssm_d[j], ssm_norm_w[j], ssm_w_out[j])
            xp, cp, hp = _ssm_prompt(xp, w, g1, b1)
            xs, cs_, hs_ = _ssm_sample(xs, state_conv[j], state_ssm[j].reshape(-1, D_INNER, D_STATE), w, g1, b1)
            st_shape = (-1, SSM_HEADS, SSM_HEAD_DIM, D_STATE)
            nc_p.append(cp); nh_p.append(hp.reshape(st_shape))
            nc_s.append(cs_); nh_s.append(hs_.reshape(st_shape))
        else:
            pw = pool_w[j].astype(BF16)
            psc = pool_scale[j].reshape(1, d)
            xp, pp = _pool_prompt(xp, pw, psc, g1, b1)
            xs, ps_ = _pool_sample(xs, state_pool[j], pw, psc, g1, b1, start=PAST_LEN)
            npool_p.append(pp); npool_s.append(ps_)
        g2 = ln_g[i, 1].reshape(1, d)
        b2 = ln_b[i, 1].reshape(1, d)
        xp = _ffn_ln(xp.reshape(-1, d), wg, wu, wd, g2, b2, layer=i).reshape(xp.shape)
        xs = _ffn_ln(xs.reshape(-1, d), wg, wu, wd, g2, b2, layer=i).reshape(xs.shape)
    return (xp, xs,
            jnp.stack(nk_p), jnp.stack(nv_p), jnp.stack(nc_p), jnp.stack(nh_p), jnp.stack(npool_p),
            jnp.stack(nk_s), jnp.stack(nv_s), jnp.stack(nc_s), jnp.stack(nh_s), jnp.stack(npool_s))
```

```python
import functools
import math

import jax
import jax.numpy as jnp
import numpy as np
from jax import lax
from jax.experimental import pallas as pl
from jax.experimental.pallas import tpu as pltpu

DEPTH = 4
N_MIXERS = 3
PAST_LEN = 8192
DEEPNORM_ALPHA = (2 * DEPTH) ** 0.25
LN_EPS = 1e-5

V7X_VMEM_BYTES = 64 * 1024 * 1024
LANES = 128
SUBLANES = 8
MXU_DIM = 256

BF16 = jnp.bfloat16
F32 = jnp.float32


def _vmem_limit(estimate_bytes):
    return int(min(V7X_VMEM_BYTES - 8 * 1024 * 1024, max(32 * 1024 * 1024, estimate_bytes * 3 // 2)))


def _layer_norm_rows(v, g, b):
    mu = jnp.mean(v, axis=-1, keepdims=True)
    d = v - mu
    var = jnp.mean(d * d, axis=-1, keepdims=True)
    return d * lax.rsqrt(var + LN_EPS) * g + b


def _resident(shape):
    nd = len(shape)
    return pl.BlockSpec(shape, lambda *_: (0,) * nd, pipeline_mode=pl.Buffered(1))


def _layer_resident(shape, layer):
    nd = len(shape)
    return pl.BlockSpec((1,) + tuple(shape), lambda *_: (layer,) + (0,) * nd, pipeline_mode=pl.Buffered(1))


def _ffn_chunks(d_ff):
    step = 2 * MXU_DIM
    edges = list(range(0, d_ff, step)) + [d_ff]
    return tuple(zip(edges[:-1], edges[1:]))


def _ffn_kernel(x_ref, wg_ref, wu_ref, wd_ref, g_ref, b_ref, o_ref, *, chunks):
    x = x_ref[...]
    xb = x.astype(BF16)
    acc = jnp.zeros(x.shape, F32)
    for c0, c1 in chunks:
        gate = jnp.dot(xb, wg_ref[0, :, c0:c1], preferred_element_type=F32)
        up = jnp.dot(xb, wu_ref[0, :, c0:c1], preferred_element_type=F32)
        h = (gate * jax.nn.sigmoid(gate)) * up
        acc = acc + jnp.dot(h.astype(BF16), wd_ref[0, c0:c1, :], preferred_element_type=F32)
    o_ref[...] = _layer_norm_rows(DEEPNORM_ALPHA * x + acc, g_ref[...], b_ref[...])


def _ffn_ln(x, wg, wu, wd, g, b, *, layer, tm=512):
    t, d = x.shape
    d_ff = wg.shape[-1]
    tm = min(tm, t)
    assert t % tm == 0
    est = 3 * d * d_ff * 2 + 4 * tm * d * 4 + 4 * tm * d_ff * 4
    return pl.pallas_call(
        functools.partial(_ffn_kernel, chunks=_ffn_chunks(d_ff)),
        grid=(t // tm,),
        in_specs=[
            pl.BlockSpec((tm, d), lambda i: (i, 0)),
            _layer_resident((d, d_ff), layer),
            _layer_resident((d, d_ff), layer),
            _layer_resident((d_ff, d), layer),
            _resident((1, d)),
            _resident((1, d)),
        ],
        out_specs=pl.BlockSpec((tm, d), lambda i: (i, 0)),
        out_shape=jax.ShapeDtypeStruct((t, d), F32),
        compiler_params=pltpu.CompilerParams(
            dimension_semantics=("arbitrary",), vmem_limit_bytes=_vmem_limit(est)),
        name="ffn_ln",
    )(x, wg, wu, wd, g, b)


HEAD_DIM = 64
N_HEADS = 16
N_KV_HEADS = 4
GQA_GROUP = N_HEADS // N_KV_HEADS
WINDOW = 128
REL_BUCKETS = 32
REL_MAX_DIST = 128
Q_DIM = N_HEADS * HEAD_DIM
KV_DIM = N_KV_HEADS * HEAD_DIM
NEG_INF = float("-inf")
CHAIN_BATCH = 2


def _t5_bucket_table(dist):
    n = np.maximum(dist, 0)
    max_exact = REL_BUCKETS // 2
    nf = np.maximum(n, 1).astype(np.float32)
    large = max_exact + (np.log(nf / np.float32(max_exact)) / np.float32(math.log(REL_MAX_DIST / max_exact))
                         * np.float32(REL_BUCKETS - max_exact)).astype(np.int32)
    large = np.minimum(large, REL_BUCKETS - 1)
    bucket = np.where(n < max_exact, n, large)
    valid = (dist >= 0) & (dist < WINDOW)
    return np.where(valid, bucket, -1).astype(np.int32)


def _bias_from_buckets(bucket, relb_ref, head):
    acc = jnp.full(bucket.shape, NEG_INF, F32)
    for bkt in range(REL_BUCKETS):
        acc = jnp.where(bucket == bkt, relb_ref[bkt, head], acc)
    return acc


def _attn_prompt_kernel(x_ref, wqkv_ref, bqkv_ref, wo_ref, bo_ref, g_ref, b_ref, bucket_ref,
                        relb_ref, sink_ref, o_ref, kout_ref, vout_ref,
                        ka_lo, ka_hi, kb_lo, kb_hi, vt, ot, bias_scr, sink_scr, *, tq):
    bi = pl.program_id(0)
    j = pl.program_id(1)
    nj = pl.num_programs(1)
    blk = WINDOW
    half_heads = GQA_GROUP // 2
    kbufs = (ka_lo, ka_hi, kb_lo, kb_hi)

    @pl.when((bi == 0) & (j == 0))
    def _build_tables():
        bucket = bucket_ref[...]
        lane = lax.broadcasted_iota(jnp.int32, (1, 2 * blk), 1)
        for kv in range(N_KV_HEADS):
            for half in range(2):
                ha = kv * GQA_GROUP + half
                hb = ha + half_heads
                pair = kv * 2 + half
                bias_scr[pair, :, 0:blk] = _bias_from_buckets(bucket, relb_ref, ha)
                bias_scr[pair, :, blk:2 * blk] = _bias_from_buckets(bucket, relb_ref, hb)
                sink_scr[pair] = jnp.where(lane < blk, sink_ref[ha], sink_ref[hb])

    @pl.when(j == 0)
    def _no_past():
        for buf in kbufs:
            buf[0:blk, :] = jnp.zeros((blk, KV_DIM), BF16)
        vt[:, 0:blk] = jnp.zeros((KV_DIM, blk), BF16)

    @pl.when(j > 0)
    def _carry():
        for buf in kbufs:
            buf[0:blk, :] = buf[tq:tq + blk, :]
        vt[:, 0:blk] = vt[:, tq:tq + blk]

    x = x_ref[0]
    qkv = jnp.dot(x.astype(BF16), wqkv_ref[0], preferred_element_type=F32) + bqkv_ref[...]
    q = (qkv[:, :Q_DIM] * (HEAD_DIM ** -0.5)).astype(BF16)
    k = qkv[:, Q_DIM:Q_DIM + KV_DIM]
    v = qkv[:, Q_DIM + KV_DIM:]
    k_sw = jnp.concatenate(
        [pltpu.roll(k[:, c * LANES:(c + 1) * LANES], HEAD_DIM, axis=1) for c in range(KV_DIM // LANES)], axis=1)
    lo = (lax.broadcasted_iota(jnp.int32, (1, KV_DIM), 1) % LANES) < HEAD_DIM
    ka_lo[blk:blk + tq, :] = jnp.where(lo, k, 0.0).astype(BF16)
    ka_hi[blk:blk + tq, :] = jnp.where(lo, 0.0, k).astype(BF16)
    kb_lo[blk:blk + tq, :] = jnp.where(lo, k_sw, 0.0).astype(BF16)
    kb_hi[blk:blk + tq, :] = jnp.where(lo, 0.0, k_sw).astype(BF16)
    vt[:, blk:blk + tq] = v.T.astype(BF16)

    @pl.when(j == nj - 1)
    def _emit_cache():
        kout_ref[0] = k[tq - WINDOW:, :]
        vout_ref[0] = v[tq - WINDOW:, :]

    ks = lax.broadcasted_iota(jnp.int32, (blk, 2 * blk), 0)
    qt = lax.broadcasted_iota(jnp.int32, (blk, 2 * blk), 1) % blk
    own = ks <= qt
    chains = [(i, kv, half) for i in range(tq // blk) for kv in range(N_KV_HEADS) for half in range(2)]

    def scores(i, kv, half):
        r0 = i * blk
        c0 = (kv // 2) * LANES
        in_lo = kv % 2 == 0
        if half == 0:
            kk = (ka_lo if in_lo else kb_lo)[r0:r0 + 2 * blk, c0:c0 + LANES]
        else:
            kk = (kb_hi if in_lo else ka_hi)[r0:r0 + 2 * blk, c0:c0 + LANES]
        qa = q[r0:r0 + blk, (2 * kv) * LANES:(2 * kv + 1) * LANES]
        qb = q[r0:r0 + blk, (2 * kv + 1) * LANES:(2 * kv + 2) * LANES]
        q_pair = jnp.concatenate([qa, qb], axis=0)
        return lax.dot_general(kk, q_pair, (((1,), (1,)), ((), ())), preferred_element_type=F32)

    def fold(i, kv, half, s2):
        s_prev = s2[0:blk, :]
        if i == 0:
            s_prev = jnp.where(j == 0, NEG_INF, s_prev)
        return jnp.where(own, s2[blk:2 * blk, :], s_prev) + bias_scr[kv * 2 + half]

    def softmax_batch(batch, s2s):
        sinks = [sink_scr[kv * 2 + half] for _, kv, half in batch]
        ss = [fold(*ch, s2) for ch, s2 in zip(batch, s2s)]
        ms = [jnp.maximum(jnp.max(s, axis=0, keepdims=True), sink) for s, sink in zip(ss, sinks)]
        ps = [jnp.exp(s - m) for s, m in zip(ss, ms)]
        invs = [1.0 / (jnp.sum(p, axis=0, keepdims=True) + jnp.exp(sink - m)) for p, m, sink in zip(ps, ms, sinks)]
        p2s = [jnp.concatenate([jnp.where(own, 0.0, p), jnp.where(own, p, 0.0)], axis=0).astype(BF16) for p in ps]
        return list(zip(p2s, invs))

    def weighted_values(i, kv, half, p2, inv):
        r0 = i * blk
        ha = kv * GQA_GROUP + half
        hb = ha + half_heads
        v_t = vt[kv * HEAD_DIM:(kv + 1) * HEAD_DIM, r0:r0 + 2 * blk]
        o_t = jnp.dot(v_t, p2, preferred_element_type=F32) * inv
        ot[ha * HEAD_DIM:(ha + 1) * HEAD_DIM, r0:r0 + blk] = o_t[:, 0:blk]
        ot[hb * HEAD_DIM:(hb + 1) * HEAD_DIM, r0:r0 + blk] = o_t[:, blk:2 * blk]

    batches = [chains[c:c + CHAIN_BATCH] for c in range(0, len(chains), CHAIN_BATCH)]
    s_next = [scores(*ch) for ch in batches[0]]
    prev = []
    for bi, batch in enumerate(batches):
        s_cur = s_next
        if bi + 1 < len(batches):
            s_next = [scores(*ch) for ch in batches[bi + 1]]
        probs = softmax_batch(batch, s_cur)
        for ch, pr in prev:
            weighted_values(*ch, *pr)
        prev = list(zip(batch, probs))
    for ch, pr in prev:
        weighted_values(*ch, *pr)

    o = ot[...].T.astype(BF16)
    y = jnp.dot(o, wo_ref[0], preferred_element_type=F32) + bo_ref[...]
    o_ref[0] = _layer_norm_rows(DEEPNORM_ALPHA * x + y, g_ref[...], b_ref[...])


def _attn_prompt(x, wqkv, bqkv, wo, bo, g, b, rel_bias, sinks, *, layer, tq=512):
    bsz, l, d = x.shape
    tq = min(tq, l)
    assert l % tq == 0 and tq % WINDOW == 0
    qi = np.arange(WINDOW)[None, :]
    si = np.arange(WINDOW)[:, None]
    bucket = jnp.asarray(_t5_bucket_table(np.where(si <= qi, qi - si, qi + WINDOW - si)))
    qkv_dim = wqkv.shape[-1]
    smem = pl.BlockSpec(memory_space=pltpu.SMEM)
    est = (d * qkv_dim + Q_DIM * d) * 2 + 5 * tq * d * 4 + N_HEADS * WINDOW * 2 * WINDOW * 4 + 3 * tq * qkv_dim * 4
    return pl.pallas_call(
        functools.partial(_attn_prompt_kernel, tq=tq),
        grid=(bsz, l // tq),
        in_specs=[
            pl.BlockSpec((1, tq, d), lambda bi, j: (bi, j, 0)),
            _layer_resident((d, qkv_dim), layer), _resident((1, qkv_dim)),
            _layer_resident((Q_DIM, d), layer), _resident((1, d)),
            _resident((1, d)), _resident((1, d)),
            _resident((WINDOW, WINDOW)),
            smem, smem,
        ],
        out_specs=[
            pl.BlockSpec((1, tq, d), lambda bi, j: (bi, j, 0)),
            pl.BlockSpec((1, WINDOW, KV_DIM), lambda bi, j: (bi, 0, 0)),
            pl.BlockSpec((1, WINDOW, KV_DIM), lambda bi, j: (bi, 0, 0)),
        ],
        out_shape=[
            jax.ShapeDtypeStruct((bsz, l, d), F32),
            jax.ShapeDtypeStruct((bsz, WINDOW, KV_DIM), F32),
            jax.ShapeDtypeStruct((bsz, WINDOW, KV_DIM), F32),
        ],
        scratch_shapes=[pltpu.VMEM((WINDOW + tq, KV_DIM), BF16)] * 4 + [
            pltpu.VMEM((KV_DIM, WINDOW + tq), BF16),
            pltpu.VMEM((Q_DIM, tq), F32),
            pltpu.VMEM((N_HEADS // 2, WINDOW, 2 * WINDOW), F32),
            pltpu.VMEM((N_HEADS // 2, 1, 2 * WINDOW), F32),
        ],
        compiler_params=pltpu.CompilerParams(
            dimension_semantics=("arbitrary", "arbitrary"), vmem_limit_bytes=_vmem_limit(est)),
        name="attn_prompt",
    )(x, wqkv, bqkv, wo, bo, g, b, bucket, rel_bias, sinks)


SAMPLE_KEYS = 2 * WINDOW
SAMPLE_CHAIN_BATCH = 8


def _attn_sample_kernel(x_ref, ck_ref, cv_ref, wqkv_ref, bqkv_ref, wo_ref, bo_ref, g_ref, b_ref, bucket_ref,
                        relb_ref, sink_ref, o_ref, nk_ref, nv_ref,
                        qbuf, obuf, bias_scr, *, bb, l):
    @pl.when(pl.program_id(0) == 0)
    def _init():
        bucket = bucket_ref[...]
        for h in range(N_HEADS):
            kv, g_ = divmod(h, GQA_GROUP)
            bias_scr[kv, g_ * l:(g_ + 1) * l, :] = _bias_from_buckets(bucket, relb_ref, h)

    x = x_ref[...]
    qkv = jnp.dot(x.astype(BF16), wqkv_ref[0], preferred_element_type=F32) + bqkv_ref[...]
    qbuf[...] = qkv[:, :Q_DIM] * (HEAD_DIM ** -0.5)
    k_new = qkv[:, Q_DIM:Q_DIM + KV_DIM]
    v_new = qkv[:, Q_DIM + KV_DIM:]

    sink_cols = []
    for kv in range(N_KV_HEADS):
        sink_cols.append(jnp.concatenate(
            [jnp.full((l, 1), sink_ref[kv * GQA_GROUP + g_], F32) for g_ in range(GQA_GROUP)], axis=0))

    pad = jnp.zeros((SAMPLE_KEYS - WINDOW - l, KV_DIM), F32)

    def keys_values(i):
        r0 = i * l
        ck = ck_ref[i]
        cv = cv_ref[i]
        kn = k_new[r0:r0 + l, :]
        vn = v_new[r0:r0 + l, :]
        nk_ref[i, 0:WINDOW - l, :] = ck[l:, :]
        nk_ref[i, WINDOW - l:WINDOW, :] = kn
        nv_ref[i, 0:WINDOW - l, :] = cv[l:, :]
        nv_ref[i, WINDOW - l:WINDOW, :] = vn
        kall = jnp.concatenate([ck, kn, pad], axis=0).astype(BF16)
        vall = jnp.concatenate([cv, vn, pad], axis=0).astype(BF16)
        return kall, vall

    def scores(i, kv, kall):
        r0 = i * l
        c0 = kv * HEAD_DIM
        qs = jnp.concatenate(
            [qbuf[r0:r0 + l, (kv * GQA_GROUP + g_) * HEAD_DIM:(kv * GQA_GROUP + g_ + 1) * HEAD_DIM]
             for g_ in range(GQA_GROUP)], axis=0).astype(BF16)
        return lax.dot_general(qs, kall[:, c0:c0 + HEAD_DIM], (((1,), (1,)), ((), ())),
                               preferred_element_type=F32)

    for i0 in range(0, bb, SAMPLE_CHAIN_BATCH):
        items = range(i0, min(bb, i0 + SAMPLE_CHAIN_BATCH))
        kvs = {i: keys_values(i) for i in items}
        chains = [(i, kv) for i in items for kv in range(N_KV_HEADS)]
        s_all = [scores(i, kv, kvs[i][0]) + bias_scr[kv] for i, kv in chains]
        m_all = [jnp.maximum(jnp.max(s, axis=-1, keepdims=True), sink_cols[kv]) for s, (i, kv) in zip(s_all, chains)]
        p_all = [jnp.exp(s - m) for s, m in zip(s_all, m_all)]
        den_all = [jnp.sum(p, axis=-1, keepdims=True) + jnp.exp(sink_cols[kv] - m)
                   for p, m, (i, kv) in zip(p_all, m_all, chains)]
        o_all = [jnp.dot(p.astype(BF16), kvs[i][1][:, kv * HEAD_DIM:(kv + 1) * HEAD_DIM], preferred_element_type=F32)
                 for p, (i, kv) in zip(p_all, chains)]
        for o, den, (i, kv) in zip(o_all, den_all, chains):
            o = o / den
            for g_ in range(GQA_GROUP):
                h = kv * GQA_GROUP + g_
                obuf[i * l:(i + 1) * l, h * HEAD_DIM:(h + 1) * HEAD_DIM] = o[g_ * l:(g_ + 1) * l, :]

    y = jnp.dot(obuf[...].astype(BF16), wo_ref[0], preferred_element_type=F32) + bo_ref[...]
    o_ref[...] = _layer_norm_rows(DEEPNORM_ALPHA * x + y, g_ref[...], b_ref[...])


def _attn_sample(x, cache_k, cache_v, wqkv, bqkv, wo, bo, g, b, rel_bias, sinks, *, layer, bb=16):
    bsz, l, d = x.shape
    assert bsz % bb == 0 and l == SUBLANES
    ti = np.arange(l)[:, None]
    ci = np.arange(SAMPLE_KEYS)[None, :]
    table = _t5_bucket_table(ti + WINDOW - ci)
    table = np.where(ci < WINDOW + l, table, -1).astype(np.int32)
    bucket = jnp.asarray(table)
    qkv_dim = wqkv.shape[-1]
    rows = bb * l
    smem = pl.BlockSpec(memory_space=pltpu.SMEM)
    est = (d * qkv_dim + Q_DIM * d) * 2 + 8 * bb * WINDOW * KV_DIM * 4 + 8 * rows * d * 4
    return pl.pallas_call(
        functools.partial(_attn_sample_kernel, bb=bb, l=l),
        grid=(bsz // bb,),
        in_specs=[
            pl.BlockSpec((rows, d), lambda i: (i, 0)),
            pl.BlockSpec((bb, WINDOW, KV_DIM), lambda i: (i, 0, 0)),
            pl.BlockSpec((bb, WINDOW, KV_DIM), lambda i: (i, 0, 0)),
            _layer_resident((d, qkv_dim), layer), _resident((1, qkv_dim)),
            _layer_resident((Q_DIM, d), layer), _resident((1, d)),
            _resident((1, d)), _resident((1, d)),
            _resident((l, SAMPLE_KEYS)),
            smem, smem,
        ],
        out_specs=[
            pl.BlockSpec((rows, d), lambda i: (i, 0)),
            pl.BlockSpec((bb, WINDOW, KV_DIM), lambda i: (i, 0, 0)),
            pl.BlockSpec((bb, WINDOW, KV_DIM), lambda i: (i, 0, 0)),
        ],
        out_shape=[
            jax.ShapeDtypeStruct((bsz * l, d), F32),
            jax.ShapeDtypeStruct((bsz, WINDOW, KV_DIM), F32),
            jax.ShapeDtypeStruct((bsz, WINDOW, KV_DIM), F32),
        ],
        scratch_shapes=[
            pltpu.VMEM((rows, Q_DIM), F32),
            pltpu.VMEM((rows, Q_DIM), F32),
            pltpu.VMEM((N_KV_HEADS, GQA_GROUP * l, SAMPLE_KEYS), F32),
        ],
        compiler_params=pltpu.CompilerParams(
            dimension_semantics=("arbitrary",), vmem_limit_bytes=_vmem_limit(est)),
        name="attn_sample",
    )(x.reshape(bsz * l, d), cache_k, cache_v, wqkv, bqkv, wo, bo, g, b, bucket, rel_bias, sinks)


POOL_WINDOWS = (2, 4, 8, 16)
POOL_PAD = max(POOL_WINDOWS)
POOL_STATE_LEN = POOL_PAD - 1


def _pool_mix(window_sum, x, cnt_of, pw_ref, scale):
    gd = x.shape[-1] // len(POOL_WINDOWS)
    outs = []
    for g, w in enumerate(POOL_WINDOWS):
        diff = window_sum(g, w) / cnt_of(w) - x[:, g * gd:(g + 1) * gd]
        outs.append(jnp.dot(diff.astype(BF16), pw_ref[g], preferred_element_type=F32))
    return jnp.concatenate(outs, axis=-1) * scale


def _pool_prompt_kernel(x_ref, pw_ref, sc_ref, g_ref, b_ref, o_ref, st_ref, buf, *, tm):
    j = pl.program_id(1)
    nj = pl.num_programs(1)
    d = x_ref.shape[-1]
    gd = d // len(POOL_WINDOWS)

    @pl.when(j == 0)
    def _no_past():
        buf[0:POOL_PAD, :] = jnp.zeros((POOL_PAD, d), F32)

    @pl.when(j > 0)
    def _carry():
        buf[0:POOL_PAD, :] = buf[tm:tm + POOL_PAD, :]

    x = x_ref[0]
    buf[POOL_PAD:POOL_PAD + tm, :] = x

    def window_sum(g, w):
        acc = x[:, g * gd:(g + 1) * gd]
        for s in range(1, w):
            acc = acc + buf[POOL_PAD - s:POOL_PAD - s + tm, g * gd:(g + 1) * gd]
        return acc

    pos1 = j * tm + lax.broadcasted_iota(jnp.int32, (tm, 1), 0) + 1

    def cnt_of(w):
        return jnp.minimum(pos1, w).astype(F32)

    y = _pool_mix(window_sum, x, cnt_of, pw_ref, sc_ref[...])
    o_ref[0] = _layer_norm_rows(DEEPNORM_ALPHA * x + y, g_ref[...], b_ref[...])

    @pl.when(j == nj - 1)
    def _emit_state():
        st_ref[0] = buf[tm + 1:tm + POOL_PAD, :]


def _pool_prompt(x, pw, scale, g, b, *, tm=512):
    bsz, l, d = x.shape
    tm = min(tm, l)
    assert l % tm == 0 and tm >= POOL_PAD
    ng, gd = pw.shape[0], pw.shape[1]
    est = 6 * tm * d * 4 + ng * gd * gd * 2
    return pl.pallas_call(
        functools.partial(_pool_prompt_kernel, tm=tm),
        grid=(bsz, l // tm),
        in_specs=[
            pl.BlockSpec((1, tm, d), lambda bi, j: (bi, j, 0)),
            _resident((ng, gd, gd)), _resident((1, d)), _resident((1, d)), _resident((1, d)),
        ],
        out_specs=[
            pl.BlockSpec((1, tm, d), lambda bi, j: (bi, j, 0)),
            pl.BlockSpec((1, POOL_STATE_LEN, d), lambda bi, j: (bi, 0, 0)),
        ],
        out_shape=[
            jax.ShapeDtypeStruct((bsz, l, d), F32),
            jax.ShapeDtypeStruct((bsz, POOL_STATE_LEN, d), F32),
        ],
        scratch_shapes=[pltpu.VMEM((POOL_PAD + tm, d), F32)],
        compiler_params=pltpu.CompilerParams(
            dimension_semantics=("arbitrary", "arbitrary"), vmem_limit_bytes=_vmem_limit(est)),
        name="pool_prompt",
    )(x, pw, scale, g, b)


def _pool_sample_kernel(x_ref, st_ref, pw_ref, sc_ref, g_ref, b_ref, o_ref, nst_ref, buf, *, bb, l):
    d = x_ref.shape[-1]
    gd = d // len(POOL_WINDOWS)
    x3 = x_ref[...]
    buf[:, 1:POOL_PAD, :] = st_ref[...]
    buf[:, POOL_PAD:POOL_PAD + l, :] = x3
    x = x3.reshape(bb * l, d)

    def window_sum(g, w):
        acc = x3[:, :, g * gd:(g + 1) * gd]
        for s in range(1, w):
            acc = acc + buf[:, POOL_PAD - s:POOL_PAD - s + l, g * gd:(g + 1) * gd]
        return acc.reshape(bb * l, gd)

    y = _pool_mix(window_sum, x, lambda w: float(w), pw_ref, sc_ref[...])
    o_ref[...] = _layer_norm_rows(DEEPNORM_ALPHA * x + y, g_ref[...], b_ref[...]).reshape(bb, l, d)
    nst_ref[...] = buf[:, l + 1:l + POOL_PAD, :]


def _pool_sample(x, state, pw, scale, g, b, *, start, bb=16):
    bsz, l, d = x.shape
    assert bsz % bb == 0 and l == SUBLANES and start + 1 >= POOL_PAD
    ng, gd = pw.shape[0], pw.shape[1]
    est = 8 * bb * (POOL_PAD + l) * d * 4 + ng * gd * gd * 2
    return pl.pallas_call(
        functools.partial(_pool_sample_kernel, bb=bb, l=l),
        grid=(bsz // bb,),
        in_specs=[
            pl.BlockSpec((bb, l, d), lambda i: (i, 0, 0)),
            pl.BlockSpec((bb, POOL_STATE_LEN, d), lambda i: (i, 0, 0)),
            _resident((ng, gd, gd)), _resident((1, d)), _resident((1, d)), _resident((1, d)),
        ],
        out_specs=[
            pl.BlockSpec((bb, l, d), lambda i: (i, 0, 0)),
            pl.BlockSpec((bb, POOL_STATE_LEN, d), lambda i: (i, 0, 0)),
        ],
        out_shape=[
            jax.ShapeDtypeStruct((bsz, l, d), F32),
            jax.ShapeDtypeStruct((bsz, POOL_STATE_LEN, d), F32),
        ],
        scratch_shapes=[pltpu.VMEM((bb, POOL_PAD + l, d), F32)],
        compiler_params=pltpu.CompilerParams(
            dimension_semantics=("arbitrary",), vmem_limit_bytes=_vmem_limit(est)),
        name="pool_sample",
    )(x, state, pw, scale, g, b)


D_INNER = 2048
SSM_HEAD_DIM = 64
SSM_HEADS = D_INNER // SSM_HEAD_DIM
SSM_GROUPS = 4
SSM_HPG = SSM_HEADS // SSM_GROUPS
D_STATE = 128
CONV_WIDTH = 4
GBN = SSM_GROUPS * D_STATE
CONV_DIM = D_INNER + 2 * GBN
GROUP_INNER = D_INNER // SSM_GROUPS
RMS_EPS = 1e-5
CONV_PAD = SUBLANES
SSM_ROWS = 128


def _split_bf16(v, parts=3):
    out = []
    r = v
    for _ in range(parts):
        p = r.astype(BF16)
        out.append(p)
        r = r - p.astype(F32)
    return out


def _expand_heads(v, e):
    return sum(jnp.dot(p, e, preferred_element_type=F32) for p in _split_bf16(v))


def _ssm_prep(xb, xp_buf, conv_state, win_ref, dtb_ref, alog_ref, tril_ref, *, nseq, q):
    rows = nseq * q
    xbc_pre = jnp.dot(xb, win_ref[:, D_INNER:D_INNER + CONV_DIM], preferred_element_type=F32)
    dt_pre = jnp.dot(xb, win_ref[:, D_INNER + CONV_DIM:], preferred_element_type=F32)
    if conv_state is not None:
        xp_buf[:, CONV_PAD - (CONV_WIDTH - 1):CONV_PAD, :] = conv_state
    xp_buf[:, CONV_PAD:CONV_PAD + q, :] = xbc_pre.reshape(nseq, q, CONV_DIM)
    new_conv = xp_buf[:, q + CONV_PAD - (CONV_WIDTH - 1):q + CONV_PAD, :]

    dtv = dt_pre + dtb_ref[...]
    dt = jnp.maximum(dtv, 0.0) + jnp.log1p(jnp.exp(-jnp.abs(dtv)))
    a = -jnp.exp(alog_ref[...])
    tril = tril_ref[...]
    acum = sum(jnp.dot(tril, p, preferred_element_type=F32) for p in _split_bf16(dt * a))
    a3 = acum.reshape(nseq, q, LANES)
    alast = jnp.broadcast_to(a3[:, q - 1:q, :], (nseq, q, LANES)).reshape(rows, LANES)
    return dict(acum=acum, acum_t=acum.T, dt_t=dt.T, causal=tril > 0.5,
                exp_a=jnp.exp(acum), dd=jnp.exp(alast - acum) * dt, new_conv=new_conv)


def _conv_silu(xp_buf, cw_ref, cb_ref, c0, c1, *, nseq, q):
    acc = cb_ref[:, c0:c1].reshape(1, 1, c1 - c0)
    for jj in range(CONV_WIDTH):
        off = CONV_PAD - (CONV_WIDTH - 1) + jj
        acc = acc + xp_buf[:, off:off + q, c0:c1] * cw_ref[jj:jj + 1, c0:c1].reshape(1, 1, c1 - c0)
    acc = acc.reshape(nseq * q, c1 - c0)
    return acc * jax.nn.sigmoid(acc)


def _ssm_group(g, xb, prep, xp_buf, win_ref, cw_ref, cb_ref, e_ref, ybuf, *, nseq, q):
    sl = slice(g * GROUP_INNER, (g + 1) * GROUP_INNER)
    z = jnp.dot(xb, win_ref[:, sl], preferred_element_type=F32)
    conv = functools.partial(_conv_silu, xp_buf, cw_ref, cb_ref, nseq=nseq, q=q)
    xs = conv(g * GROUP_INNER, (g + 1) * GROUP_INNER)
    bm = conv(D_INNER + g * D_STATE, D_INNER + (g + 1) * D_STATE)
    cm = conv(D_INNER + GBN + g * D_STATE, D_INNER + GBN + (g + 1) * D_STATE)
    cb = lax.dot_general(cm.astype(BF16), bm.astype(BF16), (((1,), (1,)), ((), ())), preferred_element_type=F32)
    acum, acum_t, dt_t, causal = prep["acum"], prep["acum_t"], prep["dt_t"], prep["causal"]
    for r in range(SSM_HPG):
        h = g * SSM_HPG + r
        seg = acum[:, h:h + 1] - acum_t[h:h + 1, :]
        lm = jnp.exp(jnp.where(causal, seg, NEG_INF))
        w = (cb * lm * dt_t[h:h + 1, :]).astype(BF16)
        xh = xs[:, r * SSM_HEAD_DIM:(r + 1) * SSM_HEAD_DIM].astype(BF16)
        ybuf[:, h * SSM_HEAD_DIM:(h + 1) * SSM_HEAD_DIM] = jnp.dot(w, xh, preferred_element_type=F32)
    e = e_ref[:, sl]
    exp_a = _expand_heads(prep["exp_a"], e)
    xdd = (xs * _expand_heads(prep["dd"], e)).astype(BF16)
    return dict(z=z, xs=xs, bm=bm, cm=cm, exp_a=exp_a, xdd=xdd)


def _gate_norm(y, z, nw):
    y = y * (z * jax.nn.sigmoid(z))
    return y * lax.rsqrt(jnp.mean(y * y, axis=-1, keepdims=True) + RMS_EPS) * nw


def _ssm_prompt_kernel(x_ref, win_ref, cw_ref, cb_ref, dtb_ref, alog_ref, dsk_ref, nw_ref, wout_ref, e_ref,
                       tril_ref, g_ref, b_ref, o_ref, cout_ref, sout_ref, xp_buf, ht, ybuf):
    j = pl.program_id(1)
    nj = pl.num_programs(1)
    q = SSM_ROWS

    @pl.when(j == 0)
    def _no_past():
        xp_buf[:, 0:CONV_PAD, :] = jnp.zeros((1, CONV_PAD, CONV_DIM), F32)
        ht[...] = jnp.zeros(ht.shape, F32)

    @pl.when(j > 0)
    def _carry():
        xp_buf[:, 0:CONV_PAD, :] = xp_buf[:, q:q + CONV_PAD, :]

    x = x_ref[0]
    xb = x.astype(BF16)
    prep = _ssm_prep(xb, xp_buf, None, win_ref, dtb_ref, alog_ref, tril_ref, nseq=1, q=q)
    out = jnp.zeros(x.shape, F32)
    for g in range(SSM_GROUPS):
        sl = slice(g * GROUP_INNER, (g + 1) * GROUP_INNER)
        f = _ssm_group(g, xb, prep, xp_buf, win_ref, cw_ref, cb_ref, e_ref, ybuf, nseq=1, q=q)
        exp_a = f["exp_a"]
        hg = ht[:, sl]
        y_off = jnp.dot(f["cm"].astype(BF16), hg.astype(BF16), preferred_element_type=F32)
        ht[:, sl] = hg * exp_a[q - 1:q, :] + jnp.dot(f["bm"].T.astype(BF16), f["xdd"], preferred_element_type=F32)
        y = ybuf[:, sl] + exp_a * y_off + dsk_ref[:, sl] * f["xs"]
        y = _gate_norm(y, f["z"], nw_ref[:, sl])
        out = out + jnp.dot(y.astype(BF16), wout_ref[sl, :], preferred_element_type=F32)
    o_ref[0] = _layer_norm_rows(DEEPNORM_ALPHA * x + out, g_ref[...], b_ref[...])

    @pl.when(j == nj - 1)
    def _emit_state():
        cout_ref[...] = prep["new_conv"]
        sout_ref[0] = ht[...].T


def _ssm_consts(nseq, q):
    rows = nseq * q
    r = np.arange(rows)
    tril = ((r[:, None] >= r[None, :]) & (r[:, None] // q == r[None, :] // q)).astype(np.float32)
    e = np.zeros((LANES, D_INNER), np.float32)
    e[np.arange(D_INNER) // SSM_HEAD_DIM, np.arange(D_INNER)] = 1.0
    return jnp.asarray(tril, BF16), jnp.asarray(e, BF16)


def _ssm_weight_specs(d, win_cols):
    return [
        _resident((d, win_cols)), _resident((CONV_WIDTH, CONV_DIM)), _resident((1, CONV_DIM)),
        _resident((1, LANES)), _resident((1, LANES)),
    ]


def _ssm_prompt(x, w, g, b):
    bsz, l, d = x.shape
    q = SSM_ROWS
    assert l % q == 0
    tril, e = _ssm_consts(1, q)
    win_cols = w["win"].shape[1]
    est = (d * win_cols + D_INNER * d) * 2 + LANES * D_INNER * 4 + 12 * q * D_INNER * 4 + 3 * q * win_cols * 4
    return pl.pallas_call(
        _ssm_prompt_kernel,
        grid=(bsz, l // q),
        in_specs=[pl.BlockSpec((1, q, d), lambda bi, j: (bi, j, 0))] + _ssm_weight_specs(d, win_cols) + [
            _resident((1, D_INNER)), _resident((1, D_INNER)), _resident((D_INNER, d)),
            _resident((LANES, D_INNER)), _resident((q, q)), _resident((1, d)), _resident((1, d)),
        ],
        out_specs=[
            pl.BlockSpec((1, q, d), lambda bi, j: (bi, j, 0)),
            pl.BlockSpec((1, CONV_WIDTH - 1, CONV_DIM), lambda bi, j: (bi, 0, 0)),
            pl.BlockSpec((1, D_INNER, D_STATE), lambda bi, j: (bi, 0, 0)),
        ],
        out_shape=[
            jax.ShapeDtypeStruct((bsz, l, d), F32),
            jax.ShapeDtypeStruct((bsz, CONV_WIDTH - 1, CONV_DIM), F32),
            jax.ShapeDtypeStruct((bsz, D_INNER, D_STATE), F32),
        ],
        scratch_shapes=[
            pltpu.VMEM((1, CONV_PAD + q, CONV_DIM), F32),
            pltpu.VMEM((D_STATE, D_INNER), F32),
            pltpu.VMEM((q, D_INNER), F32),
        ],
        compiler_params=pltpu.CompilerParams(
            dimension_semantics=("arbitrary", "arbitrary"), vmem_limit_bytes=_vmem_limit(est)),
        name="ssm_prompt",
    )(x, w["win"], w["cw"], w["cb"], w["dtb"], w["alog"], w["dsk"], w["nw"], w["wout"], e, tril, g, b)


def _ssm_sample_front_kernel(x_ref, cst_ref, win_ref, cw_ref, cb_ref, dtb_ref, alog_ref, dsk_ref, e_ref, tril_ref,
                             y_ref, z_ref, ea_ref, xdd_ref, c_ref, bt_ref, cout_ref, xp_buf, ybuf, *, nseq, q):
    xb = x_ref[...].astype(BF16)
    prep = _ssm_prep(xb, xp_buf, cst_ref[...], win_ref, dtb_ref, alog_ref, tril_ref, nseq=nseq, q=q)
    cout_ref[...] = prep["new_conv"]
    for g in range(SSM_GROUPS):
        sl = slice(g * GROUP_INNER, (g + 1) * GROUP_INNER)
        f = _ssm_group(g, xb, prep, xp_buf, win_ref, cw_ref, cb_ref, e_ref, ybuf, nseq=nseq, q=q)
        y_ref[:, sl] = ybuf[:, sl] + dsk_ref[:, sl] * f["xs"]
        z_ref[:, sl] = f["z"]
        ea_ref[:, sl] = f["exp_a"]
        xdd_ref[:, sl] = f["xdd"]
        c_ref[:, g * D_STATE:(g + 1) * D_STATE] = f["cm"]
        bt_ref[g * D_STATE:(g + 1) * D_STATE, :] = f["bm"].T.astype(BF16)


def _ssm_sample_state_kernel(st_ref, y_ref, ea_ref, c_ref, xdd_ref, bt_ref, o_ref, nst_ref, *, bb, q, nseq):
    ea = ea_ref[...]
    xdd = xdd_ref[...]
    first_seq = (pl.program_id(0) % (nseq // bb)) * bb
    lane_seq = lax.broadcasted_iota(jnp.int32, (D_STATE, nseq * q), 1) // q
    for i in range(bb):
        r0 = i * q
        ht = st_ref[i].T
        decay = ea[r0 + q - 1:r0 + q, :]
        y_parts = []
        new_parts = []
        for g in range(SSM_GROUPS):
            sl = slice(g * GROUP_INNER, (g + 1) * GROUP_INNER)
            cg = c_ref[r0:r0 + q, g * D_STATE:(g + 1) * D_STATE].astype(BF16)
            hg = ht[:, sl]
            y_parts.append(jnp.dot(cg, hg.astype(BF16), preferred_element_type=F32))
            btg = bt_ref[g * D_STATE:(g + 1) * D_STATE, :]
            btg = jnp.where(lane_seq == first_seq + i, btg, jnp.zeros_like(btg))
            new_parts.append(hg * decay[:, sl] + jnp.dot(btg, xdd[:, sl], preferred_element_type=F32))
        o_ref[r0:r0 + q, :] = y_ref[r0:r0 + q, :] + ea[r0:r0 + q, :] * jnp.concatenate(y_parts, axis=-1)
        nst_ref[i] = jnp.concatenate(new_parts, axis=-1).T


def _ssm_sample_finish_kernel(x_ref, y_ref, z_ref, nw_ref, wout_ref, g_ref, b_ref, o_ref):
    x = x_ref[...]
    out = jnp.zeros(x.shape, F32)
    for g in range(SSM_GROUPS):
        sl = slice(g * GROUP_INNER, (g + 1) * GROUP_INNER)
        y = _gate_norm(y_ref[:, sl], z_ref[:, sl], nw_ref[:, sl])
        out = out + jnp.dot(y.astype(BF16), wout_ref[sl, :], preferred_element_type=F32)
    o_ref[...] = _layer_norm_rows(DEEPNORM_ALPHA * x + out, g_ref[...], b_ref[...])


def _ssm_sample(x, conv_state, ssm_state, w, g, b, *, bb_state=4):
    bsz, q, d = x.shape
    t = bsz * q
    nseq = SSM_ROWS // q
    rows = SSM_ROWS
    assert q == SUBLANES and bsz % nseq == 0 and nseq % bb_state == 0
    tril, e = _ssm_consts(nseq, q)
    win_cols = w["win"].shape[1]
    x2 = x.reshape(t, d)
    row_spec = lambda c: pl.BlockSpec((rows, c), lambda i: (i, 0))
    est = d * win_cols * 2 + LANES * D_INNER * 4 + 16 * rows * D_INNER * 4 + 3 * rows * win_cols * 4
    ydiag, z, exp_a, xdd, cmat, bt, new_conv = pl.pallas_call(
        functools.partial(_ssm_sample_front_kernel, nseq=nseq, q=q),
        grid=(t // rows,),
        in_specs=[row_spec(d), pl.BlockSpec((nseq, CONV_WIDTH - 1, CONV_DIM), lambda i: (i, 0, 0))]
        + _ssm_weight_specs(d, win_cols) + [_resident((1, D_INNER)), _resident((LANES, D_INNER)), _resident((rows, rows))],
        out_specs=[row_spec(D_INNER), row_spec(D_INNER), row_spec(D_INNER), row_spec(D_INNER), row_spec(GBN),
                   pl.BlockSpec((GBN, rows), lambda i: (i, 0)),
                   pl.BlockSpec((nseq, CONV_WIDTH - 1, CONV_DIM), lambda i: (i, 0, 0))],
        out_shape=[
            jax.ShapeDtypeStruct((t, D_INNER), F32), jax.ShapeDtypeStruct((t, D_INNER), F32),
            jax.ShapeDtypeStruct((t, D_INNER), F32), jax.ShapeDtypeStruct((t, D_INNER), BF16),
            jax.ShapeDtypeStruct((t, GBN), F32),
            jax.ShapeDtypeStruct((t // rows * GBN, rows), BF16),
            jax.ShapeDtypeStruct((bsz, CONV_WIDTH - 1, CONV_DIM), F32),
        ],
        scratch_shapes=[pltpu.VMEM((nseq, CONV_PAD + q, CONV_DIM), F32), pltpu.VMEM((rows, D_INNER), F32)],
        compiler_params=pltpu.CompilerParams(
            dimension_semantics=("arbitrary",), vmem_limit_bytes=_vmem_limit(est)),
        name="ssm_sample_front",
    )(x2, conv_state, w["win"], w["cw"], w["cb"], w["dtb"], w["alog"], w["dsk"], e, tril)

    srows = bb_state * q
    srow_spec = lambda c: pl.BlockSpec((srows, c), lambda i: (i, 0))
    st_spec = pl.BlockSpec((bb_state, D_INNER, D_STATE), lambda i: (i, 0, 0))
    est = 4 * bb_state * D_INNER * D_STATE * 4 + 8 * D_INNER * D_STATE * 4
    per_block = nseq // bb_state
    y, new_state = pl.pallas_call(
        functools.partial(_ssm_sample_state_kernel, bb=bb_state, q=q, nseq=nseq),
        grid=(bsz // bb_state,),
        in_specs=[st_spec, srow_spec(D_INNER), srow_spec(D_INNER), srow_spec(GBN),
                  pl.BlockSpec((rows, D_INNER), lambda i: (i // per_block, 0)),
                  pl.BlockSpec((GBN, rows), lambda i: (i // per_block, 0))],
        out_specs=[srow_spec(D_INNER), st_spec],
        out_shape=[jax.ShapeDtypeStruct((t, D_INNER), F32),
                   jax.ShapeDtypeStruct((bsz, D_INNER, D_STATE), F32)],
        compiler_params=pltpu.CompilerParams(
            dimension_semantics=("arbitrary",), vmem_limit_bytes=_vmem_limit(est)),
        name="ssm_sample_state",
    )(ssm_state, ydiag, exp_a, cmat, xdd, bt)

    est = D_INNER * d * 2 + 8 * rows * D_INNER * 4
    out = pl.pallas_call(
        _ssm_sample_finish_kernel,
        grid=(t // rows,),
        in_specs=[row_spec(d), row_spec(D_INNER), row_spec(D_INNER),
                  _resident((1, D_INNER)), _resident((D_INNER, d)), _resident((1, d)), _resident((1, d))],
        out_specs=row_spec(d),
        out_shape=jax.ShapeDtypeStruct((t, d), F32),
        compiler_params=pltpu.CompilerParams(
            dimension_semantics=("arbitrary",), vmem_limit_bytes=_vmem_limit(est)),
        name="ssm_sample_finish",
    )(x2, y, z, w["nw"], w["wout"], g, b)
    return out.reshape(bsz, q, d), new_conv, new_state


def _ssm_weights(w_in, conv_w, conv_b, dt_bias, a_log, d_skip, norm_w, w_out):
    d = w_in.shape[0]
    pad = LANES - SSM_HEADS
    win = jnp.concatenate([w_in, jnp.zeros((d, pad), w_in.dtype)], axis=1).astype(BF16)
    return dict(
        win=win, cw=conv_w, cb=conv_b.reshape(1, CONV_DIM),
        dtb=jnp.pad(dt_bias, (0, pad)).reshape(1, LANES),
        alog=jnp.pad(a_log, (0, pad)).reshape(1, LANES),
        dsk=jnp.repeat(d_skip, SSM_HEAD_DIM).reshape(1, D_INNER),
        nw=norm_w.reshape(1, D_INNER), wout=w_out.astype(BF16))


def kernel(x_prompt, x_sample, cache_k, cache_v, state_conv, state_ssm, state_pool, rel_bias, attn_w_qkv, attn_b_qkv, attn_w_o, attn_b_o, attn_sinks, ssm_w_in, ssm_conv_w, ssm_conv_b, ssm_dt_bias, ssm_a_log, ssm_d, ssm_norm_w, ssm_w_out, pool_w, pool_scale, ffn_w_gate, ffn_w_up, ffn_w_down, ln_g, ln_b):
    xp, xs = x_prompt, x_sample
    d = xp.shape[-1]
    wqkv, wo = attn_w_qkv.astype(BF16), attn_w_o.astype(BF16)
    wg, wu, wd = ffn_w_gate.astype(BF16), ffn_w_up.astype(BF16), ffn_w_down.astype(BF16)
    nk_p, nv_p, nc_p, nh_p, npool_p = [], [], [], [], []
    nk_s, nv_s, nc_s, nh_s, npool_s = [], [], [], [], []
    for i in range(DEPTH):
        j = i // N_MIXERS
        kind = i % N_MIXERS
        g1 = ln_g[i, 0].reshape(1, d)
        b1 = ln_b[i, 0].reshape(1, d)
        if kind == 0:
            bqkv = attn_b_qkv[j].reshape(1, -1)
            bo = attn_b_o[j].reshape(1, d)
            xp, kp, vp = _attn_prompt(xp, wqkv, bqkv, wo, bo, g1, b1, rel_bias, attn_sinks[j], layer=j)
            ck = cache_k[j].reshape(cache_k.shape[1], WINDOW, KV_DIM)
            cv = cache_v[j].reshape(cache_v.shape[1], WINDOW, KV_DIM)
            xs, ks_, vs_ = _attn_sample(xs, ck, cv, wqkv, bqkv, wo, bo, g1, b1, rel_bias, attn_sinks[j], layer=j)
            xs = xs.reshape(x_sample.shape)
            kv_shape = (-1, WINDOW, N_KV_HEADS, HEAD_DIM)
            nk_p.append(kp.reshape(kv_shape)); nv_p.append(vp.reshape(kv_shape))
            nk_s.append(ks_.reshape(kv_shape)); nv_s.append(vs_.reshape(kv_shape))
        elif kind == 1:
            w = _ssm_weights(ssm_w_in[j], ssm_conv_w[j], ssm_conv_b[j], ssm_dt_bias[j], ssm_a_log[j],
                             ssm_d[j], ssm_norm_w[j], ssm_w_out[j])
            xp, cp, hp = _ssm_prompt(xp, w, g1, b1)
            xs, cs_, hs_ = _ssm_sample(xs, state_conv[j], state_ssm[j].reshape(-1, D_INNER, D_STATE), w, g1, b1)
            st_shape = (-1, SSM_HEADS, SSM_HEAD_DIM, D_STATE)
            nc_p.append(cp); nh_p.append(hp.reshape(st_shape))
            nc_s.append(cs_); nh_s.append(hs_.reshape(st_shape))
        else:
            pw = pool_w[j].astype(BF16)
            psc = pool_scale[j].reshape(1, d)
            xp, pp = _pool_prompt(xp, pw, psc, g1, b1)
            xs, ps_ = _pool_sample(xs, state_pool[j], pw, psc, g1, b1, start=PAST_LEN)
            npool_p.append(pp); npool_s.append(ps_)
        g2 = ln_g[i, 1].reshape(1, d)
        b2 = ln_b[i, 1].reshape(1, d)
        xp = _ffn_ln(xp.reshape(-1, d), wg, wu, wd, g2, b2, layer=i).reshape(xp.shape)
        xs = _ffn_ln(xs.reshape(-1, d), wg, wu, wd, g2, b2, layer=i).reshape(xs.shape)
    return (xp, xs,
            jnp.stack(nk_p), jnp.stack(nv_p), jnp.stack(nc_p), jnp.stack(nh_p), jnp.stack(npool_p),
            jnp.stack(nk_s), jnp.stack(nv_s), jnp.stack(nc_s), jnp.stack(nh_s), jnp.stack(npool_s))
```

```python
import functools
import math

import jax
import jax.numpy as jnp
import numpy as np
from jax import lax
from jax.experimental import pallas as pl
from jax.experimental.pallas import tpu as pltpu

DEPTH = 4
N_MIXERS = 3
PAST_LEN = 8192
DEEPNORM_ALPHA = (2 * DEPTH) ** 0.25
LN_EPS = 1e-5

V7X_VMEM_BYTES = 64 * 1024 * 1024
LANES = 128
SUBLANES = 8
MXU_DIM = 256

BF16 = jnp.bfloat16
F32 = jnp.float32


def _vmem_limit(estimate_bytes):
    return int(min(V7X_VMEM_BYTES - 8 * 1024 * 1024, max(32 * 1024 * 1024, estimate_bytes * 3 // 2)))


def _layer_norm_rows(v, g, b):
    mu = jnp.mean(v, axis=-1, keepdims=True)
    d = v - mu
    var = jnp.mean(d * d, axis=-1, keepdims=True)
    return d * lax.rsqrt(var + LN_EPS) * g + b


def _resident(shape):
    nd = len(shape)
    return pl.BlockSpec(shape, lambda *_: (0,) * nd, pipeline_mode=pl.Buffered(1))


def _layer_resident(shape, layer):
    nd = len(shape)
    return pl.BlockSpec((1,) + tuple(shape), lambda *_: (layer,) + (0,) * nd, pipeline_mode=pl.Buffered(1))


def _ffn_chunks(d_ff):
    step = 2 * MXU_DIM
    edges = list(range(0, d_ff, step)) + [d_ff]
    return tuple(zip(edges[:-1], edges[1:]))


def _ffn_kernel(x_ref, wg_ref, wu_ref, wd_ref, g_ref, b_ref, o_ref, *, chunks):
    x = x_ref[...]
    xb = x.astype(BF16)
    acc = jnp.zeros(x.shape, F32)
    for c0, c1 in chunks:
        gate = jnp.dot(xb, wg_ref[0, :, c0:c1], preferred_element_type=F32)
        up = jnp.dot(xb, wu_ref[0, :, c0:c1], preferred_element_type=F32)
        h = (gate * jax.nn.sigmoid(gate)) * up
        acc = acc + jnp.dot(h.astype(BF16), wd_ref[0, c0:c1, :], preferred_element_type=F32)
    o_ref[...] = _layer_norm_rows(DEEPNORM_ALPHA * x + acc, g_ref[...], b_ref[...])


def _ffn_ln(x, wg, wu, wd, g, b, *, layer, tm=512):
    t, d = x.shape
    d_ff = wg.shape[-1]
    tm = min(tm, t)
    assert t % tm == 0
    est = 3 * d * d_ff * 2 + 4 * tm * d * 4 + 4 * tm * d_ff * 4
    return pl.pallas_call(
        functools.partial(_ffn_kernel, chunks=_ffn_chunks(d_ff)),
        grid=(t // tm,),
        in_specs=[
            pl.BlockSpec((tm, d), lambda i: (i, 0)),
            _layer_resident((d, d_ff), layer),
            _layer_resident((d, d_ff), layer),
            _layer_resident((d_ff, d), layer),
            _resident((1, d)),
            _resident((1, d)),
        ],
        out_specs=pl.BlockSpec((tm, d), lambda i: (i, 0)),
        out_shape=jax.ShapeDtypeStruct((t, d), F32),
        compiler_params=pltpu.CompilerParams(
            dimension_semantics=("arbitrary",), vmem_limit_bytes=_vmem_limit(est)),
        name="ffn_ln",
    )(x, wg, wu, wd, g, b)


HEAD_DIM = 64
N_HEADS = 16
N_KV_HEADS = 4
GQA_GROUP = N_HEADS // N_KV_HEADS
WINDOW = 128
REL_BUCKETS = 32
REL_MAX_DIST = 128
Q_DIM = N_HEADS * HEAD_DIM
KV_DIM = N_KV_HEADS * HEAD_DIM
NEG_INF = float("-inf")
CHAIN_BATCH = 2


def _t5_bucket_table(dist):
    n = np.maximum(dist, 0)
    max_exact = REL_BUCKETS // 2
    nf = np.maximum(n, 1).astype(np.float32)
    large = max_exact + (np.log(nf / np.float32(max_exact)) / np.float32(math.log(REL_MAX_DIST / max_exact))
                         * np.float32(REL_BUCKETS - max_exact)).astype(np.int32)
    large = np.minimum(large, REL_BUCKETS - 1)
    bucket = np.where(n < max_exact, n, large)
    valid = (dist >= 0) & (dist < WINDOW)
    return np.where(valid, bucket, -1).astype(np.int32)


def _bias_from_buckets(bucket, relb_ref, head):
    acc = jnp.full(bucket.shape, NEG_INF, F32)
    for bkt in range(REL_BUCKETS):
        acc = jnp.where(bucket == bkt, relb_ref[bkt, head], acc)
    return acc


def _attn_prompt_kernel(x_ref, wqkv_ref, bqkv_ref, wo_ref, bo_ref, g_ref, b_ref, bucket_ref,
                        relb_ref, sink_ref, o_ref, kout_ref, vout_ref,
                        ka_lo, ka_hi, kb_lo, kb_hi, vt, ot, bias_scr, sink_scr, *, tq):
    bi = pl.program_id(0)
    j = pl.program_id(1)
    nj = pl.num_programs(1)
    blk = WINDOW
    half_heads = GQA_GROUP // 2
    kbufs = (ka_lo, ka_hi, kb_lo, kb_hi)

    @pl.when((bi == 0) & (j == 0))
    def _build_tables():
        bucket = bucket_ref[...]
        lane = lax.broadcasted_iota(jnp.int32, (1, 2 * blk), 1)
        for kv in range(N_KV_HEADS):
            for half in range(2):
                ha = kv * GQA_GROUP + half
                hb = ha + half_heads
                pair = kv * 2 + half
                bias_scr[pair, :, 0:blk] = _bias_from_buckets(bucket, relb_ref, ha)
                bias_scr[pair, :, blk:2 * blk] = _bias_from_buckets(bucket, relb_ref, hb)
                sink_scr[pair] = jnp.where(lane < blk, sink_ref[ha], sink_ref[hb])

    @pl.when(j == 0)
    def _no_past():
        for buf in kbufs:
            buf[0:blk, :] = jnp.zeros((blk, KV_DIM), BF16)
        vt[:, 0:blk] = jnp.zeros((KV_DIM, blk), BF16)

    @pl.when(j > 0)
    def _carry():
        for buf in kbufs:
            buf[0:blk, :] = buf[tq:tq + blk, :]
        vt[:, 0:blk] = vt[:, tq:tq + blk]

    x = x_ref[0]
    qkv = jnp.dot(x.astype(BF16), wqkv_ref[0], preferred_element_type=F32) + bqkv_ref[...]
    q = (qkv[:, :Q_DIM] * (HEAD_DIM ** -0.5)).astype(BF16)
    k = qkv[:, Q_DIM:Q_DIM + KV_DIM]
    v = qkv[:, Q_DIM + KV_DIM:]
    k_sw = jnp.concatenate(
        [pltpu.roll(k[:, c * LANES:(c + 1) * LANES], HEAD_DIM, axis=1) for c in range(KV_DIM // LANES)], axis=1)
    lo = (lax.broadcasted_iota(jnp.int32, (1, KV_DIM), 1) % LANES) < HEAD_DIM
    ka_lo[blk:blk + tq, :] = jnp.where(lo, k, 0.0).astype(BF16)
    ka_hi[blk:blk + tq, :] = jnp.where(lo, 0.0, k).astype(BF16)
    kb_lo[blk:blk + tq, :] = jnp.where(lo, k_sw, 0.0).astype(BF16)
    kb_hi[blk:blk + tq, :] = jnp.where(lo, 0.0, k_sw).astype(BF16)
    vt[:, blk:blk + tq] = v.T.astype(BF16)

    @pl.when(j == nj - 1)
    def _emit_cache():
        kout_ref[0] = k[tq - WINDOW:, :]
        vout_ref[0] = v[tq - WINDOW:, :]

    ks = lax.broadcasted_iota(jnp.int32, (blk, 2 * blk), 0)
    qt = lax.broadcasted_iota(jnp.int32, (blk, 2 * blk), 1) % blk
    own = ks <= qt
    chains = [(i, kv, half) for i in range(tq // blk) for kv in range(N_KV_HEADS) for half in range(2)]

    def scores(i, kv, half):
        r0 = i * blk
        c0 = (kv // 2) * LANES
        in_lo = kv % 2 == 0
        if half == 0:
            kk = (ka_lo if in_lo else kb_lo)[r0:r0 + 2 * blk, c0:c0 + LANES]
        else:
            kk = (kb_hi if in_lo else ka_hi)[r0:r0 + 2 * blk, c0:c0 + LANES]
        qa = q[r0:r0 + blk, (2 * kv) * LANES:(2 * kv + 1) * LANES]
        qb = q[r0:r0 + blk, (2 * kv + 1) * LANES:(2 * kv + 2) * LANES]
        q_pair = jnp.concatenate([qa, qb], axis=0)
        return lax.dot_general(kk, q_pair, (((1,), (1,)), ((), ())), preferred_element_type=F32)

    def fold(i, kv, half, s2):
        s_prev = s2[0:blk, :]
        if i == 0:
            s_prev = jnp.where(j == 0, NEG_INF, s_prev)
        return jnp.where(own, s2[blk:2 * blk, :], s_prev) + bias_scr[kv * 2 + half]

    def softmax_batch(batch, s2s):
        sinks = [sink_scr[kv * 2 + half] for _, kv, half in batch]
        ss = [fold(*ch, s2) for ch, s2 in zip(batch, s2s)]
        ms = [jnp.maximum(jnp.max(s, axis=0, keepdims=True), sink) for s, sink in zip(ss, sinks)]
        ps = [jnp.exp(s - m) for s, m in zip(ss, ms)]
        invs = [1.0 / (jnp.sum(p, axis=0, keepdims=True) + jnp.exp(sink - m)) for p, m, sink in zip(ps, ms, sinks)]
        p2s = [jnp.concatenate([jnp.where(own, 0.0, p), jnp.where(own, p, 0.0)], axis=0).astype(BF16) for p in ps]
        return list(zip(p2s, invs))

    def weighted_values(i, kv, half, p2, inv):
        r0 = i * blk
        ha = kv * GQA_GROUP + half
        hb = ha + half_heads
        v_t = vt[kv * HEAD_DIM:(kv + 1) * HEAD_DIM, r0:r0 + 2 * blk]
        o_t = jnp.dot(v_t, p2, preferred_element_type=F32) * inv
        ot[ha * HEAD_DIM:(ha + 1) * HEAD_DIM, r0:r0 + blk] = o_t[:, 0:blk]
        ot[hb * HEAD_DIM:(hb + 1) * HEAD_DIM, r0:r0 + blk] = o_t[:, blk:2 * blk]

    batches = [chains[c:c + CHAIN_BATCH] for c in range(0, len(chains), CHAIN_BATCH)]
    s_next = [scores(*ch) for ch in batches[0]]
    prev = []
    for bi, batch in enumerate(batches):
        s_cur = s_next
        if bi + 1 < len(batches):
            s_next = [scores(*ch) for ch in batches[bi + 1]]
        probs = softmax_batch(batch, s_cur)
        for ch, pr in prev:
            weighted_values(*ch, *pr)
        prev = list(zip(batch, probs))
    for ch, pr in prev:
        weighted_values(*ch, *pr)

    o = ot[...].T.astype(BF16)
    y = jnp.dot(o, wo_ref[0], preferred_element_type=F32) + bo_ref[...]
    o_ref[0] = _layer_norm_rows(DEEPNORM_ALPHA * x + y, g_ref[...], b_ref[...])


def _attn_prompt(x, wqkv, bqkv, wo, bo, g, b, rel_bias, sinks, *, layer, tq=512):
    bsz, l, d = x.shape
    tq = min(tq, l)
    assert l % tq == 0 and tq % WINDOW == 0
    qi = np.arange(WINDOW)[None, :]
    si = np.arange(WINDOW)[:, None]
    bucket = jnp.asarray(_t5_bucket_table(np.where(si <= qi, qi - si, qi + WINDOW - si)))
    qkv_dim = wqkv.shape[-1]
    smem = pl.BlockSpec(memory_space=pltpu.SMEM)
    est = (d * qkv_dim + Q_DIM * d) * 2 + 5 * tq * d * 4 + N_HEADS * WINDOW * 2 * WINDOW * 4 + 3 * tq * qkv_dim * 4
    return pl.pallas_call(
        functools.partial(_attn_prompt_kernel, tq=tq),
        grid=(bsz, l // tq),
        in_specs=[
            pl.BlockSpec((1, tq, d), lambda bi, j: (bi, j, 0)),
            _layer_resident((d, qkv_dim), layer), _resident((1, qkv_dim)),
            _layer_resident((Q_DIM, d), layer), _resident((1, d)),
            _resident((1, d)), _resident((1, d)),
            _resident((WINDOW, WINDOW)),
            smem, smem,
        ],
        out_specs=[
            pl.BlockSpec((1, tq, d), lambda bi, j: (bi, j, 0)),
            pl.BlockSpec((1, WINDOW, KV_DIM), lambda bi, j: (bi, 0, 0)),
            pl.BlockSpec((1, WINDOW, KV_DIM), lambda bi, j: (bi, 0, 0)),
        ],
        out_shape=[
            jax.ShapeDtypeStruct((bsz, l, d), F32),
            jax.ShapeDtypeStruct((bsz, WINDOW, KV_DIM), F32),
            jax.ShapeDtypeStruct((bsz, WINDOW, KV_DIM), F32),
        ],
        scratch_shapes=[pltpu.VMEM((WINDOW + tq, KV_DIM), BF16)] * 4 + [
            pltpu.VMEM((KV_DIM, WINDOW + tq), BF16),
            pltpu.VMEM((Q_DIM, tq), F32),
            pltpu.VMEM((N_HEADS // 2, WINDOW, 2 * WINDOW), F32),
            pltpu.VMEM((N_HEADS // 2, 1, 2 * WINDOW), F32),
        ],
        compiler_params=pltpu.CompilerParams(
            dimension_semantics=("arbitrary", "arbitrary"), vmem_limit_bytes=_vmem_limit(est)),
        name="attn_prompt",
    )(x, wqkv, bqkv, wo, bo, g, b, bucket, rel_bias, sinks)


SAMPLE_KEYS = 2 * WINDOW
SAMPLE_CHAIN_BATCH = 8


def _attn_sample_kernel(x_ref, ck_ref, cv_ref, wqkv_ref, bqkv_ref, wo_ref, bo_ref, g_ref, b_ref, bucket_ref,
                        relb_ref, sink_ref, *rest, bb, l, n_prev):
    o_ref, nk_ref, nv_ref, qbuf, obuf, bias_scr = rest[n_prev:]

    @pl.when(pl.program_id(0) == 0)
    def _init():
        bucket = bucket_ref[...]
        for h in range(N_HEADS):
            kv, g_ = divmod(h, GQA_GROUP)
            bias_scr[kv, g_ * l:(g_ + 1) * l, :] = _bias_from_buckets(bucket, relb_ref, h)

    x = x_ref[...]
    qkv = jnp.dot(x.astype(BF16), wqkv_ref[0], preferred_element_type=F32) + bqkv_ref[...]
    qbuf[...] = qkv[:, :Q_DIM] * (HEAD_DIM ** -0.5)
    k_new = qkv[:, Q_DIM:Q_DIM + KV_DIM]
    v_new = qkv[:, Q_DIM + KV_DIM:]

    sink_cols = []
    for kv in range(N_KV_HEADS):
        sink_cols.append(jnp.concatenate(
            [jnp.full((l, 1), sink_ref[kv * GQA_GROUP + g_], F32) for g_ in range(GQA_GROUP)], axis=0))

    kn_t = k_new.T
    vn_t = v_new.T
    lane = lax.broadcasted_iota(jnp.int32, (HEAD_DIM, WINDOW), 1)
    pad = jnp.zeros((WINDOW - l, KV_DIM), F32)

    def shifted(old_t, new_t, i):
        kept = pltpu.roll(old_t, WINDOW - l, axis=1)
        new = pltpu.roll(new_t, (WINDOW - l - i * l) % WINDOW, axis=1)
        return jnp.where(lane >= WINDOW - l, new, kept)

    def keys_values(i):
        r0 = i * l
        for kv in range(N_KV_HEADS):
            hs = slice(kv * HEAD_DIM, (kv + 1) * HEAD_DIM)
            nk_ref[0, i, kv] = shifted(ck_ref[0, i, kv], kn_t[hs, :], i)
            nv_ref[0, i, kv] = shifted(cv_ref[0, i, kv], vn_t[hs, :], i)
        kn = jnp.concatenate([k_new[r0:r0 + l, :], pad], axis=0).astype(BF16)
        vn = jnp.concatenate([v_new[r0:r0 + l, :], pad], axis=0).astype(BF16)
        return kn, vn

    def scores(i, kv, kn):
        r0 = i * l
        c0 = kv * HEAD_DIM
        qs = jnp.concatenate(
            [qbuf[r0:r0 + l, (kv * GQA_GROUP + g_) * HEAD_DIM:(kv * GQA_GROUP + g_ + 1) * HEAD_DIM]
             for g_ in range(GQA_GROUP)], axis=0).astype(BF16)
        s_cache = jnp.dot(qs, ck_ref[0, i, kv].astype(BF16), preferred_element_type=F32)
        s_new = lax.dot_general(qs, kn[:, c0:c0 + HEAD_DIM], (((1,), (1,)), ((), ())), preferred_element_type=F32)
        return jnp.concatenate([s_cache, s_new], axis=1)

    def weighted_values(i, kv, p, vn):
        c0 = kv * HEAD_DIM
        pb = p.astype(BF16)
        o_cache = lax.dot_general(pb[:, :WINDOW], cv_ref[0, i, kv].astype(BF16), (((1,), (1,)), ((), ())),
                                  preferred_element_type=F32)
        return o_cache + jnp.dot(pb[:, WINDOW:], vn[:, c0:c0 + HEAD_DIM], preferred_element_type=F32)

    for i0 in range(0, bb, SAMPLE_CHAIN_BATCH):
        items = range(i0, min(bb, i0 + SAMPLE_CHAIN_BATCH))
        kvs = {i: keys_values(i) for i in items}
        chains = [(i, kv) for i in items for kv in range(N_KV_HEADS)]
        s_all = [scores(i, kv, kvs[i][0]) + bias_scr[kv] for i, kv in chains]
        m_all = [jnp.maximum(jnp.max(s, axis=-1, keepdims=True), sink_cols[kv]) for s, (i, kv) in zip(s_all, chains)]
        p_all = [jnp.exp(s - m) for s, m in zip(s_all, m_all)]
        den_all = [jnp.sum(p, axis=-1, keepdims=True) + jnp.exp(sink_cols[kv] - m)
                   for p, m, (i, kv) in zip(p_all, m_all, chains)]
        o_all = [weighted_values(i, kv, p, kvs[i][1]) for p, (i, kv) in zip(p_all, chains)]
        for o, den, (i, kv) in zip(o_all, den_all, chains):
            o = o / den
            for g_ in range(GQA_GROUP):
                h = kv * GQA_GROUP + g_
                obuf[i * l:(i + 1) * l, h * HEAD_DIM:(h + 1) * HEAD_DIM] = o[g_ * l:(g_ + 1) * l, :]

    y = jnp.dot(obuf[...].astype(BF16), wo_ref[0], preferred_element_type=F32) + bo_ref[...]
    o_ref[...] = _layer_norm_rows(DEEPNORM_ALPHA * x + y, g_ref[...], b_ref[...])


def _attn_sample(x, cache_k_t, cache_v_t, wqkv, bqkv, wo, bo, g, b, rel_bias, sinks, *, layer, prev=None, bb=16):
    bsz, l, d = x.shape
    assert bsz % bb == 0 and l == SUBLANES and bb * l == WINDOW
    ti = np.arange(l)[:, None]
    ci = np.arange(SAMPLE_KEYS)[None, :]
    table = _t5_bucket_table(ti + WINDOW - ci)
    table = np.where(ci < WINDOW + l, table, -1).astype(np.int32)
    bucket = jnp.asarray(table)
    qkv_dim = wqkv.shape[-1]
    rows = bb * l
    smem = pl.BlockSpec(memory_space=pltpu.SMEM)
    cache_spec = pl.BlockSpec((1, bb, N_KV_HEADS, HEAD_DIM, WINDOW), lambda i: (layer, i, 0, 0, 0))
    est = (d * qkv_dim + Q_DIM * d) * 2 + 8 * bb * WINDOW * KV_DIM * 4 + 8 * rows * d * 4
    in_specs = [
        pl.BlockSpec((rows, d), lambda i: (i, 0)),
        cache_spec, cache_spec,
        _layer_resident((d, qkv_dim), layer), _resident((1, qkv_dim)),
        _layer_resident((Q_DIM, d), layer), _resident((1, d)),
        _resident((1, d)), _resident((1, d)),
        _resident((l, SAMPLE_KEYS)),
        smem, smem,
    ]
    args = [x.reshape(bsz * l, d), cache_k_t, cache_v_t, wqkv, bqkv, wo, bo, g, b, bucket, rel_bias, sinks]
    aliases = {}
    if prev is not None:
        aliases = {len(args): 1, len(args) + 1: 2}
        in_specs += [pl.BlockSpec(memory_space=pl.ANY)] * 2
        args += list(prev)
    return pl.pallas_call(
        functools.partial(_attn_sample_kernel, bb=bb, l=l, n_prev=0 if prev is None else 2),
        grid=(bsz // bb,),
        in_specs=in_specs,
        out_specs=[pl.BlockSpec((rows, d), lambda i: (i, 0)), cache_spec, cache_spec],
        out_shape=[
            jax.ShapeDtypeStruct((bsz * l, d), F32),
            jax.ShapeDtypeStruct(cache_k_t.shape, F32),
            jax.ShapeDtypeStruct(cache_v_t.shape, F32),
        ],
        scratch_shapes=[
            pltpu.VMEM((rows, Q_DIM), F32),
            pltpu.VMEM((rows, Q_DIM), F32),
            pltpu.VMEM((N_KV_HEADS, GQA_GROUP * l, SAMPLE_KEYS), F32),
        ],
        input_output_aliases=aliases,
        compiler_params=pltpu.CompilerParams(
            dimension_semantics=("arbitrary",), vmem_limit_bytes=_vmem_limit(est)),
        name="attn_sample",
    )(*args)


POOL_WINDOWS = (2, 4, 8, 16)
POOL_PAD = max(POOL_WINDOWS)
POOL_STATE_LEN = POOL_PAD - 1


def _pool_mix(window_sum, x, cnt_of, pw_ref, scale):
    gd = x.shape[-1] // len(POOL_WINDOWS)
    outs = []
    for g, w in enumerate(POOL_WINDOWS):
        diff = window_sum(g, w) / cnt_of(w) - x[:, g * gd:(g + 1) * gd]
        outs.append(jnp.dot(diff.astype(BF16), pw_ref[g], preferred_element_type=F32))
    return jnp.concatenate(outs, axis=-1) * scale


def _pool_prompt_kernel(x_ref, pw_ref, sc_ref, g_ref, b_ref, o_ref, st_ref, buf, *, tm):
    j = pl.program_id(1)
    nj = pl.num_programs(1)
    d = x_ref.shape[-1]
    gd = d // len(POOL_WINDOWS)

    @pl.when(j == 0)
    def _no_past():
        buf[0:POOL_PAD, :] = jnp.zeros((POOL_PAD, d), F32)

    @pl.when(j > 0)
    def _carry():
        buf[0:POOL_PAD, :] = buf[tm:tm + POOL_PAD, :]

    x = x_ref[0]
    buf[POOL_PAD:POOL_PAD + tm, :] = x

    def window_sum(g, w):
        acc = x[:, g * gd:(g + 1) * gd]
        for s in range(1, w):
            acc = acc + buf[POOL_PAD - s:POOL_PAD - s + tm, g * gd:(g + 1) * gd]
        return acc

    pos1 = j * tm + lax.broadcasted_iota(jnp.int32, (tm, 1), 0) + 1

    def cnt_of(w):
        return jnp.minimum(pos1, w).astype(F32)

    y = _pool_mix(window_sum, x, cnt_of, pw_ref, sc_ref[...])
    o_ref[0] = _layer_norm_rows(DEEPNORM_ALPHA * x + y, g_ref[...], b_ref[...])

    @pl.when(j == nj - 1)
    def _emit_state():
        st_ref[0] = buf[tm + 1:tm + POOL_PAD, :]


def _pool_prompt(x, pw, scale, g, b, *, tm=512):
    bsz, l, d = x.shape
    tm = min(tm, l)
    assert l % tm == 0 and tm >= POOL_PAD
    ng, gd = pw.shape[0], pw.shape[1]
    est = 6 * tm * d * 4 + ng * gd * gd * 2
    return pl.pallas_call(
        functools.partial(_pool_prompt_kernel, tm=tm),
        grid=(bsz, l // tm),
        in_specs=[
            pl.BlockSpec((1, tm, d), lambda bi, j: (bi, j, 0)),
            _resident((ng, gd, gd)), _resident((1, d)), _resident((1, d)), _resident((1, d)),
        ],
        out_specs=[
            pl.BlockSpec((1, tm, d), lambda bi, j: (bi, j, 0)),
            pl.BlockSpec((1, POOL_STATE_LEN, d), lambda bi, j: (bi, 0, 0)),
        ],
        out_shape=[
            jax.ShapeDtypeStruct((bsz, l, d), F32),
            jax.ShapeDtypeStruct((bsz, POOL_STATE_LEN, d), F32),
        ],
        scratch_shapes=[pltpu.VMEM((POOL_PAD + tm, d), F32)],
        compiler_params=pltpu.CompilerParams(
            dimension_semantics=("arbitrary", "arbitrary"), vmem_limit_bytes=_vmem_limit(est)),
        name="pool_prompt",
    )(x, pw, scale, g, b)


def _pool_sample_kernel(x_ref, st_ref, pw_ref, sc_ref, g_ref, b_ref, o_ref, nst_ref, buf, *, bb, l):
    d = x_ref.shape[-1]
    gd = d // len(POOL_WINDOWS)
    x3 = x_ref[...]
    buf[:, 1:POOL_PAD, :] = st_ref[...]
    buf[:, POOL_PAD:POOL_PAD + l, :] = x3
    x = x3.reshape(bb * l, d)

    def window_sum(g, w):
        acc = x3[:, :, g * gd:(g + 1) * gd]
        for s in range(1, w):
            acc = acc + buf[:, POOL_PAD - s:POOL_PAD - s + l, g * gd:(g + 1) * gd]
        return acc.reshape(bb * l, gd)

    y = _pool_mix(window_sum, x, lambda w: float(w), pw_ref, sc_ref[...])
    o_ref[...] = _layer_norm_rows(DEEPNORM_ALPHA * x + y, g_ref[...], b_ref[...]).reshape(bb, l, d)
    nst_ref[...] = buf[:, l + 1:l + POOL_PAD, :]


def _pool_sample(x, state, pw, scale, g, b, *, start, bb=16):
    bsz, l, d = x.shape
    assert bsz % bb == 0 and l == SUBLANES and start + 1 >= POOL_PAD
    ng, gd = pw.shape[0], pw.shape[1]
    est = 8 * bb * (POOL_PAD + l) * d * 4 + ng * gd * gd * 2
    return pl.pallas_call(
        functools.partial(_pool_sample_kernel, bb=bb, l=l),
        grid=(bsz // bb,),
        in_specs=[
            pl.BlockSpec((bb, l, d), lambda i: (i, 0, 0)),
            pl.BlockSpec((bb, POOL_STATE_LEN, d), lambda i: (i, 0, 0)),
            _resident((ng, gd, gd)), _resident((1, d)), _resident((1, d)), _resident((1, d)),
        ],
        out_specs=[
            pl.BlockSpec((bb, l, d), lambda i: (i, 0, 0)),
            pl.BlockSpec((bb, POOL_STATE_LEN, d), lambda i: (i, 0, 0)),
        ],
        out_shape=[
            jax.ShapeDtypeStruct((bsz, l, d), F32),
            jax.ShapeDtypeStruct((bsz, POOL_STATE_LEN, d), F32),
        ],
        scratch_shapes=[pltpu.VMEM((bb, POOL_PAD + l, d), F32)],
        compiler_params=pltpu.CompilerParams(
            dimension_semantics=("arbitrary",), vmem_limit_bytes=_vmem_limit(est)),
        name="pool_sample",
    )(x, state, pw, scale, g, b)


D_INNER = 2048
SSM_HEAD_DIM = 64
SSM_HEADS = D_INNER // SSM_HEAD_DIM
SSM_GROUPS = 4
SSM_HPG = SSM_HEADS // SSM_GROUPS
D_STATE = 128
CONV_WIDTH = 4
GBN = SSM_GROUPS * D_STATE
CONV_DIM = D_INNER + 2 * GBN
GROUP_INNER = D_INNER // SSM_GROUPS
RMS_EPS = 1e-5
CONV_PAD = SUBLANES
SSM_ROWS = 128


def _split_bf16(v, parts=3):
    out = []
    r = v
    for _ in range(parts):
        p = r.astype(BF16)
        out.append(p)
        r = r - p.astype(F32)
    return out


def _expand_heads(v, e):
    return sum(jnp.dot(p, e, preferred_element_type=F32) for p in _split_bf16(v, parts=2))


def _ssm_in_proj(xb, win_ref, wdt_ref, c0, c1):
    main = win_ref.shape[1]
    if c0 >= main:
        return jnp.dot(xb, wdt_ref[:, c0 - main:c1 - main], preferred_element_type=F32)
    assert c1 <= main
    return jnp.dot(xb, win_ref[:, c0:c1], preferred_element_type=F32)


def _ssm_prep(proj, xp_buf, conv_state, dtb_ref, alog_ref, tril_ref, *, nseq, q):
    rows = nseq * q
    xbc_pre = proj(D_INNER, D_INNER + CONV_DIM)
    dt_pre = proj(D_INNER + CONV_DIM, D_INNER + CONV_DIM + LANES)
    if conv_state is not None:
        xp_buf[:, CONV_PAD - (CONV_WIDTH - 1):CONV_PAD, :] = conv_state
    xp_buf[:, CONV_PAD:CONV_PAD + q, :] = xbc_pre.reshape(nseq, q, CONV_DIM)
    new_conv = xp_buf[:, q + CONV_PAD - (CONV_WIDTH - 1):q + CONV_PAD, :]

    dtv = dt_pre + dtb_ref[...]
    dt = jnp.maximum(dtv, 0.0) + jnp.log1p(jnp.exp(-jnp.abs(dtv)))
    a = -jnp.exp(alog_ref[...])
    tril = tril_ref[...]
    acum = sum(jnp.dot(tril, p, preferred_element_type=F32) for p in _split_bf16(dt * a))
    a3 = acum.reshape(nseq, q, LANES)
    alast = jnp.broadcast_to(a3[:, q - 1:q, :], (nseq, q, LANES)).reshape(rows, LANES)
    return dict(acum=acum, acum_t=acum.T, dt_t=dt.T, causal=tril > 0.5,
                exp_a=jnp.exp(acum), dd=jnp.exp(alast - acum) * dt, new_conv=new_conv)


def _conv_silu(xp_buf, cw_ref, cb_ref, c0, c1, *, nseq, q):
    acc = cb_ref[:, c0:c1].reshape(1, 1, c1 - c0)
    for jj in range(CONV_WIDTH):
        off = CONV_PAD - (CONV_WIDTH - 1) + jj
        acc = acc + xp_buf[:, off:off + q, c0:c1] * cw_ref[jj:jj + 1, c0:c1].reshape(1, 1, c1 - c0)
    acc = acc.reshape(nseq * q, c1 - c0)
    return acc * jax.nn.sigmoid(acc)


def _ssm_group_diag(g, f, proj, prep, xp_buf, cw_ref, cb_ref, ybuf, *, nseq, q):
    f["z"] = proj(g * GROUP_INNER, (g + 1) * GROUP_INNER)
    conv = functools.partial(_conv_silu, xp_buf, cw_ref, cb_ref, nseq=nseq, q=q)
    half = GROUP_INNER // 2
    xs_lo = conv(g * GROUP_INNER, g * GROUP_INNER + half)
    yield
    xs_hi = conv(g * GROUP_INNER + half, (g + 1) * GROUP_INNER)
    yield
    xs = f["xs"] = jnp.concatenate([xs_lo, xs_hi], axis=-1)
    bm = f["bm"] = conv(D_INNER + g * D_STATE, D_INNER + (g + 1) * D_STATE)
    cm = f["cm"] = conv(D_INNER + GBN + g * D_STATE, D_INNER + GBN + (g + 1) * D_STATE)
    yield
    cb = lax.dot_general(cm.astype(BF16), bm.astype(BF16), (((1,), (1,)), ((), ())), preferred_element_type=F32)
    acum, acum_t, dt_t, causal = prep["acum"], prep["acum_t"], prep["dt_t"], prep["causal"]
    for r in range(SSM_HPG):
        h = g * SSM_HPG + r
        seg = acum[:, h:h + 1] - acum_t[h:h + 1, :]
        lm = jnp.exp(jnp.where(causal, seg, NEG_INF))
        w = (cb * lm * dt_t[h:h + 1, :]).astype(BF16)
        xh = xs[:, r * SSM_HEAD_DIM:(r + 1) * SSM_HEAD_DIM].astype(BF16)
        ybuf[:, h * SSM_HEAD_DIM:(h + 1) * SSM_HEAD_DIM] = jnp.dot(w, xh, preferred_element_type=F32)
        yield


def _ssm_group_expand(g, f, prep, e_ref):
    e = e_ref[:, g * GROUP_INNER:(g + 1) * GROUP_INNER]
    f["exp_a"] = _expand_heads(prep["exp_a"], e)
    yield
    f["xdd"] = (f["xs"] * _expand_heads(prep["dd"], e)).astype(BF16)
    yield


def _interleave(*gens):
    live = list(gens)
    while live:
        for gen in list(live):
            try:
                next(gen)
            except StopIteration:
                live.remove(gen)


def _gate_norm(y, z, nw):
    y = y * (z * jax.nn.sigmoid(z))
    return y * lax.rsqrt(jnp.mean(y * y, axis=-1, keepdims=True) + RMS_EPS) * nw


def _ssm_prompt_kernel(x_ref, xn_ref, win_ref, wdt_ref, cw_ref, cb_ref, dtb_ref, alog_ref, dsk_ref, nw_ref, wout_ref, e_ref,
                       tril_ref, g_ref, b_ref, o_ref, cout_ref, sout_ref, xp_buf, ht, ybuf, zx):
    j = pl.program_id(1)
    nj = pl.num_programs(1)
    q = SSM_ROWS
    win_cols = win_ref.shape[1] + wdt_ref.shape[1]

    def project(xb, c0, c1):
        return _ssm_in_proj(xb, win_ref, wdt_ref, c0, c1)
    slot = j % 2

    @pl.when(j == 0)
    def _first_chunk():
        xp_buf[:, 0:CONV_PAD, :] = jnp.zeros((1, CONV_PAD, CONV_DIM), F32)
        ht[...] = jnp.zeros(ht.shape, F32)
        xb0 = x_ref[0].astype(BF16)
        zx[0, :, 0:win_ref.shape[1]] = project(xb0, 0, win_ref.shape[1])
        zx[0, :, win_ref.shape[1]:win_cols] = project(xb0, win_ref.shape[1], win_cols)

    @pl.when(j > 0)
    def _carry():
        xp_buf[:, 0:CONV_PAD, :] = xp_buf[:, q:q + CONV_PAD, :]

    x = x_ref[0]
    xnb = xn_ref[0].astype(BF16)
    edges = list(range(0, win_cols, MXU_DIM)) + [win_cols]
    pieces = list(zip(edges[:-1], edges[1:]))

    def next_projection(n):
        for _ in range(n):
            if pieces:
                c0, c1 = pieces.pop(0)
                zx[1 - slot, :, c0:c1] = project(xnb, c0, c1)
            yield

    def proj(c0, c1):
        return zx[slot, :, c0:c1]

    prep = _ssm_prep(proj, xp_buf, None, dtb_ref, alog_ref, tril_ref, nseq=1, q=q)
    out = [jnp.zeros(x.shape, F32)]
    fs = [dict() for _ in range(SSM_GROUPS)]

    def diag(g):
        return _ssm_group_diag(g, fs[g], proj, prep, xp_buf, cw_ref, cb_ref, ybuf, nseq=1, q=q)

    def state_terms(g):
        f = fs[g]
        sl = slice(g * GROUP_INNER, (g + 1) * GROUP_INNER)
        yield from _ssm_group_expand(g, f, prep, e_ref)
        exp_a = f["exp_a"]
        hg = ht[:, sl]
        y_off = jnp.dot(f["cm"].astype(BF16), hg.astype(BF16), preferred_element_type=F32)
        yield
        ht[:, sl] = hg * exp_a[q - 1:q, :] + jnp.dot(f["bm"].T.astype(BF16), f["xdd"], preferred_element_type=F32)
        yield
        y = ybuf[:, sl] + exp_a * y_off + dsk_ref[:, sl] * f["xs"]
        y = _gate_norm(y, f["z"], nw_ref[:, sl]).astype(BF16)
        yield
        half = GROUP_INNER // 2
        out[0] = out[0] + jnp.dot(y[:, :half], wout_ref[sl.start:sl.start + half, :], preferred_element_type=F32)
        yield
        out[0] = out[0] + jnp.dot(y[:, half:], wout_ref[sl.start + half:sl.stop, :], preferred_element_type=F32)
        yield

    per_stage = -(-len(pieces) // (SSM_GROUPS + 1))
    _interleave(diag(0), next_projection(2 * per_stage))
    for g in range(1, SSM_GROUPS):
        _interleave(diag(g), state_terms(g - 1), next_projection(per_stage))
    _interleave(state_terms(SSM_GROUPS - 1), next_projection(len(pieces)))
    o_ref[0] = _layer_norm_rows(DEEPNORM_ALPHA * x + out[0], g_ref[...], b_ref[...])

    @pl.when(j == nj - 1)
    def _emit_state():
        cout_ref[...] = prep["new_conv"]
        sout_ref[0] = ht[...].T


def _ssm_consts(nseq, q):
    rows = nseq * q
    r = np.arange(rows)
    tril = ((r[:, None] >= r[None, :]) & (r[:, None] // q == r[None, :] // q)).astype(np.float32)
    e = np.zeros((LANES, D_INNER), np.float32)
    e[np.arange(D_INNER) // SSM_HEAD_DIM, np.arange(D_INNER)] = 1.0
    return jnp.asarray(tril, BF16), jnp.asarray(e, BF16)


def _ssm_weight_specs(d):
    return [
        pl.BlockSpec((d, D_INNER + CONV_DIM), lambda *_: (0, 0), pipeline_mode=pl.Buffered(1)),
        _resident((d, LANES)), _resident((CONV_WIDTH, CONV_DIM)), _resident((1, CONV_DIM)),
        _resident((1, LANES)), _resident((1, LANES)),
    ]


def _ssm_prompt(x, w, g, b):
    bsz, l, d = x.shape
    q = SSM_ROWS
    assert l % q == 0
    tril, e = _ssm_consts(1, q)
    win_cols = D_INNER + CONV_DIM + LANES
    est = (d * win_cols + D_INNER * d) * 2 + LANES * D_INNER * 4 + 12 * q * D_INNER * 4 + 5 * q * win_cols * 4
    nj = l // q
    return pl.pallas_call(
        _ssm_prompt_kernel,
        grid=(bsz, nj),
        in_specs=[pl.BlockSpec((1, q, d), lambda bi, j: (bi, j, 0)),
                  pl.BlockSpec((1, q, d), lambda bi, j: (bi, jnp.minimum(j + 1, nj - 1), 0))]
        + _ssm_weight_specs(d) + [
            _resident((1, D_INNER)), _resident((1, D_INNER)), _resident((D_INNER, d)),
            _resident((LANES, D_INNER)), _resident((q, q)), _resident((1, d)), _resident((1, d)),
        ],
        out_specs=[
            pl.BlockSpec((1, q, d), lambda bi, j: (bi, j, 0)),
            pl.BlockSpec((1, CONV_WIDTH - 1, CONV_DIM), lambda bi, j: (bi, 0, 0)),
            pl.BlockSpec((1, D_INNER, D_STATE), lambda bi, j: (bi, 0, 0)),
        ],
        out_shape=[
            jax.ShapeDtypeStruct((bsz, l, d), F32),
            jax.ShapeDtypeStruct((bsz, CONV_WIDTH - 1, CONV_DIM), F32),
            jax.ShapeDtypeStruct((bsz, D_INNER, D_STATE), F32),
        ],
        scratch_shapes=[
            pltpu.VMEM((1, CONV_PAD + q, CONV_DIM), F32),
            pltpu.VMEM((D_STATE, D_INNER), F32),
            pltpu.VMEM((q, D_INNER), F32),
            pltpu.VMEM((2, q, win_cols), F32),
        ],
        compiler_params=pltpu.CompilerParams(
            dimension_semantics=("arbitrary", "arbitrary"), vmem_limit_bytes=_vmem_limit(est)),
        name="ssm_prompt",
    )(x, x, w["win"], w["wdt"], w["cw"], w["cb"], w["dtb"], w["alog"], w["dsk"], w["nw"], w["wout"], e, tril, g, b)


def _ssm_sample_front_kernel(x_ref, cst_ref, win_ref, wdt_ref, cw_ref, cb_ref, dtb_ref, alog_ref, dsk_ref, e_ref, tril_ref,
                             y_ref, z_ref, ea_ref, xdd_ref, c_ref, bt_ref, cout_ref, xp_buf, ybuf, *, nseq, q):
    xb = x_ref[...].astype(BF16)

    def proj(c0, c1):
        return _ssm_in_proj(xb, win_ref, wdt_ref, c0, c1)

    prep = _ssm_prep(proj, xp_buf, cst_ref[...], dtb_ref, alog_ref, tril_ref, nseq=nseq, q=q)
    cout_ref[...] = prep["new_conv"]
    for g in range(SSM_GROUPS):
        sl = slice(g * GROUP_INNER, (g + 1) * GROUP_INNER)
        f = {}
        _interleave(_ssm_group_diag(g, f, proj, prep, xp_buf, cw_ref, cb_ref, ybuf, nseq=nseq, q=q))
        _interleave(_ssm_group_expand(g, f, prep, e_ref))
        y_ref[:, sl] = ybuf[:, sl] + dsk_ref[:, sl] * f["xs"]
        z_ref[:, sl] = f["z"]
        ea_ref[:, sl] = f["exp_a"]
        xdd_ref[:, sl] = f["xdd"]
        c_ref[:, g * D_STATE:(g + 1) * D_STATE] = f["cm"]
        bt_ref[g * D_STATE:(g + 1) * D_STATE, :] = f["bm"].T.astype(BF16)


def _ssm_sample_state_kernel(st_ref, y_ref, ea_ref, c_ref, xdd_ref, bt_ref, o_ref, nst_ref, *, bb, q, nseq):
    ea = ea_ref[...]
    xdd = xdd_ref[...]
    first_seq = (pl.program_id(0) % (nseq // bb)) * bb
    lane_seq = lax.broadcasted_iota(jnp.int32, (D_STATE, nseq * q), 1) // q
    for i in range(bb):
        r0 = i * q
        ht = st_ref[i].T
        decay = ea[r0 + q - 1:r0 + q, :]
        y_parts = []
        new_parts = []
        for g in range(SSM_GROUPS):
            sl = slice(g * GROUP_INNER, (g + 1) * GROUP_INNER)
            cg = c_ref[r0:r0 + q, g * D_STATE:(g + 1) * D_STATE].astype(BF16)
            hg = ht[:, sl]
            y_parts.append(jnp.dot(cg, hg.astype(BF16), preferred_element_type=F32))
            btg = bt_ref[g * D_STATE:(g + 1) * D_STATE, :]
            btg = jnp.where(lane_seq == first_seq + i, btg, jnp.zeros_like(btg))
            new_parts.append(hg * decay[:, sl] + jnp.dot(btg, xdd[:, sl], preferred_element_type=F32))
        o_ref[r0:r0 + q, :] = y_ref[r0:r0 + q, :] + ea[r0:r0 + q, :] * jnp.concatenate(y_parts, axis=-1)
        nst_ref[i] = jnp.concatenate(new_parts, axis=-1).T


def _ssm_sample_finish_kernel(x_ref, y_ref, z_ref, nw_ref, wout_ref, g_ref, b_ref, o_ref):
    x = x_ref[...]
    out = jnp.zeros(x.shape, F32)
    for g in range(SSM_GROUPS):
        sl = slice(g * GROUP_INNER, (g + 1) * GROUP_INNER)
        y = _gate_norm(y_ref[:, sl], z_ref[:, sl], nw_ref[:, sl])
        out = out + jnp.dot(y.astype(BF16), wout_ref[sl, :], preferred_element_type=F32)
    o_ref[...] = _layer_norm_rows(DEEPNORM_ALPHA * x + out, g_ref[...], b_ref[...])


def _ssm_sample(x, conv_state, ssm_state, w, g, b, *, bb_state=4):
    bsz, q, d = x.shape
    t = bsz * q
    nseq = SSM_ROWS // q
    rows = SSM_ROWS
    assert q == SUBLANES and bsz % nseq == 0 and nseq % bb_state == 0
    tril, e = _ssm_consts(nseq, q)
    win_cols = D_INNER + CONV_DIM + LANES
    x2 = x.reshape(t, d)
    row_spec = lambda c: pl.BlockSpec((rows, c), lambda i: (i, 0))
    est = d * win_cols * 2 + LANES * D_INNER * 4 + 16 * rows * D_INNER * 4 + 3 * rows * win_cols * 4
    ydiag, z, exp_a, xdd, cmat, bt, new_conv = pl.pallas_call(
        functools.partial(_ssm_sample_front_kernel, nseq=nseq, q=q),
        grid=(t // rows,),
        in_specs=[row_spec(d), pl.BlockSpec((nseq, CONV_WIDTH - 1, CONV_DIM), lambda i: (i, 0, 0))]
        + _ssm_weight_specs(d) + [_resident((1, D_INNER)), _resident((LANES, D_INNER)), _resident((rows, rows))],
        out_specs=[row_spec(D_INNER), row_spec(D_INNER), row_spec(D_INNER), row_spec(D_INNER), row_spec(GBN),
                   pl.BlockSpec((GBN, rows), lambda i: (i, 0)),
                   pl.BlockSpec((nseq, CONV_WIDTH - 1, CONV_DIM), lambda i: (i, 0, 0))],
        out_shape=[
            jax.ShapeDtypeStruct((t, D_INNER), F32), jax.ShapeDtypeStruct((t, D_INNER), F32),
            jax.ShapeDtypeStruct((t, D_INNER), F32), jax.ShapeDtypeStruct((t, D_INNER), BF16),
            jax.ShapeDtypeStruct((t, GBN), F32),
            jax.ShapeDtypeStruct((t // rows * GBN, rows), BF16),
            jax.ShapeDtypeStruct((bsz, CONV_WIDTH - 1, CONV_DIM), F32),
        ],
        scratch_shapes=[pltpu.VMEM((nseq, CONV_PAD + q, CONV_DIM), F32), pltpu.VMEM((rows, D_INNER), F32)],
        compiler_params=pltpu.CompilerParams(
            dimension_semantics=("arbitrary",), vmem_limit_bytes=_vmem_limit(est)),
        name="ssm_sample_front",
    )(x2, conv_state, w["win"], w["wdt"], w["cw"], w["cb"], w["dtb"], w["alog"], w["dsk"], e, tril)

    srows = bb_state * q
    srow_spec = lambda c: pl.BlockSpec((srows, c), lambda i: (i, 0))
    st_spec = pl.BlockSpec((bb_state, D_INNER, D_STATE), lambda i: (i, 0, 0))
    est = 4 * bb_state * D_INNER * D_STATE * 4 + 8 * D_INNER * D_STATE * 4
    per_block = nseq // bb_state
    y, new_state = pl.pallas_call(
        functools.partial(_ssm_sample_state_kernel, bb=bb_state, q=q, nseq=nseq),
        grid=(bsz // bb_state,),
        in_specs=[st_spec, srow_spec(D_INNER), srow_spec(D_INNER), srow_spec(GBN),
                  pl.BlockSpec((rows, D_INNER), lambda i: (i // per_block, 0)),
                  pl.BlockSpec((GBN, rows), lambda i: (i // per_block, 0))],
        out_specs=[srow_spec(D_INNER), st_spec],
        out_shape=[jax.ShapeDtypeStruct((t, D_INNER), F32),
                   jax.ShapeDtypeStruct((bsz, D_INNER, D_STATE), F32)],
        compiler_params=pltpu.CompilerParams(
            dimension_semantics=("arbitrary",), vmem_limit_bytes=_vmem_limit(est)),
        name="ssm_sample_state",
    )(ssm_state, ydiag, exp_a, cmat, xdd, bt)

    est = D_INNER * d * 2 + 8 * rows * D_INNER * 4
    out = pl.pallas_call(
        _ssm_sample_finish_kernel,
        grid=(t // rows,),
        in_specs=[row_spec(d), row_spec(D_INNER), row_spec(D_INNER),
                  _resident((1, D_INNER)), _resident((D_INNER, d)), _resident((1, d)), _resident((1, d))],
        out_specs=row_spec(d),
        out_shape=jax.ShapeDtypeStruct((t, d), F32),
        compiler_params=pltpu.CompilerParams(
            dimension_semantics=("arbitrary",), vmem_limit_bytes=_vmem_limit(est)),
        name="ssm_sample_finish",
    )(x2, y, z, w["nw"], w["wout"], g, b)
    return out.reshape(bsz, q, d), new_conv, new_state


def _ssm_weights(w_in, conv_w, conv_b, dt_bias, a_log, d_skip, norm_w, w_out):
    d = w_in.shape[0]
    pad = LANES - SSM_HEADS
    wdt = jnp.pad(w_in[:, D_INNER + CONV_DIM:], ((0, 0), (0, pad))).astype(BF16)
    return dict(
        win=w_in.astype(BF16), wdt=wdt, cw=conv_w, cb=conv_b.reshape(1, CONV_DIM),
        dtb=jnp.pad(dt_bias, (0, pad)).reshape(1, LANES),
        alog=jnp.pad(a_log, (0, pad)).reshape(1, LANES),
        dsk=jnp.repeat(d_skip, SSM_HEAD_DIM).reshape(1, D_INNER),
        nw=norm_w.reshape(1, D_INNER), wout=w_out.astype(BF16))


def kernel(x_prompt, x_sample, cache_k, cache_v, state_conv, state_ssm, state_pool, rel_bias, attn_w_qkv, attn_b_qkv, attn_w_o, attn_b_o, attn_sinks, ssm_w_in, ssm_conv_w, ssm_conv_b, ssm_dt_bias, ssm_a_log, ssm_d, ssm_norm_w, ssm_w_out, pool_w, pool_scale, ffn_w_gate, ffn_w_up, ffn_w_down, ln_g, ln_b):
    xp, xs = x_prompt, x_sample
    d = xp.shape[-1]
    wqkv, wo = attn_w_qkv.astype(BF16), attn_w_o.astype(BF16)
    wg, wu, wd = ffn_w_gate.astype(BF16), ffn_w_up.astype(BF16), ffn_w_down.astype(BF16)
    cache_k_t = jnp.transpose(cache_k, (0, 1, 3, 4, 2))
    cache_v_t = jnp.transpose(cache_v, (0, 1, 3, 4, 2))
    new_cache_t = None
    nk_p, nv_p, nc_p, nh_p, npool_p = [], [], [], [], []
    nc_s, nh_s, npool_s = [], [], []
    for i in range(DEPTH):
        j = i // N_MIXERS
        kind = i % N_MIXERS
        g1 = ln_g[i, 0].reshape(1, d)
        b1 = ln_b[i, 0].reshape(1, d)
        if kind == 0:
            bqkv = attn_b_qkv[j].reshape(1, -1)
            bo = attn_b_o[j].reshape(1, d)
            xp, kp, vp = _attn_prompt(xp, wqkv, bqkv, wo, bo, g1, b1, rel_bias, attn_sinks[j], layer=j)
            xs, *new_cache_t = _attn_sample(xs, cache_k_t, cache_v_t, wqkv, bqkv, wo, bo, g1, b1, rel_bias,
                                            attn_sinks[j], layer=j, prev=new_cache_t)
            xs = xs.reshape(x_sample.shape)
            kv_shape = (-1, WINDOW, N_KV_HEADS, HEAD_DIM)
            nk_p.append(kp.reshape(kv_shape)); nv_p.append(vp.reshape(kv_shape))
        elif kind == 1:
            w = _ssm_weights(ssm_w_in[j], ssm_conv_w[j], ssm_conv_b[j], ssm_dt_bias[j], ssm_a_log[j],
                             ssm_d[j], ssm_norm_w[j], ssm_w_out[j])
            xp, cp, hp = _ssm_prompt(xp, w, g1, b1)
            xs, cs_, hs_ = _ssm_sample(xs, state_conv[j], state_ssm[j].reshape(-1, D_INNER, D_STATE), w, g1, b1)
            st_shape = (-1, SSM_HEADS, SSM_HEAD_DIM, D_STATE)
            nc_p.append(cp); nh_p.append(hp.reshape(st_shape))
            nc_s.append(cs_); nh_s.append(hs_.reshape(st_shape))
        else:
            pw = pool_w[j].astype(BF16)
            psc = pool_scale[j].reshape(1, d)
            xp, pp = _pool_prompt(xp, pw, psc, g1, b1)
            xs, ps_ = _pool_sample(xs, state_pool[j], pw, psc, g1, b1, start=PAST_LEN)
            npool_p.append(pp); npool_s.append(ps_)
        g2 = ln_g[i, 1].reshape(1, d)
        b2 = ln_b[i, 1].reshape(1, d)
        xp = _ffn_ln(xp.reshape(-1, d), wg, wu, wd, g2, b2, layer=i).reshape(xp.shape)
        xs = _ffn_ln(xs.reshape(-1, d), wg, wu, wd, g2, b2, layer=i).reshape(xs.shape)
    nk_s, nv_s = (jnp.transpose(t, (0, 1, 4, 2, 3)) for t in new_cache_t)
    return (xp, xs,
            jnp.stack(nk_p), jnp.stack(nv_p), jnp.stack(nc_p), jnp.stack(nh_p), jnp.stack(npool_p),
            nk_s, nv_s, jnp.stack(nc_s), jnp.stack(nh_s), jnp.stack(npool_s))
```

```python
import functools
import math

import jax
import jax.numpy as jnp
import numpy as np
from jax import lax
from jax.experimental import pallas as pl
from jax.experimental.pallas import tpu as pltpu

DEPTH = 4
N_MIXERS = 3
PAST_LEN = 8192
DEEPNORM_ALPHA = (2 * DEPTH) ** 0.25
LN_EPS = 1e-5

V7X_VMEM_BYTES = 64 * 1024 * 1024
LANES = 128
SUBLANES = 8
MXU_DIM = 256

BF16 = jnp.bfloat16
F32 = jnp.float32


def _vmem_limit(estimate_bytes):
    return int(min(V7X_VMEM_BYTES - 8 * 1024 * 1024, max(32 * 1024 * 1024, estimate_bytes * 3 // 2)))


def _layer_norm_rows(v, g, b):
    mu = jnp.mean(v, axis=-1, keepdims=True)
    d = v - mu
    var = jnp.mean(d * d, axis=-1, keepdims=True)
    return d * lax.rsqrt(var + LN_EPS) * g + b


def _resident(shape):
    nd = len(shape)
    return pl.BlockSpec(shape, lambda *_: (0,) * nd, pipeline_mode=pl.Buffered(1))


def _layer_resident(shape, layer):
    nd = len(shape)
    return pl.BlockSpec((1,) + tuple(shape), lambda *_: (layer,) + (0,) * nd, pipeline_mode=pl.Buffered(1))


def _ffn_chunks(d_ff):
    step = 2 * MXU_DIM
    edges = list(range(0, d_ff, step)) + [d_ff]
    return tuple(zip(edges[:-1], edges[1:]))


def _ffn_kernel(x_ref, wg_ref, wu_ref, wd_ref, g_ref, b_ref, o_ref, *, chunks):
    x = x_ref[...]
    xb = x.astype(BF16)
    acc = jnp.zeros(x.shape, F32)
    for c0, c1 in chunks:
        gate = jnp.dot(xb, wg_ref[0, :, c0:c1], preferred_element_type=F32)
        up = jnp.dot(xb, wu_ref[0, :, c0:c1], preferred_element_type=F32)
        h = (gate * jax.nn.sigmoid(gate)) * up
        acc = acc + jnp.dot(h.astype(BF16), wd_ref[0, c0:c1, :], preferred_element_type=F32)
    o_ref[...] = _layer_norm_rows(DEEPNORM_ALPHA * x + acc, g_ref[...], b_ref[...])


def _ffn_ln(x, wg, wu, wd, g, b, *, layer, tm=512):
    t, d = x.shape
    d_ff = wg.shape[-1]
    tm = min(tm, t)
    assert t % tm == 0
    est = 3 * d * d_ff * 2 + 4 * tm * d * 4 + 4 * tm * d_ff * 4
    return pl.pallas_call(
        functools.partial(_ffn_kernel, chunks=_ffn_chunks(d_ff)),
        grid=(t // tm,),
        in_specs=[
            pl.BlockSpec((tm, d), lambda i: (i, 0)),
            _layer_resident((d, d_ff), layer),
            _layer_resident((d, d_ff), layer),
            _layer_resident((d_ff, d), layer),
            _resident((1, d)),
            _resident((1, d)),
        ],
        out_specs=pl.BlockSpec((tm, d), lambda i: (i, 0)),
        out_shape=jax.ShapeDtypeStruct((t, d), F32),
        compiler_params=pltpu.CompilerParams(
            dimension_semantics=("arbitrary",), vmem_limit_bytes=_vmem_limit(est)),
        name="ffn_ln",
    )(x, wg, wu, wd, g, b)


HEAD_DIM = 64
N_HEADS = 16
N_KV_HEADS = 4
GQA_GROUP = N_HEADS // N_KV_HEADS
WINDOW = 128
REL_BUCKETS = 32
REL_MAX_DIST = 128
Q_DIM = N_HEADS * HEAD_DIM
KV_DIM = N_KV_HEADS * HEAD_DIM
NEG_INF = float("-inf")
CHAIN_BATCH = 2


def _t5_bucket_table(dist):
    n = np.maximum(dist, 0)
    max_exact = REL_BUCKETS // 2
    nf = np.maximum(n, 1).astype(np.float32)
    large = max_exact + (np.log(nf / np.float32(max_exact)) / np.float32(math.log(REL_MAX_DIST / max_exact))
                         * np.float32(REL_BUCKETS - max_exact)).astype(np.int32)
    large = np.minimum(large, REL_BUCKETS - 1)
    bucket = np.where(n < max_exact, n, large)
    valid = (dist >= 0) & (dist < WINDOW)
    return np.where(valid, bucket, -1).astype(np.int32)


def _bias_from_buckets(bucket, relb_ref, head):
    acc = jnp.full(bucket.shape, NEG_INF, F32)
    for bkt in range(REL_BUCKETS):
        acc = jnp.where(bucket == bkt, relb_ref[bkt, head], acc)
    return acc


def _attn_prompt_kernel(x_ref, wqkv_ref, bqkv_ref, wo_ref, bo_ref, g_ref, b_ref, bucket_ref,
                        relb_ref, sink_ref, o_ref, kout_ref, vout_ref,
                        ka_lo, ka_hi, kb_lo, kb_hi, vt, ot, bias_scr, sink_scr, *, tq):
    bi = pl.program_id(0)
    j = pl.program_id(1)
    nj = pl.num_programs(1)
    blk = WINDOW
    half_heads = GQA_GROUP // 2
    kbufs = (ka_lo, ka_hi, kb_lo, kb_hi)

    @pl.when((bi == 0) & (j == 0))
    def _build_tables():
        bucket = bucket_ref[...]
        lane = lax.broadcasted_iota(jnp.int32, (1, 2 * blk), 1)
        for kv in range(N_KV_HEADS):
            for half in range(2):
                ha = kv * GQA_GROUP + half
                hb = ha + half_heads
                pair = kv * 2 + half
                bias_scr[pair, :, 0:blk] = _bias_from_buckets(bucket, relb_ref, ha)
                bias_scr[pair, :, blk:2 * blk] = _bias_from_buckets(bucket, relb_ref, hb)
                sink_scr[pair] = jnp.where(lane < blk, sink_ref[ha], sink_ref[hb])

    @pl.when(j == 0)
    def _no_past():
        for buf in kbufs:
            buf[0:blk, :] = jnp.zeros((blk, KV_DIM), BF16)
        vt[:, 0:blk] = jnp.zeros((KV_DIM, blk), BF16)

    @pl.when(j > 0)
    def _carry():
        for buf in kbufs:
            buf[0:blk, :] = buf[tq:tq + blk, :]
        vt[:, 0:blk] = vt[:, tq:tq + blk]

    x = x_ref[0]
    qkv = jnp.dot(x.astype(BF16), wqkv_ref[0], preferred_element_type=F32) + bqkv_ref[...]
    q = (qkv[:, :Q_DIM] * (HEAD_DIM ** -0.5)).astype(BF16)
    k = qkv[:, Q_DIM:Q_DIM + KV_DIM]
    v = qkv[:, Q_DIM + KV_DIM:]
    k_sw = jnp.concatenate(
        [pltpu.roll(k[:, c * LANES:(c + 1) * LANES], HEAD_DIM, axis=1) for c in range(KV_DIM // LANES)], axis=1)
    lo = (lax.broadcasted_iota(jnp.int32, (1, KV_DIM), 1) % LANES) < HEAD_DIM
    ka_lo[blk:blk + tq, :] = jnp.where(lo, k, 0.0).astype(BF16)
    ka_hi[blk:blk + tq, :] = jnp.where(lo, 0.0, k).astype(BF16)
    kb_lo[blk:blk + tq, :] = jnp.where(lo, k_sw, 0.0).astype(BF16)
    kb_hi[blk:blk + tq, :] = jnp.where(lo, 0.0, k_sw).astype(BF16)
    vt[:, blk:blk + tq] = v.T.astype(BF16)

    @pl.when(j == nj - 1)
    def _emit_cache():
        kout_ref[0] = k[tq - WINDOW:, :]
        vout_ref[0] = v[tq - WINDOW:, :]

    ks = lax.broadcasted_iota(jnp.int32, (blk, 2 * blk), 0)
    qt = lax.broadcasted_iota(jnp.int32, (blk, 2 * blk), 1) % blk
    own = ks <= qt
    chains = [(i, kv, half) for i in range(tq // blk) for kv in range(N_KV_HEADS) for half in range(2)]

    def scores(i, kv, half):
        r0 = i * blk
        c0 = (kv // 2) * LANES
        in_lo = kv % 2 == 0
        if half == 0:
            kk = (ka_lo if in_lo else kb_lo)[r0:r0 + 2 * blk, c0:c0 + LANES]
        else:
            kk = (kb_hi if in_lo else ka_hi)[r0:r0 + 2 * blk, c0:c0 + LANES]
        qa = q[r0:r0 + blk, (2 * kv) * LANES:(2 * kv + 1) * LANES]
        qb = q[r0:r0 + blk, (2 * kv + 1) * LANES:(2 * kv + 2) * LANES]
        q_pair = jnp.concatenate([qa, qb], axis=0)
        return lax.dot_general(kk, q_pair, (((1,), (1,)), ((), ())), preferred_element_type=F32)

    def fold(i, kv, half, s2):
        s_prev = s2[0:blk, :]
        if i == 0:
            s_prev = jnp.where(j == 0, NEG_INF, s_prev)
        return jnp.where(own, s2[blk:2 * blk, :], s_prev) + bias_scr[kv * 2 + half]

    def softmax_batch(batch, s2s):
        sinks = [sink_scr[kv * 2 + half] for _, kv, half in batch]
        ss = [fold(*ch, s2) for ch, s2 in zip(batch, s2s)]
        ms = [jnp.maximum(jnp.max(s, axis=0, keepdims=True), sink) for s, sink in zip(ss, sinks)]
        ps = [jnp.exp(s - m) for s, m in zip(ss, ms)]
        invs = [1.0 / (jnp.sum(p, axis=0, keepdims=True) + jnp.exp(sink - m)) for p, m, sink in zip(ps, ms, sinks)]
        p2s = [jnp.concatenate([jnp.where(own, 0.0, p), jnp.where(own, p, 0.0)], axis=0).astype(BF16) for p in ps]
        return list(zip(p2s, invs))

    def weighted_values(i, kv, half, p2, inv):
        r0 = i * blk
        ha = kv * GQA_GROUP + half
        hb = ha + half_heads
        v_t = vt[kv * HEAD_DIM:(kv + 1) * HEAD_DIM, r0:r0 + 2 * blk]
        o_t = jnp.dot(v_t, p2, preferred_element_type=F32) * inv
        ot[ha * HEAD_DIM:(ha + 1) * HEAD_DIM, r0:r0 + blk] = o_t[:, 0:blk]
        ot[hb * HEAD_DIM:(hb + 1) * HEAD_DIM, r0:r0 + blk] = o_t[:, blk:2 * blk]

    batches = [chains[c:c + CHAIN_BATCH] for c in range(0, len(chains), CHAIN_BATCH)]
    s_next = [scores(*ch) for ch in batches[0]]
    prev = []
    for bi, batch in enumerate(batches):
        s_cur = s_next
        if bi + 1 < len(batches):
            s_next = [scores(*ch) for ch in batches[bi + 1]]
        probs = softmax_batch(batch, s_cur)
        for ch, pr in prev:
            weighted_values(*ch, *pr)
        prev = list(zip(batch, probs))
    for ch, pr in prev:
        weighted_values(*ch, *pr)

    o = ot[...].T.astype(BF16)
    y = jnp.dot(o, wo_ref[0], preferred_element_type=F32) + bo_ref[...]
    o_ref[0] = _layer_norm_rows(DEEPNORM_ALPHA * x + y, g_ref[...], b_ref[...])


def _attn_prompt(x, wqkv, bqkv, wo, bo, g, b, rel_bias, sinks, *, layer, tq=512):
    bsz, l, d = x.shape
    tq = min(tq, l)
    assert l % tq == 0 and tq % WINDOW == 0
    qi = np.arange(WINDOW)[None, :]
    si = np.arange(WINDOW)[:, None]
    bucket = jnp.asarray(_t5_bucket_table(np.where(si <= qi, qi - si, qi + WINDOW - si)))
    qkv_dim = wqkv.shape[-1]
    smem = pl.BlockSpec(memory_space=pltpu.SMEM)
    est = (d * qkv_dim + Q_DIM * d) * 2 + 5 * tq * d * 4 + N_HEADS * WINDOW * 2 * WINDOW * 4 + 3 * tq * qkv_dim * 4
    return pl.pallas_call(
        functools.partial(_attn_prompt_kernel, tq=tq),
        grid=(bsz, l // tq),
        in_specs=[
            pl.BlockSpec((1, tq, d), lambda bi, j: (bi, j, 0)),
            _layer_resident((d, qkv_dim), layer), _resident((1, qkv_dim)),
            _layer_resident((Q_DIM, d), layer), _resident((1, d)),
            _resident((1, d)), _resident((1, d)),
            _resident((WINDOW, WINDOW)),
            smem, smem,
        ],
        out_specs=[
            pl.BlockSpec((1, tq, d), lambda bi, j: (bi, j, 0)),
            pl.BlockSpec((1, WINDOW, KV_DIM), lambda bi, j: (bi, 0, 0)),
            pl.BlockSpec((1, WINDOW, KV_DIM), lambda bi, j: (bi, 0, 0)),
        ],
        out_shape=[
            jax.ShapeDtypeStruct((bsz, l, d), F32),
            jax.ShapeDtypeStruct((bsz, WINDOW, KV_DIM), F32),
            jax.ShapeDtypeStruct((bsz, WINDOW, KV_DIM), F32),
        ],
        scratch_shapes=[pltpu.VMEM((WINDOW + tq, KV_DIM), BF16)] * 4 + [
            pltpu.VMEM((KV_DIM, WINDOW + tq), BF16),
            pltpu.VMEM((Q_DIM, tq), F32),
            pltpu.VMEM((N_HEADS // 2, WINDOW, 2 * WINDOW), F32),
            pltpu.VMEM((N_HEADS // 2, 1, 2 * WINDOW), F32),
        ],
        compiler_params=pltpu.CompilerParams(
            dimension_semantics=("arbitrary", "arbitrary"), vmem_limit_bytes=_vmem_limit(est)),
        name="attn_prompt",
    )(x, wqkv, bqkv, wo, bo, g, b, bucket, rel_bias, sinks)


SAMPLE_KEYS = 2 * WINDOW
SAMPLE_CHAIN_BATCH = 8


def _attn_sample_kernel(x_ref, ck_ref, cv_ref, wqkv_ref, bqkv_ref, wo_ref, bo_ref, g_ref, b_ref, bucket_ref,
                        relb_ref, sink_ref, *rest, bb, l, n_prev):
    o_ref, nk_ref, nv_ref, qbuf, obuf, bias_scr = rest[n_prev:]

    @pl.when(pl.program_id(0) == 0)
    def _init():
        bucket = bucket_ref[...]
        for h in range(N_HEADS):
            kv, g_ = divmod(h, GQA_GROUP)
            bias_scr[kv, g_ * l:(g_ + 1) * l, :] = _bias_from_buckets(bucket, relb_ref, h)

    x = x_ref[...]
    qkv = jnp.dot(x.astype(BF16), wqkv_ref[0], preferred_element_type=F32) + bqkv_ref[...]
    qbuf[...] = qkv[:, :Q_DIM] * (HEAD_DIM ** -0.5)
    k_new = qkv[:, Q_DIM:Q_DIM + KV_DIM]
    v_new = qkv[:, Q_DIM + KV_DIM:]

    sink_cols = []
    for kv in range(N_KV_HEADS):
        sink_cols.append(jnp.concatenate(
            [jnp.full((l, 1), sink_ref[kv * GQA_GROUP + g_], F32) for g_ in range(GQA_GROUP)], axis=0))

    kn_t = k_new.T
    vn_t = v_new.T
    lane = lax.broadcasted_iota(jnp.int32, (HEAD_DIM, WINDOW), 1)
    pad = jnp.zeros((WINDOW - l, KV_DIM), F32)

    def shifted(old_t, new_t, i):
        kept = pltpu.roll(old_t, WINDOW - l, axis=1)
        new = pltpu.roll(new_t, (WINDOW - l - i * l) % WINDOW, axis=1)
        return jnp.where(lane >= WINDOW - l, new, kept)

    def keys_values(i):
        r0 = i * l
        for kv in range(N_KV_HEADS):
            hs = slice(kv * HEAD_DIM, (kv + 1) * HEAD_DIM)
            nk_ref[0, i, kv] = shifted(ck_ref[0, i, kv], kn_t[hs, :], i)
            nv_ref[0, i, kv] = shifted(cv_ref[0, i, kv], vn_t[hs, :], i)
        kn = jnp.concatenate([k_new[r0:r0 + l, :], pad], axis=0).astype(BF16)
        vn = jnp.concatenate([v_new[r0:r0 + l, :], pad], axis=0).astype(BF16)
        return kn, vn

    def scores(i, kv, kn):
        r0 = i * l
        c0 = kv * HEAD_DIM
        qs = jnp.concatenate(
            [qbuf[r0:r0 + l, (kv * GQA_GROUP + g_) * HEAD_DIM:(kv * GQA_GROUP + g_ + 1) * HEAD_DIM]
             for g_ in range(GQA_GROUP)], axis=0).astype(BF16)
        s_cache = jnp.dot(qs, ck_ref[0, i, kv].astype(BF16), preferred_element_type=F32)
        s_new = lax.dot_general(qs, kn[:, c0:c0 + HEAD_DIM], (((1,), (1,)), ((), ())), preferred_element_type=F32)
        return jnp.concatenate([s_cache, s_new], axis=1)

    def weighted_values(i, kv, p, vn):
        c0 = kv * HEAD_DIM
        pb = p.astype(BF16)
        o_cache = lax.dot_general(pb[:, :WINDOW], cv_ref[0, i, kv].astype(BF16), (((1,), (1,)), ((), ())),
                                  preferred_element_type=F32)
        return o_cache + jnp.dot(pb[:, WINDOW:], vn[:, c0:c0 + HEAD_DIM], preferred_element_type=F32)

    for i0 in range(0, bb, SAMPLE_CHAIN_BATCH):
        items = range(i0, min(bb, i0 + SAMPLE_CHAIN_BATCH))
        kvs = {i: keys_values(i) for i in items}
        chains = [(i, kv) for i in items for kv in range(N_KV_HEADS)]
        s_all = [scores(i, kv, kvs[i][0]) + bias_scr[kv] for i, kv in chains]
        m_all = [jnp.maximum(jnp.max(s, axis=-1, keepdims=True), sink_cols[kv]) for s, (i, kv) in zip(s_all, chains)]
        p_all = [jnp.exp(s - m) for s, m in zip(s_all, m_all)]
        den_all = [jnp.sum(p, axis=-1, keepdims=True) + jnp.exp(sink_cols[kv] - m)
                   for p, m, (i, kv) in zip(p_all, m_all, chains)]
        o_all = [weighted_values(i, kv, p, kvs[i][1]) for p, (i, kv) in zip(p_all, chains)]
        for o, den, (i, kv) in zip(o_all, den_all, chains):
            o = o / den
            for g_ in range(GQA_GROUP):
                h = kv * GQA_GROUP + g_
                obuf[i * l:(i + 1) * l, h * HEAD_DIM:(h + 1) * HEAD_DIM] = o[g_ * l:(g_ + 1) * l, :]

    y = jnp.dot(obuf[...].astype(BF16), wo_ref[0], preferred_element_type=F32) + bo_ref[...]
    o_ref[...] = _layer_norm_rows(DEEPNORM_ALPHA * x + y, g_ref[...], b_ref[...])


def _attn_sample(x, cache_k_t, cache_v_t, wqkv, bqkv, wo, bo, g, b, rel_bias, sinks, *, layer, prev=None, bb=16):
    bsz, l, d = x.shape
    assert bsz % bb == 0 and l == SUBLANES and bb * l == WINDOW
    ti = np.arange(l)[:, None]
    ci = np.arange(SAMPLE_KEYS)[None, :]
    table = _t5_bucket_table(ti + WINDOW - ci)
    table = np.where(ci < WINDOW + l, table, -1).astype(np.int32)
    bucket = jnp.asarray(table)
    qkv_dim = wqkv.shape[-1]
    rows = bb * l
    smem = pl.BlockSpec(memory_space=pltpu.SMEM)
    cache_spec = pl.BlockSpec((1, bb, N_KV_HEADS, HEAD_DIM, WINDOW), lambda i: (layer, i, 0, 0, 0))
    est = (d * qkv_dim + Q_DIM * d) * 2 + 8 * bb * WINDOW * KV_DIM * 4 + 8 * rows * d * 4
    in_specs = [
        pl.BlockSpec((rows, d), lambda i: (i, 0)),
        cache_spec, cache_spec,
        _layer_resident((d, qkv_dim), layer), _resident((1, qkv_dim)),
        _layer_resident((Q_DIM, d), layer), _resident((1, d)),
        _resident((1, d)), _resident((1, d)),
        _resident((l, SAMPLE_KEYS)),
        smem, smem,
    ]
    args = [x.reshape(bsz * l, d), cache_k_t, cache_v_t, wqkv, bqkv, wo, bo, g, b, bucket, rel_bias, sinks]
    aliases = {}
    if prev is not None:
        aliases = {len(args): 1, len(args) + 1: 2}
        in_specs += [pl.BlockSpec(memory_space=pl.ANY)] * 2
        args += list(prev)
    return pl.pallas_call(
        functools.partial(_attn_sample_kernel, bb=bb, l=l, n_prev=0 if prev is None else 2),
        grid=(bsz // bb,),
        in_specs=in_specs,
        out_specs=[pl.BlockSpec((rows, d), lambda i: (i, 0)), cache_spec, cache_spec],
        out_shape=[
            jax.ShapeDtypeStruct((bsz * l, d), F32),
            jax.ShapeDtypeStruct(cache_k_t.shape, F32),
            jax.ShapeDtypeStruct(cache_v_t.shape, F32),
        ],
        scratch_shapes=[
            pltpu.VMEM((rows, Q_DIM), F32),
            pltpu.VMEM((rows, Q_DIM), F32),
            pltpu.VMEM((N_KV_HEADS, GQA_GROUP * l, SAMPLE_KEYS), F32),
        ],
        input_output_aliases=aliases,
        compiler_params=pltpu.CompilerParams(
            dimension_semantics=("arbitrary",), vmem_limit_bytes=_vmem_limit(est)),
        name="attn_sample",
    )(*args)


POOL_WINDOWS = (2, 4, 8, 16)
POOL_PAD = max(POOL_WINDOWS)
POOL_STATE_LEN = POOL_PAD - 1


def _pool_mix(window_sum, x, cnt_of, pw_ref, scale):
    gd = x.shape[-1] // len(POOL_WINDOWS)
    outs = []
    for g, w in enumerate(POOL_WINDOWS):
        diff = window_sum(g, w) / cnt_of(w) - x[:, g * gd:(g + 1) * gd]
        outs.append(jnp.dot(diff.astype(BF16), pw_ref[g], preferred_element_type=F32))
    return jnp.concatenate(outs, axis=-1) * scale


def _pool_prompt_kernel(x_ref, pw_ref, sc_ref, g_ref, b_ref, o_ref, st_ref, buf, *, tm):
    j = pl.program_id(1)
    nj = pl.num_programs(1)
    d = x_ref.shape[-1]
    gd = d // len(POOL_WINDOWS)

    @pl.when(j == 0)
    def _no_past():
        buf[0:POOL_PAD, :] = jnp.zeros((POOL_PAD, d), F32)

    @pl.when(j > 0)
    def _carry():
        buf[0:POOL_PAD, :] = buf[tm:tm + POOL_PAD, :]

    x = x_ref[0]
    buf[POOL_PAD:POOL_PAD + tm, :] = x

    def window_sum(g, w):
        acc = x[:, g * gd:(g + 1) * gd]
        for s in range(1, w):
            acc = acc + buf[POOL_PAD - s:POOL_PAD - s + tm, g * gd:(g + 1) * gd]
        return acc

    pos1 = j * tm + lax.broadcasted_iota(jnp.int32, (tm, 1), 0) + 1

    def cnt_of(w):
        return jnp.minimum(pos1, w).astype(F32)

    y = _pool_mix(window_sum, x, cnt_of, pw_ref, sc_ref[...])
    o_ref[0] = _layer_norm_rows(DEEPNORM_ALPHA * x + y, g_ref[...], b_ref[...])

    @pl.when(j == nj - 1)
    def _emit_state():
        st_ref[0] = buf[tm + 1:tm + POOL_PAD, :]


def _pool_prompt(x, pw, scale, g, b, *, tm=512):
    bsz, l, d = x.shape
    tm = min(tm, l)
    assert l % tm == 0 and tm >= POOL_PAD
    ng, gd = pw.shape[0], pw.shape[1]
    est = 6 * tm * d * 4 + ng * gd * gd * 2
    return pl.pallas_call(
        functools.partial(_pool_prompt_kernel, tm=tm),
        grid=(bsz, l // tm),
        in_specs=[
            pl.BlockSpec((1, tm, d), lambda bi, j: (bi, j, 0)),
            _resident((ng, gd, gd)), _resident((1, d)), _resident((1, d)), _resident((1, d)),
        ],
        out_specs=[
            pl.BlockSpec((1, tm, d), lambda bi, j: (bi, j, 0)),
            pl.BlockSpec((1, POOL_STATE_LEN, d), lambda bi, j: (bi, 0, 0)),
        ],
        out_shape=[
            jax.ShapeDtypeStruct((bsz, l, d), F32),
            jax.ShapeDtypeStruct((bsz, POOL_STATE_LEN, d), F32),
        ],
        scratch_shapes=[pltpu.VMEM((POOL_PAD + tm, d), F32)],
        compiler_params=pltpu.CompilerParams(
            dimension_semantics=("arbitrary", "arbitrary"), vmem_limit_bytes=_vmem_limit(est)),
        name="pool_prompt",
    )(x, pw, scale, g, b)


def _pool_sample_kernel(x_ref, st_ref, pw_ref, sc_ref, g_ref, b_ref, o_ref, nst_ref, buf, *, bb, l):
    d = x_ref.shape[-1]
    gd = d // len(POOL_WINDOWS)
    x3 = x_ref[...]
    buf[:, 1:POOL_PAD, :] = st_ref[...]
    buf[:, POOL_PAD:POOL_PAD + l, :] = x3
    x = x3.reshape(bb * l, d)

    def window_sum(g, w):
        acc = x3[:, :, g * gd:(g + 1) * gd]
        for s in range(1, w):
            acc = acc + buf[:, POOL_PAD - s:POOL_PAD - s + l, g * gd:(g + 1) * gd]
        return acc.reshape(bb * l, gd)

    y = _pool_mix(window_sum, x, lambda w: float(w), pw_ref, sc_ref[...])
    o_ref[...] = _layer_norm_rows(DEEPNORM_ALPHA * x + y, g_ref[...], b_ref[...]).reshape(bb, l, d)
    nst_ref[...] = buf[:, l + 1:l + POOL_PAD, :]


def _pool_sample(x, state, pw, scale, g, b, *, start, bb=16):
    bsz, l, d = x.shape
    assert bsz % bb == 0 and l == SUBLANES and start + 1 >= POOL_PAD
    ng, gd = pw.shape[0], pw.shape[1]
    est = 8 * bb * (POOL_PAD + l) * d * 4 + ng * gd * gd * 2
    return pl.pallas_call(
        functools.partial(_pool_sample_kernel, bb=bb, l=l),
        grid=(bsz // bb,),
        in_specs=[
            pl.BlockSpec((bb, l, d), lambda i: (i, 0, 0)),
            pl.BlockSpec((bb, POOL_STATE_LEN, d), lambda i: (i, 0, 0)),
            _resident((ng, gd, gd)), _resident((1, d)), _resident((1, d)), _resident((1, d)),
        ],
        out_specs=[
            pl.BlockSpec((bb, l, d), lambda i: (i, 0, 0)),
            pl.BlockSpec((bb, POOL_STATE_LEN, d), lambda i: (i, 0, 0)),
        ],
        out_shape=[
            jax.ShapeDtypeStruct((bsz, l, d), F32),
            jax.ShapeDtypeStruct((bsz, POOL_STATE_LEN, d), F32),
        ],
        scratch_shapes=[pltpu.VMEM((bb, POOL_PAD + l, d), F32)],
        compiler_params=pltpu.CompilerParams(
            dimension_semantics=("arbitrary",), vmem_limit_bytes=_vmem_limit(est)),
        name="pool_sample",
    )(x, state, pw, scale, g, b)


D_INNER = 2048
SSM_HEAD_DIM = 64
SSM_HEADS = D_INNER // SSM_HEAD_DIM
SSM_GROUPS = 4
SSM_HPG = SSM_HEADS // SSM_GROUPS
D_STATE = 128
CONV_WIDTH = 4
GBN = SSM_GROUPS * D_STATE
CONV_DIM = D_INNER + 2 * GBN
GROUP_INNER = D_INNER // SSM_GROUPS
RMS_EPS = 1e-5
CONV_PAD = SUBLANES
SSM_ROWS = 128


def _split_bf16(v, parts=3):
    out = []
    r = v
    for _ in range(parts):
        p = r.astype(BF16)
        out.append(p)
        r = r - p.astype(F32)
    return out


def _expand_heads(v, e):
    return sum(jnp.dot(p, e, preferred_element_type=F32) for p in _split_bf16(v, parts=2))


def _ssm_in_proj(xb, win_ref, wdt_ref, c0, c1):
    main = win_ref.shape[1]
    if c0 >= main:
        return jnp.dot(xb, wdt_ref[:, c0 - main:c1 - main], preferred_element_type=F32)
    assert c1 <= main
    return jnp.dot(xb, win_ref[:, c0:c1], preferred_element_type=F32)


def _ssm_prep(proj, xp_buf, conv_state, dtb_ref, alog_ref, tril_ref, *, nseq, q):
    rows = nseq * q
    xbc_pre = proj(D_INNER, D_INNER + CONV_DIM)
    dt_pre = proj(D_INNER + CONV_DIM, D_INNER + CONV_DIM + LANES)
    if conv_state is not None:
        xp_buf[:, CONV_PAD - (CONV_WIDTH - 1):CONV_PAD, :] = conv_state
    xp_buf[:, CONV_PAD:CONV_PAD + q, :] = xbc_pre.reshape(nseq, q, CONV_DIM)
    new_conv = xp_buf[:, q + CONV_PAD - (CONV_WIDTH - 1):q + CONV_PAD, :]

    dtv = dt_pre + dtb_ref[...]
    dt = jnp.maximum(dtv, 0.0) + jnp.log1p(jnp.exp(-jnp.abs(dtv)))
    a = -jnp.exp(alog_ref[...])
    tril = tril_ref[...]
    acum = sum(jnp.dot(tril, p, preferred_element_type=F32) for p in _split_bf16(dt * a))
    a3 = acum.reshape(nseq, q, LANES)
    alast = jnp.broadcast_to(a3[:, q - 1:q, :], (nseq, q, LANES)).reshape(rows, LANES)
    return dict(acum=acum, acum_t=acum.T, dt_t=dt.T, causal=tril > 0.5,
                exp_a=jnp.exp(acum), dd=jnp.exp(alast - acum) * dt, new_conv=new_conv)


def _conv_silu(xp_buf, cw_ref, cb_ref, c0, c1, *, nseq, q):
    acc = cb_ref[:, c0:c1].reshape(1, 1, c1 - c0)
    for jj in range(CONV_WIDTH):
        off = CONV_PAD - (CONV_WIDTH - 1) + jj
        acc = acc + xp_buf[:, off:off + q, c0:c1] * cw_ref[jj:jj + 1, c0:c1].reshape(1, 1, c1 - c0)
    acc = acc.reshape(nseq * q, c1 - c0)
    return acc * jax.nn.sigmoid(acc)


def _ssm_group_diag(g, f, proj, prep, xp_buf, cw_ref, cb_ref, ybuf, *, nseq, q):
    f["z"] = proj(g * GROUP_INNER, (g + 1) * GROUP_INNER)
    conv = functools.partial(_conv_silu, xp_buf, cw_ref, cb_ref, nseq=nseq, q=q)
    half = GROUP_INNER // 2
    xs_lo = conv(g * GROUP_INNER, g * GROUP_INNER + half)
    yield
    xs_hi = conv(g * GROUP_INNER + half, (g + 1) * GROUP_INNER)
    yield
    xs = f["xs"] = jnp.concatenate([xs_lo, xs_hi], axis=-1)
    bm = f["bm"] = conv(D_INNER + g * D_STATE, D_INNER + (g + 1) * D_STATE)
    cm = f["cm"] = conv(D_INNER + GBN + g * D_STATE, D_INNER + GBN + (g + 1) * D_STATE)
    yield
    cb = lax.dot_general(cm.astype(BF16), bm.astype(BF16), (((1,), (1,)), ((), ())), preferred_element_type=F32)
    acum, acum_t, dt_t, causal = prep["acum"], prep["acum_t"], prep["dt_t"], prep["causal"]
    for r in range(SSM_HPG):
        h = g * SSM_HPG + r
        seg = acum[:, h:h + 1] - acum_t[h:h + 1, :]
        lm = jnp.exp(jnp.where(causal, seg, NEG_INF))
        w = (cb * lm * dt_t[h:h + 1, :]).astype(BF16)
        xh = xs[:, r * SSM_HEAD_DIM:(r + 1) * SSM_HEAD_DIM].astype(BF16)
        ybuf[:, h * SSM_HEAD_DIM:(h + 1) * SSM_HEAD_DIM] = jnp.dot(w, xh, preferred_element_type=F32)
        yield


def _ssm_group_expand(g, f, prep, e_ref):
    e = e_ref[:, g * GROUP_INNER:(g + 1) * GROUP_INNER]
    f["exp_a"] = _expand_heads(prep["exp_a"], e)
    yield
    f["xdd"] = (f["xs"] * _expand_heads(prep["dd"], e)).astype(BF16)
    yield


def _interleave(*gens):
    live = list(gens)
    while live:
        for gen in list(live):
            try:
                next(gen)
            except StopIteration:
                live.remove(gen)


def _gate_norm(y, z, nw):
    y = y * (z * jax.nn.sigmoid(z))
    return y * lax.rsqrt(jnp.mean(y * y, axis=-1, keepdims=True) + RMS_EPS) * nw


def _ssm_prompt_kernel(x_ref, win_ref, wdt_ref, cw_ref, cb_ref, dtb_ref, alog_ref, dsk_ref, nw_ref, wout_ref, e_ref,
                       tril_ref, g_ref, b_ref, o_ref, cout_ref, sout_ref, xp_buf, ht, ybuf):
    j = pl.program_id(1)
    nj = pl.num_programs(1)
    q = SSM_ROWS

    @pl.when(j == 0)
    def _no_past():
        xp_buf[:, 0:CONV_PAD, :] = jnp.zeros((1, CONV_PAD, CONV_DIM), F32)
        ht[...] = jnp.zeros(ht.shape, F32)

    @pl.when(j > 0)
    def _carry():
        xp_buf[:, 0:CONV_PAD, :] = xp_buf[:, q:q + CONV_PAD, :]

    x = x_ref[0]
    xb = x.astype(BF16)
    zx = jnp.dot(xb, win_ref[...], preferred_element_type=F32)
    dt_pre = jnp.dot(xb, wdt_ref[...], preferred_element_type=F32)
    z = zx[:, :D_INNER]

    def proj(c0, c1):
        return dt_pre if c0 >= D_INNER + CONV_DIM else zx[:, c0:c1]

    prep = _ssm_prep(proj, xp_buf, None, dtb_ref, alog_ref, tril_ref, nseq=1, q=q)
    xbc = _conv_silu(xp_buf, cw_ref, cb_ref, 0, CONV_DIM, nseq=1, q=q)
    xs = xbc[:, :D_INNER]
    acum, acum_t, dt_t, causal = prep["acum"], prep["acum_t"], prep["dt_t"], prep["causal"]
    for g in range(SSM_GROUPS):
        bg = xbc[:, D_INNER + g * D_STATE:D_INNER + (g + 1) * D_STATE].astype(BF16)
        cg = xbc[:, D_INNER + GBN + g * D_STATE:D_INNER + GBN + (g + 1) * D_STATE].astype(BF16)
        cb = lax.dot_general(cg, bg, (((1,), (1,)), ((), ())), preferred_element_type=F32)
        for r in range(SSM_HPG):
            h = g * SSM_HPG + r
            seg = acum[:, h:h + 1] - acum_t[h:h + 1, :]
            lm = jnp.exp(jnp.where(causal, seg, NEG_INF))
            w = (cb * lm * dt_t[h:h + 1, :]).astype(BF16)
            xh = xs[:, h * SSM_HEAD_DIM:(h + 1) * SSM_HEAD_DIM].astype(BF16)
            ybuf[:, h * SSM_HEAD_DIM:(h + 1) * SSM_HEAD_DIM] = jnp.dot(w, xh, preferred_element_type=F32)

    e = e_ref[...]
    exp_a = _expand_heads(prep["exp_a"], e)
    xdd = (xs * _expand_heads(prep["dd"], e)).astype(BF16)
    decay = exp_a[q - 1:q, :]
    y_off = []
    for g in range(SSM_GROUPS):
        sl = slice(g * GROUP_INNER, (g + 1) * GROUP_INNER)
        bg = xbc[:, D_INNER + g * D_STATE:D_INNER + (g + 1) * D_STATE]
        cg = xbc[:, D_INNER + GBN + g * D_STATE:D_INNER + GBN + (g + 1) * D_STATE].astype(BF16)
        hg = ht[:, sl]
        y_off.append(jnp.dot(cg, hg.astype(BF16), preferred_element_type=F32))
        ht[:, sl] = hg * decay[:, sl] + jnp.dot(bg.T.astype(BF16), xdd[:, sl], preferred_element_type=F32)
    y = ybuf[...] + exp_a * jnp.concatenate(y_off, axis=-1) + dsk_ref[...] * xs
    y = jnp.concatenate(
        [_gate_norm(y[:, g * GROUP_INNER:(g + 1) * GROUP_INNER], z[:, g * GROUP_INNER:(g + 1) * GROUP_INNER],
                    nw_ref[:, g * GROUP_INNER:(g + 1) * GROUP_INNER]) for g in range(SSM_GROUPS)], axis=-1)
    out = jnp.dot(y.astype(BF16), wout_ref[...], preferred_element_type=F32)
    o_ref[0] = _layer_norm_rows(DEEPNORM_ALPHA * x + out, g_ref[...], b_ref[...])

    @pl.when(j == nj - 1)
    def _emit_state():
        cout_ref[...] = prep["new_conv"]
        sout_ref[0] = ht[...].T


def _ssm_consts(nseq, q):
    rows = nseq * q
    r = np.arange(rows)
    tril = ((r[:, None] >= r[None, :]) & (r[:, None] // q == r[None, :] // q)).astype(np.float32)
    e = np.zeros((LANES, D_INNER), np.float32)
    e[np.arange(D_INNER) // SSM_HEAD_DIM, np.arange(D_INNER)] = 1.0
    return jnp.asarray(tril, BF16), jnp.asarray(e, BF16)


def _ssm_weight_specs(d):
    return [
        pl.BlockSpec((d, D_INNER + CONV_DIM), lambda *_: (0, 0), pipeline_mode=pl.Buffered(1)),
        _resident((d, LANES)), _resident((CONV_WIDTH, CONV_DIM)), _resident((1, CONV_DIM)),
        _resident((1, LANES)), _resident((1, LANES)),
    ]


def _ssm_prompt(x, w, g, b):
    bsz, l, d = x.shape
    q = SSM_ROWS
    assert l % q == 0
    tril, e = _ssm_consts(1, q)
    win_cols = D_INNER + CONV_DIM + LANES
    est = (d * win_cols + D_INNER * d) * 2 + LANES * D_INNER * 4 + 12 * q * D_INNER * 4 + 3 * q * win_cols * 4
    return pl.pallas_call(
        _ssm_prompt_kernel,
        grid=(bsz, l // q),
        in_specs=[pl.BlockSpec((1, q, d), lambda bi, j: (bi, j, 0))] + _ssm_weight_specs(d) + [
            _resident((1, D_INNER)), _resident((1, D_INNER)), _resident((D_INNER, d)),
            _resident((LANES, D_INNER)), _resident((q, q)), _resident((1, d)), _resident((1, d)),
        ],
        out_specs=[
            pl.BlockSpec((1, q, d), lambda bi, j: (bi, j, 0)),
            pl.BlockSpec((1, CONV_WIDTH - 1, CONV_DIM), lambda bi, j: (bi, 0, 0)),
            pl.BlockSpec((1, D_INNER, D_STATE), lambda bi, j: (bi, 0, 0)),
        ],
        out_shape=[
            jax.ShapeDtypeStruct((bsz, l, d), F32),
            jax.ShapeDtypeStruct((bsz, CONV_WIDTH - 1, CONV_DIM), F32),
            jax.ShapeDtypeStruct((bsz, D_INNER, D_STATE), F32),
        ],
        scratch_shapes=[
            pltpu.VMEM((1, CONV_PAD + q, CONV_DIM), F32),
            pltpu.VMEM((D_STATE, D_INNER), F32),
            pltpu.VMEM((q, D_INNER), F32),
        ],
        compiler_params=pltpu.CompilerParams(
            dimension_semantics=("arbitrary", "arbitrary"), vmem_limit_bytes=_vmem_limit(est)),
        name="ssm_prompt",
    )(x, w["win"], w["wdt"], w["cw"], w["cb"], w["dtb"], w["alog"], w["dsk"], w["nw"], w["wout"], e, tril, g, b)


def _ssm_sample_front_kernel(x_ref, cst_ref, win_ref, wdt_ref, cw_ref, cb_ref, dtb_ref, alog_ref, dsk_ref, e_ref, tril_ref,
                             y_ref, z_ref, ea_ref, xdd_ref, c_ref, bt_ref, cout_ref, xp_buf, ybuf, *, nseq, q):
    xb = x_ref[...].astype(BF16)

    def proj(c0, c1):
        return _ssm_in_proj(xb, win_ref, wdt_ref, c0, c1)

    prep = _ssm_prep(proj, xp_buf, cst_ref[...], dtb_ref, alog_ref, tril_ref, nseq=nseq, q=q)
    cout_ref[...] = prep["new_conv"]
    for g in range(SSM_GROUPS):
        sl = slice(g * GROUP_INNER, (g + 1) * GROUP_INNER)
        f = {}
        _interleave(_ssm_group_diag(g, f, proj, prep, xp_buf, cw_ref, cb_ref, ybuf, nseq=nseq, q=q))
        _interleave(_ssm_group_expand(g, f, prep, e_ref))
        y_ref[:, sl] = ybuf[:, sl] + dsk_ref[:, sl] * f["xs"]
        z_ref[:, sl] = f["z"]
        ea_ref[:, sl] = f["exp_a"]
        xdd_ref[:, sl] = f["xdd"]
        c_ref[:, g * D_STATE:(g + 1) * D_STATE] = f["cm"]
        bt_ref[g * D_STATE:(g + 1) * D_STATE, :] = f["bm"].T.astype(BF16)


def _ssm_sample_state_kernel(st_ref, y_ref, ea_ref, c_ref, xdd_ref, bt_ref, o_ref, nst_ref, *, bb, q, nseq):
    ea = ea_ref[...]
    xdd = xdd_ref[...]
    first_seq = (pl.program_id(0) % (nseq // bb)) * bb
    lane_seq = lax.broadcasted_iota(jnp.int32, (D_STATE, nseq * q), 1) // q
    for i in range(bb):
        r0 = i * q
        ht = st_ref[i].T
        decay = ea[r0 + q - 1:r0 + q, :]
        y_parts = []
        new_parts = []
        for g in range(SSM_GROUPS):
            sl = slice(g * GROUP_INNER, (g + 1) * GROUP_INNER)
            cg = c_ref[r0:r0 + q, g * D_STATE:(g + 1) * D_STATE].astype(BF16)
            hg = ht[:, sl]
            y_parts.append(jnp.dot(cg, hg.astype(BF16), preferred_element_type=F32))
            btg = bt_ref[g * D_STATE:(g + 1) * D_STATE, :]
            btg = jnp.where(lane_seq == first_seq + i, btg, jnp.zeros_like(btg))
            new_parts.append(hg * decay[:, sl] + jnp.dot(btg, xdd[:, sl], preferred_element_type=F32))
        o_ref[r0:r0 + q, :] = y_ref[r0:r0 + q, :] + ea[r0:r0 + q, :] * jnp.concatenate(y_parts, axis=-1)
        nst_ref[i] = jnp.concatenate(new_parts, axis=-1).T


def _ssm_sample_finish_kernel(x_ref, y_ref, z_ref, nw_ref, wout_ref, g_ref, b_ref, o_ref):
    x = x_ref[...]
    out = jnp.zeros(x.shape, F32)
    for g in range(SSM_GROUPS):
        sl = slice(g * GROUP_INNER, (g + 1) * GROUP_INNER)
        y = _gate_norm(y_ref[:, sl], z_ref[:, sl], nw_ref[:, sl])
        out = out + jnp.dot(y.astype(BF16), wout_ref[sl, :], preferred_element_type=F32)
    o_ref[...] = _layer_norm_rows(DEEPNORM_ALPHA * x + out, g_ref[...], b_ref[...])


def _ssm_sample(x, conv_state, ssm_state, w, g, b, *, bb_state=4):
    bsz, q, d = x.shape
    t = bsz * q
    nseq = SSM_ROWS // q
    rows = SSM_ROWS
    assert q == SUBLANES and bsz % nseq == 0 and nseq % bb_state == 0
    tril, e = _ssm_consts(nseq, q)
    win_cols = D_INNER + CONV_DIM + LANES
    x2 = x.reshape(t, d)
    row_spec = lambda c: pl.BlockSpec((rows, c), lambda i: (i, 0))
    est = d * win_cols * 2 + LANES * D_INNER * 4 + 16 * rows * D_INNER * 4 + 3 * rows * win_cols * 4
    ydiag, z, exp_a, xdd, cmat, bt, new_conv = pl.pallas_call(
        functools.partial(_ssm_sample_front_kernel, nseq=nseq, q=q),
        grid=(t // rows,),
        in_specs=[row_spec(d), pl.BlockSpec((nseq, CONV_WIDTH - 1, CONV_DIM), lambda i: (i, 0, 0))]
        + _ssm_weight_specs(d) + [_resident((1, D_INNER)), _resident((LANES, D_INNER)), _resident((rows, rows))],
        out_specs=[row_spec(D_INNER), row_spec(D_INNER), row_spec(D_INNER), row_spec(D_INNER), row_spec(GBN),
                   pl.BlockSpec((GBN, rows), lambda i: (i, 0)),
                   pl.BlockSpec((nseq, CONV_WIDTH - 1, CONV_DIM), lambda i: (i, 0, 0))],
        out_shape=[
            jax.ShapeDtypeStruct((t, D_INNER), F32), jax.ShapeDtypeStruct((t, D_INNER), F32),
            jax.ShapeDtypeStruct((t, D_INNER), F32), jax.ShapeDtypeStruct((t, D_INNER), BF16),
            jax.ShapeDtypeStruct((t, GBN), F32),
            jax.ShapeDtypeStruct((t // rows * GBN, rows), BF16),
            jax.ShapeDtypeStruct((bsz, CONV_WIDTH - 1, CONV_DIM), F32),
        ],
        scratch_shapes=[pltpu.VMEM((nseq, CONV_PAD + q, CONV_DIM), F32), pltpu.VMEM((rows, D_INNER), F32)],
        compiler_params=pltpu.CompilerParams(
            dimension_semantics=("arbitrary",), vmem_limit_bytes=_vmem_limit(est)),
        name="ssm_sample_front",
    )(x2, conv_state, w["win"], w["wdt"], w["cw"], w["cb"], w["dtb"], w["alog"], w["dsk"], e, tril)

    srows = bb_state * q
    srow_spec = lambda c: pl.BlockSpec((srows, c), lambda i: (i, 0))
    st_spec = pl.BlockSpec((bb_state, D_INNER, D_STATE), lambda i: (i, 0, 0))
    est = 4 * bb_state * D_INNER * D_STATE * 4 + 8 * D_INNER * D_STATE * 4
    per_block = nseq // bb_state
    y, new_state = pl.pallas_call(
        functools.partial(_ssm_sample_state_kernel, bb=bb_state, q=q, nseq=nseq),
        grid=(bsz // bb_state,),
        in_specs=[st_spec, srow_spec(D_INNER), srow_spec(D_INNER), srow_spec(GBN),
                  pl.BlockSpec((rows, D_INNER), lambda i: (i // per_block, 0)),
                  pl.BlockSpec((GBN, rows), lambda i: (i // per_block, 0))],
        out_specs=[srow_spec(D_INNER), st_spec],
        out_shape=[jax.ShapeDtypeStruct((t, D_INNER), F32),
                   jax.ShapeDtypeStruct((bsz, D_INNER, D_STATE), F32)],
        compiler_params=pltpu.CompilerParams(
            dimension_semantics=("arbitrary",), vmem_limit_bytes=_vmem_limit(est)),
        name="ssm_sample_state",
    )(ssm_state, ydiag, exp_a, cmat, xdd, bt)

    est = D_INNER * d * 2 + 8 * rows * D_INNER * 4
    out = pl.pallas_call(
        _ssm_sample_finish_kernel,
        grid=(t // rows,),
        in_specs=[row_spec(d), row_spec(D_INNER), row_spec(D_INNER),
                  _resident((1, D_INNER)), _resident((D_INNER, d)), _resident((1, d)), _resident((1, d))],
        out_specs=row_spec(d),
        out_shape=jax.ShapeDtypeStruct((t, d), F32),
        compiler_params=pltpu.CompilerParams(
            dimension_semantics=("arbitrary",), vmem_limit_bytes=_vmem_limit(est)),
        name="ssm_sample_finish",
    )(x2, y, z, w["nw"], w["wout"], g, b)
    return out.reshape(bsz, q, d), new_conv, new_state


def _ssm_weights(w_in, conv_w, conv_b, dt_bias, a_log, d_skip, norm_w, w_out):
    d = w_in.shape[0]
    pad = LANES - SSM_HEADS
    wdt = jnp.pad(w_in[:, D_INNER + CONV_DIM:], ((0, 0), (0, pad))).astype(BF16)
    return dict(
        win=w_in.astype(BF16), wdt=wdt, cw=conv_w, cb=conv_b.reshape(1, CONV_DIM),
        dtb=jnp.pad(dt_bias, (0, pad)).reshape(1, LANES),
        alog=jnp.pad(a_log, (0, pad)).reshape(1, LANES),
        dsk=jnp.repeat(d_skip, SSM_HEAD_DIM).reshape(1, D_INNER),
        nw=norm_w.reshape(1, D_INNER), wout=w_out.astype(BF16))


def kernel(x_prompt, x_sample, cache_k, cache_v, state_conv, state_ssm, state_pool, rel_bias, attn_w_qkv, attn_b_qkv, attn_w_o, attn_b_o, attn_sinks, ssm_w_in, ssm_conv_w, ssm_conv_b, ssm_dt_bias, ssm_a_log, ssm_d, ssm_norm_w, ssm_w_out, pool_w, pool_scale, ffn_w_gate, ffn_w_up, ffn_w_down, ln_g, ln_b):
    xp, xs = x_prompt, x_sample
    d = xp.shape[-1]
    wqkv, wo = attn_w_qkv.astype(BF16), attn_w_o.astype(BF16)
    wg, wu, wd = ffn_w_gate.astype(BF16), ffn_w_up.astype(BF16), ffn_w_down.astype(BF16)
    cache_k_t = jnp.transpose(cache_k, (0, 1, 3, 4, 2))
    cache_v_t = jnp.transpose(cache_v, (0, 1, 3, 4, 2))
    new_cache_t = None
    nk_p, nv_p, nc_p, nh_p, npool_p = [], [], [], [], []
    nc_s, nh_s, npool_s = [], [], []
    for i in range(DEPTH):
        j = i // N_MIXERS
        kind = i % N_MIXERS
        g1 = ln_g[i, 0].reshape(1, d)
        b1 = ln_b[i, 0].reshape(1, d)
        if kind == 0:
            bqkv = attn_b_qkv[j].reshape(1, -1)
            bo = attn_b_o[j].reshape(1, d)
            xp, kp, vp = _attn_prompt(xp, wqkv, bqkv, wo, bo, g1, b1, rel_bias, attn_sinks[j], layer=j)
            xs, *new_cache_t = _attn_sample(xs, cache_k_t, cache_v_t, wqkv, bqkv, wo, bo, g1, b1, rel_bias,
                                            attn_sinks[j], layer=j, prev=new_cache_t)
            xs = xs.reshape(x_sample.shape)
            kv_shape = (-1, WINDOW, N_KV_HEADS, HEAD_DIM)
            nk_p.append(kp.reshape(kv_shape)); nv_p.append(vp.reshape(kv_shape))
        elif kind == 1:
            w = _ssm_weights(ssm_w_in[j], ssm_conv_w[j], ssm_conv_b[j], ssm_dt_bias[j], ssm_a_log[j],
                             ssm_d[j], ssm_norm_w[j], ssm_w_out[j])
            xp, cp, hp = _ssm_prompt(xp, w, g1, b1)
            xs, cs_, hs_ = _ssm_sample(xs, state_conv[j], state_ssm[j].reshape(-1, D_INNER, D_STATE), w, g1, b1)
            st_shape = (-1, SSM_HEADS, SSM_HEAD_DIM, D_STATE)
            nc_p.append(cp); nh_p.append(hp.reshape(st_shape))
            nc_s.append(cs_); nh_s.append(hs_.reshape(st_shape))
        else:
            pw = pool_w[j].astype(BF16)
            psc = pool_scale[j].reshape(1, d)
            xp, pp = _pool_prompt(xp, pw, psc, g1, b1)
            xs, ps_ = _pool_sample(xs, state_pool[j], pw, psc, g1, b1, start=PAST_LEN)
            npool_p.append(pp); npool_s.append(ps_)
        g2 = ln_g[i, 1].reshape(1, d)
        b2 = ln_b[i, 1].reshape(1, d)
        xp = _ffn_ln(xp.reshape(-1, d), wg, wu, wd, g2, b2, layer=i).reshape(xp.shape)
        xs = _ffn_ln(xs.reshape(-1, d), wg, wu, wd, g2, b2, layer=i).reshape(xs.shape)
    nk_s, nv_s = (jnp.transpose(t, (0, 1, 4, 2, 3)) for t in new_cache_t)
    return (xp, xs,
            jnp.stack(nk_p), jnp.stack(nv_p), jnp.stack(nc_p), jnp.stack(nh_p), jnp.stack(npool_p),
            nk_s, nv_s, jnp.stack(nc_s), jnp.stack(nh_s), jnp.stack(npool_s))
```

```python
import functools
import math

import jax
import jax.numpy as jnp
import numpy as np
from jax import lax
from jax.experimental import pallas as pl
from jax.experimental.pallas import tpu as pltpu

DEPTH = 4
N_MIXERS = 3
PAST_LEN = 8192
DEEPNORM_ALPHA = (2 * DEPTH) ** 0.25
LN_EPS = 1e-5

V7X_VMEM_BYTES = 64 * 1024 * 1024
LANES = 128
SUBLANES = 8
MXU_DIM = 256

BF16 = jnp.bfloat16
F32 = jnp.float32


def _vmem_limit(estimate_bytes):
    return int(min(V7X_VMEM_BYTES - 8 * 1024 * 1024, max(32 * 1024 * 1024, estimate_bytes * 3 // 2)))


def _layer_norm_rows(v, g, b):
    mu = jnp.mean(v, axis=-1, keepdims=True)
    d = v - mu
    var = jnp.mean(d * d, axis=-1, keepdims=True)
    return d * lax.rsqrt(var + LN_EPS) * g + b


def _resident(shape):
    nd = len(shape)
    return pl.BlockSpec(shape, lambda *_: (0,) * nd, pipeline_mode=pl.Buffered(1))


def _layer_resident(shape, layer):
    nd = len(shape)
    return pl.BlockSpec((1,) + tuple(shape), lambda *_: (layer,) + (0,) * nd, pipeline_mode=pl.Buffered(1))


def _ffn_chunks(d_ff):
    step = 2 * MXU_DIM
    edges = list(range(0, d_ff, step)) + [d_ff]
    return tuple(zip(edges[:-1], edges[1:]))


def _ffn_kernel(x_ref, wg_ref, wu_ref, wd_ref, g_ref, b_ref, o_ref, *, chunks):
    x = x_ref[...]
    xb = x.astype(BF16)
    acc = jnp.zeros(x.shape, F32)
    for c0, c1 in chunks:
        gate = jnp.dot(xb, wg_ref[0, :, c0:c1], preferred_element_type=F32)
        up = jnp.dot(xb, wu_ref[0, :, c0:c1], preferred_element_type=F32)
        h = (gate * jax.nn.sigmoid(gate)) * up
        acc = acc + jnp.dot(h.astype(BF16), wd_ref[0, c0:c1, :], preferred_element_type=F32)
    o_ref[...] = _layer_norm_rows(DEEPNORM_ALPHA * x + acc, g_ref[...], b_ref[...])


def _ffn_ln(x, wg, wu, wd, g, b, *, layer, tm=512):
    t, d = x.shape
    d_ff = wg.shape[-1]
    tm = min(tm, t)
    assert t % tm == 0
    est = 3 * d * d_ff * 2 + 4 * tm * d * 4 + 4 * tm * d_ff * 4
    return pl.pallas_call(
        functools.partial(_ffn_kernel, chunks=_ffn_chunks(d_ff)),
        grid=(t // tm,),
        in_specs=[
            pl.BlockSpec((tm, d), lambda i: (i, 0)),
            _layer_resident((d, d_ff), layer),
            _layer_resident((d, d_ff), layer),
            _layer_resident((d_ff, d), layer),
            _resident((1, d)),
            _resident((1, d)),
        ],
        out_specs=pl.BlockSpec((tm, d), lambda i: (i, 0)),
        out_shape=jax.ShapeDtypeStruct((t, d), F32),
        compiler_params=pltpu.CompilerParams(
            dimension_semantics=("arbitrary",), vmem_limit_bytes=_vmem_limit(est)),
        name="ffn_ln",
    )(x, wg, wu, wd, g, b)


HEAD_DIM = 64
N_HEADS = 16
N_KV_HEADS = 4
GQA_GROUP = N_HEADS // N_KV_HEADS
WINDOW = 128
REL_BUCKETS = 32
REL_MAX_DIST = 128
Q_DIM = N_HEADS * HEAD_DIM
KV_DIM = N_KV_HEADS * HEAD_DIM
NEG_INF = float("-inf")
CHAIN_BATCH = 2


def _t5_bucket_table(dist):
    n = np.maximum(dist, 0)
    max_exact = REL_BUCKETS // 2
    nf = np.maximum(n, 1).astype(np.float32)
    large = max_exact + (np.log(nf / np.float32(max_exact)) / np.float32(math.log(REL_MAX_DIST / max_exact))
                         * np.float32(REL_BUCKETS - max_exact)).astype(np.int32)
    large = np.minimum(large, REL_BUCKETS - 1)
    bucket = np.where(n < max_exact, n, large)
    valid = (dist >= 0) & (dist < WINDOW)
    return np.where(valid, bucket, -1).astype(np.int32)


def _bias_from_buckets(bucket, relb_ref, head):
    acc = jnp.full(bucket.shape, NEG_INF, F32)
    for bkt in range(REL_BUCKETS):
        acc = jnp.where(bucket == bkt, relb_ref[bkt, head], acc)
    return acc


def _attn_prompt_kernel(x_ref, wqkv_ref, bqkv_ref, wo_ref, bo_ref, g_ref, b_ref, bucket_ref,
                        relb_ref, sink_ref, o_ref, kout_ref, vout_ref,
                        ka_lo, ka_hi, kb_lo, kb_hi, vt, ot, bias_scr, sink_scr, *, tq):
    bi = pl.program_id(0)
    j = pl.program_id(1)
    nj = pl.num_programs(1)
    blk = WINDOW
    half_heads = GQA_GROUP // 2
    kbufs = (ka_lo, ka_hi, kb_lo, kb_hi)

    @pl.when((bi == 0) & (j == 0))
    def _build_tables():
        bucket = bucket_ref[...]
        lane = lax.broadcasted_iota(jnp.int32, (1, 2 * blk), 1)
        for kv in range(N_KV_HEADS):
            for half in range(2):
                ha = kv * GQA_GROUP + half
                hb = ha + half_heads
                pair = kv * 2 + half
                bias_scr[pair, :, 0:blk] = _bias_from_buckets(bucket, relb_ref, ha)
                bias_scr[pair, :, blk:2 * blk] = _bias_from_buckets(bucket, relb_ref, hb)
                sink_scr[pair] = jnp.where(lane < blk, sink_ref[ha], sink_ref[hb])

    @pl.when(j == 0)
    def _no_past():
        for buf in kbufs:
            buf[0:blk, :] = jnp.zeros((blk, KV_DIM), BF16)
        vt[:, 0:blk] = jnp.zeros((KV_DIM, blk), BF16)

    @pl.when(j > 0)
    def _carry():
        for buf in kbufs:
            buf[0:blk, :] = buf[tq:tq + blk, :]
        vt[:, 0:blk] = vt[:, tq:tq + blk]

    x = x_ref[0]
    qkv = jnp.dot(x.astype(BF16), wqkv_ref[0], preferred_element_type=F32) + bqkv_ref[...]
    q = (qkv[:, :Q_DIM] * (HEAD_DIM ** -0.5)).astype(BF16)
    k = qkv[:, Q_DIM:Q_DIM + KV_DIM]
    v = qkv[:, Q_DIM + KV_DIM:]
    k_sw = jnp.concatenate(
        [pltpu.roll(k[:, c * LANES:(c + 1) * LANES], HEAD_DIM, axis=1) for c in range(KV_DIM // LANES)], axis=1)
    lo = (lax.broadcasted_iota(jnp.int32, (1, KV_DIM), 1) % LANES) < HEAD_DIM
    ka_lo[blk:blk + tq, :] = jnp.where(lo, k, 0.0).astype(BF16)
    ka_hi[blk:blk + tq, :] = jnp.where(lo, 0.0, k).astype(BF16)
    kb_lo[blk:blk + tq, :] = jnp.where(lo, k_sw, 0.0).astype(BF16)
    kb_hi[blk:blk + tq, :] = jnp.where(lo, 0.0, k_sw).astype(BF16)
    vt[:, blk:blk + tq] = v.T.astype(BF16)

    @pl.when(j == nj - 1)
    def _emit_cache():
        kout_ref[0] = k[tq - WINDOW:, :]
        vout_ref[0] = v[tq - WINDOW:, :]

    ks = lax.broadcasted_iota(jnp.int32, (blk, 2 * blk), 0)
    qt = lax.broadcasted_iota(jnp.int32, (blk, 2 * blk), 1) % blk
    own = ks <= qt
    chains = [(i, kv, half) for i in range(tq // blk) for kv in range(N_KV_HEADS) for half in range(2)]

    def scores(i, kv, half):
        r0 = i * blk
        c0 = (kv // 2) * LANES
        in_lo = kv % 2 == 0
        if half == 0:
            kk = (ka_lo if in_lo else kb_lo)[r0:r0 + 2 * blk, c0:c0 + LANES]
        else:
            kk = (kb_hi if in_lo else ka_hi)[r0:r0 + 2 * blk, c0:c0 + LANES]
        qa = q[r0:r0 + blk, (2 * kv) * LANES:(2 * kv + 1) * LANES]
        qb = q[r0:r0 + blk, (2 * kv + 1) * LANES:(2 * kv + 2) * LANES]
        q_pair = jnp.concatenate([qa, qb], axis=0)
        return lax.dot_general(kk, q_pair, (((1,), (1,)), ((), ())), preferred_element_type=F32)

    def fold(i, kv, half, s2):
        s_prev = s2[0:blk, :]
        if i == 0:
            s_prev = jnp.where(j == 0, NEG_INF, s_prev)
        return jnp.where(own, s2[blk:2 * blk, :], s_prev) + bias_scr[kv * 2 + half]

    def softmax_batch(batch, s2s):
        sinks = [sink_scr[kv * 2 + half] for _, kv, half in batch]
        ss = [fold(*ch, s2) for ch, s2 in zip(batch, s2s)]
        ms = [jnp.maximum(jnp.max(s, axis=0, keepdims=True), sink) for s, sink in zip(ss, sinks)]
        ps = [jnp.exp(s - m) for s, m in zip(ss, ms)]
        invs = [1.0 / (jnp.sum(p, axis=0, keepdims=True) + jnp.exp(sink - m)) for p, m, sink in zip(ps, ms, sinks)]
        p2s = [jnp.concatenate([jnp.where(own, 0.0, p), jnp.where(own, p, 0.0)], axis=0).astype(BF16) for p in ps]
        return list(zip(p2s, invs))

    def weighted_values(i, kv, half, p2, inv):
        r0 = i * blk
        ha = kv * GQA_GROUP + half
        hb = ha + half_heads
        v_t = vt[kv * HEAD_DIM:(kv + 1) * HEAD_DIM, r0:r0 + 2 * blk]
        o_t = jnp.dot(v_t, p2, preferred_element_type=F32) * inv
        ot[ha * HEAD_DIM:(ha + 1) * HEAD_DIM, r0:r0 + blk] = o_t[:, 0:blk]
        ot[hb * HEAD_DIM:(hb + 1) * HEAD_DIM, r0:r0 + blk] = o_t[:, blk:2 * blk]

    batches = [chains[c:c + CHAIN_BATCH] for c in range(0, len(chains), CHAIN_BATCH)]
    s_next = [scores(*ch) for ch in batches[0]]
    prev = []
    for bi, batch in enumerate(batches):
        s_cur = s_next
        if bi + 1 < len(batches):
            s_next = [scores(*ch) for ch in batches[bi + 1]]
        probs = softmax_batch(batch, s_cur)
        for ch, pr in prev:
            weighted_values(*ch, *pr)
        prev = list(zip(batch, probs))
    for ch, pr in prev:
        weighted_values(*ch, *pr)

    o = ot[...].T.astype(BF16)
    y = jnp.dot(o, wo_ref[0], preferred_element_type=F32) + bo_ref[...]
    o_ref[0] = _layer_norm_rows(DEEPNORM_ALPHA * x + y, g_ref[...], b_ref[...])


def _attn_prompt(x, wqkv, bqkv, wo, bo, g, b, rel_bias, sinks, *, layer, tq=512):
    bsz, l, d = x.shape
    tq = min(tq, l)
    assert l % tq == 0 and tq % WINDOW == 0
    qi = np.arange(WINDOW)[None, :]
    si = np.arange(WINDOW)[:, None]
    bucket = jnp.asarray(_t5_bucket_table(np.where(si <= qi, qi - si, qi + WINDOW - si)))
    qkv_dim = wqkv.shape[-1]
    smem = pl.BlockSpec(memory_space=pltpu.SMEM)
    est = (d * qkv_dim + Q_DIM * d) * 2 + 5 * tq * d * 4 + N_HEADS * WINDOW * 2 * WINDOW * 4 + 3 * tq * qkv_dim * 4
    return pl.pallas_call(
        functools.partial(_attn_prompt_kernel, tq=tq),
        grid=(bsz, l // tq),
        in_specs=[
            pl.BlockSpec((1, tq, d), lambda bi, j: (bi, j, 0)),
            _layer_resident((d, qkv_dim), layer), _resident((1, qkv_dim)),
            _layer_resident((Q_DIM, d), layer), _resident((1, d)),
            _resident((1, d)), _resident((1, d)),
            _resident((WINDOW, WINDOW)),
            smem, smem,
        ],
        out_specs=[
            pl.BlockSpec((1, tq, d), lambda bi, j: (bi, j, 0)),
            pl.BlockSpec((1, WINDOW, KV_DIM), lambda bi, j: (bi, 0, 0)),
            pl.BlockSpec((1, WINDOW, KV_DIM), lambda bi, j: (bi, 0, 0)),
        ],
        out_shape=[
            jax.ShapeDtypeStruct((bsz, l, d), F32),
            jax.ShapeDtypeStruct((bsz, WINDOW, KV_DIM), F32),
            jax.ShapeDtypeStruct((bsz, WINDOW, KV_DIM), F32),
        ],
        scratch_shapes=[pltpu.VMEM((WINDOW + tq, KV_DIM), BF16)] * 4 + [
            pltpu.VMEM((KV_DIM, WINDOW + tq), BF16),
            pltpu.VMEM((Q_DIM, tq), F32),
            pltpu.VMEM((N_HEADS // 2, WINDOW, 2 * WINDOW), F32),
            pltpu.VMEM((N_HEADS // 2, 1, 2 * WINDOW), F32),
        ],
        compiler_params=pltpu.CompilerParams(
            dimension_semantics=("arbitrary", "arbitrary"), vmem_limit_bytes=_vmem_limit(est)),
        name="attn_prompt",
    )(x, wqkv, bqkv, wo, bo, g, b, bucket, rel_bias, sinks)


SAMPLE_KEYS = 2 * WINDOW
SAMPLE_CHAIN_BATCH = 8


def _attn_sample_kernel(x_ref, ck_ref, cv_ref, wqkv_ref, bqkv_ref, wo_ref, bo_ref, g_ref, b_ref, bucket_ref,
                        relb_ref, sink_ref, *rest, bb, l, n_prev, out_layer):
    o_ref, nk_ref, nv_ref, qbuf, obuf, bias_scr = rest[n_prev:]
    for other in range(nk_ref.shape[0]):
        if other != out_layer:
            nk_ref[other] = jnp.zeros(nk_ref.shape[1:], F32)
            nv_ref[other] = jnp.zeros(nv_ref.shape[1:], F32)

    @pl.when(pl.program_id(0) == 0)
    def _init():
        bucket = bucket_ref[...]
        for h in range(N_HEADS):
            kv, g_ = divmod(h, GQA_GROUP)
            bias_scr[kv, g_ * l:(g_ + 1) * l, :] = _bias_from_buckets(bucket, relb_ref, h)

    x = x_ref[...]
    qkv = jnp.dot(x.astype(BF16), wqkv_ref[0], preferred_element_type=F32) + bqkv_ref[...]
    qbuf[...] = qkv[:, :Q_DIM] * (HEAD_DIM ** -0.5)
    k_new = qkv[:, Q_DIM:Q_DIM + KV_DIM]
    v_new = qkv[:, Q_DIM + KV_DIM:]

    sink_cols = []
    for kv in range(N_KV_HEADS):
        sink_cols.append(jnp.concatenate(
            [jnp.full((l, 1), sink_ref[kv * GQA_GROUP + g_], F32) for g_ in range(GQA_GROUP)], axis=0))

    kn_t = k_new.T
    vn_t = v_new.T
    lane = lax.broadcasted_iota(jnp.int32, (HEAD_DIM, WINDOW), 1)
    pad = jnp.zeros((WINDOW - l, KV_DIM), F32)

    def shifted(old_t, new_t, i):
        kept = pltpu.roll(old_t, WINDOW - l, axis=1)
        new = pltpu.roll(new_t, (WINDOW - l - i * l) % WINDOW, axis=1)
        return jnp.where(lane >= WINDOW - l, new, kept)

    def keys_values(i):
        r0 = i * l
        for kv in range(N_KV_HEADS):
            hs = slice(kv * HEAD_DIM, (kv + 1) * HEAD_DIM)
            nk_ref[out_layer, i, kv] = shifted(ck_ref[0, i, kv], kn_t[hs, :], i)
            nv_ref[out_layer, i, kv] = shifted(cv_ref[0, i, kv], vn_t[hs, :], i)
        kn = jnp.concatenate([k_new[r0:r0 + l, :], pad], axis=0).astype(BF16)
        vn = jnp.concatenate([v_new[r0:r0 + l, :], pad], axis=0).astype(BF16)
        return kn, vn

    def scores(i, kv, kn):
        r0 = i * l
        c0 = kv * HEAD_DIM
        qs = jnp.concatenate(
            [qbuf[r0:r0 + l, (kv * GQA_GROUP + g_) * HEAD_DIM:(kv * GQA_GROUP + g_ + 1) * HEAD_DIM]
             for g_ in range(GQA_GROUP)], axis=0).astype(BF16)
        s_cache = jnp.dot(qs, ck_ref[0, i, kv].astype(BF16), preferred_element_type=F32)
        s_new = lax.dot_general(qs, kn[:, c0:c0 + HEAD_DIM], (((1,), (1,)), ((), ())), preferred_element_type=F32)
        return jnp.concatenate([s_cache, s_new], axis=1)

    def weighted_values(i, kv, p, vn):
        c0 = kv * HEAD_DIM
        pb = p.astype(BF16)
        o_cache = lax.dot_general(pb[:, :WINDOW], cv_ref[0, i, kv].astype(BF16), (((1,), (1,)), ((), ())),
                                  preferred_element_type=F32)
        return o_cache + jnp.dot(pb[:, WINDOW:], vn[:, c0:c0 + HEAD_DIM], preferred_element_type=F32)

    for i0 in range(0, bb, SAMPLE_CHAIN_BATCH):
        items = range(i0, min(bb, i0 + SAMPLE_CHAIN_BATCH))
        kvs = {i: keys_values(i) for i in items}
        chains = [(i, kv) for i in items for kv in range(N_KV_HEADS)]
        s_all = [scores(i, kv, kvs[i][0]) + bias_scr[kv] for i, kv in chains]
        m_all = [jnp.maximum(jnp.max(s, axis=-1, keepdims=True), sink_cols[kv]) for s, (i, kv) in zip(s_all, chains)]
        p_all = [jnp.exp(s - m) for s, m in zip(s_all, m_all)]
        den_all = [jnp.sum(p, axis=-1, keepdims=True) + jnp.exp(sink_cols[kv] - m)
                   for p, m, (i, kv) in zip(p_all, m_all, chains)]
        o_all = [weighted_values(i, kv, p, kvs[i][1]) for p, (i, kv) in zip(p_all, chains)]
        for o, den, (i, kv) in zip(o_all, den_all, chains):
            o = o / den
            for g_ in range(GQA_GROUP):
                h = kv * GQA_GROUP + g_
                obuf[i * l:(i + 1) * l, h * HEAD_DIM:(h + 1) * HEAD_DIM] = o[g_ * l:(g_ + 1) * l, :]

    y = jnp.dot(obuf[...].astype(BF16), wo_ref[0], preferred_element_type=F32) + bo_ref[...]
    o_ref[...] = _layer_norm_rows(DEEPNORM_ALPHA * x + y, g_ref[...], b_ref[...])


def _attn_sample(x, cache_k_t, cache_v_t, wqkv, bqkv, wo, bo, g, b, rel_bias, sinks, *, layer, prev=None, bb=16):
    bsz, l, d = x.shape
    assert bsz % bb == 0 and l == SUBLANES and bb * l == WINDOW
    ti = np.arange(l)[:, None]
    ci = np.arange(SAMPLE_KEYS)[None, :]
    table = _t5_bucket_table(ti + WINDOW - ci)
    table = np.where(ci < WINDOW + l, table, -1).astype(np.int32)
    bucket = jnp.asarray(table)
    qkv_dim = wqkv.shape[-1]
    rows = bb * l
    smem = pl.BlockSpec(memory_space=pltpu.SMEM)
    cache_spec = pl.BlockSpec((1, bb, N_KV_HEADS, HEAD_DIM, WINDOW), lambda i: (layer, i, 0, 0, 0))
    est = (d * qkv_dim + Q_DIM * d) * 2 + 8 * bb * WINDOW * KV_DIM * 4 + 8 * rows * d * 4
    in_specs = [
        pl.BlockSpec((rows, d), lambda i: (i, 0)),
        cache_spec, cache_spec,
        _layer_resident((d, qkv_dim), layer), _resident((1, qkv_dim)),
        _layer_resident((Q_DIM, d), layer), _resident((1, d)),
        _resident((1, d)), _resident((1, d)),
        _resident((l, SAMPLE_KEYS)),
        smem, smem,
    ]
    args = [x.reshape(bsz * l, d), cache_k_t, cache_v_t, wqkv, bqkv, wo, bo, g, b, bucket, rel_bias, sinks]
    n_layers = cache_k_t.shape[0]
    if prev is None:
        aliases, out_layer = {}, layer
        out_cache_spec = pl.BlockSpec((n_layers, bb, N_KV_HEADS, HEAD_DIM, WINDOW), lambda i: (0, i, 0, 0, 0))
    else:
        aliases, out_layer = {len(args): 1, len(args) + 1: 2}, 0
        out_cache_spec = cache_spec
        in_specs += [pl.BlockSpec(memory_space=pl.ANY)] * 2
        args += list(prev)
    return pl.pallas_call(
        functools.partial(_attn_sample_kernel, bb=bb, l=l, n_prev=len(aliases), out_layer=out_layer),
        grid=(bsz // bb,),
        in_specs=in_specs,
        out_specs=[pl.BlockSpec((rows, d), lambda i: (i, 0)), out_cache_spec, out_cache_spec],
        out_shape=[
            jax.ShapeDtypeStruct((bsz * l, d), F32),
            jax.ShapeDtypeStruct(cache_k_t.shape, F32),
            jax.ShapeDtypeStruct(cache_v_t.shape, F32),
        ],
        scratch_shapes=[
            pltpu.VMEM((rows, Q_DIM), F32),
            pltpu.VMEM((rows, Q_DIM), F32),
            pltpu.VMEM((N_KV_HEADS, GQA_GROUP * l, SAMPLE_KEYS), F32),
        ],
        input_output_aliases=aliases,
        compiler_params=pltpu.CompilerParams(
            dimension_semantics=("arbitrary",), vmem_limit_bytes=_vmem_limit(est)),
        name="attn_sample",
    )(*args)


POOL_WINDOWS = (2, 4, 8, 16)
POOL_PAD = max(POOL_WINDOWS)
POOL_STATE_LEN = POOL_PAD - 1
POOL_HIST = len(POOL_WINDOWS) * SUBLANES


def _pool_mix(window_sum, x, cnt_of, pw_ref, scale):
    gd = x.shape[-1] // len(POOL_WINDOWS)
    outs = []
    for g, w in enumerate(POOL_WINDOWS):
        diff = window_sum(g, w) / cnt_of(w) - x[:, g * gd:(g + 1) * gd]
        outs.append(jnp.dot(diff.astype(BF16), pw_ref[g], preferred_element_type=F32))
    return jnp.concatenate(outs, axis=-1) * scale


def _pool_prompt_kernel(x_ref, pw_ref, sc_ref, g_ref, b_ref, o_ref, st_ref, buf, lvl, *, tm):
    j = pl.program_id(1)
    nj = pl.num_programs(1)
    d = x_ref.shape[-1]
    gd = d // len(POOL_WINDOWS)
    hist, rows = POOL_HIST, POOL_HIST + tm

    @pl.when(j == 0)
    def _no_past():
        buf[0:hist, :] = jnp.zeros((hist, d), F32)

    @pl.when(j > 0)
    def _carry():
        buf[0:hist, :] = buf[tm:tm + hist, :]

    x = x_ref[0]
    buf[hist:rows, :] = x

    sums = {}
    src = buf
    for k, w in enumerate(POOL_WINDOWS):
        step, c0, r0 = w // 2, k * gd, (k + 1) * SUBLANES
        cur = src[r0:rows, c0:d] + src[r0 - step:rows - step, c0:d]
        sums[k] = cur[hist - r0:, 0:gd]
        if k + 1 < len(POOL_WINDOWS):
            lvl[k, r0:rows, c0:d] = cur
            src = lvl.at[k]

    def window_sum(g, w):
        return sums[g]

    pos1 = j * tm + lax.broadcasted_iota(jnp.int32, (tm, 1), 0) + 1

    def cnt_of(w):
        return jnp.minimum(pos1, w).astype(F32)

    y = _pool_mix(window_sum, x, cnt_of, pw_ref, sc_ref[...])
    o_ref[0] = _layer_norm_rows(DEEPNORM_ALPHA * x + y, g_ref[...], b_ref[...])

    @pl.when(j == nj - 1)
    def _emit_state():
        st_ref[0] = buf[rows - POOL_STATE_LEN:rows, :]


def _pool_prompt(x, pw, scale, g, b, *, tm=512):
    bsz, l, d = x.shape
    tm = min(tm, l)
    assert l % tm == 0 and tm >= POOL_HIST >= POOL_STATE_LEN and POOL_WINDOWS == (2, 4, 8, 16)
    ng, gd = pw.shape[0], pw.shape[1]
    est = 10 * tm * d * 4 + ng * gd * gd * 2
    return pl.pallas_call(
        functools.partial(_pool_prompt_kernel, tm=tm),
        grid=(bsz, l // tm),
        in_specs=[
            pl.BlockSpec((1, tm, d), lambda bi, j: (bi, j, 0)),
            _resident((ng, gd, gd)), _resident((1, d)), _resident((1, d)), _resident((1, d)),
        ],
        out_specs=[
            pl.BlockSpec((1, tm, d), lambda bi, j: (bi, j, 0)),
            pl.BlockSpec((1, POOL_STATE_LEN, d), lambda bi, j: (bi, 0, 0)),
        ],
        out_shape=[
            jax.ShapeDtypeStruct((bsz, l, d), F32),
            jax.ShapeDtypeStruct((bsz, POOL_STATE_LEN, d), F32),
        ],
        scratch_shapes=[pltpu.VMEM((POOL_HIST + tm, d), F32),
                        pltpu.VMEM((len(POOL_WINDOWS) - 1, POOL_HIST + tm, d), F32)],
        compiler_params=pltpu.CompilerParams(
            dimension_semantics=("arbitrary", "arbitrary"), vmem_limit_bytes=_vmem_limit(est)),
        name="pool_prompt",
    )(x, pw, scale, g, b)


def _pool_sample_kernel(x_ref, st_ref, pw_ref, sc_ref, g_ref, b_ref, o_ref, nst_ref, buf, *, bb, l):
    d = x_ref.shape[-1]
    gd = d // len(POOL_WINDOWS)
    x3 = x_ref[...]
    buf[:, 1:POOL_PAD, :] = st_ref[...]
    buf[:, POOL_PAD:POOL_PAD + l, :] = x3
    x = x3.reshape(bb * l, d)

    def window_sum(g, w):
        acc = x3[:, :, g * gd:(g + 1) * gd]
        for s in range(1, w):
            acc = acc + buf[:, POOL_PAD - s:POOL_PAD - s + l, g * gd:(g + 1) * gd]
        return acc.reshape(bb * l, gd)

    y = _pool_mix(window_sum, x, lambda w: float(w), pw_ref, sc_ref[...])
    o_ref[...] = _layer_norm_rows(DEEPNORM_ALPHA * x + y, g_ref[...], b_ref[...]).reshape(bb, l, d)
    nst_ref[...] = buf[:, l + 1:l + POOL_PAD, :]


def _pool_sample(x, state, pw, scale, g, b, *, start, bb=16):
    bsz, l, d = x.shape
    assert bsz % bb == 0 and l == SUBLANES and start + 1 >= POOL_PAD
    ng, gd = pw.shape[0], pw.shape[1]
    est = 8 * bb * (POOL_PAD + l) * d * 4 + ng * gd * gd * 2
    return pl.pallas_call(
        functools.partial(_pool_sample_kernel, bb=bb, l=l),
        grid=(bsz // bb,),
        in_specs=[
            pl.BlockSpec((bb, l, d), lambda i: (i, 0, 0)),
            pl.BlockSpec((bb, POOL_STATE_LEN, d), lambda i: (i, 0, 0)),
            _resident((ng, gd, gd)), _resident((1, d)), _resident((1, d)), _resident((1, d)),
        ],
        out_specs=[
            pl.BlockSpec((bb, l, d), lambda i: (i, 0, 0)),
            pl.BlockSpec((bb, POOL_STATE_LEN, d), lambda i: (i, 0, 0)),
        ],
        out_shape=[
            jax.ShapeDtypeStruct((bsz, l, d), F32),
            jax.ShapeDtypeStruct((bsz, POOL_STATE_LEN, d), F32),
        ],
        scratch_shapes=[pltpu.VMEM((bb, POOL_PAD + l, d), F32)],
        compiler_params=pltpu.CompilerParams(
            dimension_semantics=("arbitrary",), vmem_limit_bytes=_vmem_limit(est)),
        name="pool_sample",
    )(x, state, pw, scale, g, b)


D_INNER = 2048
SSM_HEAD_DIM = 64
SSM_HEADS = D_INNER // SSM_HEAD_DIM
SSM_GROUPS = 4
SSM_HPG = SSM_HEADS // SSM_GROUPS
D_STATE = 128
CONV_WIDTH = 4
GBN = SSM_GROUPS * D_STATE
CONV_DIM = D_INNER + 2 * GBN
GROUP_INNER = D_INNER // SSM_GROUPS
RMS_EPS = 1e-5
CONV_PAD = SUBLANES
SSM_ROWS = 128


def _split_bf16(v, parts=3):
    out = []
    r = v
    for _ in range(parts):
        p = r.astype(BF16)
        out.append(p)
        r = r - p.astype(F32)
    return out


def _expand_heads(v, e):
    return sum(jnp.dot(p, e, preferred_element_type=F32) for p in _split_bf16(v, parts=2))


def _ssm_in_proj(xb, win_ref, wdt_ref, c0, c1):
    main = win_ref.shape[1]
    if c0 >= main:
        return jnp.dot(xb, wdt_ref[:, c0 - main:c1 - main], preferred_element_type=F32)
    assert c1 <= main
    return jnp.dot(xb, win_ref[:, c0:c1], preferred_element_type=F32)


def _ssm_prep(proj, xp_buf, conv_state, dtb_ref, alog_ref, tril_ref, *, nseq, q):
    rows = nseq * q
    xbc_pre = proj(D_INNER, D_INNER + CONV_DIM)
    dt_pre = proj(D_INNER + CONV_DIM, D_INNER + CONV_DIM + LANES)
    if conv_state is not None:
        xp_buf[:, CONV_PAD - (CONV_WIDTH - 1):CONV_PAD, :] = conv_state
    xp_buf[:, CONV_PAD:CONV_PAD + q, :] = xbc_pre.reshape(nseq, q, CONV_DIM)
    new_conv = xp_buf[:, q + CONV_PAD - (CONV_WIDTH - 1):q + CONV_PAD, :]

    dtv = dt_pre + dtb_ref[...]
    dt = jnp.maximum(dtv, 0.0) + jnp.log1p(jnp.exp(-jnp.abs(dtv)))
    a = -jnp.exp(alog_ref[...])
    tril = tril_ref[...]
    acum = sum(jnp.dot(tril, p, preferred_element_type=F32) for p in _split_bf16(dt * a))
    a3 = acum.reshape(nseq, q, LANES)
    alast = jnp.broadcast_to(a3[:, q - 1:q, :], (nseq, q, LANES)).reshape(rows, LANES)
    return dict(acum=acum, acum_t=acum.T, dt_t=dt.T, causal=tril > 0.5,
                exp_a=jnp.exp(acum), dd=jnp.exp(alast - acum) * dt, new_conv=new_conv)


def _conv_silu(xp_buf, cw_ref, cb_ref, c0, c1, *, nseq, q):
    acc = cb_ref[:, c0:c1].reshape(1, 1, c1 - c0)
    for jj in range(CONV_WIDTH):
        off = CONV_PAD - (CONV_WIDTH - 1) + jj
        acc = acc + xp_buf[:, off:off + q, c0:c1] * cw_ref[jj:jj + 1, c0:c1].reshape(1, 1, c1 - c0)
    acc = acc.reshape(nseq * q, c1 - c0)
    return acc * jax.nn.sigmoid(acc)


def _ssm_group_diag(g, f, proj, prep, xp_buf, cw_ref, cb_ref, ybuf, *, nseq, q):
    f["z"] = proj(g * GROUP_INNER, (g + 1) * GROUP_INNER)
    conv = functools.partial(_conv_silu, xp_buf, cw_ref, cb_ref, nseq=nseq, q=q)
    half = GROUP_INNER // 2
    xs_lo = conv(g * GROUP_INNER, g * GROUP_INNER + half)
    yield
    xs_hi = conv(g * GROUP_INNER + half, (g + 1) * GROUP_INNER)
    yield
    xs = f["xs"] = jnp.concatenate([xs_lo, xs_hi], axis=-1)
    bm = f["bm"] = conv(D_INNER + g * D_STATE, D_INNER + (g + 1) * D_STATE)
    cm = f["cm"] = conv(D_INNER + GBN + g * D_STATE, D_INNER + GBN + (g + 1) * D_STATE)
    yield
    cb = lax.dot_general(cm.astype(BF16), bm.astype(BF16), (((1,), (1,)), ((), ())), preferred_element_type=F32)
    acum, acum_t, dt_t, causal = prep["acum"], prep["acum_t"], prep["dt_t"], prep["causal"]
    for r in range(SSM_HPG):
        h = g * SSM_HPG + r
        seg = acum[:, h:h + 1] - acum_t[h:h + 1, :]
        lm = jnp.exp(jnp.where(causal, seg, NEG_INF))
        w = (cb * lm * dt_t[h:h + 1, :]).astype(BF16)
        xh = xs[:, r * SSM_HEAD_DIM:(r + 1) * SSM_HEAD_DIM].astype(BF16)
        ybuf[:, h * SSM_HEAD_DIM:(h + 1) * SSM_HEAD_DIM] = jnp.dot(w, xh, preferred_element_type=F32)
        yield


def _ssm_group_expand(g, f, prep, e_ref):
    e = e_ref[:, g * GROUP_INNER:(g + 1) * GROUP_INNER]
    f["exp_a"] = _expand_heads(prep["exp_a"], e)
    yield
    f["xdd"] = (f["xs"] * _expand_heads(prep["dd"], e)).astype(BF16)
    yield


def _interleave(*gens):
    live = list(gens)
    while live:
        for gen in list(live):
            try:
                next(gen)
            except StopIteration:
                live.remove(gen)


def _gate_norm(y, z, nw):
    y = y * (z * jax.nn.sigmoid(z))
    return y * lax.rsqrt(jnp.mean(y * y, axis=-1, keepdims=True) + RMS_EPS) * nw


def _ssm_prompt_kernel(x_ref, win_ref, wdt_ref, cw_ref, cb_ref, dtb_ref, alog_ref, dsk_ref, nw_ref, wout_ref, e_ref,
                       tril_ref, g_ref, b_ref, o_ref, cout_ref, sout_ref, xp_buf, ht, ybuf):
    j = pl.program_id(1)
    nj = pl.num_programs(1)
    q = SSM_ROWS

    @pl.when(j == 0)
    def _no_past():
        xp_buf[:, 0:CONV_PAD, :] = jnp.zeros((1, CONV_PAD, CONV_DIM), F32)
        ht[...] = jnp.zeros(ht.shape, F32)

    @pl.when(j > 0)
    def _carry():
        xp_buf[:, 0:CONV_PAD, :] = xp_buf[:, q:q + CONV_PAD, :]

    x = x_ref[0]
    xb = x.astype(BF16)
    zx = jnp.dot(xb, win_ref[...], preferred_element_type=F32)
    dt_pre = jnp.dot(xb, wdt_ref[...], preferred_element_type=F32)
    z = zx[:, :D_INNER]

    def proj(c0, c1):
        return dt_pre if c0 >= D_INNER + CONV_DIM else zx[:, c0:c1]

    prep = _ssm_prep(proj, xp_buf, None, dtb_ref, alog_ref, tril_ref, nseq=1, q=q)
    xbc = _conv_silu(xp_buf, cw_ref, cb_ref, 0, CONV_DIM, nseq=1, q=q)
    xs = xbc[:, :D_INNER]
    acum, acum_t, dt_t, causal = prep["acum"], prep["acum_t"], prep["dt_t"], prep["causal"]
    for g in range(SSM_GROUPS):
        bg = xbc[:, D_INNER + g * D_STATE:D_INNER + (g + 1) * D_STATE].astype(BF16)
        cg = xbc[:, D_INNER + GBN + g * D_STATE:D_INNER + GBN + (g + 1) * D_STATE].astype(BF16)
        cb = lax.dot_general(cg, bg, (((1,), (1,)), ((), ())), preferred_element_type=F32)
        for r in range(SSM_HPG):
            h = g * SSM_HPG + r
            seg = acum[:, h:h + 1] - acum_t[h:h + 1, :]
            lm = jnp.exp(jnp.where(causal, seg, NEG_INF))
            w = (cb * lm * dt_t[h:h + 1, :]).astype(BF16)
            xh = xs[:, h * SSM_HEAD_DIM:(h + 1) * SSM_HEAD_DIM].astype(BF16)
            ybuf[:, h * SSM_HEAD_DIM:(h + 1) * SSM_HEAD_DIM] = jnp.dot(w, xh, preferred_element_type=F32)

    e = e_ref[...]
    exp_a = _expand_heads(prep["exp_a"], e)
    xdd = (xs * _expand_heads(prep["dd"], e)).astype(BF16)
    decay = exp_a[q - 1:q, :]
    y_off = []
    for g in range(SSM_GROUPS):
        sl = slice(g * GROUP_INNER, (g + 1) * GROUP_INNER)
        bg = xbc[:, D_INNER + g * D_STATE:D_INNER + (g + 1) * D_STATE]
        cg = xbc[:, D_INNER + GBN + g * D_STATE:D_INNER + GBN + (g + 1) * D_STATE].astype(BF16)
        hg = ht[:, sl]
        y_off.append(jnp.dot(cg, hg.astype(BF16), preferred_element_type=F32))
        ht[:, sl] = hg * decay[:, sl] + jnp.dot(bg.T.astype(BF16), xdd[:, sl], preferred_element_type=F32)
    y = ybuf[...] + exp_a * jnp.concatenate(y_off, axis=-1) + dsk_ref[...] * xs
    y = jnp.concatenate(
        [_gate_norm(y[:, g * GROUP_INNER:(g + 1) * GROUP_INNER], z[:, g * GROUP_INNER:(g + 1) * GROUP_INNER],
                    nw_ref[:, g * GROUP_INNER:(g + 1) * GROUP_INNER]) for g in range(SSM_GROUPS)], axis=-1)
    out = jnp.dot(y.astype(BF16), wout_ref[...], preferred_element_type=F32)
    o_ref[0] = _layer_norm_rows(DEEPNORM_ALPHA * x + out, g_ref[...], b_ref[...])

    @pl.when(j == nj - 1)
    def _emit_state():
        cout_ref[...] = prep["new_conv"]
        sout_ref[0] = ht[...].T


def _ssm_consts(nseq, q):
    rows = nseq * q
    r = np.arange(rows)
    tril = ((r[:, None] >= r[None, :]) & (r[:, None] // q == r[None, :] // q)).astype(np.float32)
    e = np.zeros((LANES, D_INNER), np.float32)
    e[np.arange(D_INNER) // SSM_HEAD_DIM, np.arange(D_INNER)] = 1.0
    return jnp.asarray(tril, BF16), jnp.asarray(e, BF16)


def _ssm_weight_specs(d):
    return [
        pl.BlockSpec((d, D_INNER + CONV_DIM), lambda *_: (0, 0), pipeline_mode=pl.Buffered(1)),
        _resident((d, LANES)), _resident((CONV_WIDTH, CONV_DIM)), _resident((1, CONV_DIM)),
        _resident((1, LANES)), _resident((1, LANES)),
    ]


def _ssm_prompt(x, w, g, b):
    bsz, l, d = x.shape
    q = SSM_ROWS
    assert l % q == 0
    tril, e = _ssm_consts(1, q)
    win_cols = D_INNER + CONV_DIM + LANES
    est = (d * win_cols + D_INNER * d) * 2 + LANES * D_INNER * 4 + 12 * q * D_INNER * 4 + 3 * q * win_cols * 4
    return pl.pallas_call(
        _ssm_prompt_kernel,
        grid=(bsz, l // q),
        in_specs=[pl.BlockSpec((1, q, d), lambda bi, j: (bi, j, 0))] + _ssm_weight_specs(d) + [
            _resident((1, D_INNER)), _resident((1, D_INNER)), _resident((D_INNER, d)),
            _resident((LANES, D_INNER)), _resident((q, q)), _resident((1, d)), _resident((1, d)),
        ],
        out_specs=[
            pl.BlockSpec((1, q, d), lambda bi, j: (bi, j, 0)),
            pl.BlockSpec((1, CONV_WIDTH - 1, CONV_DIM), lambda bi, j: (bi, 0, 0)),
            pl.BlockSpec((1, D_INNER, D_STATE), lambda bi, j: (bi, 0, 0)),
        ],
        out_shape=[
            jax.ShapeDtypeStruct((bsz, l, d), F32),
            jax.ShapeDtypeStruct((bsz, CONV_WIDTH - 1, CONV_DIM), F32),
            jax.ShapeDtypeStruct((bsz, D_INNER, D_STATE), F32),
        ],
        scratch_shapes=[
            pltpu.VMEM((1, CONV_PAD + q, CONV_DIM), F32),
            pltpu.VMEM((D_STATE, D_INNER), F32),
            pltpu.VMEM((q, D_INNER), F32),
        ],
        compiler_params=pltpu.CompilerParams(
            dimension_semantics=("arbitrary", "arbitrary"), vmem_limit_bytes=_vmem_limit(est)),
        name="ssm_prompt",
    )(x, w["win"], w["wdt"], w["cw"], w["cb"], w["dtb"], w["alog"], w["dsk"], w["nw"], w["wout"], e, tril, g, b)


def _ssm_sample_front_kernel(x_ref, cst_ref, win_ref, wdt_ref, cw_ref, cb_ref, dtb_ref, alog_ref, dsk_ref, e_ref, tril_ref,
                             y_ref, z_ref, ea_ref, xdd_ref, c_ref, bt_ref, cout_ref, xp_buf, ybuf, *, nseq, q):
    xb = x_ref[...].astype(BF16)

    def proj(c0, c1):
        return _ssm_in_proj(xb, win_ref, wdt_ref, c0, c1)

    prep = _ssm_prep(proj, xp_buf, cst_ref[...], dtb_ref, alog_ref, tril_ref, nseq=nseq, q=q)
    cout_ref[...] = prep["new_conv"]
    for g in range(SSM_GROUPS):
        sl = slice(g * GROUP_INNER, (g + 1) * GROUP_INNER)
        f = {}
        _interleave(_ssm_group_diag(g, f, proj, prep, xp_buf, cw_ref, cb_ref, ybuf, nseq=nseq, q=q))
        _interleave(_ssm_group_expand(g, f, prep, e_ref))
        y_ref[:, sl] = ybuf[:, sl] + dsk_ref[:, sl] * f["xs"]
        z_ref[:, sl] = f["z"]
        ea_ref[:, sl] = f["exp_a"]
        xdd_ref[:, sl] = f["xdd"]
        c_ref[:, g * D_STATE:(g + 1) * D_STATE] = f["cm"]
        bt_ref[g * D_STATE:(g + 1) * D_STATE, :] = f["bm"].T.astype(BF16)


def _ssm_sample_state_kernel(st_ref, y_ref, ea_ref, c_ref, xdd_ref, bt_ref, o_ref, nst_ref, *, bb, q, nseq):
    ea = ea_ref[...]
    xdd = xdd_ref[...]
    first_seq = (pl.program_id(0) % (nseq // bb)) * bb
    lane_seq = lax.broadcasted_iota(jnp.int32, (D_STATE, nseq * q), 1) // q
    for i in range(bb):
        r0 = i * q
        ht = st_ref[i].T
        decay = ea[r0 + q - 1:r0 + q, :]
        y_parts = []
        new_parts = []
        for g in range(SSM_GROUPS):
            sl = slice(g * GROUP_INNER, (g + 1) * GROUP_INNER)
            cg = c_ref[r0:r0 + q, g * D_STATE:(g + 1) * D_STATE].astype(BF16)
            hg = ht[:, sl]
            y_parts.append(jnp.dot(cg, hg.astype(BF16), preferred_element_type=F32))
            btg = bt_ref[g * D_STATE:(g + 1) * D_STATE, :]
            btg = jnp.where(lane_seq == first_seq + i, btg, jnp.zeros_like(btg))
            new_parts.append(hg * decay[:, sl] + jnp.dot(btg, xdd[:, sl], preferred_element_type=F32))
        o_ref[r0:r0 + q, :] = y_ref[r0:r0 + q, :] + ea[r0:r0 + q, :] * jnp.concatenate(y_parts, axis=-1)
        nst_ref[i] = jnp.concatenate(new_parts, axis=-1).T


def _ssm_sample_finish_kernel(x_ref, y_ref, z_ref, nw_ref, wout_ref, g_ref, b_ref, o_ref):
    x = x_ref[...]
    out = jnp.zeros(x.shape, F32)
    for g in range(SSM_GROUPS):
        sl = slice(g * GROUP_INNER, (g + 1) * GROUP_INNER)
        y = _gate_norm(y_ref[:, sl], z_ref[:, sl], nw_ref[:, sl])
        out = out + jnp.dot(y.astype(BF16), wout_ref[sl, :], preferred_element_type=F32)
    o_ref[...] = _layer_norm_rows(DEEPNORM_ALPHA * x + out, g_ref[...], b_ref[...])


def _ssm_sample(x, conv_state, ssm_state, w, g, b, *, bb_state=4):
    bsz, q, d = x.shape
    t = bsz * q
    nseq = SSM_ROWS // q
    rows = SSM_ROWS
    assert q == SUBLANES and bsz % nseq == 0 and nseq % bb_state == 0
    tril, e = _ssm_consts(nseq, q)
    win_cols = D_INNER + CONV_DIM + LANES
    x2 = x.reshape(t, d)
    row_spec = lambda c: pl.BlockSpec((rows, c), lambda i: (i, 0))
    est = d * win_cols * 2 + LANES * D_INNER * 4 + 16 * rows * D_INNER * 4 + 3 * rows * win_cols * 4
    ydiag, z, exp_a, xdd, cmat, bt, new_conv = pl.pallas_call(
        functools.partial(_ssm_sample_front_kernel, nseq=nseq, q=q),
        grid=(t // rows,),
        in_specs=[row_spec(d), pl.BlockSpec((nseq, CONV_WIDTH - 1, CONV_DIM), lambda i: (i, 0, 0))]
        + _ssm_weight_specs(d) + [_resident((1, D_INNER)), _resident((LANES, D_INNER)), _resident((rows, rows))],
        out_specs=[row_spec(D_INNER), row_spec(D_INNER), row_spec(D_INNER), row_spec(D_INNER), row_spec(GBN),
                   pl.BlockSpec((GBN, rows), lambda i: (i, 0)),
                   pl.BlockSpec((nseq, CONV_WIDTH - 1, CONV_DIM), lambda i: (i, 0, 0))],
        out_shape=[
            jax.ShapeDtypeStruct((t, D_INNER), F32), jax.ShapeDtypeStruct((t, D_INNER), F32),
            jax.ShapeDtypeStruct((t, D_INNER), F32), jax.ShapeDtypeStruct((t, D_INNER), BF16),
            jax.ShapeDtypeStruct((t, GBN), F32),
            jax.ShapeDtypeStruct((t // rows * GBN, rows), BF16),
            jax.ShapeDtypeStruct((bsz, CONV_WIDTH - 1, CONV_DIM), F32),
        ],
        scratch_shapes=[pltpu.VMEM((nseq, CONV_PAD + q, CONV_DIM), F32), pltpu.VMEM((rows, D_INNER), F32)],
        compiler_params=pltpu.CompilerParams(
            dimension_semantics=("arbitrary",), vmem_limit_bytes=_vmem_limit(est)),
        name="ssm_sample_front",
    )(x2, conv_state, w["win"], w["wdt"], w["cw"], w["cb"], w["dtb"], w["alog"], w["dsk"], e, tril)

    srows = bb_state * q
    srow_spec = lambda c: pl.BlockSpec((srows, c), lambda i: (i, 0))
    st_spec = pl.BlockSpec((bb_state, D_INNER, D_STATE), lambda i: (i, 0, 0))
    est = 4 * bb_state * D_INNER * D_STATE * 4 + 8 * D_INNER * D_STATE * 4
    per_block = nseq // bb_state
    y, new_state = pl.pallas_call(
        functools.partial(_ssm_sample_state_kernel, bb=bb_state, q=q, nseq=nseq),
        grid=(bsz // bb_state,),
        in_specs=[st_spec, srow_spec(D_INNER), srow_spec(D_INNER), srow_spec(GBN),
                  pl.BlockSpec((rows, D_INNER), lambda i: (i // per_block, 0)),
                  pl.BlockSpec((GBN, rows), lambda i: (i // per_block, 0))],
        out_specs=[srow_spec(D_INNER), st_spec],
        out_shape=[jax.ShapeDtypeStruct((t, D_INNER), F32),
                   jax.ShapeDtypeStruct((bsz, D_INNER, D_STATE), F32)],
        compiler_params=pltpu.CompilerParams(
            dimension_semantics=("arbitrary",), vmem_limit_bytes=_vmem_limit(est)),
        name="ssm_sample_state",
    )(ssm_state, ydiag, exp_a, cmat, xdd, bt)

    est = D_INNER * d * 2 + 8 * rows * D_INNER * 4
    out = pl.pallas_call(
        _ssm_sample_finish_kernel,
        grid=(t // rows,),
        in_specs=[row_spec(d), row_spec(D_INNER), row_spec(D_INNER),
                  _resident((1, D_INNER)), _resident((D_INNER, d)), _resident((1, d)), _resident((1, d))],
        out_specs=row_spec(d),
        out_shape=jax.ShapeDtypeStruct((t, d), F32),
        compiler_params=pltpu.CompilerParams(
            dimension_semantics=("arbitrary",), vmem_limit_bytes=_vmem_limit(est)),
        name="ssm_sample_finish",
    )(x2, y, z, w["nw"], w["wout"], g, b)
    return out.reshape(bsz, q, d), new_conv, new_state


def _ssm_weights(w_in, conv_w, conv_b, dt_bias, a_log, d_skip, norm_w, w_out):
    d = w_in.shape[0]
    pad = LANES - SSM_HEADS
    wdt = jnp.pad(w_in[:, D_INNER + CONV_DIM:], ((0, 0), (0, pad))).astype(BF16)
    return dict(
        win=w_in.astype(BF16), wdt=wdt, cw=conv_w, cb=conv_b.reshape(1, CONV_DIM),
        dtb=jnp.pad(dt_bias, (0, pad)).reshape(1, LANES),
        alog=jnp.pad(a_log, (0, pad)).reshape(1, LANES),
        dsk=jnp.repeat(d_skip, SSM_HEAD_DIM).reshape(1, D_INNER),
        nw=norm_w.reshape(1, D_INNER), wout=w_out.astype(BF16))


def kernel(x_prompt, x_sample, cache_k, cache_v, state_conv, state_ssm, state_pool, rel_bias, attn_w_qkv, attn_b_qkv, attn_w_o, attn_b_o, attn_sinks, ssm_w_in, ssm_conv_w, ssm_conv_b, ssm_dt_bias, ssm_a_log, ssm_d, ssm_norm_w, ssm_w_out, pool_w, pool_scale, ffn_w_gate, ffn_w_up, ffn_w_down, ln_g, ln_b):
    xp, xs = x_prompt, x_sample
    d = xp.shape[-1]
    wqkv, wo = attn_w_qkv.astype(BF16), attn_w_o.astype(BF16)
    wg, wu, wd = ffn_w_gate.astype(BF16), ffn_w_up.astype(BF16), ffn_w_down.astype(BF16)
    cache_k_t = jnp.transpose(cache_k, (0, 1, 3, 4, 2))
    cache_v_t = jnp.transpose(cache_v, (0, 1, 3, 4, 2))
    new_cache_t = None
    nk_p, nv_p, nc_p, nh_p, npool_p = [], [], [], [], []
    nc_s, nh_s, npool_s = [], [], []
    for i in range(DEPTH):
        j = i // N_MIXERS
        kind = i % N_MIXERS
        g1 = ln_g[i, 0].reshape(1, d)
        b1 = ln_b[i, 0].reshape(1, d)
        if kind == 0:
            bqkv = attn_b_qkv[j].reshape(1, -1)
            bo = attn_b_o[j].reshape(1, d)
            xp, kp, vp = _attn_prompt(xp, wqkv, bqkv, wo, bo, g1, b1, rel_bias, attn_sinks[j], layer=j)
            xs, *new_cache_t = _attn_sample(xs, cache_k_t, cache_v_t, wqkv, bqkv, wo, bo, g1, b1, rel_bias,
                                            attn_sinks[j], layer=j, prev=new_cache_t)
            xs = xs.reshape(x_sample.shape)
            kv_shape = (-1, WINDOW, N_KV_HEADS, HEAD_DIM)
            nk_p.append(kp.reshape(kv_shape)); nv_p.append(vp.reshape(kv_shape))
        elif kind == 1:
            w = _ssm_weights(ssm_w_in[j], ssm_conv_w[j], ssm_conv_b[j], ssm_dt_bias[j], ssm_a_log[j],
                             ssm_d[j], ssm_norm_w[j], ssm_w_out[j])
            xp, cp, hp = _ssm_prompt(xp, w, g1, b1)
            xs, cs_, hs_ = _ssm_sample(xs, state_conv[j], state_ssm[j].reshape(-1, D_INNER, D_STATE), w, g1, b1)
            st_shape = (-1, SSM_HEADS, SSM_HEAD_DIM, D_STATE)
            nc_p.append(cp); nh_p.append(hp.reshape(st_shape))
            nc_s.append(cs_); nh_s.append(hs_.reshape(st_shape))
        else:
            pw = pool_w[j].astype(BF16)
            psc = pool_scale[j].reshape(1, d)
            xp, pp = _pool_prompt(xp, pw, psc, g1, b1)
            xs, ps_ = _pool_sample(xs, state_pool[j], pw, psc, g1, b1, start=PAST_LEN)
            npool_p.append(pp); npool_s.append(ps_)
        g2 = ln_g[i, 1].reshape(1, d)
        b2 = ln_b[i, 1].reshape(1, d)
        xp = _ffn_ln(xp.reshape(-1, d), wg, wu, wd, g2, b2, layer=i).reshape(xp.shape)
        xs = _ffn_ln(xs.reshape(-1, d), wg, wu, wd, g2, b2, layer=i).reshape(xs.shape)
    nk_s, nv_s = (jnp.transpose(t, (0, 1, 4, 2, 3)) for t in new_cache_t)
    return (xp, xs,
            jnp.stack(nk_p), jnp.stack(nv_p), jnp.stack(nc_p), jnp.stack(nh_p), jnp.stack(npool_p),
            nk_s, nv_s, jnp.stack(nc_s), jnp.stack(nh_s), jnp.stack(npool_s))
```

```python
import functools
import math

import jax
import jax.numpy as jnp
import numpy as np
from jax import lax
from jax.experimental import pallas as pl
from jax.experimental.pallas import tpu as pltpu

DEPTH = 4
N_MIXERS = 3
PAST_LEN = 8192
DEEPNORM_ALPHA = (2 * DEPTH) ** 0.25
LN_EPS = 1e-5

V7X_VMEM_BYTES = 64 * 1024 * 1024
LANES = 128
SUBLANES = 8
MXU_DIM = 256

BF16 = jnp.bfloat16
F32 = jnp.float32


def _vmem_limit(estimate_bytes):
    return int(min(V7X_VMEM_BYTES - 8 * 1024 * 1024, max(32 * 1024 * 1024, estimate_bytes * 3 // 2)))


def _layer_norm_rows(v, g, b):
    mu = jnp.mean(v, axis=-1, keepdims=True)
    d = v - mu
    var = jnp.mean(d * d, axis=-1, keepdims=True)
    return d * lax.rsqrt(var + LN_EPS) * g + b


def _resident(shape):
    nd = len(shape)
    return pl.BlockSpec(shape, lambda *_: (0,) * nd, pipeline_mode=pl.Buffered(1))


def _layer_resident(shape, layer):
    nd = len(shape)
    return pl.BlockSpec((1,) + tuple(shape), lambda *_: (layer,) + (0,) * nd, pipeline_mode=pl.Buffered(1))


def _ffn_chunks(d_ff):
    step = 2 * MXU_DIM
    edges = list(range(0, d_ff, step)) + [d_ff]
    return tuple(zip(edges[:-1], edges[1:]))


def _ffn_kernel(x_ref, wg_ref, wu_ref, wd_ref, g_ref, b_ref, o_ref, *, chunks):
    x = x_ref[...]
    xb = x.astype(BF16)
    acc = jnp.zeros(x.shape, F32)
    for c0, c1 in chunks:
        gate = jnp.dot(xb, wg_ref[0, :, c0:c1], preferred_element_type=F32)
        up = jnp.dot(xb, wu_ref[0, :, c0:c1], preferred_element_type=F32)
        h = (gate * jax.nn.sigmoid(gate)) * up
        acc = acc + jnp.dot(h.astype(BF16), wd_ref[0, c0:c1, :], preferred_element_type=F32)
    o_ref[...] = _layer_norm_rows(DEEPNORM_ALPHA * x + acc, g_ref[...], b_ref[...])


def _ffn_ln(x, wg, wu, wd, g, b, *, layer, tm=512):
    t, d = x.shape
    d_ff = wg.shape[-1]
    tm = min(tm, t)
    assert t % tm == 0
    est = 3 * d * d_ff * 2 + 4 * tm * d * 4 + 4 * tm * d_ff * 4
    return pl.pallas_call(
        functools.partial(_ffn_kernel, chunks=_ffn_chunks(d_ff)),
        grid=(t // tm,),
        in_specs=[
            pl.BlockSpec((tm, d), lambda i: (i, 0)),
            _layer_resident((d, d_ff), layer),
            _layer_resident((d, d_ff), layer),
            _layer_resident((d_ff, d), layer),
            _resident((1, d)),
            _resident((1, d)),
        ],
        out_specs=pl.BlockSpec((tm, d), lambda i: (i, 0)),
        out_shape=jax.ShapeDtypeStruct((t, d), F32),
        compiler_params=pltpu.CompilerParams(
            dimension_semantics=("arbitrary",), vmem_limit_bytes=_vmem_limit(est)),
        name="ffn_ln",
    )(x, wg, wu, wd, g, b)


HEAD_DIM = 64
N_HEADS = 16
N_KV_HEADS = 4
GQA_GROUP = N_HEADS // N_KV_HEADS
WINDOW = 128
REL_BUCKETS = 32
REL_MAX_DIST = 128
Q_DIM = N_HEADS * HEAD_DIM
KV_DIM = N_KV_HEADS * HEAD_DIM
NEG_INF = float("-inf")
CHAIN_BATCH = 2


def _t5_bucket_table(dist):
    n = np.maximum(dist, 0)
    max_exact = REL_BUCKETS // 2
    nf = np.maximum(n, 1).astype(np.float32)
    large = max_exact + (np.log(nf / np.float32(max_exact)) / np.float32(math.log(REL_MAX_DIST / max_exact))
                         * np.float32(REL_BUCKETS - max_exact)).astype(np.int32)
    large = np.minimum(large, REL_BUCKETS - 1)
    bucket = np.where(n < max_exact, n, large)
    valid = (dist >= 0) & (dist < WINDOW)
    return np.where(valid, bucket, -1).astype(np.int32)


def _bias_from_buckets(bucket, relb_ref, head):
    acc = jnp.full(bucket.shape, NEG_INF, F32)
    for bkt in range(REL_BUCKETS):
        acc = jnp.where(bucket == bkt, relb_ref[bkt, head], acc)
    return acc


def _attn_prompt_kernel(x_ref, wqkv_ref, bqkv_ref, wo_ref, bo_ref, g_ref, b_ref, bucket_ref,
                        relb_ref, sink_ref, o_ref, kout_ref, vout_ref,
                        ka_lo, ka_hi, kb_lo, kb_hi, vt, ot, bias_scr, sink_scr, *, tq):
    bi = pl.program_id(0)
    j = pl.program_id(1)
    nj = pl.num_programs(1)
    blk = WINDOW
    half_heads = GQA_GROUP // 2
    kbufs = (ka_lo, ka_hi, kb_lo, kb_hi)

    @pl.when((bi == 0) & (j == 0))
    def _build_tables():
        bucket = bucket_ref[...]
        lane = lax.broadcasted_iota(jnp.int32, (1, 2 * blk), 1)
        for kv in range(N_KV_HEADS):
            for half in range(2):
                ha = kv * GQA_GROUP + half
                hb = ha + half_heads
                pair = kv * 2 + half
                bias_scr[pair, :, 0:blk] = _bias_from_buckets(bucket, relb_ref, ha)
                bias_scr[pair, :, blk:2 * blk] = _bias_from_buckets(bucket, relb_ref, hb)
                sink_scr[pair] = jnp.where(lane < blk, sink_ref[ha], sink_ref[hb])

    @pl.when(j == 0)
    def _no_past():
        for buf in kbufs:
            buf[0:blk, :] = jnp.zeros((blk, KV_DIM), BF16)
        vt[:, 0:blk] = jnp.zeros((KV_DIM, blk), BF16)

    @pl.when(j > 0)
    def _carry():
        for buf in kbufs:
            buf[0:blk, :] = buf[tq:tq + blk, :]
        vt[:, 0:blk] = vt[:, tq:tq + blk]

    x = x_ref[0]
    qkv = jnp.dot(x.astype(BF16), wqkv_ref[0], preferred_element_type=F32) + bqkv_ref[...]
    q = (qkv[:, :Q_DIM] * (HEAD_DIM ** -0.5)).astype(BF16)
    k = qkv[:, Q_DIM:Q_DIM + KV_DIM]
    v = qkv[:, Q_DIM + KV_DIM:]
    k_sw = jnp.concatenate(
        [pltpu.roll(k[:, c * LANES:(c + 1) * LANES], HEAD_DIM, axis=1) for c in range(KV_DIM // LANES)], axis=1)
    lo = (lax.broadcasted_iota(jnp.int32, (1, KV_DIM), 1) % LANES) < HEAD_DIM
    ka_lo[blk:blk + tq, :] = jnp.where(lo, k, 0.0).astype(BF16)
    ka_hi[blk:blk + tq, :] = jnp.where(lo, 0.0, k).astype(BF16)
    kb_lo[blk:blk + tq, :] = jnp.where(lo, k_sw, 0.0).astype(BF16)
    kb_hi[blk:blk + tq, :] = jnp.where(lo, 0.0, k_sw).astype(BF16)
    vt[:, blk:blk + tq] = v.T.astype(BF16)

    @pl.when(j == nj - 1)
    def _emit_cache():
        kout_ref[0] = k[tq - WINDOW:, :]
        vout_ref[0] = v[tq - WINDOW:, :]

    ks = lax.broadcasted_iota(jnp.int32, (blk, 2 * blk), 0)
    qt = lax.broadcasted_iota(jnp.int32, (blk, 2 * blk), 1) % blk
    own = ks <= qt
    chains = [(i, kv, half) for i in range(tq // blk) for kv in range(N_KV_HEADS) for half in range(2)]

    def scores(i, kv, half):
        r0 = i * blk
        c0 = (kv // 2) * LANES
        in_lo = kv % 2 == 0
        if half == 0:
            kk = (ka_lo if in_lo else kb_lo)[r0:r0 + 2 * blk, c0:c0 + LANES]
        else:
            kk = (kb_hi if in_lo else ka_hi)[r0:r0 + 2 * blk, c0:c0 + LANES]
        qa = q[r0:r0 + blk, (2 * kv) * LANES:(2 * kv + 1) * LANES]
        qb = q[r0:r0 + blk, (2 * kv + 1) * LANES:(2 * kv + 2) * LANES]
        q_pair = jnp.concatenate([qa, qb], axis=0)
        return lax.dot_general(kk, q_pair, (((1,), (1,)), ((), ())), preferred_element_type=F32)

    def fold(i, kv, half, s2):
        s_prev = s2[0:blk, :]
        if i == 0:
            s_prev = jnp.where(j == 0, NEG_INF, s_prev)
        return jnp.where(own, s2[blk:2 * blk, :], s_prev) + bias_scr[kv * 2 + half]

    def softmax_batch(batch, s2s):
        sinks = [sink_scr[kv * 2 + half] for _, kv, half in batch]
        ss = [fold(*ch, s2) for ch, s2 in zip(batch, s2s)]
        ms = [jnp.maximum(jnp.max(s, axis=0, keepdims=True), sink) for s, sink in zip(ss, sinks)]
        ps = [jnp.exp(s - m) for s, m in zip(ss, ms)]
        invs = [1.0 / (jnp.sum(p, axis=0, keepdims=True) + jnp.exp(sink - m)) for p, m, sink in zip(ps, ms, sinks)]
        p2s = [jnp.concatenate([jnp.where(own, 0.0, p), jnp.where(own, p, 0.0)], axis=0).astype(BF16) for p in ps]
        return list(zip(p2s, invs))

    def weighted_values(i, kv, half, p2, inv):
        r0 = i * blk
        ha = kv * GQA_GROUP + half
        hb = ha + half_heads
        v_t = vt[kv * HEAD_DIM:(kv + 1) * HEAD_DIM, r0:r0 + 2 * blk]
        o_t = jnp.dot(v_t, p2, preferred_element_type=F32) * inv
        ot[ha * HEAD_DIM:(ha + 1) * HEAD_DIM, r0:r0 + blk] = o_t[:, 0:blk]
        ot[hb * HEAD_DIM:(hb + 1) * HEAD_DIM, r0:r0 + blk] = o_t[:, blk:2 * blk]

    batches = [chains[c:c + CHAIN_BATCH] for c in range(0, len(chains), CHAIN_BATCH)]
    s_next = [scores(*ch) for ch in batches[0]]
    prev = []
    for bi, batch in enumerate(batches):
        s_cur = s_next
        if bi + 1 < len(batches):
            s_next = [scores(*ch) for ch in batches[bi + 1]]
        probs = softmax_batch(batch, s_cur)
        for ch, pr in prev:
            weighted_values(*ch, *pr)
        prev = list(zip(batch, probs))
    for ch, pr in prev:
        weighted_values(*ch, *pr)

    o = ot[...].T.astype(BF16)
    y = jnp.dot(o, wo_ref[0], preferred_element_type=F32) + bo_ref[...]
    o_ref[0] = _layer_norm_rows(DEEPNORM_ALPHA * x + y, g_ref[...], b_ref[...])


def _attn_prompt(x, wqkv, bqkv, wo, bo, g, b, rel_bias, sinks, *, layer, tq=1024):
    bsz, l, d = x.shape
    tq = min(tq, l)
    assert l % tq == 0 and tq % WINDOW == 0
    qi = np.arange(WINDOW)[None, :]
    si = np.arange(WINDOW)[:, None]
    bucket = jnp.asarray(_t5_bucket_table(np.where(si <= qi, qi - si, qi + WINDOW - si)))
    qkv_dim = wqkv.shape[-1]
    smem = pl.BlockSpec(memory_space=pltpu.SMEM)
    est = (d * qkv_dim + Q_DIM * d) * 2 + 5 * tq * d * 4 + N_HEADS * WINDOW * 2 * WINDOW * 4 + 3 * tq * qkv_dim * 4
    return pl.pallas_call(
        functools.partial(_attn_prompt_kernel, tq=tq),
        grid=(bsz, l // tq),
        in_specs=[
            pl.BlockSpec((1, tq, d), lambda bi, j: (bi, j, 0)),
            _layer_resident((d, qkv_dim), layer), _resident((1, qkv_dim)),
            _layer_resident((Q_DIM, d), layer), _resident((1, d)),
            _resident((1, d)), _resident((1, d)),
            _resident((WINDOW, WINDOW)),
            smem, smem,
        ],
        out_specs=[
            pl.BlockSpec((1, tq, d), lambda bi, j: (bi, j, 0)),
            pl.BlockSpec((1, WINDOW, KV_DIM), lambda bi, j: (bi, 0, 0)),
            pl.BlockSpec((1, WINDOW, KV_DIM), lambda bi, j: (bi, 0, 0)),
        ],
        out_shape=[
            jax.ShapeDtypeStruct((bsz, l, d), F32),
            jax.ShapeDtypeStruct((bsz, WINDOW, KV_DIM), F32),
            jax.ShapeDtypeStruct((bsz, WINDOW, KV_DIM), F32),
        ],
        scratch_shapes=[pltpu.VMEM((WINDOW + tq, KV_DIM), BF16)] * 4 + [
            pltpu.VMEM((KV_DIM, WINDOW + tq), BF16),
            pltpu.VMEM((Q_DIM, tq), F32),
            pltpu.VMEM((N_HEADS // 2, WINDOW, 2 * WINDOW), F32),
            pltpu.VMEM((N_HEADS // 2, 1, 2 * WINDOW), F32),
        ],
        compiler_params=pltpu.CompilerParams(
            dimension_semantics=("arbitrary", "arbitrary"), vmem_limit_bytes=_vmem_limit(est)),
        name="attn_prompt",
    )(x, wqkv, bqkv, wo, bo, g, b, bucket, rel_bias, sinks)


SAMPLE_KEYS = 2 * WINDOW
SAMPLE_CHAIN_BATCH = 8


def _attn_sample_kernel(x_ref, ck_ref, cv_ref, wqkv_ref, bqkv_ref, wo_ref, bo_ref, g_ref, b_ref, bucket_ref,
                        relb_ref, sink_ref, *rest, bb, l, n_prev, out_layer):
    o_ref, nk_ref, nv_ref, qbuf, obuf, bias_scr = rest[n_prev:]
    for other in range(nk_ref.shape[0]):
        if other != out_layer:
            nk_ref[other] = jnp.zeros(nk_ref.shape[1:], F32)
            nv_ref[other] = jnp.zeros(nv_ref.shape[1:], F32)

    @pl.when(pl.program_id(0) == 0)
    def _init():
        bucket = bucket_ref[...]
        for h in range(N_HEADS):
            kv, g_ = divmod(h, GQA_GROUP)
            bias_scr[kv, g_ * l:(g_ + 1) * l, :] = _bias_from_buckets(bucket, relb_ref, h)

    x = x_ref[...]
    qkv = jnp.dot(x.astype(BF16), wqkv_ref[0], preferred_element_type=F32) + bqkv_ref[...]
    qbuf[...] = qkv[:, :Q_DIM] * (HEAD_DIM ** -0.5)
    k_new = qkv[:, Q_DIM:Q_DIM + KV_DIM]
    v_new = qkv[:, Q_DIM + KV_DIM:]

    sink_cols = []
    for kv in range(N_KV_HEADS):
        sink_cols.append(jnp.concatenate(
            [jnp.full((l, 1), sink_ref[kv * GQA_GROUP + g_], F32) for g_ in range(GQA_GROUP)], axis=0))

    kn_t = k_new.T
    vn_t = v_new.T
    lane = lax.broadcasted_iota(jnp.int32, (HEAD_DIM, WINDOW), 1)
    pad = jnp.zeros((WINDOW - l, KV_DIM), F32)

    def shifted(old_t, new_t, i):
        kept = pltpu.roll(old_t, WINDOW - l, axis=1)
        new = pltpu.roll(new_t, (WINDOW - l - i * l) % WINDOW, axis=1)
        return jnp.where(lane >= WINDOW - l, new, kept)

    def keys_values(i):
        r0 = i * l
        for kv in range(N_KV_HEADS):
            hs = slice(kv * HEAD_DIM, (kv + 1) * HEAD_DIM)
            nk_ref[out_layer, i, kv] = shifted(ck_ref[0, i, kv], kn_t[hs, :], i)
            nv_ref[out_layer, i, kv] = shifted(cv_ref[0, i, kv], vn_t[hs, :], i)
        kn = jnp.concatenate([k_new[r0:r0 + l, :], pad], axis=0).astype(BF16)
        vn = jnp.concatenate([v_new[r0:r0 + l, :], pad], axis=0).astype(BF16)
        return kn, vn

    def scores(i, kv, kn):
        r0 = i * l
        c0 = kv * HEAD_DIM
        qs = jnp.concatenate(
            [qbuf[r0:r0 + l, (kv * GQA_GROUP + g_) * HEAD_DIM:(kv * GQA_GROUP + g_ + 1) * HEAD_DIM]
             for g_ in range(GQA_GROUP)], axis=0).astype(BF16)
        s_cache = jnp.dot(qs, ck_ref[0, i, kv].astype(BF16), preferred_element_type=F32)
        s_new = lax.dot_general(qs, kn[:, c0:c0 + HEAD_DIM], (((1,), (1,)), ((), ())), preferred_element_type=F32)
        return jnp.concatenate([s_cache, s_new], axis=1)

    def weighted_values(i, kv, p, vn):
        c0 = kv * HEAD_DIM
        pb = p.astype(BF16)
        o_cache = lax.dot_general(pb[:, :WINDOW], cv_ref[0, i, kv].astype(BF16), (((1,), (1,)), ((), ())),
                                  preferred_element_type=F32)
        return o_cache + jnp.dot(pb[:, WINDOW:], vn[:, c0:c0 + HEAD_DIM], preferred_element_type=F32)

    for i0 in range(0, bb, SAMPLE_CHAIN_BATCH):
        items = range(i0, min(bb, i0 + SAMPLE_CHAIN_BATCH))
        kvs = {i: keys_values(i) for i in items}
        chains = [(i, kv) for i in items for kv in range(N_KV_HEADS)]
        s_all = [scores(i, kv, kvs[i][0]) + bias_scr[kv] for i, kv in chains]
        m_all = [jnp.maximum(jnp.max(s, axis=-1, keepdims=True), sink_cols[kv]) for s, (i, kv) in zip(s_all, chains)]
        p_all = [jnp.exp(s - m) for s, m in zip(s_all, m_all)]
        den_all = [jnp.sum(p, axis=-1, keepdims=True) + jnp.exp(sink_cols[kv] - m)
                   for p, m, (i, kv) in zip(p_all, m_all, chains)]
        o_all = [weighted_values(i, kv, p, kvs[i][1]) for p, (i, kv) in zip(p_all, chains)]
        for o, den, (i, kv) in zip(o_all, den_all, chains):
            o = o / den
            for g_ in range(GQA_GROUP):
                h = kv * GQA_GROUP + g_
                obuf[i * l:(i + 1) * l, h * HEAD_DIM:(h + 1) * HEAD_DIM] = o[g_ * l:(g_ + 1) * l, :]

    y = jnp.dot(obuf[...].astype(BF16), wo_ref[0], preferred_element_type=F32) + bo_ref[...]
    o_ref[...] = _layer_norm_rows(DEEPNORM_ALPHA * x + y, g_ref[...], b_ref[...])


def _attn_sample(x, cache_k_t, cache_v_t, wqkv, bqkv, wo, bo, g, b, rel_bias, sinks, *, layer, prev=None, bb=16):
    bsz, l, d = x.shape
    assert bsz % bb == 0 and l == SUBLANES and bb * l == WINDOW
    ti = np.arange(l)[:, None]
    ci = np.arange(SAMPLE_KEYS)[None, :]
    table = _t5_bucket_table(ti + WINDOW - ci)
    table = np.where(ci < WINDOW + l, table, -1).astype(np.int32)
    bucket = jnp.asarray(table)
    qkv_dim = wqkv.shape[-1]
    rows = bb * l
    smem = pl.BlockSpec(memory_space=pltpu.SMEM)
    cache_spec = pl.BlockSpec((1, bb, N_KV_HEADS, HEAD_DIM, WINDOW), lambda i: (layer, i, 0, 0, 0))
    est = (d * qkv_dim + Q_DIM * d) * 2 + 8 * bb * WINDOW * KV_DIM * 4 + 8 * rows * d * 4
    in_specs = [
        pl.BlockSpec((rows, d), lambda i: (i, 0)),
        cache_spec, cache_spec,
        _layer_resident((d, qkv_dim), layer), _resident((1, qkv_dim)),
        _layer_resident((Q_DIM, d), layer), _resident((1, d)),
        _resident((1, d)), _resident((1, d)),
        _resident((l, SAMPLE_KEYS)),
        smem, smem,
    ]
    args = [x.reshape(bsz * l, d), cache_k_t, cache_v_t, wqkv, bqkv, wo, bo, g, b, bucket, rel_bias, sinks]
    n_layers = cache_k_t.shape[0]
    if prev is None:
        aliases, out_layer = {}, layer
        out_cache_spec = pl.BlockSpec((n_layers, bb, N_KV_HEADS, HEAD_DIM, WINDOW), lambda i: (0, i, 0, 0, 0))
    else:
        aliases, out_layer = {len(args): 1, len(args) + 1: 2}, 0
        out_cache_spec = cache_spec
        in_specs += [pl.BlockSpec(memory_space=pl.ANY)] * 2
        args += list(prev)
    return pl.pallas_call(
        functools.partial(_attn_sample_kernel, bb=bb, l=l, n_prev=len(aliases), out_layer=out_layer),
        grid=(bsz // bb,),
        in_specs=in_specs,
        out_specs=[pl.BlockSpec((rows, d), lambda i: (i, 0)), out_cache_spec, out_cache_spec],
        out_shape=[
            jax.ShapeDtypeStruct((bsz * l, d), F32),
            jax.ShapeDtypeStruct(cache_k_t.shape, F32),
            jax.ShapeDtypeStruct(cache_v_t.shape, F32),
        ],
        scratch_shapes=[
            pltpu.VMEM((rows, Q_DIM), F32),
            pltpu.VMEM((rows, Q_DIM), F32),
            pltpu.VMEM((N_KV_HEADS, GQA_GROUP * l, SAMPLE_KEYS), F32),
        ],
        input_output_aliases=aliases,
        compiler_params=pltpu.CompilerParams(
            dimension_semantics=("arbitrary",), vmem_limit_bytes=_vmem_limit(est)),
        name="attn_sample",
    )(*args)


POOL_WINDOWS = (2, 4, 8, 16)
POOL_PAD = max(POOL_WINDOWS)
POOL_STATE_LEN = POOL_PAD - 1
POOL_HIST = len(POOL_WINDOWS) * SUBLANES


def _pool_mix(window_sum, x, cnt_of, pw_ref, scale):
    gd = x.shape[-1] // len(POOL_WINDOWS)
    outs = []
    for g, w in enumerate(POOL_WINDOWS):
        diff = window_sum(g, w) / cnt_of(w) - x[:, g * gd:(g + 1) * gd]
        outs.append(jnp.dot(diff.astype(BF16), pw_ref[g], preferred_element_type=F32))
    return jnp.concatenate(outs, axis=-1) * scale


def _pool_prompt_kernel(x_ref, pw_ref, sc_ref, g_ref, b_ref, o_ref, st_ref, buf, lvl, *, tm):
    j = pl.program_id(1)
    nj = pl.num_programs(1)
    d = x_ref.shape[-1]
    gd = d // len(POOL_WINDOWS)
    hist, rows = POOL_HIST, POOL_HIST + tm

    @pl.when(j == 0)
    def _no_past():
        buf[0:hist, :] = jnp.zeros((hist, d), F32)

    @pl.when(j > 0)
    def _carry():
        buf[0:hist, :] = buf[tm:tm + hist, :]

    x = x_ref[0]
    buf[hist:rows, :] = x

    sums = {}
    src = buf
    for k, w in enumerate(POOL_WINDOWS):
        step, c0, r0 = w // 2, k * gd, (k + 1) * SUBLANES
        cur = src[r0:rows, c0:d] + src[r0 - step:rows - step, c0:d]
        sums[k] = cur[hist - r0:, 0:gd]
        if k + 1 < len(POOL_WINDOWS):
            lvl[k, r0:rows, c0:d] = cur
            src = lvl.at[k]

    def window_sum(g, w):
        return sums[g]

    pos1 = j * tm + lax.broadcasted_iota(jnp.int32, (tm, 1), 0) + 1

    def cnt_of(w):
        return jnp.minimum(pos1, w).astype(F32)

    y = _pool_mix(window_sum, x, cnt_of, pw_ref, sc_ref[...])
    o_ref[0] = _layer_norm_rows(DEEPNORM_ALPHA * x + y, g_ref[...], b_ref[...])

    @pl.when(j == nj - 1)
    def _emit_state():
        st_ref[0] = buf[rows - POOL_STATE_LEN:rows, :]


def _pool_prompt(x, pw, scale, g, b, *, tm=512):
    bsz, l, d = x.shape
    tm = min(tm, l)
    assert l % tm == 0 and tm >= POOL_HIST >= POOL_STATE_LEN and POOL_WINDOWS == (2, 4, 8, 16)
    ng, gd = pw.shape[0], pw.shape[1]
    est = 10 * tm * d * 4 + ng * gd * gd * 2
    return pl.pallas_call(
        functools.partial(_pool_prompt_kernel, tm=tm),
        grid=(bsz, l // tm),
        in_specs=[
            pl.BlockSpec((1, tm, d), lambda bi, j: (bi, j, 0)),
            _resident((ng, gd, gd)), _resident((1, d)), _resident((1, d)), _resident((1, d)),
        ],
        out_specs=[
            pl.BlockSpec((1, tm, d), lambda bi, j: (bi, j, 0)),
            pl.BlockSpec((1, POOL_STATE_LEN, d), lambda bi, j: (bi, 0, 0)),
        ],
        out_shape=[
            jax.ShapeDtypeStruct((bsz, l, d), F32),
            jax.ShapeDtypeStruct((bsz, POOL_STATE_LEN, d), F32),
        ],
        scratch_shapes=[pltpu.VMEM((POOL_HIST + tm, d), F32),
                        pltpu.VMEM((len(POOL_WINDOWS) - 1, POOL_HIST + tm, d), F32)],
        compiler_params=pltpu.CompilerParams(
            dimension_semantics=("arbitrary", "arbitrary"), vmem_limit_bytes=_vmem_limit(est)),
        name="pool_prompt",
    )(x, pw, scale, g, b)


def _pool_sample_kernel(x_ref, st_ref, pw_ref, sc_ref, g_ref, b_ref, o_ref, nst_ref, buf, *, bb, l):
    d = x_ref.shape[-1]
    gd = d // len(POOL_WINDOWS)
    x3 = x_ref[...]
    buf[:, 1:POOL_PAD, :] = st_ref[...]
    buf[:, POOL_PAD:POOL_PAD + l, :] = x3
    x = x3.reshape(bb * l, d)

    def window_sum(g, w):
        acc = x3[:, :, g * gd:(g + 1) * gd]
        for s in range(1, w):
            acc = acc + buf[:, POOL_PAD - s:POOL_PAD - s + l, g * gd:(g + 1) * gd]
        return acc.reshape(bb * l, gd)

    y = _pool_mix(window_sum, x, lambda w: float(w), pw_ref, sc_ref[...])
    o_ref[...] = _layer_norm_rows(DEEPNORM_ALPHA * x + y, g_ref[...], b_ref[...]).reshape(bb, l, d)
    nst_ref[...] = buf[:, l + 1:l + POOL_PAD, :]


def _pool_sample(x, state, pw, scale, g, b, *, start, bb=16):
    bsz, l, d = x.shape
    assert bsz % bb == 0 and l == SUBLANES and start + 1 >= POOL_PAD
    ng, gd = pw.shape[0], pw.shape[1]
    est = 8 * bb * (POOL_PAD + l) * d * 4 + ng * gd * gd * 2
    return pl.pallas_call(
        functools.partial(_pool_sample_kernel, bb=bb, l=l),
        grid=(bsz // bb,),
        in_specs=[
            pl.BlockSpec((bb, l, d), lambda i: (i, 0, 0)),
            pl.BlockSpec((bb, POOL_STATE_LEN, d), lambda i: (i, 0, 0)),
            _resident((ng, gd, gd)), _resident((1, d)), _resident((1, d)), _resident((1, d)),
        ],
        out_specs=[
            pl.BlockSpec((bb, l, d), lambda i: (i, 0, 0)),
            pl.BlockSpec((bb, POOL_STATE_LEN, d), lambda i: (i, 0, 0)),
        ],
        out_shape=[
            jax.ShapeDtypeStruct((bsz, l, d), F32),
            jax.ShapeDtypeStruct((bsz, POOL_STATE_LEN, d), F32),
        ],
        scratch_shapes=[pltpu.VMEM((bb, POOL_PAD + l, d), F32)],
        compiler_params=pltpu.CompilerParams(
            dimension_semantics=("arbitrary",), vmem_limit_bytes=_vmem_limit(est)),
        name="pool_sample",
    )(x, state, pw, scale, g, b)


D_INNER = 2048
SSM_HEAD_DIM = 64
SSM_HEADS = D_INNER // SSM_HEAD_DIM
SSM_GROUPS = 4
SSM_HPG = SSM_HEADS // SSM_GROUPS
D_STATE = 128
CONV_WIDTH = 4
GBN = SSM_GROUPS * D_STATE
CONV_DIM = D_INNER + 2 * GBN
GROUP_INNER = D_INNER // SSM_GROUPS
RMS_EPS = 1e-5
CONV_PAD = SUBLANES
SSM_ROWS = 128


def _split_bf16(v, parts=3):
    out = []
    r = v
    for _ in range(parts):
        p = r.astype(BF16)
        out.append(p)
        r = r - p.astype(F32)
    return out


def _expand_heads(v, e):
    return sum(jnp.dot(p, e, preferred_element_type=F32) for p in _split_bf16(v, parts=2))


def _ssm_in_proj(xb, win_ref, wdt_ref, c0, c1):
    main = win_ref.shape[1]
    if c0 >= main:
        return jnp.dot(xb, wdt_ref[:, c0 - main:c1 - main], preferred_element_type=F32)
    assert c1 <= main
    return jnp.dot(xb, win_ref[:, c0:c1], preferred_element_type=F32)


def _ssm_prep(proj, xp_buf, conv_state, dtb_ref, alog_ref, tril_ref, *, nseq, q):
    rows = nseq * q
    xbc_pre = proj(D_INNER, D_INNER + CONV_DIM)
    dt_pre = proj(D_INNER + CONV_DIM, D_INNER + CONV_DIM + LANES)
    if conv_state is not None:
        xp_buf[:, CONV_PAD - (CONV_WIDTH - 1):CONV_PAD, :] = conv_state
    xp_buf[:, CONV_PAD:CONV_PAD + q, :] = xbc_pre.reshape(nseq, q, CONV_DIM)
    new_conv = xp_buf[:, q + CONV_PAD - (CONV_WIDTH - 1):q + CONV_PAD, :]

    dtv = dt_pre + dtb_ref[...]
    dt = jnp.maximum(dtv, 0.0) + jnp.log1p(jnp.exp(-jnp.abs(dtv)))
    a = -jnp.exp(alog_ref[...])
    tril = tril_ref[...]
    acum = sum(jnp.dot(tril, p, preferred_element_type=F32) for p in _split_bf16(dt * a))
    a3 = acum.reshape(nseq, q, LANES)
    alast = jnp.broadcast_to(a3[:, q - 1:q, :], (nseq, q, LANES)).reshape(rows, LANES)
    return dict(acum=acum, acum_t=acum.T, dt_t=dt.T, causal=tril > 0.5,
                exp_a=jnp.exp(acum), dd=jnp.exp(alast - acum) * dt, new_conv=new_conv)


def _conv_silu(xp_buf, cw_ref, cb_ref, c0, c1, *, nseq, q):
    acc = cb_ref[:, c0:c1].reshape(1, 1, c1 - c0)
    for jj in range(CONV_WIDTH):
        off = CONV_PAD - (CONV_WIDTH - 1) + jj
        acc = acc + xp_buf[:, off:off + q, c0:c1] * cw_ref[jj:jj + 1, c0:c1].reshape(1, 1, c1 - c0)
    acc = acc.reshape(nseq * q, c1 - c0)
    return acc * jax.nn.sigmoid(acc)


def _ssm_group_diag(g, f, proj, prep, xp_buf, cw_ref, cb_ref, ybuf, *, nseq, q):
    f["z"] = proj(g * GROUP_INNER, (g + 1) * GROUP_INNER)
    conv = functools.partial(_conv_silu, xp_buf, cw_ref, cb_ref, nseq=nseq, q=q)
    half = GROUP_INNER // 2
    xs_lo = conv(g * GROUP_INNER, g * GROUP_INNER + half)
    yield
    xs_hi = conv(g * GROUP_INNER + half, (g + 1) * GROUP_INNER)
    yield
    xs = f["xs"] = jnp.concatenate([xs_lo, xs_hi], axis=-1)
    bm = f["bm"] = conv(D_INNER + g * D_STATE, D_INNER + (g + 1) * D_STATE)
    cm = f["cm"] = conv(D_INNER + GBN + g * D_STATE, D_INNER + GBN + (g + 1) * D_STATE)
    yield
    cb = lax.dot_general(cm.astype(BF16), bm.astype(BF16), (((1,), (1,)), ((), ())), preferred_element_type=F32)
    acum, acum_t, dt_t, causal = prep["acum"], prep["acum_t"], prep["dt_t"], prep["causal"]
    for r in range(SSM_HPG):
        h = g * SSM_HPG + r
        seg = acum[:, h:h + 1] - acum_t[h:h + 1, :]
        lm = jnp.exp(jnp.where(causal, seg, NEG_INF))
        w = (cb * lm * dt_t[h:h + 1, :]).astype(BF16)
        xh = xs[:, r * SSM_HEAD_DIM:(r + 1) * SSM_HEAD_DIM].astype(BF16)
        ybuf[:, h * SSM_HEAD_DIM:(h + 1) * SSM_HEAD_DIM] = jnp.dot(w, xh, preferred_element_type=F32)
        yield


def _ssm_group_expand(g, f, prep, e_ref):
    e = e_ref[:, g * GROUP_INNER:(g + 1) * GROUP_INNER]
    f["exp_a"] = _expand_heads(prep["exp_a"], e)
    yield
    f["xdd"] = (f["xs"] * _expand_heads(prep["dd"], e)).astype(BF16)
    yield


def _interleave(*gens):
    live = list(gens)
    while live:
        for gen in list(live):
            try:
                next(gen)
            except StopIteration:
                live.remove(gen)


def _gate_norm(y, z, nw):
    y = y * (z * jax.nn.sigmoid(z))
    return y * lax.rsqrt(jnp.mean(y * y, axis=-1, keepdims=True) + RMS_EPS) * nw


def _ssm_prompt_kernel(x_ref, win_ref, wdt_ref, cw_ref, cb_ref, dtb_ref, alog_ref, dsk_ref, nw_ref, wout_ref, e_ref,
                       tril_ref, g_ref, b_ref, o_ref, cout_ref, sout_ref, xp_buf, ht, ybuf):
    j = pl.program_id(1)
    nj = pl.num_programs(1)
    q = SSM_ROWS

    @pl.when(j == 0)
    def _no_past():
        xp_buf[:, 0:CONV_PAD, :] = jnp.zeros((1, CONV_PAD, CONV_DIM), F32)
        ht[...] = jnp.zeros(ht.shape, F32)

    @pl.when(j > 0)
    def _carry():
        xp_buf[:, 0:CONV_PAD, :] = xp_buf[:, q:q + CONV_PAD, :]

    x = x_ref[0]
    xb = x.astype(BF16)
    zx = jnp.dot(xb, win_ref[...], preferred_element_type=F32)
    dt_pre = jnp.dot(xb, wdt_ref[...], preferred_element_type=F32)
    z = zx[:, :D_INNER]

    def proj(c0, c1):
        return dt_pre if c0 >= D_INNER + CONV_DIM else zx[:, c0:c1]

    prep = _ssm_prep(proj, xp_buf, None, dtb_ref, alog_ref, tril_ref, nseq=1, q=q)
    xbc = _conv_silu(xp_buf, cw_ref, cb_ref, 0, CONV_DIM, nseq=1, q=q)
    xs = xbc[:, :D_INNER]
    acum, acum_t, dt_t, causal = prep["acum"], prep["acum_t"], prep["dt_t"], prep["causal"]
    for g in range(SSM_GROUPS):
        bg = xbc[:, D_INNER + g * D_STATE:D_INNER + (g + 1) * D_STATE].astype(BF16)
        cg = xbc[:, D_INNER + GBN + g * D_STATE:D_INNER + GBN + (g + 1) * D_STATE].astype(BF16)
        cb = lax.dot_general(cg, bg, (((1,), (1,)), ((), ())), preferred_element_type=F32)
        for r in range(SSM_HPG):
            h = g * SSM_HPG + r
            seg = acum[:, h:h + 1] - acum_t[h:h + 1, :]
            lm = jnp.exp(jnp.where(causal, seg, NEG_INF))
            w = (cb * lm * dt_t[h:h + 1, :]).astype(BF16)
            xh = xs[:, h * SSM_HEAD_DIM:(h + 1) * SSM_HEAD_DIM].astype(BF16)
            ybuf[:, h * SSM_HEAD_DIM:(h + 1) * SSM_HEAD_DIM] = jnp.dot(w, xh, preferred_element_type=F32)

    e = e_ref[...]
    exp_a = _expand_heads(prep["exp_a"], e)
    xdd = (xs * _expand_heads(prep["dd"], e)).astype(BF16)
    decay = exp_a[q - 1:q, :]
    y_off = []
    for g in range(SSM_GROUPS):
        sl = slice(g * GROUP_INNER, (g + 1) * GROUP_INNER)
        bg = xbc[:, D_INNER + g * D_STATE:D_INNER + (g + 1) * D_STATE]
        cg = xbc[:, D_INNER + GBN + g * D_STATE:D_INNER + GBN + (g + 1) * D_STATE].astype(BF16)
        hg = ht[:, sl]
        y_off.append(jnp.dot(cg, hg.astype(BF16), preferred_element_type=F32))
        ht[:, sl] = hg * decay[:, sl] + jnp.dot(bg.T.astype(BF16), xdd[:, sl], preferred_element_type=F32)
    y = ybuf[...] + exp_a * jnp.concatenate(y_off, axis=-1) + dsk_ref[...] * xs
    y = jnp.concatenate(
        [_gate_norm(y[:, g * GROUP_INNER:(g + 1) * GROUP_INNER], z[:, g * GROUP_INNER:(g + 1) * GROUP_INNER],
                    nw_ref[:, g * GROUP_INNER:(g + 1) * GROUP_INNER]) for g in range(SSM_GROUPS)], axis=-1)
    out = jnp.dot(y.astype(BF16), wout_ref[...], preferred_element_type=F32)
    o_ref[0] = _layer_norm_rows(DEEPNORM_ALPHA * x + out, g_ref[...], b_ref[...])

    @pl.when(j == nj - 1)
    def _emit_state():
        cout_ref[...] = prep["new_conv"]
        sout_ref[0] = ht[...].T


def _ssm_consts(nseq, q):
    rows = nseq * q
    r = np.arange(rows)
    tril = ((r[:, None] >= r[None, :]) & (r[:, None] // q == r[None, :] // q)).astype(np.float32)
    e = np.zeros((LANES, D_INNER), np.float32)
    e[np.arange(D_INNER) // SSM_HEAD_DIM, np.arange(D_INNER)] = 1.0
    return jnp.asarray(tril, BF16), jnp.asarray(e, BF16)


def _ssm_weight_specs(d):
    return [
        pl.BlockSpec((d, D_INNER + CONV_DIM), lambda *_: (0, 0), pipeline_mode=pl.Buffered(1)),
        _resident((d, LANES)), _resident((CONV_WIDTH, CONV_DIM)), _resident((1, CONV_DIM)),
        _resident((1, LANES)), _resident((1, LANES)),
    ]


def _ssm_prompt(x, w, g, b):
    bsz, l, d = x.shape
    q = SSM_ROWS
    assert l % q == 0
    tril, e = _ssm_consts(1, q)
    win_cols = D_INNER + CONV_DIM + LANES
    est = (d * win_cols + D_INNER * d) * 2 + LANES * D_INNER * 4 + 12 * q * D_INNER * 4 + 3 * q * win_cols * 4
    return pl.pallas_call(
        _ssm_prompt_kernel,
        grid=(bsz, l // q),
        in_specs=[pl.BlockSpec((1, q, d), lambda bi, j: (bi, j, 0))] + _ssm_weight_specs(d) + [
            _resident((1, D_INNER)), _resident((1, D_INNER)), _resident((D_INNER, d)),
            _resident((LANES, D_INNER)), _resident((q, q)), _resident((1, d)), _resident((1, d)),
        ],
        out_specs=[
            pl.BlockSpec((1, q, d), lambda bi, j: (bi, j, 0)),
            pl.BlockSpec((1, CONV_WIDTH - 1, CONV_DIM), lambda bi, j: (bi, 0, 0)),
            pl.BlockSpec((1, D_INNER, D_STATE), lambda bi, j: (bi, 0, 0)),
        ],
        out_shape=[
            jax.ShapeDtypeStruct((bsz, l, d), F32),
            jax.ShapeDtypeStruct((bsz, CONV_WIDTH - 1, CONV_DIM), F32),
            jax.ShapeDtypeStruct((bsz, D_INNER, D_STATE), F32),
        ],
        scratch_shapes=[
            pltpu.VMEM((1, CONV_PAD + q, CONV_DIM), F32),
            pltpu.VMEM((D_STATE, D_INNER), F32),
            pltpu.VMEM((q, D_INNER), F32),
        ],
        compiler_params=pltpu.CompilerParams(
            dimension_semantics=("arbitrary", "arbitrary"), vmem_limit_bytes=_vmem_limit(est)),
        name="ssm_prompt",
    )(x, w["win"], w["wdt"], w["cw"], w["cb"], w["dtb"], w["alog"], w["dsk"], w["nw"], w["wout"], e, tril, g, b)


def _ssm_sample_front_kernel(x_ref, cst_ref, win_ref, wdt_ref, cw_ref, cb_ref, dtb_ref, alog_ref, dsk_ref, e_ref, tril_ref,
                             y_ref, z_ref, ea_ref, xdd_ref, c_ref, bt_ref, cout_ref, xp_buf, ybuf, *, nseq, q):
    xb = x_ref[...].astype(BF16)

    def proj(c0, c1):
        return _ssm_in_proj(xb, win_ref, wdt_ref, c0, c1)

    prep = _ssm_prep(proj, xp_buf, cst_ref[...], dtb_ref, alog_ref, tril_ref, nseq=nseq, q=q)
    cout_ref[...] = prep["new_conv"]
    for g in range(SSM_GROUPS):
        sl = slice(g * GROUP_INNER, (g + 1) * GROUP_INNER)
        f = {}
        _interleave(_ssm_group_diag(g, f, proj, prep, xp_buf, cw_ref, cb_ref, ybuf, nseq=nseq, q=q))
        _interleave(_ssm_group_expand(g, f, prep, e_ref))
        y_ref[:, sl] = ybuf[:, sl] + dsk_ref[:, sl] * f["xs"]
        z_ref[:, sl] = f["z"]
        ea_ref[:, sl] = f["exp_a"]
        xdd_ref[:, sl] = f["xdd"]
        c_ref[:, g * D_STATE:(g + 1) * D_STATE] = f["cm"]
        bt_ref[g * D_STATE:(g + 1) * D_STATE, :] = f["bm"].T.astype(BF16)


def _ssm_sample_state_kernel(st_ref, y_ref, ea_ref, c_ref, xdd_ref, bt_ref, o_ref, nst_ref, *, bb, q, nseq):
    ea = ea_ref[...]
    xdd = xdd_ref[...]
    first_seq = (pl.program_id(0) % (nseq // bb)) * bb
    lane_seq = lax.broadcasted_iota(jnp.int32, (D_STATE, nseq * q), 1) // q
    for i in range(bb):
        r0 = i * q
        ht = st_ref[i].T
        decay = ea[r0 + q - 1:r0 + q, :]
        y_parts = []
        new_parts = []
        for g in range(SSM_GROUPS):
            sl = slice(g * GROUP_INNER, (g + 1) * GROUP_INNER)
            cg = c_ref[r0:r0 + q, g * D_STATE:(g + 1) * D_STATE].astype(BF16)
            hg = ht[:, sl]
            y_parts.append(jnp.dot(cg, hg.astype(BF16), preferred_element_type=F32))
            btg = bt_ref[g * D_STATE:(g + 1) * D_STATE, :]
            btg = jnp.where(lane_seq == first_seq + i, btg, jnp.zeros_like(btg))
            new_parts.append(hg * decay[:, sl] + jnp.dot(btg, xdd[:, sl], preferred_element_type=F32))
        o_ref[r0:r0 + q, :] = y_ref[r0:r0 + q, :] + ea[r0:r0 + q, :] * jnp.concatenate(y_parts, axis=-1)
        nst_ref[i] = jnp.concatenate(new_parts, axis=-1).T


def _ssm_sample_finish_kernel(x_ref, y_ref, z_ref, nw_ref, wout_ref, g_ref, b_ref, o_ref):
    x = x_ref[...]
    out = jnp.zeros(x.shape, F32)
    for g in range(SSM_GROUPS):
        sl = slice(g * GROUP_INNER, (g + 1) * GROUP_INNER)
        y = _gate_norm(y_ref[:, sl], z_ref[:, sl], nw_ref[:, sl])
        out = out + jnp.dot(y.astype(BF16), wout_ref[sl, :], preferred_element_type=F32)
    o_ref[...] = _layer_norm_rows(DEEPNORM_ALPHA * x + out, g_ref[...], b_ref[...])


def _ssm_sample(x, conv_state, ssm_state, w, g, b, *, bb_state=8):
    bsz, q, d = x.shape
    t = bsz * q
    nseq = SSM_ROWS // q
    rows = SSM_ROWS
    assert q == SUBLANES and bsz % nseq == 0 and nseq % bb_state == 0
    tril, e = _ssm_consts(nseq, q)
    win_cols = D_INNER + CONV_DIM + LANES
    x2 = x.reshape(t, d)
    row_spec = lambda c: pl.BlockSpec((rows, c), lambda i: (i, 0))
    est = d * win_cols * 2 + LANES * D_INNER * 4 + 16 * rows * D_INNER * 4 + 3 * rows * win_cols * 4
    ydiag, z, exp_a, xdd, cmat, bt, new_conv = pl.pallas_call(
        functools.partial(_ssm_sample_front_kernel, nseq=nseq, q=q),
        grid=(t // rows,),
        in_specs=[row_spec(d), pl.BlockSpec((nseq, CONV_WIDTH - 1, CONV_DIM), lambda i: (i, 0, 0))]
        + _ssm_weight_specs(d) + [_resident((1, D_INNER)), _resident((LANES, D_INNER)), _resident((rows, rows))],
        out_specs=[row_spec(D_INNER), row_spec(D_INNER), row_spec(D_INNER), row_spec(D_INNER), row_spec(GBN),
                   pl.BlockSpec((GBN, rows), lambda i: (i, 0)),
                   pl.BlockSpec((nseq, CONV_WIDTH - 1, CONV_DIM), lambda i: (i, 0, 0))],
        out_shape=[
            jax.ShapeDtypeStruct((t, D_INNER), F32), jax.ShapeDtypeStruct((t, D_INNER), F32),
            jax.ShapeDtypeStruct((t, D_INNER), F32), jax.ShapeDtypeStruct((t, D_INNER), BF16),
            jax.ShapeDtypeStruct((t, GBN), F32),
            jax.ShapeDtypeStruct((t // rows * GBN, rows), BF16),
            jax.ShapeDtypeStruct((bsz, CONV_WIDTH - 1, CONV_DIM), F32),
        ],
        scratch_shapes=[pltpu.VMEM((nseq, CONV_PAD + q, CONV_DIM), F32), pltpu.VMEM((rows, D_INNER), F32)],
        compiler_params=pltpu.CompilerParams(
            dimension_semantics=("arbitrary",), vmem_limit_bytes=_vmem_limit(est)),
        name="ssm_sample_front",
    )(x2, conv_state, w["win"], w["wdt"], w["cw"], w["cb"], w["dtb"], w["alog"], w["dsk"], e, tril)

    srows = bb_state * q
    srow_spec = lambda c: pl.BlockSpec((srows, c), lambda i: (i, 0))
    st_spec = pl.BlockSpec((bb_state, D_INNER, D_STATE), lambda i: (i, 0, 0))
    est = 4 * bb_state * D_INNER * D_STATE * 4 + 8 * D_INNER * D_STATE * 4
    per_block = nseq // bb_state
    y, new_state = pl.pallas_call(
        functools.partial(_ssm_sample_state_kernel, bb=bb_state, q=q, nseq=nseq),
        grid=(bsz // bb_state,),
        in_specs=[st_spec, srow_spec(D_INNER), srow_spec(D_INNER), srow_spec(GBN),
                  pl.BlockSpec((rows, D_INNER), lambda i: (i // per_block, 0)),
                  pl.BlockSpec((GBN, rows), lambda i: (i // per_block, 0))],
        out_specs=[srow_spec(D_INNER), st_spec],
        out_shape=[jax.ShapeDtypeStruct((t, D_INNER), F32),
                   jax.ShapeDtypeStruct((bsz, D_INNER, D_STATE), F32)],
        compiler_params=pltpu.CompilerParams(
            dimension_semantics=("arbitrary",), vmem_limit_bytes=_vmem_limit(est)),
        name="ssm_sample_state",
    )(ssm_state, ydiag, exp_a, cmat, xdd, bt)

    est = D_INNER * d * 2 + 8 * rows * D_INNER * 4
    out = pl.pallas_call(
        _ssm_sample_finish_kernel,
        grid=(t // rows,),
        in_specs=[row_spec(d), row_spec(D_INNER), row_spec(D_INNER),
                  _resident((1, D_INNER)), _resident((D_INNER, d)), _resident((1, d)), _resident((1, d))],
        out_specs=row_spec(d),
        out_shape=jax.ShapeDtypeStruct((t, d), F32),
        compiler_params=pltpu.CompilerParams(
            dimension_semantics=("arbitrary",), vmem_limit_bytes=_vmem_limit(est)),
        name="ssm_sample_finish",
    )(x2, y, z, w["nw"], w["wout"], g, b)
    return out.reshape(bsz, q, d), new_conv, new_state


def _ssm_weights(w_in, conv_w, conv_b, dt_bias, a_log, d_skip, norm_w, w_out):
    d = w_in.shape[0]
    pad = LANES - SSM_HEADS
    wdt = jnp.pad(w_in[:, D_INNER + CONV_DIM:], ((0, 0), (0, pad))).astype(BF16)
    return dict(
        win=w_in.astype(BF16), wdt=wdt, cw=conv_w, cb=conv_b.reshape(1, CONV_DIM),
        dtb=jnp.pad(dt_bias, (0, pad)).reshape(1, LANES),
        alog=jnp.pad(a_log, (0, pad)).reshape(1, LANES),
        dsk=jnp.repeat(d_skip, SSM_HEAD_DIM).reshape(1, D_INNER),
        nw=norm_w.reshape(1, D_INNER), wout=w_out.astype(BF16))


def kernel(x_prompt, x_sample, cache_k, cache_v, state_conv, state_ssm, state_pool, rel_bias, attn_w_qkv, attn_b_qkv, attn_w_o, attn_b_o, attn_sinks, ssm_w_in, ssm_conv_w, ssm_conv_b, ssm_dt_bias, ssm_a_log, ssm_d, ssm_norm_w, ssm_w_out, pool_w, pool_scale, ffn_w_gate, ffn_w_up, ffn_w_down, ln_g, ln_b):
    xp, xs = x_prompt, x_sample
    d = xp.shape[-1]
    wqkv, wo = attn_w_qkv.astype(BF16), attn_w_o.astype(BF16)
    wg, wu, wd = ffn_w_gate.astype(BF16), ffn_w_up.astype(BF16), ffn_w_down.astype(BF16)
    cache_k_t = jnp.transpose(cache_k, (0, 1, 3, 4, 2))
    cache_v_t = jnp.transpose(cache_v, (0, 1, 3, 4, 2))
    new_cache_t = None
    nk_p, nv_p, nc_p, nh_p, npool_p = [], [], [], [], []
    nc_s, nh_s, npool_s = [], [], []
    for i in range(DEPTH):
        j = i // N_MIXERS
        kind = i % N_MIXERS
        g1 = ln_g[i, 0].reshape(1, d)
        b1 = ln_b[i, 0].reshape(1, d)
        if kind == 0:
            bqkv = attn_b_qkv[j].reshape(1, -1)
            bo = attn_b_o[j].reshape(1, d)
            xp, kp, vp = _attn_prompt(xp, wqkv, bqkv, wo, bo, g1, b1, rel_bias, attn_sinks[j], layer=j)
            xs, *new_cache_t = _attn_sample(xs, cache_k_t, cache_v_t, wqkv, bqkv, wo, bo, g1, b1, rel_bias,
                                            attn_sinks[j], layer=j, prev=new_cache_t)
            xs = xs.reshape(x_sample.shape)
            kv_shape = (-1, WINDOW, N_KV_HEADS, HEAD_DIM)
            nk_p.append(kp.reshape(kv_shape)); nv_p.append(vp.reshape(kv_shape))
        elif kind == 1:
            w = _ssm_weights(ssm_w_in[j], ssm_conv_w[j], ssm_conv_b[j], ssm_dt_bias[j], ssm_a_log[j],
                             ssm_d[j], ssm_norm_w[j], ssm_w_out[j])
            xp, cp, hp = _ssm_prompt(xp, w, g1, b1)
            xs, cs_, hs_ = _ssm_sample(xs, state_conv[j], state_ssm[j].reshape(-1, D_INNER, D_STATE), w, g1, b1)
            st_shape = (-1, SSM_HEADS, SSM_HEAD_DIM, D_STATE)
            nc_p.append(cp); nh_p.append(hp.reshape(st_shape))
            nc_s.append(cs_); nh_s.append(hs_.reshape(st_shape))
        else:
            pw = pool_w[j].astype(BF16)
            psc = pool_scale[j].reshape(1, d)
            xp, pp = _pool_prompt(xp, pw, psc, g1, b1)
            xs, ps_ = _pool_sample(xs, state_pool[j], pw, psc, g1, b1, start=PAST_LEN)
            npool_p.append(pp); npool_s.append(ps_)
        g2 = ln_g[i, 1].reshape(1, d)
        b2 = ln_b[i, 1].reshape(1, d)
        xp = _ffn_ln(xp.reshape(-1, d), wg, wu, wd, g2, b2, layer=i).reshape(xp.shape)
        xs = _ffn_ln(xs.reshape(-1, d), wg, wu, wd, g2, b2, layer=i).reshape(xs.shape)
    nk_s, nv_s = (jnp.transpose(t, (0, 1, 4, 2, 3)) for t in new_cache_t)
    return (xp, xs,
            jnp.stack(nk_p), jnp.stack(nv_p), jnp.stack(nc_p), jnp.stack(nh_p), jnp.stack(npool_p),
            nk_s, nv_s, jnp.stack(nc_s), jnp.stack(nh_s), jnp.stack(npool_s))
```

```python
import functools
import math

import jax
import jax.numpy as jnp
import numpy as np
from jax import lax
from jax.experimental import pallas as pl
from jax.experimental.pallas import tpu as pltpu

DEPTH = 4
N_MIXERS = 3
PAST_LEN = 8192
DEEPNORM_ALPHA = (2 * DEPTH) ** 0.25
LN_EPS = 1e-5

V7X_VMEM_BYTES = 64 * 1024 * 1024
LANES = 128
SUBLANES = 8
MXU_DIM = 256

BF16 = jnp.bfloat16
F32 = jnp.float32


def _vmem_limit(estimate_bytes):
    return int(min(V7X_VMEM_BYTES - 8 * 1024 * 1024, max(32 * 1024 * 1024, estimate_bytes * 3 // 2)))


def _layer_norm_rows(v, g, b):
    mu = jnp.mean(v, axis=-1, keepdims=True)
    d = v - mu
    var = jnp.mean(d * d, axis=-1, keepdims=True)
    return d * lax.rsqrt(var + LN_EPS) * g + b


def _resident(shape):
    nd = len(shape)
    return pl.BlockSpec(shape, lambda *_: (0,) * nd, pipeline_mode=pl.Buffered(1))


def _layer_resident(shape, layer):
    nd = len(shape)
    return pl.BlockSpec((1,) + tuple(shape), lambda *_: (layer,) + (0,) * nd, pipeline_mode=pl.Buffered(1))


def _ffn_chunks(d_ff):
    step = 2 * MXU_DIM
    edges = list(range(0, d_ff, step)) + [d_ff]
    return tuple(zip(edges[:-1], edges[1:]))


def _ffn_kernel(x_ref, wg_ref, wu_ref, wd_ref, g_ref, b_ref, *rest, chunks, n_cast):
    casts, (o_ref, *cast_outs) = rest[:n_cast], rest[n_cast:]
    for src, dst in zip(casts, cast_outs):
        dst[...] = src[0].astype(BF16)
    x = x_ref[...]
    xb = x.astype(BF16)
    acc = jnp.zeros(x.shape, F32)
    for c0, c1 in chunks:
        gate = jnp.dot(xb, wg_ref[0, :, c0:c1], preferred_element_type=F32)
        up = jnp.dot(xb, wu_ref[0, :, c0:c1], preferred_element_type=F32)
        h = (gate * jax.nn.sigmoid(gate)) * up
        acc = acc + jnp.dot(h.astype(BF16), wd_ref[0, c0:c1, :], preferred_element_type=F32)
    o_ref[...] = _layer_norm_rows(DEEPNORM_ALPHA * x + acc, g_ref[...], b_ref[...])


def _cast_slab_specs(w32, layer, steps):
    _, rows, cols = w32.shape
    slab = rows // steps if rows % (steps * 2 * SUBLANES) == 0 else LANES
    assert rows % slab == 0 and rows // slab <= steps
    last = rows // slab - 1
    return (pl.BlockSpec((1, slab, cols), lambda i: (layer, jnp.minimum(i, last), 0)),
            pl.BlockSpec((slab, cols), lambda i: (jnp.minimum(i, last), 0)),
            jax.ShapeDtypeStruct((rows, cols), BF16))


def _ffn_ln(x, wg, wu, wd, g, b, *, layer, cast_next=None, tm=512):
    t, d = x.shape
    d_ff = wg.shape[-1]
    tm = min(tm, t)
    assert t % tm == 0
    steps = t // tm
    est = 3 * d * d_ff * 2 + 4 * tm * d * 4 + 4 * tm * d_ff * 4
    in_specs = [
        pl.BlockSpec((tm, d), lambda i: (i, 0)),
        _layer_resident((d, d_ff), layer),
        _layer_resident((d, d_ff), layer),
        _layer_resident((d_ff, d), layer),
        _resident((1, d)),
        _resident((1, d)),
    ]
    out_specs = [pl.BlockSpec((tm, d), lambda i: (i, 0))]
    out_shape = [jax.ShapeDtypeStruct((t, d), F32)]
    args = [x, wg, wu, wd, g, b]
    n_cast = 0
    if cast_next is not None:
        weights32, nxt = cast_next
        for w32 in weights32:
            src_spec, dst_spec, dst_shape = _cast_slab_specs(w32, nxt, steps)
            in_specs.append(src_spec)
            out_specs.append(dst_spec)
            out_shape.append(dst_shape)
            args.append(w32)
        n_cast = len(weights32)
    outs = pl.pallas_call(
        functools.partial(_ffn_kernel, chunks=_ffn_chunks(d_ff), n_cast=n_cast),
        grid=(steps,),
        in_specs=in_specs,
        out_specs=out_specs,
        out_shape=out_shape,
        compiler_params=pltpu.CompilerParams(
            dimension_semantics=("arbitrary",), vmem_limit_bytes=_vmem_limit(est)),
        name="ffn_ln",
    )(*args)
    return outs[0], tuple(outs[1:])


HEAD_DIM = 64
N_HEADS = 16
N_KV_HEADS = 4
GQA_GROUP = N_HEADS // N_KV_HEADS
WINDOW = 128
REL_BUCKETS = 32
REL_MAX_DIST = 128
Q_DIM = N_HEADS * HEAD_DIM
KV_DIM = N_KV_HEADS * HEAD_DIM
NEG_INF = float("-inf")
CHAIN_BATCH = 2


def _t5_bucket_table(dist):
    n = np.maximum(dist, 0)
    max_exact = REL_BUCKETS // 2
    nf = np.maximum(n, 1).astype(np.float32)
    large = max_exact + (np.log(nf / np.float32(max_exact)) / np.float32(math.log(REL_MAX_DIST / max_exact))
                         * np.float32(REL_BUCKETS - max_exact)).astype(np.int32)
    large = np.minimum(large, REL_BUCKETS - 1)
    bucket = np.where(n < max_exact, n, large)
    valid = (dist >= 0) & (dist < WINDOW)
    return np.where(valid, bucket, -1).astype(np.int32)


def _bias_from_buckets(bucket, relb_ref, head):
    acc = jnp.full(bucket.shape, NEG_INF, F32)
    for bkt in range(REL_BUCKETS):
        acc = jnp.where(bucket == bkt, relb_ref[bkt, head], acc)
    return acc


def _attn_prompt_kernel(x_ref, wqkv_ref, bqkv_ref, wo_ref, bo_ref, g_ref, b_ref, bucket_ref,
                        relb_ref, sink_ref, o_ref, kout_ref, vout_ref,
                        ka_lo, ka_hi, kb_lo, kb_hi, vt, ot, bias_scr, sink_scr, *, tq):
    bi = pl.program_id(0)
    j = pl.program_id(1)
    nj = pl.num_programs(1)
    blk = WINDOW
    half_heads = GQA_GROUP // 2
    kbufs = (ka_lo, ka_hi, kb_lo, kb_hi)

    @pl.when((bi == 0) & (j == 0))
    def _build_tables():
        bucket = bucket_ref[...]
        lane = lax.broadcasted_iota(jnp.int32, (1, 2 * blk), 1)
        for kv in range(N_KV_HEADS):
            for half in range(2):
                ha = kv * GQA_GROUP + half
                hb = ha + half_heads
                pair = kv * 2 + half
                bias_scr[pair, :, 0:blk] = _bias_from_buckets(bucket, relb_ref, ha)
                bias_scr[pair, :, blk:2 * blk] = _bias_from_buckets(bucket, relb_ref, hb)
                sink_scr[pair] = jnp.where(lane < blk, sink_ref[ha], sink_ref[hb])

    @pl.when(j == 0)
    def _no_past():
        for buf in kbufs:
            buf[0:blk, :] = jnp.zeros((blk, KV_DIM), BF16)
        vt[:, 0:blk] = jnp.zeros((KV_DIM, blk), BF16)

    @pl.when(j > 0)
    def _carry():
        for buf in kbufs:
            buf[0:blk, :] = buf[tq:tq + blk, :]
        vt[:, 0:blk] = vt[:, tq:tq + blk]

    x = x_ref[0]
    qkv = jnp.dot(x.astype(BF16), wqkv_ref[0], preferred_element_type=F32) + bqkv_ref[...]
    q = (qkv[:, :Q_DIM] * (HEAD_DIM ** -0.5)).astype(BF16)
    k = qkv[:, Q_DIM:Q_DIM + KV_DIM]
    v = qkv[:, Q_DIM + KV_DIM:]
    k_sw = jnp.concatenate(
        [pltpu.roll(k[:, c * LANES:(c + 1) * LANES], HEAD_DIM, axis=1) for c in range(KV_DIM // LANES)], axis=1)
    lo = (lax.broadcasted_iota(jnp.int32, (1, KV_DIM), 1) % LANES) < HEAD_DIM
    ka_lo[blk:blk + tq, :] = jnp.where(lo, k, 0.0).astype(BF16)
    ka_hi[blk:blk + tq, :] = jnp.where(lo, 0.0, k).astype(BF16)
    kb_lo[blk:blk + tq, :] = jnp.where(lo, k_sw, 0.0).astype(BF16)
    kb_hi[blk:blk + tq, :] = jnp.where(lo, 0.0, k_sw).astype(BF16)
    vt[:, blk:blk + tq] = v.T.astype(BF16)

    @pl.when(j == nj - 1)
    def _emit_cache():
        kout_ref[0] = k[tq - WINDOW:, :]
        vout_ref[0] = v[tq - WINDOW:, :]

    ks = lax.broadcasted_iota(jnp.int32, (blk, 2 * blk), 0)
    qt = lax.broadcasted_iota(jnp.int32, (blk, 2 * blk), 1) % blk
    own = ks <= qt
    chains = [(i, kv, half) for i in range(tq // blk) for kv in range(N_KV_HEADS) for half in range(2)]

    def scores(i, kv, half):
        r0 = i * blk
        c0 = (kv // 2) * LANES
        in_lo = kv % 2 == 0
        if half == 0:
            kk = (ka_lo if in_lo else kb_lo)[r0:r0 + 2 * blk, c0:c0 + LANES]
        else:
            kk = (kb_hi if in_lo else ka_hi)[r0:r0 + 2 * blk, c0:c0 + LANES]
        qa = q[r0:r0 + blk, (2 * kv) * LANES:(2 * kv + 1) * LANES]
        qb = q[r0:r0 + blk, (2 * kv + 1) * LANES:(2 * kv + 2) * LANES]
        q_pair = jnp.concatenate([qa, qb], axis=0)
        return lax.dot_general(kk, q_pair, (((1,), (1,)), ((), ())), preferred_element_type=F32)

    def fold(i, kv, half, s2):
        s_prev = s2[0:blk, :]
        if i == 0:
            s_prev = jnp.where(j == 0, NEG_INF, s_prev)
        return jnp.where(own, s2[blk:2 * blk, :], s_prev) + bias_scr[kv * 2 + half]

    def softmax_batch(batch, s2s):
        sinks = [sink_scr[kv * 2 + half] for _, kv, half in batch]
        ss = [fold(*ch, s2) for ch, s2 in zip(batch, s2s)]
        ms = [jnp.maximum(jnp.max(s, axis=0, keepdims=True), sink) for s, sink in zip(ss, sinks)]
        ps = [jnp.exp(s - m) for s, m in zip(ss, ms)]
        invs = [1.0 / (jnp.sum(p, axis=0, keepdims=True) + jnp.exp(sink - m)) for p, m, sink in zip(ps, ms, sinks)]
        p2s = [jnp.concatenate([jnp.where(own, 0.0, p), jnp.where(own, p, 0.0)], axis=0).astype(BF16) for p in ps]
        return list(zip(p2s, invs))

    def weighted_values(i, kv, half, p2, inv):
        r0 = i * blk
        ha = kv * GQA_GROUP + half
        hb = ha + half_heads
        v_t = vt[kv * HEAD_DIM:(kv + 1) * HEAD_DIM, r0:r0 + 2 * blk]
        o_t = jnp.dot(v_t, p2, preferred_element_type=F32) * inv
        ot[ha * HEAD_DIM:(ha + 1) * HEAD_DIM, r0:r0 + blk] = o_t[:, 0:blk]
        ot[hb * HEAD_DIM:(hb + 1) * HEAD_DIM, r0:r0 + blk] = o_t[:, blk:2 * blk]

    batches = [chains[c:c + CHAIN_BATCH] for c in range(0, len(chains), CHAIN_BATCH)]
    s_next = [scores(*ch) for ch in batches[0]]
    prev = []
    for bi, batch in enumerate(batches):
        s_cur = s_next
        if bi + 1 < len(batches):
            s_next = [scores(*ch) for ch in batches[bi + 1]]
        probs = softmax_batch(batch, s_cur)
        for ch, pr in prev:
            weighted_values(*ch, *pr)
        prev = list(zip(batch, probs))
    for ch, pr in prev:
        weighted_values(*ch, *pr)

    o = ot[...].T.astype(BF16)
    y = jnp.dot(o, wo_ref[0], preferred_element_type=F32) + bo_ref[...]
    o_ref[0] = _layer_norm_rows(DEEPNORM_ALPHA * x + y, g_ref[...], b_ref[...])


def _attn_prompt(x, wqkv, bqkv, wo, bo, g, b, rel_bias, sinks, *, layer, tq=1024):
    bsz, l, d = x.shape
    tq = min(tq, l)
    assert l % tq == 0 and tq % WINDOW == 0
    qi = np.arange(WINDOW)[None, :]
    si = np.arange(WINDOW)[:, None]
    bucket = jnp.asarray(_t5_bucket_table(np.where(si <= qi, qi - si, qi + WINDOW - si)))
    qkv_dim = wqkv.shape[-1]
    smem = pl.BlockSpec(memory_space=pltpu.SMEM)
    est = (d * qkv_dim + Q_DIM * d) * 2 + 5 * tq * d * 4 + N_HEADS * WINDOW * 2 * WINDOW * 4 + 3 * tq * qkv_dim * 4
    return pl.pallas_call(
        functools.partial(_attn_prompt_kernel, tq=tq),
        grid=(bsz, l // tq),
        in_specs=[
            pl.BlockSpec((1, tq, d), lambda bi, j: (bi, j, 0)),
            _layer_resident((d, qkv_dim), layer), _resident((1, qkv_dim)),
            _layer_resident((Q_DIM, d), layer), _resident((1, d)),
            _resident((1, d)), _resident((1, d)),
            _resident((WINDOW, WINDOW)),
            smem, smem,
        ],
        out_specs=[
            pl.BlockSpec((1, tq, d), lambda bi, j: (bi, j, 0)),
            pl.BlockSpec((1, WINDOW, KV_DIM), lambda bi, j: (bi, 0, 0)),
            pl.BlockSpec((1, WINDOW, KV_DIM), lambda bi, j: (bi, 0, 0)),
        ],
        out_shape=[
            jax.ShapeDtypeStruct((bsz, l, d), F32),
            jax.ShapeDtypeStruct((bsz, WINDOW, KV_DIM), F32),
            jax.ShapeDtypeStruct((bsz, WINDOW, KV_DIM), F32),
        ],
        scratch_shapes=[pltpu.VMEM((WINDOW + tq, KV_DIM), BF16)] * 4 + [
            pltpu.VMEM((KV_DIM, WINDOW + tq), BF16),
            pltpu.VMEM((Q_DIM, tq), F32),
            pltpu.VMEM((N_HEADS // 2, WINDOW, 2 * WINDOW), F32),
            pltpu.VMEM((N_HEADS // 2, 1, 2 * WINDOW), F32),
        ],
        compiler_params=pltpu.CompilerParams(
            dimension_semantics=("arbitrary", "arbitrary"), vmem_limit_bytes=_vmem_limit(est)),
        name="attn_prompt",
    )(x, wqkv, bqkv, wo, bo, g, b, bucket, rel_bias, sinks)


SAMPLE_KEYS = 2 * WINDOW
SAMPLE_CHAIN_BATCH = 8


def _attn_sample_kernel(x_ref, ck_ref, cv_ref, wqkv_ref, bqkv_ref, wo_ref, bo_ref, g_ref, b_ref, bucket_ref,
                        relb_ref, sink_ref, *rest, bb, l, n_prev, out_layer):
    o_ref, nk_ref, nv_ref, qbuf, obuf, bias_scr = rest[n_prev:]
    for other in range(nk_ref.shape[0]):
        if other != out_layer:
            nk_ref[other] = jnp.zeros(nk_ref.shape[1:], F32)
            nv_ref[other] = jnp.zeros(nv_ref.shape[1:], F32)

    @pl.when(pl.program_id(0) == 0)
    def _init():
        bucket = bucket_ref[...]
        for h in range(N_HEADS):
            kv, g_ = divmod(h, GQA_GROUP)
            bias_scr[kv, g_ * l:(g_ + 1) * l, :] = _bias_from_buckets(bucket, relb_ref, h)

    x = x_ref[...]
    qkv = jnp.dot(x.astype(BF16), wqkv_ref[0], preferred_element_type=F32) + bqkv_ref[...]
    qbuf[...] = qkv[:, :Q_DIM] * (HEAD_DIM ** -0.5)
    k_new = qkv[:, Q_DIM:Q_DIM + KV_DIM]
    v_new = qkv[:, Q_DIM + KV_DIM:]

    sink_cols = []
    for kv in range(N_KV_HEADS):
        sink_cols.append(jnp.concatenate(
            [jnp.full((l, 1), sink_ref[kv * GQA_GROUP + g_], F32) for g_ in range(GQA_GROUP)], axis=0))

    kn_t = k_new.T
    vn_t = v_new.T
    lane = lax.broadcasted_iota(jnp.int32, (HEAD_DIM, WINDOW), 1)
    pad = jnp.zeros((WINDOW - l, KV_DIM), F32)

    def shifted(old_t, new_t, i):
        kept = pltpu.roll(old_t, WINDOW - l, axis=1)
        new = pltpu.roll(new_t, (WINDOW - l - i * l) % WINDOW, axis=1)
        return jnp.where(lane >= WINDOW - l, new, kept)

    def keys_values(i):
        r0 = i * l
        for kv in range(N_KV_HEADS):
            hs = slice(kv * HEAD_DIM, (kv + 1) * HEAD_DIM)
            nk_ref[out_layer, i, kv] = shifted(ck_ref[0, i, kv], kn_t[hs, :], i)
            nv_ref[out_layer, i, kv] = shifted(cv_ref[0, i, kv], vn_t[hs, :], i)
        kn = jnp.concatenate([k_new[r0:r0 + l, :], pad], axis=0).astype(BF16)
        vn = jnp.concatenate([v_new[r0:r0 + l, :], pad], axis=0).astype(BF16)
        return kn, vn

    def scores(i, kv, kn):
        r0 = i * l
        c0 = kv * HEAD_DIM
        qs = jnp.concatenate(
            [qbuf[r0:r0 + l, (kv * GQA_GROUP + g_) * HEAD_DIM:(kv * GQA_GROUP + g_ + 1) * HEAD_DIM]
             for g_ in range(GQA_GROUP)], axis=0).astype(BF16)
        s_cache = jnp.dot(qs, ck_ref[0, i, kv].astype(BF16), preferred_element_type=F32)
        s_new = lax.dot_general(qs, kn[:, c0:c0 + HEAD_DIM], (((1,), (1,)), ((), ())), preferred_element_type=F32)
        return jnp.concatenate([s_cache, s_new], axis=1)

    def weighted_values(i, kv, p, vn):
        c0 = kv * HEAD_DIM
        pb = p.astype(BF16)
        o_cache = lax.dot_general(pb[:, :WINDOW], cv_ref[0, i, kv].astype(BF16), (((1,), (1,)), ((), ())),
                                  preferred_element_type=F32)
        return o_cache + jnp.dot(pb[:, WINDOW:], vn[:, c0:c0 + HEAD_DIM], preferred_element_type=F32)

    for i0 in range(0, bb, SAMPLE_CHAIN_BATCH):
        items = range(i0, min(bb, i0 + SAMPLE_CHAIN_BATCH))
        kvs = {i: keys_values(i) for i in items}
        chains = [(i, kv) for i in items for kv in range(N_KV_HEADS)]
        s_all = [scores(i, kv, kvs[i][0]) + bias_scr[kv] for i, kv in chains]
        m_all = [jnp.maximum(jnp.max(s, axis=-1, keepdims=True), sink_cols[kv]) for s, (i, kv) in zip(s_all, chains)]
        p_all = [jnp.exp(s - m) for s, m in zip(s_all, m_all)]
        den_all = [jnp.sum(p, axis=-1, keepdims=True) + jnp.exp(sink_cols[kv] - m)
                   for p, m, (i, kv) in zip(p_all, m_all, chains)]
        o_all = [weighted_values(i, kv, p, kvs[i][1]) for p, (i, kv) in zip(p_all, chains)]
        for o, den, (i, kv) in zip(o_all, den_all, chains):
            o = o / den
            for g_ in range(GQA_GROUP):
                h = kv * GQA_GROUP + g_
                obuf[i * l:(i + 1) * l, h * HEAD_DIM:(h + 1) * HEAD_DIM] = o[g_ * l:(g_ + 1) * l, :]

    y = jnp.dot(obuf[...].astype(BF16), wo_ref[0], preferred_element_type=F32) + bo_ref[...]
    o_ref[...] = _layer_norm_rows(DEEPNORM_ALPHA * x + y, g_ref[...], b_ref[...])


def _attn_sample(x, cache_k_t, cache_v_t, wqkv, bqkv, wo, bo, g, b, rel_bias, sinks, *, layer, prev=None, bb=16):
    bsz, l, d = x.shape
    assert bsz % bb == 0 and l == SUBLANES and bb * l == WINDOW
    ti = np.arange(l)[:, None]
    ci = np.arange(SAMPLE_KEYS)[None, :]
    table = _t5_bucket_table(ti + WINDOW - ci)
    table = np.where(ci < WINDOW + l, table, -1).astype(np.int32)
    bucket = jnp.asarray(table)
    qkv_dim = wqkv.shape[-1]
    rows = bb * l
    smem = pl.BlockSpec(memory_space=pltpu.SMEM)
    cache_spec = pl.BlockSpec((1, bb, N_KV_HEADS, HEAD_DIM, WINDOW), lambda i: (layer, i, 0, 0, 0))
    est = (d * qkv_dim + Q_DIM * d) * 2 + 8 * bb * WINDOW * KV_DIM * 4 + 8 * rows * d * 4
    in_specs = [
        pl.BlockSpec((rows, d), lambda i: (i, 0)),
        cache_spec, cache_spec,
        _layer_resident((d, qkv_dim), layer), _resident((1, qkv_dim)),
        _layer_resident((Q_DIM, d), layer), _resident((1, d)),
        _resident((1, d)), _resident((1, d)),
        _resident((l, SAMPLE_KEYS)),
        smem, smem,
    ]
    args = [x.reshape(bsz * l, d), cache_k_t, cache_v_t, wqkv, bqkv, wo, bo, g, b, bucket, rel_bias, sinks]
    n_layers = cache_k_t.shape[0]
    if prev is None:
        aliases, out_layer = {}, layer
        out_cache_spec = pl.BlockSpec((n_layers, bb, N_KV_HEADS, HEAD_DIM, WINDOW), lambda i: (0, i, 0, 0, 0))
    else:
        aliases, out_layer = {len(args): 1, len(args) + 1: 2}, 0
        out_cache_spec = cache_spec
        in_specs += [pl.BlockSpec(memory_space=pl.ANY)] * 2
        args += list(prev)
    return pl.pallas_call(
        functools.partial(_attn_sample_kernel, bb=bb, l=l, n_prev=len(aliases), out_layer=out_layer),
        grid=(bsz // bb,),
        in_specs=in_specs,
        out_specs=[pl.BlockSpec((rows, d), lambda i: (i, 0)), out_cache_spec, out_cache_spec],
        out_shape=[
            jax.ShapeDtypeStruct((bsz * l, d), F32),
            jax.ShapeDtypeStruct(cache_k_t.shape, F32),
            jax.ShapeDtypeStruct(cache_v_t.shape, F32),
        ],
        scratch_shapes=[
            pltpu.VMEM((rows, Q_DIM), F32),
            pltpu.VMEM((rows, Q_DIM), F32),
            pltpu.VMEM((N_KV_HEADS, GQA_GROUP * l, SAMPLE_KEYS), F32),
        ],
        input_output_aliases=aliases,
        compiler_params=pltpu.CompilerParams(
            dimension_semantics=("arbitrary",), vmem_limit_bytes=_vmem_limit(est)),
        name="attn_sample",
    )(*args)


POOL_WINDOWS = (2, 4, 8, 16)
POOL_PAD = max(POOL_WINDOWS)
POOL_STATE_LEN = POOL_PAD - 1
POOL_HIST = len(POOL_WINDOWS) * SUBLANES


def _pool_mix(window_sum, x, cnt_of, pw_ref, scale):
    gd = x.shape[-1] // len(POOL_WINDOWS)
    outs = []
    for g, w in enumerate(POOL_WINDOWS):
        diff = window_sum(g, w) / cnt_of(w) - x[:, g * gd:(g + 1) * gd]
        outs.append(jnp.dot(diff.astype(BF16), pw_ref[g], preferred_element_type=F32))
    return jnp.concatenate(outs, axis=-1) * scale


def _pool_prompt_kernel(x_ref, pw_ref, sc_ref, g_ref, b_ref, o_ref, st_ref, buf, lvl, *, tm):
    j = pl.program_id(1)
    nj = pl.num_programs(1)
    d = x_ref.shape[-1]
    gd = d // len(POOL_WINDOWS)
    hist, rows = POOL_HIST, POOL_HIST + tm

    @pl.when(j == 0)
    def _no_past():
        buf[0:hist, :] = jnp.zeros((hist, d), F32)

    @pl.when(j > 0)
    def _carry():
        buf[0:hist, :] = buf[tm:tm + hist, :]

    x = x_ref[0]
    buf[hist:rows, :] = x

    sums = {}
    src = buf
    for k, w in enumerate(POOL_WINDOWS):
        step, c0, r0 = w // 2, k * gd, (k + 1) * SUBLANES
        cur = src[r0:rows, c0:d] + src[r0 - step:rows - step, c0:d]
        sums[k] = cur[hist - r0:, 0:gd]
        if k + 1 < len(POOL_WINDOWS):
            lvl[k, r0:rows, c0:d] = cur
            src = lvl.at[k]

    def window_sum(g, w):
        return sums[g]

    pos1 = j * tm + lax.broadcasted_iota(jnp.int32, (tm, 1), 0) + 1

    def cnt_of(w):
        return jnp.minimum(pos1, w).astype(F32)

    y = _pool_mix(window_sum, x, cnt_of, pw_ref, sc_ref[...])
    o_ref[0] = _layer_norm_rows(DEEPNORM_ALPHA * x + y, g_ref[...], b_ref[...])

    @pl.when(j == nj - 1)
    def _emit_state():
        st_ref[0] = buf[rows - POOL_STATE_LEN:rows, :]


def _pool_prompt(x, pw, scale, g, b, *, tm=512):
    bsz, l, d = x.shape
    tm = min(tm, l)
    assert l % tm == 0 and tm >= POOL_HIST >= POOL_STATE_LEN and POOL_WINDOWS == (2, 4, 8, 16)
    ng, gd = pw.shape[0], pw.shape[1]
    est = 10 * tm * d * 4 + ng * gd * gd * 2
    return pl.pallas_call(
        functools.partial(_pool_prompt_kernel, tm=tm),
        grid=(bsz, l // tm),
        in_specs=[
            pl.BlockSpec((1, tm, d), lambda bi, j: (bi, j, 0)),
            _resident((ng, gd, gd)), _resident((1, d)), _resident((1, d)), _resident((1, d)),
        ],
        out_specs=[
            pl.BlockSpec((1, tm, d), lambda bi, j: (bi, j, 0)),
            pl.BlockSpec((1, POOL_STATE_LEN, d), lambda bi, j: (bi, 0, 0)),
        ],
        out_shape=[
            jax.ShapeDtypeStruct((bsz, l, d), F32),
            jax.ShapeDtypeStruct((bsz, POOL_STATE_LEN, d), F32),
        ],
        scratch_shapes=[pltpu.VMEM((POOL_HIST + tm, d), F32),
                        pltpu.VMEM((len(POOL_WINDOWS) - 1, POOL_HIST + tm, d), F32)],
        compiler_params=pltpu.CompilerParams(
            dimension_semantics=("arbitrary", "arbitrary"), vmem_limit_bytes=_vmem_limit(est)),
        name="pool_prompt",
    )(x, pw, scale, g, b)


def _pool_sample_kernel(x_ref, st_ref, pw_ref, sc_ref, g_ref, b_ref, o_ref, nst_ref, buf, *, bb, l):
    d = x_ref.shape[-1]
    gd = d // len(POOL_WINDOWS)
    x3 = x_ref[...]
    buf[:, 1:POOL_PAD, :] = st_ref[...]
    buf[:, POOL_PAD:POOL_PAD + l, :] = x3
    x = x3.reshape(bb * l, d)

    def window_sum(g, w):
        acc = x3[:, :, g * gd:(g + 1) * gd]
        for s in range(1, w):
            acc = acc + buf[:, POOL_PAD - s:POOL_PAD - s + l, g * gd:(g + 1) * gd]
        return acc.reshape(bb * l, gd)

    y = _pool_mix(window_sum, x, lambda w: float(w), pw_ref, sc_ref[...])
    o_ref[...] = _layer_norm_rows(DEEPNORM_ALPHA * x + y, g_ref[...], b_ref[...]).reshape(bb, l, d)
    nst_ref[...] = buf[:, l + 1:l + POOL_PAD, :]


def _pool_sample(x, state, pw, scale, g, b, *, start, bb=16):
    bsz, l, d = x.shape
    assert bsz % bb == 0 and l == SUBLANES and start + 1 >= POOL_PAD
    ng, gd = pw.shape[0], pw.shape[1]
    est = 8 * bb * (POOL_PAD + l) * d * 4 + ng * gd * gd * 2
    return pl.pallas_call(
        functools.partial(_pool_sample_kernel, bb=bb, l=l),
        grid=(bsz // bb,),
        in_specs=[
            pl.BlockSpec((bb, l, d), lambda i: (i, 0, 0)),
            pl.BlockSpec((bb, POOL_STATE_LEN, d), lambda i: (i, 0, 0)),
            _resident((ng, gd, gd)), _resident((1, d)), _resident((1, d)), _resident((1, d)),
        ],
        out_specs=[
            pl.BlockSpec((bb, l, d), lambda i: (i, 0, 0)),
            pl.BlockSpec((bb, POOL_STATE_LEN, d), lambda i: (i, 0, 0)),
        ],
        out_shape=[
            jax.ShapeDtypeStruct((bsz, l, d), F32),
            jax.ShapeDtypeStruct((bsz, POOL_STATE_LEN, d), F32),
        ],
        scratch_shapes=[pltpu.VMEM((bb, POOL_PAD + l, d), F32)],
        compiler_params=pltpu.CompilerParams(
            dimension_semantics=("arbitrary",), vmem_limit_bytes=_vmem_limit(est)),
        name="pool_sample",
    )(x, state, pw, scale, g, b)


D_INNER = 2048
SSM_HEAD_DIM = 64
SSM_HEADS = D_INNER // SSM_HEAD_DIM
SSM_GROUPS = 4
SSM_HPG = SSM_HEADS // SSM_GROUPS
D_STATE = 128
CONV_WIDTH = 4
GBN = SSM_GROUPS * D_STATE
CONV_DIM = D_INNER + 2 * GBN
GROUP_INNER = D_INNER // SSM_GROUPS
RMS_EPS = 1e-5
CONV_PAD = SUBLANES
SSM_ROWS = 128


def _split_bf16(v, parts=3):
    out = []
    r = v
    for _ in range(parts):
        p = r.astype(BF16)
        out.append(p)
        r = r - p.astype(F32)
    return out


def _expand_heads(v, e):
    return sum(jnp.dot(p, e, preferred_element_type=F32) for p in _split_bf16(v, parts=2))


def _ssm_in_proj(xb, win_ref, wdt_ref, c0, c1):
    main = win_ref.shape[1]
    if c0 >= main:
        return jnp.dot(xb, wdt_ref[:, c0 - main:c1 - main], preferred_element_type=F32)
    assert c1 <= main
    return jnp.dot(xb, win_ref[:, c0:c1], preferred_element_type=F32)


def _ssm_prep(proj, xp_buf, conv_state, dtb_ref, alog_ref, tril_ref, *, nseq, q):
    rows = nseq * q
    xbc_pre = proj(D_INNER, D_INNER + CONV_DIM)
    dt_pre = proj(D_INNER + CONV_DIM, D_INNER + CONV_DIM + LANES)
    if conv_state is not None:
        xp_buf[:, CONV_PAD - (CONV_WIDTH - 1):CONV_PAD, :] = conv_state
    xp_buf[:, CONV_PAD:CONV_PAD + q, :] = xbc_pre.reshape(nseq, q, CONV_DIM)
    new_conv = xp_buf[:, q + CONV_PAD - (CONV_WIDTH - 1):q + CONV_PAD, :]

    dtv = dt_pre + dtb_ref[...]
    dt = jnp.maximum(dtv, 0.0) + jnp.log1p(jnp.exp(-jnp.abs(dtv)))
    a = -jnp.exp(alog_ref[...])
    tril = tril_ref[...]
    acum = sum(jnp.dot(tril, p, preferred_element_type=F32) for p in _split_bf16(dt * a))
    a3 = acum.reshape(nseq, q, LANES)
    alast = jnp.broadcast_to(a3[:, q - 1:q, :], (nseq, q, LANES)).reshape(rows, LANES)
    return dict(acum=acum, acum_t=acum.T, dt_t=dt.T, causal=tril > 0.5,
                exp_a=jnp.exp(acum), dd=jnp.exp(alast - acum) * dt, new_conv=new_conv)


def _conv_silu(xp_buf, cw_ref, cb_ref, c0, c1, *, nseq, q):
    acc = cb_ref[:, c0:c1].reshape(1, 1, c1 - c0)
    for jj in range(CONV_WIDTH):
        off = CONV_PAD - (CONV_WIDTH - 1) + jj
        acc = acc + xp_buf[:, off:off + q, c0:c1] * cw_ref[jj:jj + 1, c0:c1].reshape(1, 1, c1 - c0)
    acc = acc.reshape(nseq * q, c1 - c0)
    return acc * jax.nn.sigmoid(acc)


def _ssm_group_diag(g, f, proj, prep, xp_buf, cw_ref, cb_ref, ybuf, *, nseq, q):
    f["z"] = proj(g * GROUP_INNER, (g + 1) * GROUP_INNER)
    conv = functools.partial(_conv_silu, xp_buf, cw_ref, cb_ref, nseq=nseq, q=q)
    half = GROUP_INNER // 2
    xs_lo = conv(g * GROUP_INNER, g * GROUP_INNER + half)
    yield
    xs_hi = conv(g * GROUP_INNER + half, (g + 1) * GROUP_INNER)
    yield
    xs = f["xs"] = jnp.concatenate([xs_lo, xs_hi], axis=-1)
    bm = f["bm"] = conv(D_INNER + g * D_STATE, D_INNER + (g + 1) * D_STATE)
    cm = f["cm"] = conv(D_INNER + GBN + g * D_STATE, D_INNER + GBN + (g + 1) * D_STATE)
    yield
    cb = lax.dot_general(cm.astype(BF16), bm.astype(BF16), (((1,), (1,)), ((), ())), preferred_element_type=F32)
    acum, acum_t, dt_t, causal = prep["acum"], prep["acum_t"], prep["dt_t"], prep["causal"]
    for r in range(SSM_HPG):
        h = g * SSM_HPG + r
        seg = acum[:, h:h + 1] - acum_t[h:h + 1, :]
        lm = jnp.exp(jnp.where(causal, seg, NEG_INF))
        w = (cb * lm * dt_t[h:h + 1, :]).astype(BF16)
        xh = xs[:, r * SSM_HEAD_DIM:(r + 1) * SSM_HEAD_DIM].astype(BF16)
        ybuf[:, h * SSM_HEAD_DIM:(h + 1) * SSM_HEAD_DIM] = jnp.dot(w, xh, preferred_element_type=F32)
        yield


def _ssm_group_expand(g, f, prep, e_ref):
    e = e_ref[:, g * GROUP_INNER:(g + 1) * GROUP_INNER]
    f["exp_a"] = _expand_heads(prep["exp_a"], e)
    yield
    f["xdd"] = (f["xs"] * _expand_heads(prep["dd"], e)).astype(BF16)
    yield


def _interleave(*gens):
    live = list(gens)
    while live:
        for gen in list(live):
            try:
                next(gen)
            except StopIteration:
                live.remove(gen)


def _gate_norm(y, z, nw):
    y = y * (z * jax.nn.sigmoid(z))
    return y * lax.rsqrt(jnp.mean(y * y, axis=-1, keepdims=True) + RMS_EPS) * nw


def _ssm_prompt_kernel(x_ref, win_ref, wdt_ref, cw_ref, cb_ref, dtb_ref, alog_ref, dsk_ref, nw_ref, wout_ref, e_ref,
                       tril_ref, g_ref, b_ref, o_ref, cout_ref, sout_ref, xp_buf, ht, ybuf):
    j = pl.program_id(1)
    nj = pl.num_programs(1)
    q = SSM_ROWS

    @pl.when(j == 0)
    def _no_past():
        xp_buf[:, 0:CONV_PAD, :] = jnp.zeros((1, CONV_PAD, CONV_DIM), F32)
        ht[...] = jnp.zeros(ht.shape, F32)

    @pl.when(j > 0)
    def _carry():
        xp_buf[:, 0:CONV_PAD, :] = xp_buf[:, q:q + CONV_PAD, :]

    x = x_ref[0]
    xb = x.astype(BF16)
    zx = jnp.dot(xb, win_ref[...], preferred_element_type=F32)
    dt_pre = jnp.dot(xb, wdt_ref[...], preferred_element_type=F32)
    z = zx[:, :D_INNER]

    def proj(c0, c1):
        return dt_pre if c0 >= D_INNER + CONV_DIM else zx[:, c0:c1]

    prep = _ssm_prep(proj, xp_buf, None, dtb_ref, alog_ref, tril_ref, nseq=1, q=q)
    xbc = _conv_silu(xp_buf, cw_ref, cb_ref, 0, CONV_DIM, nseq=1, q=q)
    xs = xbc[:, :D_INNER]
    acum, acum_t, dt_t, causal = prep["acum"], prep["acum_t"], prep["dt_t"], prep["causal"]
    for g in range(SSM_GROUPS):
        bg = xbc[:, D_INNER + g * D_STATE:D_INNER + (g + 1) * D_STATE].astype(BF16)
        cg = xbc[:, D_INNER + GBN + g * D_STATE:D_INNER + GBN + (g + 1) * D_STATE].astype(BF16)
        cb = lax.dot_general(cg, bg, (((1,), (1,)), ((), ())), preferred_element_type=F32)
        for r in range(SSM_HPG):
            h = g * SSM_HPG + r
            seg = acum[:, h:h + 1] - acum_t[h:h + 1, :]
            lm = jnp.exp(jnp.where(causal, seg, NEG_INF))
            w = (cb * lm * dt_t[h:h + 1, :]).astype(BF16)
            xh = xs[:, h * SSM_HEAD_DIM:(h + 1) * SSM_HEAD_DIM].astype(BF16)
            ybuf[:, h * SSM_HEAD_DIM:(h + 1) * SSM_HEAD_DIM] = jnp.dot(w, xh, preferred_element_type=F32)

    e = e_ref[...]
    exp_a = _expand_heads(prep["exp_a"], e)
    xdd = (xs * _expand_heads(prep["dd"], e)).astype(BF16)
    decay = exp_a[q - 1:q, :]
    y_off = []
    for g in range(SSM_GROUPS):
        sl = slice(g * GROUP_INNER, (g + 1) * GROUP_INNER)
        bg = xbc[:, D_INNER + g * D_STATE:D_INNER + (g + 1) * D_STATE]
        cg = xbc[:, D_INNER + GBN + g * D_STATE:D_INNER + GBN + (g + 1) * D_STATE].astype(BF16)
        hg = ht[:, sl]
        y_off.append(jnp.dot(cg, hg.astype(BF16), preferred_element_type=F32))
        ht[:, sl] = hg * decay[:, sl] + jnp.dot(bg.T.astype(BF16), xdd[:, sl], preferred_element_type=F32)
    y = ybuf[...] + exp_a * jnp.concatenate(y_off, axis=-1) + dsk_ref[...] * xs
    y = jnp.concatenate(
        [_gate_norm(y[:, g * GROUP_INNER:(g + 1) * GROUP_INNER], z[:, g * GROUP_INNER:(g + 1) * GROUP_INNER],
                    nw_ref[:, g * GROUP_INNER:(g + 1) * GROUP_INNER]) for g in range(SSM_GROUPS)], axis=-1)
    out = jnp.dot(y.astype(BF16), wout_ref[...], preferred_element_type=F32)
    o_ref[0] = _layer_norm_rows(DEEPNORM_ALPHA * x + out, g_ref[...], b_ref[...])

    @pl.when(j == nj - 1)
    def _emit_state():
        cout_ref[...] = prep["new_conv"]
        sout_ref[0] = ht[...].T


def _ssm_consts(nseq, q):
    rows = nseq * q
    r = np.arange(rows)
    tril = ((r[:, None] >= r[None, :]) & (r[:, None] // q == r[None, :] // q)).astype(np.float32)
    e = np.zeros((LANES, D_INNER), np.float32)
    e[np.arange(D_INNER) // SSM_HEAD_DIM, np.arange(D_INNER)] = 1.0
    return jnp.asarray(tril, BF16), jnp.asarray(e, BF16)


def _ssm_weight_specs(d):
    return [
        pl.BlockSpec((d, D_INNER + CONV_DIM), lambda *_: (0, 0), pipeline_mode=pl.Buffered(1)),
        _resident((d, LANES)), _resident((CONV_WIDTH, CONV_DIM)), _resident((1, CONV_DIM)),
        _resident((1, LANES)), _resident((1, LANES)),
    ]


def _ssm_prompt(x, w, g, b):
    bsz, l, d = x.shape
    q = SSM_ROWS
    assert l % q == 0
    tril, e = _ssm_consts(1, q)
    win_cols = D_INNER + CONV_DIM + LANES
    est = (d * win_cols + D_INNER * d) * 2 + LANES * D_INNER * 4 + 12 * q * D_INNER * 4 + 3 * q * win_cols * 4
    return pl.pallas_call(
        _ssm_prompt_kernel,
        grid=(bsz, l // q),
        in_specs=[pl.BlockSpec((1, q, d), lambda bi, j: (bi, j, 0))] + _ssm_weight_specs(d) + [
            _resident((1, D_INNER)), _resident((1, D_INNER)), _resident((D_INNER, d)),
            _resident((LANES, D_INNER)), _resident((q, q)), _resident((1, d)), _resident((1, d)),
        ],
        out_specs=[
            pl.BlockSpec((1, q, d), lambda bi, j: (bi, j, 0)),
            pl.BlockSpec((1, CONV_WIDTH - 1, CONV_DIM), lambda bi, j: (bi, 0, 0)),
            pl.BlockSpec((1, D_INNER, D_STATE), lambda bi, j: (bi, 0, 0)),
        ],
        out_shape=[
            jax.ShapeDtypeStruct((bsz, l, d), F32),
            jax.ShapeDtypeStruct((bsz, CONV_WIDTH - 1, CONV_DIM), F32),
            jax.ShapeDtypeStruct((bsz, D_INNER, D_STATE), F32),
        ],
        scratch_shapes=[
            pltpu.VMEM((1, CONV_PAD + q, CONV_DIM), F32),
            pltpu.VMEM((D_STATE, D_INNER), F32),
            pltpu.VMEM((q, D_INNER), F32),
        ],
        compiler_params=pltpu.CompilerParams(
            dimension_semantics=("arbitrary", "arbitrary"), vmem_limit_bytes=_vmem_limit(est)),
        name="ssm_prompt",
    )(x, w["win"], w["wdt"], w["cw"], w["cb"], w["dtb"], w["alog"], w["dsk"], w["nw"], w["wout"], e, tril, g, b)


def _ssm_sample_front_kernel(x_ref, cst_ref, win_ref, wdt_ref, cw_ref, cb_ref, dtb_ref, alog_ref, dsk_ref, e_ref, tril_ref,
                             y_ref, z_ref, ea_ref, xdd_ref, c_ref, bt_ref, cout_ref, xp_buf, ybuf, *, nseq, q):
    xb = x_ref[...].astype(BF16)

    def proj(c0, c1):
        return _ssm_in_proj(xb, win_ref, wdt_ref, c0, c1)

    prep = _ssm_prep(proj, xp_buf, cst_ref[...], dtb_ref, alog_ref, tril_ref, nseq=nseq, q=q)
    cout_ref[...] = prep["new_conv"]
    for g in range(SSM_GROUPS):
        sl = slice(g * GROUP_INNER, (g + 1) * GROUP_INNER)
        f = {}
        _interleave(_ssm_group_diag(g, f, proj, prep, xp_buf, cw_ref, cb_ref, ybuf, nseq=nseq, q=q))
        _interleave(_ssm_group_expand(g, f, prep, e_ref))
        y_ref[:, sl] = ybuf[:, sl] + dsk_ref[:, sl] * f["xs"]
        z_ref[:, sl] = f["z"]
        ea_ref[:, sl] = f["exp_a"]
        xdd_ref[:, sl] = f["xdd"]
        c_ref[:, g * D_STATE:(g + 1) * D_STATE] = f["cm"]
        bt_ref[g * D_STATE:(g + 1) * D_STATE, :] = f["bm"].T.astype(BF16)


def _ssm_sample_state_kernel(st_ref, y_ref, ea_ref, c_ref, xdd_ref, bt_ref, o_ref, nst_ref, *, bb, q, nseq):
    ea = ea_ref[...]
    xdd = xdd_ref[...]
    first_seq = (pl.program_id(0) % (nseq // bb)) * bb
    lane_seq = lax.broadcasted_iota(jnp.int32, (D_STATE, nseq * q), 1) // q
    for i in range(bb):
        r0 = i * q
        ht = st_ref[i].T
        decay = ea[r0 + q - 1:r0 + q, :]
        y_parts = []
        new_parts = []
        for g in range(SSM_GROUPS):
            sl = slice(g * GROUP_INNER, (g + 1) * GROUP_INNER)
            cg = c_ref[r0:r0 + q, g * D_STATE:(g + 1) * D_STATE].astype(BF16)
            hg = ht[:, sl]
            y_parts.append(jnp.dot(cg, hg.astype(BF16), preferred_element_type=F32))
            btg = bt_ref[g * D_STATE:(g + 1) * D_STATE, :]
            btg = jnp.where(lane_seq == first_seq + i, btg, jnp.zeros_like(btg))
            new_parts.append(hg * decay[:, sl] + jnp.dot(btg, xdd[:, sl], preferred_element_type=F32))
        o_ref[r0:r0 + q, :] = y_ref[r0:r0 + q, :] + ea[r0:r0 + q, :] * jnp.concatenate(y_parts, axis=-1)
        nst_ref[i] = jnp.concatenate(new_parts, axis=-1).T


def _ssm_sample_finish_kernel(x_ref, y_ref, z_ref, nw_ref, wout_ref, g_ref, b_ref, o_ref):
    x = x_ref[...]
    out = jnp.zeros(x.shape, F32)
    for g in range(SSM_GROUPS):
        sl = slice(g * GROUP_INNER, (g + 1) * GROUP_INNER)
        y = _gate_norm(y_ref[:, sl], z_ref[:, sl], nw_ref[:, sl])
        out = out + jnp.dot(y.astype(BF16), wout_ref[sl, :], preferred_element_type=F32)
    o_ref[...] = _layer_norm_rows(DEEPNORM_ALPHA * x + out, g_ref[...], b_ref[...])


def _ssm_sample(x, conv_state, ssm_state, w, g, b, *, bb_state=8):
    bsz, q, d = x.shape
    t = bsz * q
    nseq = SSM_ROWS // q
    rows = SSM_ROWS
    assert q == SUBLANES and bsz % nseq == 0 and nseq % bb_state == 0
    tril, e = _ssm_consts(nseq, q)
    win_cols = D_INNER + CONV_DIM + LANES
    x2 = x.reshape(t, d)
    row_spec = lambda c: pl.BlockSpec((rows, c), lambda i: (i, 0))
    est = d * win_cols * 2 + LANES * D_INNER * 4 + 16 * rows * D_INNER * 4 + 3 * rows * win_cols * 4
    ydiag, z, exp_a, xdd, cmat, bt, new_conv = pl.pallas_call(
        functools.partial(_ssm_sample_front_kernel, nseq=nseq, q=q),
        grid=(t // rows,),
        in_specs=[row_spec(d), pl.BlockSpec((nseq, CONV_WIDTH - 1, CONV_DIM), lambda i: (i, 0, 0))]
        + _ssm_weight_specs(d) + [_resident((1, D_INNER)), _resident((LANES, D_INNER)), _resident((rows, rows))],
        out_specs=[row_spec(D_INNER), row_spec(D_INNER), row_spec(D_INNER), row_spec(D_INNER), row_spec(GBN),
                   pl.BlockSpec((GBN, rows), lambda i: (i, 0)),
                   pl.BlockSpec((nseq, CONV_WIDTH - 1, CONV_DIM), lambda i: (i, 0, 0))],
        out_shape=[
            jax.ShapeDtypeStruct((t, D_INNER), F32), jax.ShapeDtypeStruct((t, D_INNER), F32),
            jax.ShapeDtypeStruct((t, D_INNER), F32), jax.ShapeDtypeStruct((t, D_INNER), BF16),
            jax.ShapeDtypeStruct((t, GBN), F32),
            jax.ShapeDtypeStruct((t // rows * GBN, rows), BF16),
            jax.ShapeDtypeStruct((bsz, CONV_WIDTH - 1, CONV_DIM), F32),
        ],
        scratch_shapes=[pltpu.VMEM((nseq, CONV_PAD + q, CONV_DIM), F32), pltpu.VMEM((rows, D_INNER), F32)],
        compiler_params=pltpu.CompilerParams(
            dimension_semantics=("arbitrary",), vmem_limit_bytes=_vmem_limit(est)),
        name="ssm_sample_front",
    )(x2, conv_state, w["win"], w["wdt"], w["cw"], w["cb"], w["dtb"], w["alog"], w["dsk"], e, tril)

    srows = bb_state * q
    srow_spec = lambda c: pl.BlockSpec((srows, c), lambda i: (i, 0))
    st_spec = pl.BlockSpec((bb_state, D_INNER, D_STATE), lambda i: (i, 0, 0))
    est = 4 * bb_state * D_INNER * D_STATE * 4 + 8 * D_INNER * D_STATE * 4
    per_block = nseq // bb_state
    y, new_state = pl.pallas_call(
        functools.partial(_ssm_sample_state_kernel, bb=bb_state, q=q, nseq=nseq),
        grid=(bsz // bb_state,),
        in_specs=[st_spec, srow_spec(D_INNER), srow_spec(D_INNER), srow_spec(GBN),
                  pl.BlockSpec((rows, D_INNER), lambda i: (i // per_block, 0)),
                  pl.BlockSpec((GBN, rows), lambda i: (i // per_block, 0))],
        out_specs=[srow_spec(D_INNER), st_spec],
        out_shape=[jax.ShapeDtypeStruct((t, D_INNER), F32),
                   jax.ShapeDtypeStruct((bsz, D_INNER, D_STATE), F32)],
        compiler_params=pltpu.CompilerParams(
            dimension_semantics=("arbitrary",), vmem_limit_bytes=_vmem_limit(est)),
        name="ssm_sample_state",
    )(ssm_state, ydiag, exp_a, cmat, xdd, bt)

    est = D_INNER * d * 2 + 8 * rows * D_INNER * 4
    out = pl.pallas_call(
        _ssm_sample_finish_kernel,
        grid=(t // rows,),
        in_specs=[row_spec(d), row_spec(D_INNER), row_spec(D_INNER),
                  _resident((1, D_INNER)), _resident((D_INNER, d)), _resident((1, d)), _resident((1, d))],
        out_specs=row_spec(d),
        out_shape=jax.ShapeDtypeStruct((t, d), F32),
        compiler_params=pltpu.CompilerParams(
            dimension_semantics=("arbitrary",), vmem_limit_bytes=_vmem_limit(est)),
        name="ssm_sample_finish",
    )(x2, y, z, w["nw"], w["wout"], g, b)
    return out.reshape(bsz, q, d), new_conv, new_state


def _ssm_weights(w_in, conv_w, conv_b, dt_bias, a_log, d_skip, norm_w, w_out):
    pad = LANES - SSM_HEADS
    wdt = jnp.pad(w_in[:, D_INNER + CONV_DIM:], ((0, 0), (0, pad))).astype(BF16)
    return dict(
        win=w_in.astype(BF16), wdt=wdt, cw=conv_w, cb=conv_b.reshape(1, CONV_DIM),
        dtb=jnp.pad(dt_bias, (0, pad)).reshape(1, LANES),
        alog=jnp.pad(a_log, (0, pad)).reshape(1, LANES),
        dsk=jnp.repeat(d_skip, SSM_HEAD_DIM).reshape(1, D_INNER),
        nw=norm_w.reshape(1, D_INNER), wout=w_out.astype(BF16))


def kernel(x_prompt, x_sample, cache_k, cache_v, state_conv, state_ssm, state_pool, rel_bias, attn_w_qkv, attn_b_qkv, attn_w_o, attn_b_o, attn_sinks, ssm_w_in, ssm_conv_w, ssm_conv_b, ssm_dt_bias, ssm_a_log, ssm_d, ssm_norm_w, ssm_w_out, pool_w, pool_scale, ffn_w_gate, ffn_w_up, ffn_w_down, ln_g, ln_b):
    xp, xs = x_prompt, x_sample
    d = xp.shape[-1]
    wqkv, wo = attn_w_qkv.astype(BF16), attn_w_o.astype(BF16)
    ffn_w32 = (ffn_w_gate, ffn_w_up, ffn_w_down)
    ffn_w = tuple(w[0:1].astype(BF16) for w in ffn_w32)
    cache_k_t = jnp.transpose(cache_k, (0, 1, 3, 4, 2))
    cache_v_t = jnp.transpose(cache_v, (0, 1, 3, 4, 2))
    new_cache_t = None
    nk_p, nv_p, nc_p, nh_p, npool_p = [], [], [], [], []
    nc_s, nh_s, npool_s = [], [], []
    for i in range(DEPTH):
        j = i // N_MIXERS
        kind = i % N_MIXERS
        g1 = ln_g[i, 0].reshape(1, d)
        b1 = ln_b[i, 0].reshape(1, d)
        if kind == 0:
            bqkv = attn_b_qkv[j].reshape(1, -1)
            bo = attn_b_o[j].reshape(1, d)
            xp, kp, vp = _attn_prompt(xp, wqkv, bqkv, wo, bo, g1, b1, rel_bias, attn_sinks[j], layer=j)
            xs, *new_cache_t = _attn_sample(xs, cache_k_t, cache_v_t, wqkv, bqkv, wo, bo, g1, b1, rel_bias,
                                            attn_sinks[j], layer=j, prev=new_cache_t)
            xs = xs.reshape(x_sample.shape)
            kv_shape = (-1, WINDOW, N_KV_HEADS, HEAD_DIM)
            nk_p.append(kp.reshape(kv_shape)); nv_p.append(vp.reshape(kv_shape))
        elif kind == 1:
            w = _ssm_weights(ssm_w_in[j], ssm_conv_w[j], ssm_conv_b[j], ssm_dt_bias[j], ssm_a_log[j],
                             ssm_d[j], ssm_norm_w[j], ssm_w_out[j])
            xp, cp, hp = _ssm_prompt(xp, w, g1, b1)
            xs, cs_, hs_ = _ssm_sample(xs, state_conv[j], state_ssm[j].reshape(-1, D_INNER, D_STATE), w, g1, b1)
            st_shape = (-1, SSM_HEADS, SSM_HEAD_DIM, D_STATE)
            nc_p.append(cp); nh_p.append(hp.reshape(st_shape))
            nc_s.append(cs_); nh_s.append(hs_.reshape(st_shape))
        else:
            pw = pool_w[j].astype(BF16)
            psc = pool_scale[j].reshape(1, d)
            xp, pp = _pool_prompt(xp, pw, psc, g1, b1)
            xs, ps_ = _pool_sample(xs, state_pool[j], pw, psc, g1, b1, start=PAST_LEN)
            npool_p.append(pp); npool_s.append(ps_)
        g2 = ln_g[i, 1].reshape(1, d)
        b2 = ln_b[i, 1].reshape(1, d)
        cast_next = (ffn_w32, i + 1) if i + 1 < DEPTH else None
        xp2, next_w = _ffn_ln(xp.reshape(-1, d), *ffn_w, g2, b2, layer=0, cast_next=cast_next)
        xs2, _ = _ffn_ln(xs.reshape(-1, d), *ffn_w, g2, b2, layer=0)
        xp, xs = xp2.reshape(xp.shape), xs2.reshape(xs.shape)
        if next_w:
            ffn_w = tuple(w[None] for w in next_w)
    nk_s, nv_s = (jnp.transpose(t, (0, 1, 4, 2, 3)) for t in new_cache_t)
    return (xp, xs,
            jnp.stack(nk_p), jnp.stack(nv_p), jnp.stack(nc_p), jnp.stack(nh_p), jnp.stack(npool_p),
            nk_s, nv_s, jnp.stack(nc_s), jnp.stack(nh_s), jnp.stack(npool_s))
```

```python
import functools
import math

import jax
import jax.numpy as jnp
import numpy as np
from jax import lax
from jax.experimental import pallas as pl
from jax.experimental.pallas import tpu as pltpu

DEPTH = 4
N_MIXERS = 3
PAST_LEN = 8192
DEEPNORM_ALPHA = (2 * DEPTH) ** 0.25
LN_EPS = 1e-5

V7X_VMEM_BYTES = 64 * 1024 * 1024
LANES = 128
SUBLANES = 8
MXU_DIM = 256

BF16 = jnp.bfloat16
F32 = jnp.float32


def _vmem_limit(estimate_bytes):
    return int(min(V7X_VMEM_BYTES - 8 * 1024 * 1024, max(32 * 1024 * 1024, estimate_bytes * 3 // 2)))


def _layer_norm_rows(v, g, b):
    mu = jnp.mean(v, axis=-1, keepdims=True)
    d = v - mu
    var = jnp.mean(d * d, axis=-1, keepdims=True)
    return d * lax.rsqrt(var + LN_EPS) * g + b


def _resident(shape):
    nd = len(shape)
    return pl.BlockSpec(shape, lambda *_: (0,) * nd, pipeline_mode=pl.Buffered(1))


def _layer_resident(shape, layer):
    nd = len(shape)
    return pl.BlockSpec((1,) + tuple(shape), lambda *_: (layer,) + (0,) * nd, pipeline_mode=pl.Buffered(1))


def _ffn_chunks(d_ff):
    step = 2 * MXU_DIM
    edges = list(range(0, d_ff, step)) + [d_ff]
    return tuple(zip(edges[:-1], edges[1:]))


def _ffn_kernel(x_ref, wg_ref, wu_ref, wd_ref, g_ref, b_ref, *rest, chunks, n_cast):
    casts, (o_ref, *cast_outs) = rest[:n_cast], rest[n_cast:]
    _run_cast_jobs(casts, cast_outs)
    x = x_ref[...]
    xb = x.astype(BF16)
    acc = jnp.zeros(x.shape, F32)
    for c0, c1 in chunks:
        gate = jnp.dot(xb, wg_ref[0, :, c0:c1], preferred_element_type=F32)
        up = jnp.dot(xb, wu_ref[0, :, c0:c1], preferred_element_type=F32)
        h = (gate * jax.nn.sigmoid(gate)) * up
        acc = acc + jnp.dot(h.astype(BF16), wd_ref[0, c0:c1, :], preferred_element_type=F32)
    o_ref[...] = _layer_norm_rows(DEEPNORM_ALPHA * x + acc, g_ref[...], b_ref[...])


def _cast_slab_specs(w32, layer, steps, step_of=lambda i: i):
    _, rows, cols = w32.shape
    if rows % (steps * 2 * SUBLANES) == 0:
        slab = rows // steps
    else:
        slab = next(s for s in range(LANES, rows + 1, LANES) if rows % s == 0 and rows // s <= steps)
    last = rows // slab - 1
    return (pl.BlockSpec((1, slab, cols), lambda *idx: (layer, jnp.minimum(step_of(*idx), last), 0)),
            pl.BlockSpec((slab, cols), lambda *idx: (jnp.minimum(step_of(*idx), last), 0)),
            jax.ShapeDtypeStruct((rows, cols), BF16))


def _cast_jobs_specs(cast_jobs, steps, step_of=lambda i: i):
    specs = [_cast_slab_specs(w32, layer, steps, step_of) for w32, layer in cast_jobs]
    return ([s[0] for s in specs], [s[1] for s in specs], [s[2] for s in specs], [w32 for w32, _ in cast_jobs])


def _run_cast_jobs(srcs, dsts):
    for src, dst in zip(srcs, dsts):
        dst[...] = src[0].astype(BF16)


def _ffn_ln(x, wg, wu, wd, g, b, *, layer, cast_jobs=(), tm=512):
    t, d = x.shape
    d_ff = wg.shape[-1]
    tm = min(tm, t)
    assert t % tm == 0
    steps = t // tm
    est = 3 * d * d_ff * 2 + 4 * tm * d * 4 + 4 * tm * d_ff * 4
    in_specs = [
        pl.BlockSpec((tm, d), lambda i: (i, 0)),
        _layer_resident((d, d_ff), layer),
        _layer_resident((d, d_ff), layer),
        _layer_resident((d_ff, d), layer),
        _resident((1, d)),
        _resident((1, d)),
    ]
    cast_in, cast_out, cast_shape, cast_args = _cast_jobs_specs(cast_jobs, steps)
    outs = pl.pallas_call(
        functools.partial(_ffn_kernel, chunks=_ffn_chunks(d_ff), n_cast=len(cast_jobs)),
        grid=(steps,),
        in_specs=in_specs + cast_in,
        out_specs=[pl.BlockSpec((tm, d), lambda i: (i, 0))] + cast_out,
        out_shape=[jax.ShapeDtypeStruct((t, d), F32)] + cast_shape,
        compiler_params=pltpu.CompilerParams(
            dimension_semantics=("arbitrary",), vmem_limit_bytes=_vmem_limit(est)),
        name="ffn_ln",
    )(x, wg, wu, wd, g, b, *cast_args)
    return outs[0], tuple(outs[1:])


HEAD_DIM = 64
N_HEADS = 16
N_KV_HEADS = 4
GQA_GROUP = N_HEADS // N_KV_HEADS
WINDOW = 128
REL_BUCKETS = 32
REL_MAX_DIST = 128
Q_DIM = N_HEADS * HEAD_DIM
KV_DIM = N_KV_HEADS * HEAD_DIM
NEG_INF = float("-inf")
CHAIN_BATCH = 2


def _t5_bucket_table(dist):
    n = np.maximum(dist, 0)
    max_exact = REL_BUCKETS // 2
    nf = np.maximum(n, 1).astype(np.float32)
    large = max_exact + (np.log(nf / np.float32(max_exact)) / np.float32(math.log(REL_MAX_DIST / max_exact))
                         * np.float32(REL_BUCKETS - max_exact)).astype(np.int32)
    large = np.minimum(large, REL_BUCKETS - 1)
    bucket = np.where(n < max_exact, n, large)
    valid = (dist >= 0) & (dist < WINDOW)
    return np.where(valid, bucket, -1).astype(np.int32)


def _bias_from_buckets(bucket, relb_ref, head):
    acc = jnp.full(bucket.shape, NEG_INF, F32)
    for bkt in range(REL_BUCKETS):
        acc = jnp.where(bucket == bkt, relb_ref[bkt, head], acc)
    return acc


def _attn_prompt_kernel(x_ref, wqkv_ref, bqkv_ref, wo_ref, bo_ref, g_ref, b_ref, bucket_ref,
                        relb_ref, sink_ref, *rest, tq, n_cast):
    casts = rest[:n_cast]
    o_ref, kout_ref, vout_ref = rest[n_cast:n_cast + 3]
    cast_outs = rest[n_cast + 3:2 * n_cast + 3]
    ka_lo, ka_hi, kb_lo, kb_hi, vt, ot, bias_scr, sink_scr = rest[2 * n_cast + 3:]
    _run_cast_jobs(casts, cast_outs)
    bi = pl.program_id(0)
    j = pl.program_id(1)
    nj = pl.num_programs(1)
    blk = WINDOW
    half_heads = GQA_GROUP // 2
    kbufs = (ka_lo, ka_hi, kb_lo, kb_hi)

    @pl.when((bi == 0) & (j == 0))
    def _build_tables():
        bucket = bucket_ref[...]
        lane = lax.broadcasted_iota(jnp.int32, (1, 2 * blk), 1)
        for kv in range(N_KV_HEADS):
            for half in range(2):
                ha = kv * GQA_GROUP + half
                hb = ha + half_heads
                pair = kv * 2 + half
                bias_scr[pair, :, 0:blk] = _bias_from_buckets(bucket, relb_ref, ha)
                bias_scr[pair, :, blk:2 * blk] = _bias_from_buckets(bucket, relb_ref, hb)
                sink_scr[pair] = jnp.where(lane < blk, sink_ref[ha], sink_ref[hb])

    @pl.when(j == 0)
    def _no_past():
        for buf in kbufs:
            buf[0:blk, :] = jnp.zeros((blk, KV_DIM), BF16)
        vt[:, 0:blk] = jnp.zeros((KV_DIM, blk), BF16)

    @pl.when(j > 0)
    def _carry():
        for buf in kbufs:
            buf[0:blk, :] = buf[tq:tq + blk, :]
        vt[:, 0:blk] = vt[:, tq:tq + blk]

    x = x_ref[0]
    qkv = jnp.dot(x.astype(BF16), wqkv_ref[0], preferred_element_type=F32) + bqkv_ref[...]
    q = (qkv[:, :Q_DIM] * (HEAD_DIM ** -0.5)).astype(BF16)
    k = qkv[:, Q_DIM:Q_DIM + KV_DIM]
    v = qkv[:, Q_DIM + KV_DIM:]
    k_sw = jnp.concatenate(
        [pltpu.roll(k[:, c * LANES:(c + 1) * LANES], HEAD_DIM, axis=1) for c in range(KV_DIM // LANES)], axis=1)
    lo = (lax.broadcasted_iota(jnp.int32, (1, KV_DIM), 1) % LANES) < HEAD_DIM
    ka_lo[blk:blk + tq, :] = jnp.where(lo, k, 0.0).astype(BF16)
    ka_hi[blk:blk + tq, :] = jnp.where(lo, 0.0, k).astype(BF16)
    kb_lo[blk:blk + tq, :] = jnp.where(lo, k_sw, 0.0).astype(BF16)
    kb_hi[blk:blk + tq, :] = jnp.where(lo, 0.0, k_sw).astype(BF16)
    vt[:, blk:blk + tq] = v.T.astype(BF16)

    @pl.when(j == nj - 1)
    def _emit_cache():
        kout_ref[0] = k[tq - WINDOW:, :]
        vout_ref[0] = v[tq - WINDOW:, :]

    ks = lax.broadcasted_iota(jnp.int32, (blk, 2 * blk), 0)
    qt = lax.broadcasted_iota(jnp.int32, (blk, 2 * blk), 1) % blk
    own = ks <= qt
    chains = [(i, kv, half) for i in range(tq // blk) for kv in range(N_KV_HEADS) for half in range(2)]

    def scores(i, kv, half):
        r0 = i * blk
        c0 = (kv // 2) * LANES
        in_lo = kv % 2 == 0
        if half == 0:
            kk = (ka_lo if in_lo else kb_lo)[r0:r0 + 2 * blk, c0:c0 + LANES]
        else:
            kk = (kb_hi if in_lo else ka_hi)[r0:r0 + 2 * blk, c0:c0 + LANES]
        qa = q[r0:r0 + blk, (2 * kv) * LANES:(2 * kv + 1) * LANES]
        qb = q[r0:r0 + blk, (2 * kv + 1) * LANES:(2 * kv + 2) * LANES]
        q_pair = jnp.concatenate([qa, qb], axis=0)
        return lax.dot_general(kk, q_pair, (((1,), (1,)), ((), ())), preferred_element_type=F32)

    def fold(i, kv, half, s2):
        s_prev = s2[0:blk, :]
        if i == 0:
            s_prev = jnp.where(j == 0, NEG_INF, s_prev)
        return jnp.where(own, s2[blk:2 * blk, :], s_prev) + bias_scr[kv * 2 + half]

    def softmax_batch(batch, s2s):
        sinks = [sink_scr[kv * 2 + half] for _, kv, half in batch]
        ss = [fold(*ch, s2) for ch, s2 in zip(batch, s2s)]
        ms = [jnp.maximum(jnp.max(s, axis=0, keepdims=True), sink) for s, sink in zip(ss, sinks)]
        ps = [jnp.exp(s - m) for s, m in zip(ss, ms)]
        invs = [1.0 / (jnp.sum(p, axis=0, keepdims=True) + jnp.exp(sink - m)) for p, m, sink in zip(ps, ms, sinks)]
        p2s = [jnp.concatenate([jnp.where(own, 0.0, p), jnp.where(own, p, 0.0)], axis=0).astype(BF16) for p in ps]
        return list(zip(p2s, invs))

    def weighted_values(i, kv, half, p2, inv):
        r0 = i * blk
        ha = kv * GQA_GROUP + half
        hb = ha + half_heads
        v_t = vt[kv * HEAD_DIM:(kv + 1) * HEAD_DIM, r0:r0 + 2 * blk]
        o_t = jnp.dot(v_t, p2, preferred_element_type=F32) * inv
        ot[ha * HEAD_DIM:(ha + 1) * HEAD_DIM, r0:r0 + blk] = o_t[:, 0:blk]
        ot[hb * HEAD_DIM:(hb + 1) * HEAD_DIM, r0:r0 + blk] = o_t[:, blk:2 * blk]

    batches = [chains[c:c + CHAIN_BATCH] for c in range(0, len(chains), CHAIN_BATCH)]
    s_next = [scores(*ch) for ch in batches[0]]
    prev = []
    for bi, batch in enumerate(batches):
        s_cur = s_next
        if bi + 1 < len(batches):
            s_next = [scores(*ch) for ch in batches[bi + 1]]
        probs = softmax_batch(batch, s_cur)
        for ch, pr in prev:
            weighted_values(*ch, *pr)
        prev = list(zip(batch, probs))
    for ch, pr in prev:
        weighted_values(*ch, *pr)

    o = ot[...].T.astype(BF16)
    y = jnp.dot(o, wo_ref[0], preferred_element_type=F32) + bo_ref[...]
    o_ref[0] = _layer_norm_rows(DEEPNORM_ALPHA * x + y, g_ref[...], b_ref[...])


def _attn_prompt(x, wqkv, bqkv, wo, bo, g, b, rel_bias, sinks, *, layer, cast_jobs=(), tq=1024):
    bsz, l, d = x.shape
    tq = min(tq, l)
    assert l % tq == 0 and tq % WINDOW == 0
    nj = l // tq
    cast_in, cast_out, cast_shape, cast_args = _cast_jobs_specs(cast_jobs, bsz * nj, lambda bi, j: bi * nj + j)
    qi = np.arange(WINDOW)[None, :]
    si = np.arange(WINDOW)[:, None]
    bucket = jnp.asarray(_t5_bucket_table(np.where(si <= qi, qi - si, qi + WINDOW - si)))
    qkv_dim = wqkv.shape[-1]
    smem = pl.BlockSpec(memory_space=pltpu.SMEM)
    est = (d * qkv_dim + Q_DIM * d) * 2 + 5 * tq * d * 4 + N_HEADS * WINDOW * 2 * WINDOW * 4 + 3 * tq * qkv_dim * 4
    return pl.pallas_call(
        functools.partial(_attn_prompt_kernel, tq=tq, n_cast=len(cast_jobs)),
        grid=(bsz, nj),
        in_specs=[
            pl.BlockSpec((1, tq, d), lambda bi, j: (bi, j, 0)),
            _layer_resident((d, qkv_dim), layer), _resident((1, qkv_dim)),
            _layer_resident((Q_DIM, d), layer), _resident((1, d)),
            _resident((1, d)), _resident((1, d)),
            _resident((WINDOW, WINDOW)),
            smem, smem,
        ] + cast_in,
        out_specs=[
            pl.BlockSpec((1, tq, d), lambda bi, j: (bi, j, 0)),
            pl.BlockSpec((1, WINDOW, KV_DIM), lambda bi, j: (bi, 0, 0)),
            pl.BlockSpec((1, WINDOW, KV_DIM), lambda bi, j: (bi, 0, 0)),
        ] + cast_out,
        out_shape=[
            jax.ShapeDtypeStruct((bsz, l, d), F32),
            jax.ShapeDtypeStruct((bsz, WINDOW, KV_DIM), F32),
            jax.ShapeDtypeStruct((bsz, WINDOW, KV_DIM), F32),
        ] + cast_shape,
        scratch_shapes=[pltpu.VMEM((WINDOW + tq, KV_DIM), BF16)] * 4 + [
            pltpu.VMEM((KV_DIM, WINDOW + tq), BF16),
            pltpu.VMEM((Q_DIM, tq), F32),
            pltpu.VMEM((N_HEADS // 2, WINDOW, 2 * WINDOW), F32),
            pltpu.VMEM((N_HEADS // 2, 1, 2 * WINDOW), F32),
        ],
        compiler_params=pltpu.CompilerParams(
            dimension_semantics=("arbitrary", "arbitrary"), vmem_limit_bytes=_vmem_limit(est)),
        name="attn_prompt",
    )(x, wqkv, bqkv, wo, bo, g, b, bucket, rel_bias, sinks, *cast_args)


SAMPLE_KEYS = 2 * WINDOW
SAMPLE_CHAIN_BATCH = 8


def _attn_sample_kernel(x_ref, ck_ref, cv_ref, wqkv_ref, bqkv_ref, wo_ref, bo_ref, g_ref, b_ref, bucket_ref,
                        relb_ref, sink_ref, *rest, bb, l, n_prev, out_layer):
    o_ref, nk_ref, nv_ref, qbuf, obuf, bias_scr = rest[n_prev:]
    for other in range(nk_ref.shape[0]):
        if other != out_layer:
            nk_ref[other] = jnp.zeros(nk_ref.shape[1:], F32)
            nv_ref[other] = jnp.zeros(nv_ref.shape[1:], F32)

    @pl.when(pl.program_id(0) == 0)
    def _init():
        bucket = bucket_ref[...]
        for h in range(N_HEADS):
            kv, g_ = divmod(h, GQA_GROUP)
            bias_scr[kv, g_ * l:(g_ + 1) * l, :] = _bias_from_buckets(bucket, relb_ref, h)

    x = x_ref[...]
    qkv = jnp.dot(x.astype(BF16), wqkv_ref[0], preferred_element_type=F32) + bqkv_ref[...]
    qbuf[...] = qkv[:, :Q_DIM] * (HEAD_DIM ** -0.5)
    k_new = qkv[:, Q_DIM:Q_DIM + KV_DIM]
    v_new = qkv[:, Q_DIM + KV_DIM:]

    sink_cols = []
    for kv in range(N_KV_HEADS):
        sink_cols.append(jnp.concatenate(
            [jnp.full((l, 1), sink_ref[kv * GQA_GROUP + g_], F32) for g_ in range(GQA_GROUP)], axis=0))

    kn_t = k_new.T
    vn_t = v_new.T
    lane = lax.broadcasted_iota(jnp.int32, (HEAD_DIM, WINDOW), 1)
    pad = jnp.zeros((WINDOW - l, KV_DIM), F32)

    def shifted(old_t, new_t, i):
        kept = pltpu.roll(old_t, WINDOW - l, axis=1)
        new = pltpu.roll(new_t, (WINDOW - l - i * l) % WINDOW, axis=1)
        return jnp.where(lane >= WINDOW - l, new, kept)

    def keys_values(i):
        r0 = i * l
        for kv in range(N_KV_HEADS):
            hs = slice(kv * HEAD_DIM, (kv + 1) * HEAD_DIM)
            nk_ref[out_layer, i, kv] = shifted(ck_ref[0, i, kv], kn_t[hs, :], i)
            nv_ref[out_layer, i, kv] = shifted(cv_ref[0, i, kv], vn_t[hs, :], i)
        kn = jnp.concatenate([k_new[r0:r0 + l, :], pad], axis=0).astype(BF16)
        vn = jnp.concatenate([v_new[r0:r0 + l, :], pad], axis=0).astype(BF16)
        return kn, vn

    def scores(i, kv, kn):
        r0 = i * l
        c0 = kv * HEAD_DIM
        qs = jnp.concatenate(
            [qbuf[r0:r0 + l, (kv * GQA_GROUP + g_) * HEAD_DIM:(kv * GQA_GROUP + g_ + 1) * HEAD_DIM]
             for g_ in range(GQA_GROUP)], axis=0).astype(BF16)
        s_cache = jnp.dot(qs, ck_ref[0, i, kv].astype(BF16), preferred_element_type=F32)
        s_new = lax.dot_general(qs, kn[:, c0:c0 + HEAD_DIM], (((1,), (1,)), ((), ())), preferred_element_type=F32)
        return jnp.concatenate([s_cache, s_new], axis=1)

    def weighted_values(i, kv, p, vn):
        c0 = kv * HEAD_DIM
        pb = p.astype(BF16)
        o_cache = lax.dot_general(pb[:, :WINDOW], cv_ref[0, i, kv].astype(BF16), (((1,), (1,)), ((), ())),
                                  preferred_element_type=F32)
        return o_cache + jnp.dot(pb[:, WINDOW:], vn[:, c0:c0 + HEAD_DIM], preferred_element_type=F32)

    for i0 in range(0, bb, SAMPLE_CHAIN_BATCH):
        items = range(i0, min(bb, i0 + SAMPLE_CHAIN_BATCH))
        kvs = {i: keys_values(i) for i in items}
        chains = [(i, kv) for i in items for kv in range(N_KV_HEADS)]
        s_all = [scores(i, kv, kvs[i][0]) + bias_scr[kv] for i, kv in chains]
        m_all = [jnp.maximum(jnp.max(s, axis=-1, keepdims=True), sink_cols[kv]) for s, (i, kv) in zip(s_all, chains)]
        p_all = [jnp.exp(s - m) for s, m in zip(s_all, m_all)]
        den_all = [jnp.sum(p, axis=-1, keepdims=True) + jnp.exp(sink_cols[kv] - m)
                   for p, m, (i, kv) in zip(p_all, m_all, chains)]
        o_all = [weighted_values(i, kv, p, kvs[i][1]) for p, (i, kv) in zip(p_all, chains)]
        for o, den, (i, kv) in zip(o_all, den_all, chains):
            o = o / den
            for g_ in range(GQA_GROUP):
                h = kv * GQA_GROUP + g_
                obuf[i * l:(i + 1) * l, h * HEAD_DIM:(h + 1) * HEAD_DIM] = o[g_ * l:(g_ + 1) * l, :]

    y = jnp.dot(obuf[...].astype(BF16), wo_ref[0], preferred_element_type=F32) + bo_ref[...]
    o_ref[...] = _layer_norm_rows(DEEPNORM_ALPHA * x + y, g_ref[...], b_ref[...])


def _attn_sample(x, cache_k_t, cache_v_t, wqkv, bqkv, wo, bo, g, b, rel_bias, sinks, *, layer, cache_layer, prev=None,
                 bb=16):
    bsz, l, d = x.shape
    assert bsz % bb == 0 and l == SUBLANES and bb * l == WINDOW
    ti = np.arange(l)[:, None]
    ci = np.arange(SAMPLE_KEYS)[None, :]
    table = _t5_bucket_table(ti + WINDOW - ci)
    table = np.where(ci < WINDOW + l, table, -1).astype(np.int32)
    bucket = jnp.asarray(table)
    qkv_dim = wqkv.shape[-1]
    rows = bb * l
    smem = pl.BlockSpec(memory_space=pltpu.SMEM)
    cache_spec = pl.BlockSpec((1, bb, N_KV_HEADS, HEAD_DIM, WINDOW), lambda i: (cache_layer, i, 0, 0, 0))
    est = (d * qkv_dim + Q_DIM * d) * 2 + 8 * bb * WINDOW * KV_DIM * 4 + 8 * rows * d * 4
    in_specs = [
        pl.BlockSpec((rows, d), lambda i: (i, 0)),
        cache_spec, cache_spec,
        _layer_resident((d, qkv_dim), layer), _resident((1, qkv_dim)),
        _layer_resident((Q_DIM, d), layer), _resident((1, d)),
        _resident((1, d)), _resident((1, d)),
        _resident((l, SAMPLE_KEYS)),
        smem, smem,
    ]
    args = [x.reshape(bsz * l, d), cache_k_t, cache_v_t, wqkv, bqkv, wo, bo, g, b, bucket, rel_bias, sinks]
    n_layers = cache_k_t.shape[0]
    if prev is None:
        aliases, out_layer = {}, cache_layer
        out_cache_spec = pl.BlockSpec((n_layers, bb, N_KV_HEADS, HEAD_DIM, WINDOW), lambda i: (0, i, 0, 0, 0))
    else:
        aliases, out_layer = {len(args): 1, len(args) + 1: 2}, 0
        out_cache_spec = cache_spec
        in_specs += [pl.BlockSpec(memory_space=pl.ANY)] * 2
        args += list(prev)
    return pl.pallas_call(
        functools.partial(_attn_sample_kernel, bb=bb, l=l, n_prev=len(aliases), out_layer=out_layer),
        grid=(bsz // bb,),
        in_specs=in_specs,
        out_specs=[pl.BlockSpec((rows, d), lambda i: (i, 0)), out_cache_spec, out_cache_spec],
        out_shape=[
            jax.ShapeDtypeStruct((bsz * l, d), F32),
            jax.ShapeDtypeStruct(cache_k_t.shape, F32),
            jax.ShapeDtypeStruct(cache_v_t.shape, F32),
        ],
        scratch_shapes=[
            pltpu.VMEM((rows, Q_DIM), F32),
            pltpu.VMEM((rows, Q_DIM), F32),
            pltpu.VMEM((N_KV_HEADS, GQA_GROUP * l, SAMPLE_KEYS), F32),
        ],
        input_output_aliases=aliases,
        compiler_params=pltpu.CompilerParams(
            dimension_semantics=("arbitrary",), vmem_limit_bytes=_vmem_limit(est)),
        name="attn_sample",
    )(*args)


POOL_WINDOWS = (2, 4, 8, 16)
POOL_PAD = max(POOL_WINDOWS)
POOL_STATE_LEN = POOL_PAD - 1
POOL_HIST = len(POOL_WINDOWS) * SUBLANES


def _pool_mix(window_sum, x, cnt_of, pw_ref, scale):
    gd = x.shape[-1] // len(POOL_WINDOWS)
    outs = []
    for g, w in enumerate(POOL_WINDOWS):
        diff = window_sum(g, w) / cnt_of(w) - x[:, g * gd:(g + 1) * gd]
        outs.append(jnp.dot(diff.astype(BF16), pw_ref[g], preferred_element_type=F32))
    return jnp.concatenate(outs, axis=-1) * scale


def _pool_prompt_kernel(x_ref, pw_ref, sc_ref, g_ref, b_ref, o_ref, st_ref, buf, lvl, *, tm):
    j = pl.program_id(1)
    nj = pl.num_programs(1)
    d = x_ref.shape[-1]
    gd = d // len(POOL_WINDOWS)
    hist, rows = POOL_HIST, POOL_HIST + tm

    @pl.when(j == 0)
    def _no_past():
        buf[0:hist, :] = jnp.zeros((hist, d), F32)

    @pl.when(j > 0)
    def _carry():
        buf[0:hist, :] = buf[tm:tm + hist, :]

    x = x_ref[0]
    buf[hist:rows, :] = x

    sums = {}
    src = buf
    for k, w in enumerate(POOL_WINDOWS):
        step, c0, r0 = w // 2, k * gd, (k + 1) * SUBLANES
        cur = src[r0:rows, c0:d] + src[r0 - step:rows - step, c0:d]
        sums[k] = cur[hist - r0:, 0:gd]
        if k + 1 < len(POOL_WINDOWS):
            lvl[k, r0:rows, c0:d] = cur
            src = lvl.at[k]

    def window_sum(g, w):
        return sums[g]

    pos1 = j * tm + lax.broadcasted_iota(jnp.int32, (tm, 1), 0) + 1

    def cnt_of(w):
        return jnp.minimum(pos1, w).astype(F32)

    y = _pool_mix(window_sum, x, cnt_of, pw_ref, sc_ref[...])
    o_ref[0] = _layer_norm_rows(DEEPNORM_ALPHA * x + y, g_ref[...], b_ref[...])

    @pl.when(j == nj - 1)
    def _emit_state():
        st_ref[0] = buf[rows - POOL_STATE_LEN:rows, :]


def _pool_prompt(x, pw, scale, g, b, *, tm=512):
    bsz, l, d = x.shape
    tm = min(tm, l)
    assert l % tm == 0 and tm >= POOL_HIST >= POOL_STATE_LEN and POOL_WINDOWS == (2, 4, 8, 16)
    ng, gd = pw.shape[0], pw.shape[1]
    est = 10 * tm * d * 4 + ng * gd * gd * 2
    return pl.pallas_call(
        functools.partial(_pool_prompt_kernel, tm=tm),
        grid=(bsz, l // tm),
        in_specs=[
            pl.BlockSpec((1, tm, d), lambda bi, j: (bi, j, 0)),
            _resident((ng, gd, gd)), _resident((1, d)), _resident((1, d)), _resident((1, d)),
        ],
        out_specs=[
            pl.BlockSpec((1, tm, d), lambda bi, j: (bi, j, 0)),
            pl.BlockSpec((1, POOL_STATE_LEN, d), lambda bi, j: (bi, 0, 0)),
        ],
        out_shape=[
            jax.ShapeDtypeStruct((bsz, l, d), F32),
            jax.ShapeDtypeStruct((bsz, POOL_STATE_LEN, d), F32),
        ],
        scratch_shapes=[pltpu.VMEM((POOL_HIST + tm, d), F32),
                        pltpu.VMEM((len(POOL_WINDOWS) - 1, POOL_HIST + tm, d), F32)],
        compiler_params=pltpu.CompilerParams(
            dimension_semantics=("arbitrary", "arbitrary"), vmem_limit_bytes=_vmem_limit(est)),
        name="pool_prompt",
    )(x, pw, scale, g, b)


def _pool_sample_kernel(x_ref, st_ref, pw_ref, sc_ref, g_ref, b_ref, o_ref, nst_ref, buf, *, bb, l):
    d = x_ref.shape[-1]
    gd = d // len(POOL_WINDOWS)
    x3 = x_ref[...]
    buf[:, 1:POOL_PAD, :] = st_ref[...]
    buf[:, POOL_PAD:POOL_PAD + l, :] = x3
    x = x3.reshape(bb * l, d)

    def window_sum(g, w):
        acc = x3[:, :, g * gd:(g + 1) * gd]
        for s in range(1, w):
            acc = acc + buf[:, POOL_PAD - s:POOL_PAD - s + l, g * gd:(g + 1) * gd]
        return acc.reshape(bb * l, gd)

    y = _pool_mix(window_sum, x, lambda w: float(w), pw_ref, sc_ref[...])
    o_ref[...] = _layer_norm_rows(DEEPNORM_ALPHA * x + y, g_ref[...], b_ref[...]).reshape(bb, l, d)
    nst_ref[...] = buf[:, l + 1:l + POOL_PAD, :]


def _pool_sample(x, state, pw, scale, g, b, *, start, bb=16):
    bsz, l, d = x.shape
    assert bsz % bb == 0 and l == SUBLANES and start + 1 >= POOL_PAD
    ng, gd = pw.shape[0], pw.shape[1]
    est = 8 * bb * (POOL_PAD + l) * d * 4 + ng * gd * gd * 2
    return pl.pallas_call(
        functools.partial(_pool_sample_kernel, bb=bb, l=l),
        grid=(bsz // bb,),
        in_specs=[
            pl.BlockSpec((bb, l, d), lambda i: (i, 0, 0)),
            pl.BlockSpec((bb, POOL_STATE_LEN, d), lambda i: (i, 0, 0)),
            _resident((ng, gd, gd)), _resident((1, d)), _resident((1, d)), _resident((1, d)),
        ],
        out_specs=[
            pl.BlockSpec((bb, l, d), lambda i: (i, 0, 0)),
            pl.BlockSpec((bb, POOL_STATE_LEN, d), lambda i: (i, 0, 0)),
        ],
        out_shape=[
            jax.ShapeDtypeStruct((bsz, l, d), F32),
            jax.ShapeDtypeStruct((bsz, POOL_STATE_LEN, d), F32),
        ],
        scratch_shapes=[pltpu.VMEM((bb, POOL_PAD + l, d), F32)],
        compiler_params=pltpu.CompilerParams(
            dimension_semantics=("arbitrary",), vmem_limit_bytes=_vmem_limit(est)),
        name="pool_sample",
    )(x, state, pw, scale, g, b)


D_INNER = 2048
SSM_HEAD_DIM = 64
SSM_HEADS = D_INNER // SSM_HEAD_DIM
SSM_GROUPS = 4
SSM_HPG = SSM_HEADS // SSM_GROUPS
D_STATE = 128
CONV_WIDTH = 4
GBN = SSM_GROUPS * D_STATE
CONV_DIM = D_INNER + 2 * GBN
GROUP_INNER = D_INNER // SSM_GROUPS
RMS_EPS = 1e-5
CONV_PAD = SUBLANES
SSM_ROWS = 128


def _split_bf16(v, parts=3):
    out = []
    r = v
    for _ in range(parts):
        p = r.astype(BF16)
        out.append(p)
        r = r - p.astype(F32)
    return out


def _expand_heads(v, e):
    return sum(jnp.dot(p, e, preferred_element_type=F32) for p in _split_bf16(v, parts=2))


def _ssm_in_proj(xb, win_ref, wdt_ref, c0, c1):
    main = win_ref.shape[1]
    if c0 >= main:
        return jnp.dot(xb, wdt_ref[:, c0 - main:c1 - main], preferred_element_type=F32)
    assert c1 <= main
    return jnp.dot(xb, win_ref[:, c0:c1], preferred_element_type=F32)


def _ssm_prep(proj, xp_buf, conv_state, dtb_ref, alog_ref, tril_ref, *, nseq, q):
    rows = nseq * q
    xbc_pre = proj(D_INNER, D_INNER + CONV_DIM)
    dt_pre = proj(D_INNER + CONV_DIM, D_INNER + CONV_DIM + LANES)
    if conv_state is not None:
        xp_buf[:, CONV_PAD - (CONV_WIDTH - 1):CONV_PAD, :] = conv_state
    xp_buf[:, CONV_PAD:CONV_PAD + q, :] = xbc_pre.reshape(nseq, q, CONV_DIM)
    new_conv = xp_buf[:, q + CONV_PAD - (CONV_WIDTH - 1):q + CONV_PAD, :]

    dtv = dt_pre + dtb_ref[...]
    dt = jnp.maximum(dtv, 0.0) + jnp.log1p(jnp.exp(-jnp.abs(dtv)))
    a = -jnp.exp(alog_ref[...])
    tril = tril_ref[...]
    acum = sum(jnp.dot(tril, p, preferred_element_type=F32) for p in _split_bf16(dt * a))
    a3 = acum.reshape(nseq, q, LANES)
    alast = jnp.broadcast_to(a3[:, q - 1:q, :], (nseq, q, LANES)).reshape(rows, LANES)
    return dict(acum=acum, acum_t=acum.T, dt_t=dt.T, causal=tril > 0.5,
                exp_a=jnp.exp(acum), dd=jnp.exp(alast - acum) * dt, new_conv=new_conv)


def _conv_silu(xp_buf, cw_ref, cb_ref, c0, c1, *, nseq, q):
    acc = cb_ref[:, c0:c1].reshape(1, 1, c1 - c0)
    for jj in range(CONV_WIDTH):
        off = CONV_PAD - (CONV_WIDTH - 1) + jj
        acc = acc + xp_buf[:, off:off + q, c0:c1] * cw_ref[jj:jj + 1, c0:c1].reshape(1, 1, c1 - c0)
    acc = acc.reshape(nseq * q, c1 - c0)
    return acc * jax.nn.sigmoid(acc)


def _ssm_group_diag(g, f, proj, prep, xp_buf, cw_ref, cb_ref, ybuf, *, nseq, q):
    f["z"] = proj(g * GROUP_INNER, (g + 1) * GROUP_INNER)
    conv = functools.partial(_conv_silu, xp_buf, cw_ref, cb_ref, nseq=nseq, q=q)
    half = GROUP_INNER // 2
    xs_lo = conv(g * GROUP_INNER, g * GROUP_INNER + half)
    yield
    xs_hi = conv(g * GROUP_INNER + half, (g + 1) * GROUP_INNER)
    yield
    xs = f["xs"] = jnp.concatenate([xs_lo, xs_hi], axis=-1)
    bm = f["bm"] = conv(D_INNER + g * D_STATE, D_INNER + (g + 1) * D_STATE)
    cm = f["cm"] = conv(D_INNER + GBN + g * D_STATE, D_INNER + GBN + (g + 1) * D_STATE)
    yield
    cb = lax.dot_general(cm.astype(BF16), bm.astype(BF16), (((1,), (1,)), ((), ())), preferred_element_type=F32)
    acum, acum_t, dt_t, causal = prep["acum"], prep["acum_t"], prep["dt_t"], prep["causal"]
    for r in range(SSM_HPG):
        h = g * SSM_HPG + r
        seg = acum[:, h:h + 1] - acum_t[h:h + 1, :]
        lm = jnp.exp(jnp.where(causal, seg, NEG_INF))
        w = (cb * lm * dt_t[h:h + 1, :]).astype(BF16)
        xh = xs[:, r * SSM_HEAD_DIM:(r + 1) * SSM_HEAD_DIM].astype(BF16)
        ybuf[:, h * SSM_HEAD_DIM:(h + 1) * SSM_HEAD_DIM] = jnp.dot(w, xh, preferred_element_type=F32)
        yield


def _ssm_group_expand(g, f, prep, e_ref):
    e = e_ref[:, g * GROUP_INNER:(g + 1) * GROUP_INNER]
    f["exp_a"] = _expand_heads(prep["exp_a"], e)
    yield
    f["xdd"] = (f["xs"] * _expand_heads(prep["dd"], e)).astype(BF16)
    yield


def _interleave(*gens):
    live = list(gens)
    while live:
        for gen in list(live):
            try:
                next(gen)
            except StopIteration:
                live.remove(gen)


def _gate_norm(y, z, nw):
    y = y * (z * jax.nn.sigmoid(z))
    return y * lax.rsqrt(jnp.mean(y * y, axis=-1, keepdims=True) + RMS_EPS) * nw


def _ssm_prompt_kernel(x_ref, win_ref, wdt_ref, cw_ref, cb_ref, dtb_ref, alog_ref, dsk_ref, nw_ref, wout_ref, e_ref,
                       tril_ref, g_ref, b_ref, o_ref, cout_ref, sout_ref, xp_buf, ht, ybuf):
    j = pl.program_id(1)
    nj = pl.num_programs(1)
    q = SSM_ROWS

    @pl.when(j == 0)
    def _no_past():
        xp_buf[:, 0:CONV_PAD, :] = jnp.zeros((1, CONV_PAD, CONV_DIM), F32)
        ht[...] = jnp.zeros(ht.shape, F32)

    @pl.when(j > 0)
    def _carry():
        xp_buf[:, 0:CONV_PAD, :] = xp_buf[:, q:q + CONV_PAD, :]

    x = x_ref[0]
    xb = x.astype(BF16)
    zx = jnp.dot(xb, win_ref[...], preferred_element_type=F32)
    dt_pre = jnp.dot(xb, wdt_ref[...], preferred_element_type=F32)
    z = zx[:, :D_INNER]

    def proj(c0, c1):
        return dt_pre if c0 >= D_INNER + CONV_DIM else zx[:, c0:c1]

    prep = _ssm_prep(proj, xp_buf, None, dtb_ref, alog_ref, tril_ref, nseq=1, q=q)
    xbc = _conv_silu(xp_buf, cw_ref, cb_ref, 0, CONV_DIM, nseq=1, q=q)
    xs = xbc[:, :D_INNER]
    acum, acum_t, dt_t, causal = prep["acum"], prep["acum_t"], prep["dt_t"], prep["causal"]
    for g in range(SSM_GROUPS):
        bg = xbc[:, D_INNER + g * D_STATE:D_INNER + (g + 1) * D_STATE].astype(BF16)
        cg = xbc[:, D_INNER + GBN + g * D_STATE:D_INNER + GBN + (g + 1) * D_STATE].astype(BF16)
        cb = lax.dot_general(cg, bg, (((1,), (1,)), ((), ())), preferred_element_type=F32)
        for r in range(SSM_HPG):
            h = g * SSM_HPG + r
            seg = acum[:, h:h + 1] - acum_t[h:h + 1, :]
            lm = jnp.exp(jnp.where(causal, seg, NEG_INF))
            w = (cb * lm * dt_t[h:h + 1, :]).astype(BF16)
            xh = xs[:, h * SSM_HEAD_DIM:(h + 1) * SSM_HEAD_DIM].astype(BF16)
            ybuf[:, h * SSM_HEAD_DIM:(h + 1) * SSM_HEAD_DIM] = jnp.dot(w, xh, preferred_element_type=F32)

    e = e_ref[...]
    exp_a = _expand_heads(prep["exp_a"], e)
    xdd = (xs * _expand_heads(prep["dd"], e)).astype(BF16)
    decay = exp_a[q - 1:q, :]
    y_off = []
    for g in range(SSM_GROUPS):
        sl = slice(g * GROUP_INNER, (g + 1) * GROUP_INNER)
        bg = xbc[:, D_INNER + g * D_STATE:D_INNER + (g + 1) * D_STATE]
        cg = xbc[:, D_INNER + GBN + g * D_STATE:D_INNER + GBN + (g + 1) * D_STATE].astype(BF16)
        hg = ht[:, sl]
        y_off.append(jnp.dot(cg, hg.astype(BF16), preferred_element_type=F32))
        ht[:, sl] = hg * decay[:, sl] + jnp.dot(bg.T.astype(BF16), xdd[:, sl], preferred_element_type=F32)
    y = ybuf[...] + exp_a * jnp.concatenate(y_off, axis=-1) + dsk_ref[...] * xs
    y = jnp.concatenate(
        [_gate_norm(y[:, g * GROUP_INNER:(g + 1) * GROUP_INNER], z[:, g * GROUP_INNER:(g + 1) * GROUP_INNER],
                    nw_ref[:, g * GROUP_INNER:(g + 1) * GROUP_INNER]) for g in range(SSM_GROUPS)], axis=-1)
    out = jnp.dot(y.astype(BF16), wout_ref[...], preferred_element_type=F32)
    o_ref[0] = _layer_norm_rows(DEEPNORM_ALPHA * x + out, g_ref[...], b_ref[...])

    @pl.when(j == nj - 1)
    def _emit_state():
        cout_ref[...] = prep["new_conv"]
        sout_ref[0] = ht[...].T


def _ssm_consts(nseq, q):
    rows = nseq * q
    r = np.arange(rows)
    tril = ((r[:, None] >= r[None, :]) & (r[:, None] // q == r[None, :] // q)).astype(np.float32)
    e = np.zeros((LANES, D_INNER), np.float32)
    e[np.arange(D_INNER) // SSM_HEAD_DIM, np.arange(D_INNER)] = 1.0
    return jnp.asarray(tril, BF16), jnp.asarray(e, BF16)


def _ssm_weight_specs(d):
    return [
        pl.BlockSpec((d, D_INNER + CONV_DIM), lambda *_: (0, 0), pipeline_mode=pl.Buffered(1)),
        _resident((d, LANES)), _resident((CONV_WIDTH, CONV_DIM)), _resident((1, CONV_DIM)),
        _resident((1, LANES)), _resident((1, LANES)),
    ]


def _ssm_prompt(x, w, g, b):
    bsz, l, d = x.shape
    q = SSM_ROWS
    assert l % q == 0
    tril, e = _ssm_consts(1, q)
    win_cols = D_INNER + CONV_DIM + LANES
    est = (d * win_cols + D_INNER * d) * 2 + LANES * D_INNER * 4 + 12 * q * D_INNER * 4 + 3 * q * win_cols * 4
    return pl.pallas_call(
        _ssm_prompt_kernel,
        grid=(bsz, l // q),
        in_specs=[pl.BlockSpec((1, q, d), lambda bi, j: (bi, j, 0))] + _ssm_weight_specs(d) + [
            _resident((1, D_INNER)), _resident((1, D_INNER)), _resident((D_INNER, d)),
            _resident((LANES, D_INNER)), _resident((q, q)), _resident((1, d)), _resident((1, d)),
        ],
        out_specs=[
            pl.BlockSpec((1, q, d), lambda bi, j: (bi, j, 0)),
            pl.BlockSpec((1, CONV_WIDTH - 1, CONV_DIM), lambda bi, j: (bi, 0, 0)),
            pl.BlockSpec((1, D_INNER, D_STATE), lambda bi, j: (bi, 0, 0)),
        ],
        out_shape=[
            jax.ShapeDtypeStruct((bsz, l, d), F32),
            jax.ShapeDtypeStruct((bsz, CONV_WIDTH - 1, CONV_DIM), F32),
            jax.ShapeDtypeStruct((bsz, D_INNER, D_STATE), F32),
        ],
        scratch_shapes=[
            pltpu.VMEM((1, CONV_PAD + q, CONV_DIM), F32),
            pltpu.VMEM((D_STATE, D_INNER), F32),
            pltpu.VMEM((q, D_INNER), F32),
        ],
        compiler_params=pltpu.CompilerParams(
            dimension_semantics=("arbitrary", "arbitrary"), vmem_limit_bytes=_vmem_limit(est)),
        name="ssm_prompt",
    )(x, w["win"], w["wdt"], w["cw"], w["cb"], w["dtb"], w["alog"], w["dsk"], w["nw"], w["wout"], e, tril, g, b)


def _ssm_sample_front_kernel(x_ref, cst_ref, win_ref, wdt_ref, cw_ref, cb_ref, dtb_ref, alog_ref, dsk_ref, e_ref, tril_ref,
                             y_ref, z_ref, ea_ref, xdd_ref, c_ref, bt_ref, cout_ref, xp_buf, ybuf, *, nseq, q):
    xb = x_ref[...].astype(BF16)

    def proj(c0, c1):
        return _ssm_in_proj(xb, win_ref, wdt_ref, c0, c1)

    prep = _ssm_prep(proj, xp_buf, cst_ref[...], dtb_ref, alog_ref, tril_ref, nseq=nseq, q=q)
    cout_ref[...] = prep["new_conv"]
    for g in range(SSM_GROUPS):
        sl = slice(g * GROUP_INNER, (g + 1) * GROUP_INNER)
        f = {}
        _interleave(_ssm_group_diag(g, f, proj, prep, xp_buf, cw_ref, cb_ref, ybuf, nseq=nseq, q=q))
        _interleave(_ssm_group_expand(g, f, prep, e_ref))
        y_ref[:, sl] = ybuf[:, sl] + dsk_ref[:, sl] * f["xs"]
        z_ref[:, sl] = f["z"]
        ea_ref[:, sl] = f["exp_a"]
        xdd_ref[:, sl] = f["xdd"]
        c_ref[:, g * D_STATE:(g + 1) * D_STATE] = f["cm"]
        bt_ref[g * D_STATE:(g + 1) * D_STATE, :] = f["bm"].T.astype(BF16)


def _ssm_sample_state_kernel(st_ref, y_ref, ea_ref, c_ref, xdd_ref, bt_ref, o_ref, nst_ref, *, bb, q, nseq):
    ea = ea_ref[...]
    xdd = xdd_ref[...]
    first_seq = (pl.program_id(0) % (nseq // bb)) * bb
    lane_seq = lax.broadcasted_iota(jnp.int32, (D_STATE, nseq * q), 1) // q
    for i in range(bb):
        r0 = i * q
        ht = st_ref[i].T
        decay = ea[r0 + q - 1:r0 + q, :]
        y_parts = []
        new_parts = []
        for g in range(SSM_GROUPS):
            sl = slice(g * GROUP_INNER, (g + 1) * GROUP_INNER)
            cg = c_ref[r0:r0 + q, g * D_STATE:(g + 1) * D_STATE].astype(BF16)
            hg = ht[:, sl]
            y_parts.append(jnp.dot(cg, hg.astype(BF16), preferred_element_type=F32))
            btg = bt_ref[g * D_STATE:(g + 1) * D_STATE, :]
            btg = jnp.where(lane_seq == first_seq + i, btg, jnp.zeros_like(btg))
            new_parts.append(hg * decay[:, sl] + jnp.dot(btg, xdd[:, sl], preferred_element_type=F32))
        o_ref[r0:r0 + q, :] = y_ref[r0:r0 + q, :] + ea[r0:r0 + q, :] * jnp.concatenate(y_parts, axis=-1)
        nst_ref[i] = jnp.concatenate(new_parts, axis=-1).T


def _ssm_sample_finish_kernel(x_ref, y_ref, z_ref, nw_ref, wout_ref, g_ref, b_ref, o_ref):
    x = x_ref[...]
    out = jnp.zeros(x.shape, F32)
    for g in range(SSM_GROUPS):
        sl = slice(g * GROUP_INNER, (g + 1) * GROUP_INNER)
        y = _gate_norm(y_ref[:, sl], z_ref[:, sl], nw_ref[:, sl])
        out = out + jnp.dot(y.astype(BF16), wout_ref[sl, :], preferred_element_type=F32)
    o_ref[...] = _layer_norm_rows(DEEPNORM_ALPHA * x + out, g_ref[...], b_ref[...])


def _ssm_sample(x, conv_state, ssm_state, w, g, b, *, bb_state=8):
    bsz, q, d = x.shape
    t = bsz * q
    nseq = SSM_ROWS // q
    rows = SSM_ROWS
    assert q == SUBLANES and bsz % nseq == 0 and nseq % bb_state == 0
    tril, e = _ssm_consts(nseq, q)
    win_cols = D_INNER + CONV_DIM + LANES
    x2 = x.reshape(t, d)
    row_spec = lambda c: pl.BlockSpec((rows, c), lambda i: (i, 0))
    est = d * win_cols * 2 + LANES * D_INNER * 4 + 16 * rows * D_INNER * 4 + 3 * rows * win_cols * 4
    ydiag, z, exp_a, xdd, cmat, bt, new_conv = pl.pallas_call(
        functools.partial(_ssm_sample_front_kernel, nseq=nseq, q=q),
        grid=(t // rows,),
        in_specs=[row_spec(d), pl.BlockSpec((nseq, CONV_WIDTH - 1, CONV_DIM), lambda i: (i, 0, 0))]
        + _ssm_weight_specs(d) + [_resident((1, D_INNER)), _resident((LANES, D_INNER)), _resident((rows, rows))],
        out_specs=[row_spec(D_INNER), row_spec(D_INNER), row_spec(D_INNER), row_spec(D_INNER), row_spec(GBN),
                   pl.BlockSpec((GBN, rows), lambda i: (i, 0)),
                   pl.BlockSpec((nseq, CONV_WIDTH - 1, CONV_DIM), lambda i: (i, 0, 0))],
        out_shape=[
            jax.ShapeDtypeStruct((t, D_INNER), F32), jax.ShapeDtypeStruct((t, D_INNER), F32),
            jax.ShapeDtypeStruct((t, D_INNER), F32), jax.ShapeDtypeStruct((t, D_INNER), BF16),
            jax.ShapeDtypeStruct((t, GBN), F32),
            jax.ShapeDtypeStruct((t // rows * GBN, rows), BF16),
            jax.ShapeDtypeStruct((bsz, CONV_WIDTH - 1, CONV_DIM), F32),
        ],
        scratch_shapes=[pltpu.VMEM((nseq, CONV_PAD + q, CONV_DIM), F32), pltpu.VMEM((rows, D_INNER), F32)],
        compiler_params=pltpu.CompilerParams(
            dimension_semantics=("arbitrary",), vmem_limit_bytes=_vmem_limit(est)),
        name="ssm_sample_front",
    )(x2, conv_state, w["win"], w["wdt"], w["cw"], w["cb"], w["dtb"], w["alog"], w["dsk"], e, tril)

    srows = bb_state * q
    srow_spec = lambda c: pl.BlockSpec((srows, c), lambda i: (i, 0))
    st_spec = pl.BlockSpec((bb_state, D_INNER, D_STATE), lambda i: (i, 0, 0))
    est = 4 * bb_state * D_INNER * D_STATE * 4 + 8 * D_INNER * D_STATE * 4
    per_block = nseq // bb_state
    y, new_state = pl.pallas_call(
        functools.partial(_ssm_sample_state_kernel, bb=bb_state, q=q, nseq=nseq),
        grid=(bsz // bb_state,),
        in_specs=[st_spec, srow_spec(D_INNER), srow_spec(D_INNER), srow_spec(GBN),
                  pl.BlockSpec((rows, D_INNER), lambda i: (i // per_block, 0)),
                  pl.BlockSpec((GBN, rows), lambda i: (i // per_block, 0))],
        out_specs=[srow_spec(D_INNER), st_spec],
        out_shape=[jax.ShapeDtypeStruct((t, D_INNER), F32),
                   jax.ShapeDtypeStruct((bsz, D_INNER, D_STATE), F32)],
        compiler_params=pltpu.CompilerParams(
            dimension_semantics=("arbitrary",), vmem_limit_bytes=_vmem_limit(est)),
        name="ssm_sample_state",
    )(ssm_state, ydiag, exp_a, cmat, xdd, bt)

    est = D_INNER * d * 2 + 8 * rows * D_INNER * 4
    out = pl.pallas_call(
        _ssm_sample_finish_kernel,
        grid=(t // rows,),
        in_specs=[row_spec(d), row_spec(D_INNER), row_spec(D_INNER),
                  _resident((1, D_INNER)), _resident((D_INNER, d)), _resident((1, d)), _resident((1, d))],
        out_specs=row_spec(d),
        out_shape=jax.ShapeDtypeStruct((t, d), F32),
        compiler_params=pltpu.CompilerParams(
            dimension_semantics=("arbitrary",), vmem_limit_bytes=_vmem_limit(est)),
        name="ssm_sample_finish",
    )(x2, y, z, w["nw"], w["wout"], g, b)
    return out.reshape(bsz, q, d), new_conv, new_state


def _ssm_weights(w_in, conv_w, conv_b, dt_bias, a_log, d_skip, norm_w, w_out, win_bf16, wout_bf16):
    pad = LANES - SSM_HEADS
    wdt = jnp.pad(w_in[:, D_INNER + CONV_DIM:], ((0, 0), (0, pad))).astype(BF16)
    return dict(
        win=win_bf16, wdt=wdt, cw=conv_w, cb=conv_b.reshape(1, CONV_DIM),
        dtb=jnp.pad(dt_bias, (0, pad)).reshape(1, LANES),
        alog=jnp.pad(a_log, (0, pad)).reshape(1, LANES),
        dsk=jnp.repeat(d_skip, SSM_HEAD_DIM).reshape(1, D_INNER),
        nw=norm_w.reshape(1, D_INNER), wout=wout_bf16)


def kernel(x_prompt, x_sample, cache_k, cache_v, state_conv, state_ssm, state_pool, rel_bias, attn_w_qkv, attn_b_qkv, attn_w_o, attn_b_o, attn_sinks, ssm_w_in, ssm_conv_w, ssm_conv_b, ssm_dt_bias, ssm_a_log, ssm_d, ssm_norm_w, ssm_w_out, pool_w, pool_scale, ffn_w_gate, ffn_w_up, ffn_w_down, ln_g, ln_b):
    xp, xs = x_prompt, x_sample
    d = xp.shape[-1]
    ffn_w32 = (ffn_w_gate, ffn_w_up, ffn_w_down)
    mixer_w32 = {0: (attn_w_qkv, attn_w_o), 1: (ssm_w_in, ssm_w_out)}

    def layer_jobs(i):
        if i >= DEPTH:
            return []
        jobs = [(w, i) for w in ffn_w32]
        return jobs + [(w, i // N_MIXERS) for w in mixer_w32.get(i % N_MIXERS, ())]

    def split_cast(results):
        ffn = tuple(w[None] for w in results[:len(ffn_w32)])
        return ffn, tuple(results[len(ffn_w32):])

    mixer_w = tuple(w[0:1].astype(BF16) for w in mixer_w32[0])
    ffn_w = None
    cache_k_t = jnp.transpose(cache_k, (0, 1, 3, 4, 2))
    cache_v_t = jnp.transpose(cache_v, (0, 1, 3, 4, 2))
    new_cache_t = None
    nk_p, nv_p, nc_p, nh_p, npool_p = [], [], [], [], []
    nc_s, nh_s, npool_s = [], [], []
    for i in range(DEPTH):
        j = i // N_MIXERS
        kind = i % N_MIXERS
        g1 = ln_g[i, 0].reshape(1, d)
        b1 = ln_b[i, 0].reshape(1, d)
        if kind == 0:
            bqkv = attn_b_qkv[j].reshape(1, -1)
            bo = attn_b_o[j].reshape(1, d)
            wqkv, wo = (w if w.ndim == 3 else w[None] for w in mixer_w)
            jobs = [(w, i) for w in ffn_w32] if ffn_w is None else []
            xp, kp, vp, *cast = _attn_prompt(xp, wqkv, bqkv, wo, bo, g1, b1, rel_bias, attn_sinks[j], layer=0,
                                             cast_jobs=jobs)
            if jobs:
                ffn_w, _ = split_cast(cast)
            xs, *new_cache_t = _attn_sample(xs, cache_k_t, cache_v_t, wqkv, bqkv, wo, bo, g1, b1, rel_bias,
                                            attn_sinks[j], layer=0, cache_layer=j, prev=new_cache_t)
            xs = xs.reshape(x_sample.shape)
            kv_shape = (-1, WINDOW, N_KV_HEADS, HEAD_DIM)
            nk_p.append(kp.reshape(kv_shape)); nv_p.append(vp.reshape(kv_shape))
        elif kind == 1:
            w = _ssm_weights(ssm_w_in[j], ssm_conv_w[j], ssm_conv_b[j], ssm_dt_bias[j], ssm_a_log[j],
                             ssm_d[j], ssm_norm_w[j], ssm_w_out[j], *mixer_w)
            xp, cp, hp = _ssm_prompt(xp, w, g1, b1)
            xs, cs_, hs_ = _ssm_sample(xs, state_conv[j], state_ssm[j].reshape(-1, D_INNER, D_STATE), w, g1, b1)
            st_shape = (-1, SSM_HEADS, SSM_HEAD_DIM, D_STATE)
            nc_p.append(cp); nh_p.append(hp.reshape(st_shape))
            nc_s.append(cs_); nh_s.append(hs_.reshape(st_shape))
        else:
            pw = pool_w[j].astype(BF16)
            psc = pool_scale[j].reshape(1, d)
            xp, pp = _pool_prompt(xp, pw, psc, g1, b1)
            xs, ps_ = _pool_sample(xs, state_pool[j], pw, psc, g1, b1, start=PAST_LEN)
            npool_p.append(pp); npool_s.append(ps_)
        g2 = ln_g[i, 1].reshape(1, d)
        b2 = ln_b[i, 1].reshape(1, d)
        xp2, cast = _ffn_ln(xp.reshape(-1, d), *ffn_w, g2, b2, layer=0, cast_jobs=layer_jobs(i + 1))
        xs2, _ = _ffn_ln(xs.reshape(-1, d), *ffn_w, g2, b2, layer=0)
        xp, xs = xp2.reshape(xp.shape), xs2.reshape(xs.shape)
        if cast:
            ffn_w, mixer_w = split_cast(cast)
    nk_s, nv_s = (jnp.transpose(t, (0, 1, 4, 2, 3)) for t in new_cache_t)
    return (xp, xs,
            jnp.stack(nk_p), jnp.stack(nv_p), jnp.stack(nc_p), jnp.stack(nh_p), jnp.stack(npool_p),
            nk_s, nv_s, jnp.stack(nc_s), jnp.stack(nh_s), jnp.stack(npool_s))
```

```python
import functools
import math

import jax
import jax.numpy as jnp
import numpy as np
from jax import lax
from jax.experimental import pallas as pl
from jax.experimental.pallas import tpu as pltpu

DEPTH = 4
N_MIXERS = 3
PAST_LEN = 8192
DEEPNORM_ALPHA = (2 * DEPTH) ** 0.25
LN_EPS = 1e-5

V7X_VMEM_BYTES = 64 * 1024 * 1024
LANES = 128
SUBLANES = 8
MXU_DIM = 256

BF16 = jnp.bfloat16
F32 = jnp.float32


def _vmem_limit(estimate_bytes):
    return int(min(V7X_VMEM_BYTES - 8 * 1024 * 1024, max(32 * 1024 * 1024, estimate_bytes * 3 // 2)))


def _layer_norm_rows(v, g, b):
    mu = jnp.mean(v, axis=-1, keepdims=True)
    d = v - mu
    var = jnp.mean(d * d, axis=-1, keepdims=True)
    return d * lax.rsqrt(var + LN_EPS) * g + b


def _resident(shape):
    nd = len(shape)
    return pl.BlockSpec(shape, lambda *_: (0,) * nd, pipeline_mode=pl.Buffered(1))


def _layer_resident(shape, layer):
    nd = len(shape)
    return pl.BlockSpec((1,) + tuple(shape), lambda *_: (layer,) + (0,) * nd, pipeline_mode=pl.Buffered(1))


def _ffn_chunks(d_ff):
    step = 2 * MXU_DIM
    edges = list(range(0, d_ff, step)) + [d_ff]
    return tuple(zip(edges[:-1], edges[1:]))


def _ffn_kernel(x_ref, wg_ref, wu_ref, wd_ref, g_ref, b_ref, *rest, chunks, n_cast):
    casts, (o_ref, *cast_outs) = rest[:n_cast], rest[n_cast:]
    _run_cast_jobs(casts, cast_outs)
    x = x_ref[...]
    xb = x.astype(BF16)
    acc = jnp.zeros(x.shape, F32)
    for c0, c1 in chunks:
        gate = jnp.dot(xb, wg_ref[0, :, c0:c1], preferred_element_type=F32)
        up = jnp.dot(xb, wu_ref[0, :, c0:c1], preferred_element_type=F32)
        h = (gate * jax.nn.sigmoid(gate)) * up
        acc = acc + jnp.dot(h.astype(BF16), wd_ref[0, c0:c1, :], preferred_element_type=F32)
    o_ref[...] = _layer_norm_rows(DEEPNORM_ALPHA * x + acc, g_ref[...], b_ref[...])


def _cast_slab_specs(w32, layer, steps, step_of=lambda i: i):
    _, rows, cols = w32.shape
    if rows % (steps * 2 * SUBLANES) == 0:
        slab = rows // steps
    else:
        slab = next(s for s in range(LANES, rows + 1, LANES) if rows % s == 0 and rows // s <= steps)
    last = rows // slab - 1
    return (pl.BlockSpec((1, slab, cols), lambda *idx: (layer, jnp.minimum(step_of(*idx), last), 0)),
            pl.BlockSpec((slab, cols), lambda *idx: (jnp.minimum(step_of(*idx), last), 0)),
            jax.ShapeDtypeStruct((rows, cols), BF16))


def _cast_jobs_specs(cast_jobs, steps, step_of=lambda i: i):
    specs = [_cast_slab_specs(w32, layer, steps, step_of) for w32, layer in cast_jobs]
    return ([s[0] for s in specs], [s[1] for s in specs], [s[2] for s in specs], [w32 for w32, _ in cast_jobs])


def _run_cast_jobs(srcs, dsts):
    for src, dst in zip(srcs, dsts):
        dst[...] = src[0].astype(BF16)


def _ffn_ln(x, wg, wu, wd, g, b, *, layer, cast_jobs=(), tm=512):
    t, d = x.shape
    d_ff = wg.shape[-1]
    tm = min(tm, t)
    assert t % tm == 0
    steps = t // tm
    est = 3 * d * d_ff * 2 + 4 * tm * d * 4 + 4 * tm * d_ff * 4
    in_specs = [
        pl.BlockSpec((tm, d), lambda i: (i, 0)),
        _layer_resident((d, d_ff), layer),
        _layer_resident((d, d_ff), layer),
        _layer_resident((d_ff, d), layer),
        _resident((1, d)),
        _resident((1, d)),
    ]
    cast_in, cast_out, cast_shape, cast_args = _cast_jobs_specs(cast_jobs, steps)
    outs = pl.pallas_call(
        functools.partial(_ffn_kernel, chunks=_ffn_chunks(d_ff), n_cast=len(cast_jobs)),
        grid=(steps,),
        in_specs=in_specs + cast_in,
        out_specs=[pl.BlockSpec((tm, d), lambda i: (i, 0))] + cast_out,
        out_shape=[jax.ShapeDtypeStruct((t, d), F32)] + cast_shape,
        compiler_params=pltpu.CompilerParams(
            dimension_semantics=("arbitrary",), vmem_limit_bytes=_vmem_limit(est)),
        name="ffn_ln",
    )(x, wg, wu, wd, g, b, *cast_args)
    return outs[0], tuple(outs[1:])


HEAD_DIM = 64
N_HEADS = 16
N_KV_HEADS = 4
GQA_GROUP = N_HEADS // N_KV_HEADS
WINDOW = 128
REL_BUCKETS = 32
REL_MAX_DIST = 128
Q_DIM = N_HEADS * HEAD_DIM
KV_DIM = N_KV_HEADS * HEAD_DIM
NEG_INF = float("-inf")
CHAIN_BATCH = 2


def _t5_bucket_table(dist):
    n = np.maximum(dist, 0)
    max_exact = REL_BUCKETS // 2
    nf = np.maximum(n, 1).astype(np.float32)
    large = max_exact + (np.log(nf / np.float32(max_exact)) / np.float32(math.log(REL_MAX_DIST / max_exact))
                         * np.float32(REL_BUCKETS - max_exact)).astype(np.int32)
    large = np.minimum(large, REL_BUCKETS - 1)
    bucket = np.where(n < max_exact, n, large)
    valid = (dist >= 0) & (dist < WINDOW)
    return np.where(valid, bucket, -1).astype(np.int32)


def _bias_from_buckets(bucket, relb_ref, head):
    acc = jnp.full(bucket.shape, NEG_INF, F32)
    for bkt in range(REL_BUCKETS):
        acc = jnp.where(bucket == bkt, relb_ref[bkt, head], acc)
    return acc


def _attn_prompt_kernel(x_ref, wqkv_ref, bqkv_ref, wo_ref, bo_ref, g_ref, b_ref, bucket_ref,
                        relb_ref, sink_ref, *rest, tq, n_cast):
    casts = rest[:n_cast]
    o_ref, kout_ref, vout_ref = rest[n_cast:n_cast + 3]
    cast_outs = rest[n_cast + 3:2 * n_cast + 3]
    ka_lo, ka_hi, kb_lo, kb_hi, vt, ot, bias_scr, sink_scr = rest[2 * n_cast + 3:]
    _run_cast_jobs(casts, cast_outs)
    bi = pl.program_id(0)
    j = pl.program_id(1)
    nj = pl.num_programs(1)
    blk = WINDOW
    half_heads = GQA_GROUP // 2
    kbufs = (ka_lo, ka_hi, kb_lo, kb_hi)

    @pl.when((bi == 0) & (j == 0))
    def _build_tables():
        bucket = bucket_ref[...]
        lane = lax.broadcasted_iota(jnp.int32, (1, 2 * blk), 1)
        for kv in range(N_KV_HEADS):
            for half in range(2):
                ha = kv * GQA_GROUP + half
                hb = ha + half_heads
                pair = kv * 2 + half
                bias_scr[pair, :, 0:blk] = _bias_from_buckets(bucket, relb_ref, ha)
                bias_scr[pair, :, blk:2 * blk] = _bias_from_buckets(bucket, relb_ref, hb)
                sink_scr[pair] = jnp.where(lane < blk, sink_ref[ha], sink_ref[hb])

    @pl.when(j == 0)
    def _no_past():
        for buf in kbufs:
            buf[0:blk, :] = jnp.zeros((blk, KV_DIM), BF16)
        vt[:, 0:blk] = jnp.zeros((KV_DIM, blk), BF16)

    @pl.when(j > 0)
    def _carry():
        for buf in kbufs:
            buf[0:blk, :] = buf[tq:tq + blk, :]
        vt[:, 0:blk] = vt[:, tq:tq + blk]

    x = x_ref[0]
    qkv = jnp.dot(x.astype(BF16), wqkv_ref[0], preferred_element_type=F32) + bqkv_ref[...]
    q = (qkv[:, :Q_DIM] * (HEAD_DIM ** -0.5)).astype(BF16)
    k = qkv[:, Q_DIM:Q_DIM + KV_DIM]
    v = qkv[:, Q_DIM + KV_DIM:]
    k_sw = jnp.concatenate(
        [pltpu.roll(k[:, c * LANES:(c + 1) * LANES], HEAD_DIM, axis=1) for c in range(KV_DIM // LANES)], axis=1)
    lo = (lax.broadcasted_iota(jnp.int32, (1, KV_DIM), 1) % LANES) < HEAD_DIM
    ka_lo[blk:blk + tq, :] = jnp.where(lo, k, 0.0).astype(BF16)
    ka_hi[blk:blk + tq, :] = jnp.where(lo, 0.0, k).astype(BF16)
    kb_lo[blk:blk + tq, :] = jnp.where(lo, k_sw, 0.0).astype(BF16)
    kb_hi[blk:blk + tq, :] = jnp.where(lo, 0.0, k_sw).astype(BF16)
    vt[:, blk:blk + tq] = v.T.astype(BF16)

    @pl.when(j == nj - 1)
    def _emit_cache():
        kout_ref[0] = k[tq - WINDOW:, :]
        vout_ref[0] = v[tq - WINDOW:, :]

    ks = lax.broadcasted_iota(jnp.int32, (blk, 2 * blk), 0)
    qt = lax.broadcasted_iota(jnp.int32, (blk, 2 * blk), 1) % blk
    own = ks <= qt
    chains = [(i, kv, half) for i in range(tq // blk) for kv in range(N_KV_HEADS) for half in range(2)]

    def scores(i, kv, half):
        r0 = i * blk
        c0 = (kv // 2) * LANES
        in_lo = kv % 2 == 0
        if half == 0:
            kk = (ka_lo if in_lo else kb_lo)[r0:r0 + 2 * blk, c0:c0 + LANES]
        else:
            kk = (kb_hi if in_lo else ka_hi)[r0:r0 + 2 * blk, c0:c0 + LANES]
        qa = q[r0:r0 + blk, (2 * kv) * LANES:(2 * kv + 1) * LANES]
        qb = q[r0:r0 + blk, (2 * kv + 1) * LANES:(2 * kv + 2) * LANES]
        q_pair = jnp.concatenate([qa, qb], axis=0)
        return lax.dot_general(kk, q_pair, (((1,), (1,)), ((), ())), preferred_element_type=F32)

    def fold(i, kv, half, s2):
        s_prev = s2[0:blk, :]
        if i == 0:
            s_prev = jnp.where(j == 0, NEG_INF, s_prev)
        return jnp.where(own, s2[blk:2 * blk, :], s_prev) + bias_scr[kv * 2 + half]

    def softmax_batch(batch, s2s):
        sinks = [sink_scr[kv * 2 + half] for _, kv, half in batch]
        ss = [fold(*ch, s2) for ch, s2 in zip(batch, s2s)]
        ms = [jnp.maximum(jnp.max(s, axis=0, keepdims=True), sink) for s, sink in zip(ss, sinks)]
        ps = [jnp.exp(s - m) for s, m in zip(ss, ms)]
        invs = [1.0 / (jnp.sum(p, axis=0, keepdims=True) + jnp.exp(sink - m)) for p, m, sink in zip(ps, ms, sinks)]
        p2s = [jnp.concatenate([jnp.where(own, 0.0, p), jnp.where(own, p, 0.0)], axis=0).astype(BF16) for p in ps]
        return list(zip(p2s, invs))

    def weighted_values(i, kv, half, p2, inv):
        r0 = i * blk
        ha = kv * GQA_GROUP + half
        hb = ha + half_heads
        v_t = vt[kv * HEAD_DIM:(kv + 1) * HEAD_DIM, r0:r0 + 2 * blk]
        o_t = jnp.dot(v_t, p2, preferred_element_type=F32) * inv
        ot[ha * HEAD_DIM:(ha + 1) * HEAD_DIM, r0:r0 + blk] = o_t[:, 0:blk]
        ot[hb * HEAD_DIM:(hb + 1) * HEAD_DIM, r0:r0 + blk] = o_t[:, blk:2 * blk]

    batches = [chains[c:c + CHAIN_BATCH] for c in range(0, len(chains), CHAIN_BATCH)]
    s_next = [scores(*ch) for ch in batches[0]]
    prev = []
    for bi, batch in enumerate(batches):
        s_cur = s_next
        if bi + 1 < len(batches):
            s_next = [scores(*ch) for ch in batches[bi + 1]]
        probs = softmax_batch(batch, s_cur)
        for ch, pr in prev:
            weighted_values(*ch, *pr)
        prev = list(zip(batch, probs))
    for ch, pr in prev:
        weighted_values(*ch, *pr)

    o = ot[...].T.astype(BF16)
    y = jnp.dot(o, wo_ref[0], preferred_element_type=F32) + bo_ref[...]
    o_ref[0] = _layer_norm_rows(DEEPNORM_ALPHA * x + y, g_ref[...], b_ref[...])


def _attn_prompt(x, wqkv, bqkv, wo, bo, g, b, rel_bias, sinks, *, layer, cast_jobs=(), tq=1024):
    bsz, l, d = x.shape
    tq = min(tq, l)
    assert l % tq == 0 and tq % WINDOW == 0
    nj = l // tq
    cast_in, cast_out, cast_shape, cast_args = _cast_jobs_specs(cast_jobs, bsz * nj, lambda bi, j: bi * nj + j)
    qi = np.arange(WINDOW)[None, :]
    si = np.arange(WINDOW)[:, None]
    bucket = jnp.asarray(_t5_bucket_table(np.where(si <= qi, qi - si, qi + WINDOW - si)))
    qkv_dim = wqkv.shape[-1]
    smem = pl.BlockSpec(memory_space=pltpu.SMEM)
    est = (d * qkv_dim + Q_DIM * d) * 2 + 5 * tq * d * 4 + N_HEADS * WINDOW * 2 * WINDOW * 4 + 3 * tq * qkv_dim * 4
    return pl.pallas_call(
        functools.partial(_attn_prompt_kernel, tq=tq, n_cast=len(cast_jobs)),
        grid=(bsz, nj),
        in_specs=[
            pl.BlockSpec((1, tq, d), lambda bi, j: (bi, j, 0)),
            _layer_resident((d, qkv_dim), layer), _resident((1, qkv_dim)),
            _layer_resident((Q_DIM, d), layer), _resident((1, d)),
            _resident((1, d)), _resident((1, d)),
            _resident((WINDOW, WINDOW)),
            smem, smem,
        ] + cast_in,
        out_specs=[
            pl.BlockSpec((1, tq, d), lambda bi, j: (bi, j, 0)),
            pl.BlockSpec((1, WINDOW, KV_DIM), lambda bi, j: (bi, 0, 0)),
            pl.BlockSpec((1, WINDOW, KV_DIM), lambda bi, j: (bi, 0, 0)),
        ] + cast_out,
        out_shape=[
            jax.ShapeDtypeStruct((bsz, l, d), F32),
            jax.ShapeDtypeStruct((bsz, WINDOW, KV_DIM), F32),
            jax.ShapeDtypeStruct((bsz, WINDOW, KV_DIM), F32),
        ] + cast_shape,
        scratch_shapes=[pltpu.VMEM((WINDOW + tq, KV_DIM), BF16)] * 4 + [
            pltpu.VMEM((KV_DIM, WINDOW + tq), BF16),
            pltpu.VMEM((Q_DIM, tq), F32),
            pltpu.VMEM((N_HEADS // 2, WINDOW, 2 * WINDOW), F32),
            pltpu.VMEM((N_HEADS // 2, 1, 2 * WINDOW), F32),
        ],
        compiler_params=pltpu.CompilerParams(
            dimension_semantics=("arbitrary", "arbitrary"), vmem_limit_bytes=_vmem_limit(est)),
        name="attn_prompt",
    )(x, wqkv, bqkv, wo, bo, g, b, bucket, rel_bias, sinks, *cast_args)


SAMPLE_KEYS = 2 * WINDOW
SAMPLE_CHAIN_BATCH = 8


def _attn_sample_kernel(x_ref, ck_ref, cv_ref, wqkv_ref, bqkv_ref, wo_ref, bo_ref, g_ref, b_ref, bucket_ref,
                        relb_ref, sink_ref, *rest, bb, l, n_prev, out_layer):
    o_ref, nk_ref, nv_ref, qbuf, obuf, bias_scr = rest[n_prev:]
    for other in range(nk_ref.shape[0]):
        if other != out_layer:
            nk_ref[other] = jnp.zeros(nk_ref.shape[1:], F32)
            nv_ref[other] = jnp.zeros(nv_ref.shape[1:], F32)

    @pl.when(pl.program_id(0) == 0)
    def _init():
        bucket = bucket_ref[...]
        for h in range(N_HEADS):
            kv, g_ = divmod(h, GQA_GROUP)
            bias_scr[kv, g_ * l:(g_ + 1) * l, :] = _bias_from_buckets(bucket, relb_ref, h)

    x = x_ref[...]
    qkv = jnp.dot(x.astype(BF16), wqkv_ref[0], preferred_element_type=F32) + bqkv_ref[...]
    qbuf[...] = qkv[:, :Q_DIM] * (HEAD_DIM ** -0.5)
    k_new = qkv[:, Q_DIM:Q_DIM + KV_DIM]
    v_new = qkv[:, Q_DIM + KV_DIM:]

    sink_cols = []
    for kv in range(N_KV_HEADS):
        sink_cols.append(jnp.concatenate(
            [jnp.full((l, 1), sink_ref[kv * GQA_GROUP + g_], F32) for g_ in range(GQA_GROUP)], axis=0))

    kn_t = k_new.T
    vn_t = v_new.T
    lane = lax.broadcasted_iota(jnp.int32, (HEAD_DIM, WINDOW), 1)
    pad = jnp.zeros((WINDOW - l, KV_DIM), F32)

    def shifted(old_t, new_t, i):
        kept = pltpu.roll(old_t, WINDOW - l, axis=1)
        new = pltpu.roll(new_t, (WINDOW - l - i * l) % WINDOW, axis=1)
        return jnp.where(lane >= WINDOW - l, new, kept)

    def keys_values(i):
        r0 = i * l
        for kv in range(N_KV_HEADS):
            hs = slice(kv * HEAD_DIM, (kv + 1) * HEAD_DIM)
            nk_ref[out_layer, i, kv] = shifted(ck_ref[0, i, kv], kn_t[hs, :], i)
            nv_ref[out_layer, i, kv] = shifted(cv_ref[0, i, kv], vn_t[hs, :], i)
        kn = jnp.concatenate([k_new[r0:r0 + l, :], pad], axis=0).astype(BF16)
        vn = jnp.concatenate([v_new[r0:r0 + l, :], pad], axis=0).astype(BF16)
        return kn, vn

    def scores(i, kv, kn):
        r0 = i * l
        c0 = kv * HEAD_DIM
        qs = jnp.concatenate(
            [qbuf[r0:r0 + l, (kv * GQA_GROUP + g_) * HEAD_DIM:(kv * GQA_GROUP + g_ + 1) * HEAD_DIM]
             for g_ in range(GQA_GROUP)], axis=0).astype(BF16)
        s_cache = jnp.dot(qs, ck_ref[0, i, kv].astype(BF16), preferred_element_type=F32)
        s_new = lax.dot_general(qs, kn[:, c0:c0 + HEAD_DIM], (((1,), (1,)), ((), ())), preferred_element_type=F32)
        return jnp.concatenate([s_cache, s_new], axis=1)

    def weighted_values(i, kv, p, vn):
        c0 = kv * HEAD_DIM
        pb = p.astype(BF16)
        o_cache = lax.dot_general(pb[:, :WINDOW], cv_ref[0, i, kv].astype(BF16), (((1,), (1,)), ((), ())),
                                  preferred_element_type=F32)
        return o_cache + jnp.dot(pb[:, WINDOW:], vn[:, c0:c0 + HEAD_DIM], preferred_element_type=F32)

    for i0 in range(0, bb, SAMPLE_CHAIN_BATCH):
        items = range(i0, min(bb, i0 + SAMPLE_CHAIN_BATCH))
        kvs = {i: keys_values(i) for i in items}
        chains = [(i, kv) for i in items for kv in range(N_KV_HEADS)]
        s_all = [scores(i, kv, kvs[i][0]) + bias_scr[kv] for i, kv in chains]
        m_all = [jnp.maximum(jnp.max(s, axis=-1, keepdims=True), sink_cols[kv]) for s, (i, kv) in zip(s_all, chains)]
        p_all = [jnp.exp(s - m) for s, m in zip(s_all, m_all)]
        den_all = [jnp.sum(p, axis=-1, keepdims=True) + jnp.exp(sink_cols[kv] - m)
                   for p, m, (i, kv) in zip(p_all, m_all, chains)]
        o_all = [weighted_values(i, kv, p, kvs[i][1]) for p, (i, kv) in zip(p_all, chains)]
        for o, den, (i, kv) in zip(o_all, den_all, chains):
            o = o / den
            for g_ in range(GQA_GROUP):
                h = kv * GQA_GROUP + g_
                obuf[i * l:(i + 1) * l, h * HEAD_DIM:(h + 1) * HEAD_DIM] = o[g_ * l:(g_ + 1) * l, :]

    y = jnp.dot(obuf[...].astype(BF16), wo_ref[0], preferred_element_type=F32) + bo_ref[...]
    o_ref[...] = _layer_norm_rows(DEEPNORM_ALPHA * x + y, g_ref[...], b_ref[...])


def _attn_sample(x, cache_k_t, cache_v_t, wqkv, bqkv, wo, bo, g, b, rel_bias, sinks, *, layer, cache_layer, prev=None,
                 bb=16):
    bsz, l, d = x.shape
    assert bsz % bb == 0 and l == SUBLANES and bb * l == WINDOW
    ti = np.arange(l)[:, None]
    ci = np.arange(SAMPLE_KEYS)[None, :]
    table = _t5_bucket_table(ti + WINDOW - ci)
    table = np.where(ci < WINDOW + l, table, -1).astype(np.int32)
    bucket = jnp.asarray(table)
    qkv_dim = wqkv.shape[-1]
    rows = bb * l
    smem = pl.BlockSpec(memory_space=pltpu.SMEM)
    cache_spec = pl.BlockSpec((1, bb, N_KV_HEADS, HEAD_DIM, WINDOW), lambda i: (cache_layer, i, 0, 0, 0))
    est = (d * qkv_dim + Q_DIM * d) * 2 + 8 * bb * WINDOW * KV_DIM * 4 + 8 * rows * d * 4
    in_specs = [
        pl.BlockSpec((rows, d), lambda i: (i, 0)),
        cache_spec, cache_spec,
        _layer_resident((d, qkv_dim), layer), _resident((1, qkv_dim)),
        _layer_resident((Q_DIM, d), layer), _resident((1, d)),
        _resident((1, d)), _resident((1, d)),
        _resident((l, SAMPLE_KEYS)),
        smem, smem,
    ]
    args = [x.reshape(bsz * l, d), cache_k_t, cache_v_t, wqkv, bqkv, wo, bo, g, b, bucket, rel_bias, sinks]
    n_layers = cache_k_t.shape[0]
    if prev is None:
        aliases, out_layer = {}, cache_layer
        out_cache_spec = pl.BlockSpec((n_layers, bb, N_KV_HEADS, HEAD_DIM, WINDOW), lambda i: (0, i, 0, 0, 0))
    else:
        aliases, out_layer = {len(args): 1, len(args) + 1: 2}, 0
        out_cache_spec = cache_spec
        in_specs += [pl.BlockSpec(memory_space=pl.ANY)] * 2
        args += list(prev)
    return pl.pallas_call(
        functools.partial(_attn_sample_kernel, bb=bb, l=l, n_prev=len(aliases), out_layer=out_layer),
        grid=(bsz // bb,),
        in_specs=in_specs,
        out_specs=[pl.BlockSpec((rows, d), lambda i: (i, 0)), out_cache_spec, out_cache_spec],
        out_shape=[
            jax.ShapeDtypeStruct((bsz * l, d), F32),
            jax.ShapeDtypeStruct(cache_k_t.shape, F32),
            jax.ShapeDtypeStruct(cache_v_t.shape, F32),
        ],
        scratch_shapes=[
            pltpu.VMEM((rows, Q_DIM), F32),
            pltpu.VMEM((rows, Q_DIM), F32),
            pltpu.VMEM((N_KV_HEADS, GQA_GROUP * l, SAMPLE_KEYS), F32),
        ],
        input_output_aliases=aliases,
        compiler_params=pltpu.CompilerParams(
            dimension_semantics=("arbitrary",), vmem_limit_bytes=_vmem_limit(est)),
        name="attn_sample",
    )(*args)


POOL_WINDOWS = (2, 4, 8, 16)
POOL_PAD = max(POOL_WINDOWS)
POOL_STATE_LEN = POOL_PAD - 1
POOL_HIST = len(POOL_WINDOWS) * SUBLANES


def _pool_mix(window_sum, x, cnt_of, pw_ref, scale):
    gd = x.shape[-1] // len(POOL_WINDOWS)
    outs = []
    for g, w in enumerate(POOL_WINDOWS):
        diff = window_sum(g, w) / cnt_of(w) - x[:, g * gd:(g + 1) * gd]
        outs.append(jnp.dot(diff.astype(BF16), pw_ref[g], preferred_element_type=F32))
    return jnp.concatenate(outs, axis=-1) * scale


def _pool_prompt_kernel(x_ref, pw_ref, sc_ref, g_ref, b_ref, o_ref, st_ref, buf, lvl, *, tm):
    j = pl.program_id(1)
    nj = pl.num_programs(1)
    d = x_ref.shape[-1]
    gd = d // len(POOL_WINDOWS)
    hist, rows = POOL_HIST, POOL_HIST + tm

    @pl.when(j == 0)
    def _no_past():
        buf[0:hist, :] = jnp.zeros((hist, d), F32)

    @pl.when(j > 0)
    def _carry():
        buf[0:hist, :] = buf[tm:tm + hist, :]

    x = x_ref[0]
    buf[hist:rows, :] = x

    sums = {}
    src = buf
    for k, w in enumerate(POOL_WINDOWS):
        step, c0, r0 = w // 2, k * gd, (k + 1) * SUBLANES
        cur = src[r0:rows, c0:d] + src[r0 - step:rows - step, c0:d]
        sums[k] = cur[hist - r0:, 0:gd]
        if k + 1 < len(POOL_WINDOWS):
            lvl[k, r0:rows, c0:d] = cur
            src = lvl.at[k]

    def window_sum(g, w):
        return sums[g]

    pos1 = j * tm + lax.broadcasted_iota(jnp.int32, (tm, 1), 0) + 1

    def cnt_of(w):
        return jnp.minimum(pos1, w).astype(F32)

    y = _pool_mix(window_sum, x, cnt_of, pw_ref, sc_ref[...])
    o_ref[0] = _layer_norm_rows(DEEPNORM_ALPHA * x + y, g_ref[...], b_ref[...])

    @pl.when(j == nj - 1)
    def _emit_state():
        st_ref[0] = buf[rows - POOL_STATE_LEN:rows, :]


def _pool_prompt(x, pw, scale, g, b, *, tm=512):
    bsz, l, d = x.shape
    tm = min(tm, l)
    assert l % tm == 0 and tm >= POOL_HIST >= POOL_STATE_LEN and POOL_WINDOWS == (2, 4, 8, 16)
    ng, gd = pw.shape[0], pw.shape[1]
    est = 10 * tm * d * 4 + ng * gd * gd * 2
    return pl.pallas_call(
        functools.partial(_pool_prompt_kernel, tm=tm),
        grid=(bsz, l // tm),
        in_specs=[
            pl.BlockSpec((1, tm, d), lambda bi, j: (bi, j, 0)),
            _resident((ng, gd, gd)), _resident((1, d)), _resident((1, d)), _resident((1, d)),
        ],
        out_specs=[
            pl.BlockSpec((1, tm, d), lambda bi, j: (bi, j, 0)),
            pl.BlockSpec((1, POOL_STATE_LEN, d), lambda bi, j: (bi, 0, 0)),
        ],
        out_shape=[
            jax.ShapeDtypeStruct((bsz, l, d), F32),
            jax.ShapeDtypeStruct((bsz, POOL_STATE_LEN, d), F32),
        ],
        scratch_shapes=[pltpu.VMEM((POOL_HIST + tm, d), F32),
                        pltpu.VMEM((len(POOL_WINDOWS) - 1, POOL_HIST + tm, d), F32)],
        compiler_params=pltpu.CompilerParams(
            dimension_semantics=("arbitrary", "arbitrary"), vmem_limit_bytes=_vmem_limit(est)),
        name="pool_prompt",
    )(x, pw, scale, g, b)


def _pool_sample_kernel(x_ref, st_ref, pw_ref, sc_ref, g_ref, b_ref, o_ref, nst_ref, buf, *, bb, l):
    d = x_ref.shape[-1]
    gd = d // len(POOL_WINDOWS)
    x3 = x_ref[...]
    buf[:, 1:POOL_PAD, :] = st_ref[...]
    buf[:, POOL_PAD:POOL_PAD + l, :] = x3
    x = x3.reshape(bb * l, d)

    def window_sum(g, w):
        acc = x3[:, :, g * gd:(g + 1) * gd]
        for s in range(1, w):
            acc = acc + buf[:, POOL_PAD - s:POOL_PAD - s + l, g * gd:(g + 1) * gd]
        return acc.reshape(bb * l, gd)

    y = _pool_mix(window_sum, x, lambda w: float(w), pw_ref, sc_ref[...])
    o_ref[...] = _layer_norm_rows(DEEPNORM_ALPHA * x + y, g_ref[...], b_ref[...]).reshape(bb, l, d)
    nst_ref[...] = buf[:, l + 1:l + POOL_PAD, :]


def _pool_sample(x, state, pw, scale, g, b, *, start, bb=16):
    bsz, l, d = x.shape
    assert bsz % bb == 0 and l == SUBLANES and start + 1 >= POOL_PAD
    ng, gd = pw.shape[0], pw.shape[1]
    est = 8 * bb * (POOL_PAD + l) * d * 4 + ng * gd * gd * 2
    return pl.pallas_call(
        functools.partial(_pool_sample_kernel, bb=bb, l=l),
        grid=(bsz // bb,),
        in_specs=[
            pl.BlockSpec((bb, l, d), lambda i: (i, 0, 0)),
            pl.BlockSpec((bb, POOL_STATE_LEN, d), lambda i: (i, 0, 0)),
            _resident((ng, gd, gd)), _resident((1, d)), _resident((1, d)), _resident((1, d)),
        ],
        out_specs=[
            pl.BlockSpec((bb, l, d), lambda i: (i, 0, 0)),
            pl.BlockSpec((bb, POOL_STATE_LEN, d), lambda i: (i, 0, 0)),
        ],
        out_shape=[
            jax.ShapeDtypeStruct((bsz, l, d), F32),
            jax.ShapeDtypeStruct((bsz, POOL_STATE_LEN, d), F32),
        ],
        scratch_shapes=[pltpu.VMEM((bb, POOL_PAD + l, d), F32)],
        compiler_params=pltpu.CompilerParams(
            dimension_semantics=("arbitrary",), vmem_limit_bytes=_vmem_limit(est)),
        name="pool_sample",
    )(x, state, pw, scale, g, b)


D_INNER = 2048
SSM_HEAD_DIM = 64
SSM_HEADS = D_INNER // SSM_HEAD_DIM
SSM_GROUPS = 4
SSM_HPG = SSM_HEADS // SSM_GROUPS
D_STATE = 128
CONV_WIDTH = 4
GBN = SSM_GROUPS * D_STATE
CONV_DIM = D_INNER + 2 * GBN
GROUP_INNER = D_INNER // SSM_GROUPS
RMS_EPS = 1e-5
CONV_PAD = SUBLANES
SSM_ROWS = 128


def _split_bf16(v, parts=3):
    out = []
    r = v
    for _ in range(parts):
        p = r.astype(BF16)
        out.append(p)
        r = r - p.astype(F32)
    return out


def _expand_heads(v, e):
    return sum(jnp.dot(p, e, preferred_element_type=F32) for p in _split_bf16(v, parts=2))


def _ssm_in_proj(xb, win_ref, wdt_ref, c0, c1):
    main = win_ref.shape[1]
    if c0 >= main:
        return jnp.dot(xb, wdt_ref[:, c0 - main:c1 - main], preferred_element_type=F32)
    assert c1 <= main
    return jnp.dot(xb, win_ref[:, c0:c1], preferred_element_type=F32)


def _ssm_prep(proj, xp_buf, conv_state, dtb_ref, alog_ref, tril_ref, *, nseq, q):
    rows = nseq * q
    xbc_pre = proj(D_INNER, D_INNER + CONV_DIM)
    dt_pre = proj(D_INNER + CONV_DIM, D_INNER + CONV_DIM + LANES)
    if conv_state is not None:
        xp_buf[:, CONV_PAD - (CONV_WIDTH - 1):CONV_PAD, :] = conv_state
    xp_buf[:, CONV_PAD:CONV_PAD + q, :] = xbc_pre.reshape(nseq, q, CONV_DIM)
    new_conv = xp_buf[:, q + CONV_PAD - (CONV_WIDTH - 1):q + CONV_PAD, :]

    dtv = dt_pre + dtb_ref[...]
    dt = jnp.maximum(dtv, 0.0) + jnp.log1p(jnp.exp(-jnp.abs(dtv)))
    a = -jnp.exp(alog_ref[...])
    tril = tril_ref[...]
    acum = sum(jnp.dot(tril, p, preferred_element_type=F32) for p in _split_bf16(dt * a))
    a3 = acum.reshape(nseq, q, LANES)
    alast = jnp.broadcast_to(a3[:, q - 1:q, :], (nseq, q, LANES)).reshape(rows, LANES)
    return dict(acum=acum, acum_t=acum.T, dt_t=dt.T, causal=tril > 0.5,
                exp_a=jnp.exp(acum), dd=jnp.exp(alast - acum) * dt, new_conv=new_conv)


def _conv_silu(xp_buf, cw_ref, cb_ref, c0, c1, *, nseq, q):
    acc = cb_ref[:, c0:c1].reshape(1, 1, c1 - c0)
    for jj in range(CONV_WIDTH):
        off = CONV_PAD - (CONV_WIDTH - 1) + jj
        acc = acc + xp_buf[:, off:off + q, c0:c1] * cw_ref[jj:jj + 1, c0:c1].reshape(1, 1, c1 - c0)
    acc = acc.reshape(nseq * q, c1 - c0)
    return acc * jax.nn.sigmoid(acc)


def _ssm_group_diag(g, f, proj, prep, xp_buf, cw_ref, cb_ref, ybuf, *, nseq, q):
    f["z"] = proj(g * GROUP_INNER, (g + 1) * GROUP_INNER)
    conv = functools.partial(_conv_silu, xp_buf, cw_ref, cb_ref, nseq=nseq, q=q)
    half = GROUP_INNER // 2
    xs_lo = conv(g * GROUP_INNER, g * GROUP_INNER + half)
    yield
    xs_hi = conv(g * GROUP_INNER + half, (g + 1) * GROUP_INNER)
    yield
    xs = f["xs"] = jnp.concatenate([xs_lo, xs_hi], axis=-1)
    bm = f["bm"] = conv(D_INNER + g * D_STATE, D_INNER + (g + 1) * D_STATE)
    cm = f["cm"] = conv(D_INNER + GBN + g * D_STATE, D_INNER + GBN + (g + 1) * D_STATE)
    yield
    cb = lax.dot_general(cm.astype(BF16), bm.astype(BF16), (((1,), (1,)), ((), ())), preferred_element_type=F32)
    acum, acum_t, dt_t, causal = prep["acum"], prep["acum_t"], prep["dt_t"], prep["causal"]
    for r in range(SSM_HPG):
        h = g * SSM_HPG + r
        seg = acum[:, h:h + 1] - acum_t[h:h + 1, :]
        lm = jnp.exp(jnp.where(causal, seg, NEG_INF))
        w = (cb * lm * dt_t[h:h + 1, :]).astype(BF16)
        xh = xs[:, r * SSM_HEAD_DIM:(r + 1) * SSM_HEAD_DIM].astype(BF16)
        ybuf[:, h * SSM_HEAD_DIM:(h + 1) * SSM_HEAD_DIM] = jnp.dot(w, xh, preferred_element_type=F32)
        yield


def _ssm_group_expand(g, f, prep, e_ref):
    e = e_ref[:, g * GROUP_INNER:(g + 1) * GROUP_INNER]
    f["exp_a"] = _expand_heads(prep["exp_a"], e)
    yield
    f["xdd"] = (f["xs"] * _expand_heads(prep["dd"], e)).astype(BF16)
    yield


def _interleave(*gens):
    live = list(gens)
    while live:
        for gen in list(live):
            try:
                next(gen)
            except StopIteration:
                live.remove(gen)


def _gate_norm(y, z, nw):
    y = y * (z * jax.nn.sigmoid(z))
    return y * lax.rsqrt(jnp.mean(y * y, axis=-1, keepdims=True) + RMS_EPS) * nw


def _ssm_prompt_kernel(x_ref, win_ref, wdt_ref, cw_ref, cb_ref, dtb_ref, alog_ref, dsk_ref, nw_ref, wout_ref, e_ref,
                       tril_ref, g_ref, b_ref, o_ref, cout_ref, sout_ref, xp_buf, ht, ybuf):
    j = pl.program_id(1)
    nj = pl.num_programs(1)
    q = SSM_ROWS

    @pl.when(j == 0)
    def _no_past():
        xp_buf[:, 0:CONV_PAD, :] = jnp.zeros((1, CONV_PAD, CONV_DIM), F32)
        ht[...] = jnp.zeros(ht.shape, F32)

    @pl.when(j > 0)
    def _carry():
        xp_buf[:, 0:CONV_PAD, :] = xp_buf[:, q:q + CONV_PAD, :]

    x = x_ref[0]
    xb = x.astype(BF16)
    zx = jnp.dot(xb, win_ref[...], preferred_element_type=F32)
    dt_pre = jnp.dot(xb, wdt_ref[...], preferred_element_type=F32)
    z = zx[:, :D_INNER]

    def proj(c0, c1):
        return dt_pre if c0 >= D_INNER + CONV_DIM else zx[:, c0:c1]

    prep = _ssm_prep(proj, xp_buf, None, dtb_ref, alog_ref, tril_ref, nseq=1, q=q)
    xbc = _conv_silu(xp_buf, cw_ref, cb_ref, 0, CONV_DIM, nseq=1, q=q)
    xs = xbc[:, :D_INNER]
    acum, acum_t, dt_t, causal = prep["acum"], prep["acum_t"], prep["dt_t"], prep["causal"]
    for g in range(SSM_GROUPS):
        bg = xbc[:, D_INNER + g * D_STATE:D_INNER + (g + 1) * D_STATE].astype(BF16)
        cg = xbc[:, D_INNER + GBN + g * D_STATE:D_INNER + GBN + (g + 1) * D_STATE].astype(BF16)
        cb = lax.dot_general(cg, bg, (((1,), (1,)), ((), ())), preferred_element_type=F32)
        for r in range(SSM_HPG):
            h = g * SSM_HPG + r
            seg = acum[:, h:h + 1] - acum_t[h:h + 1, :]
            lm = jnp.exp(jnp.where(causal, seg, NEG_INF))
            w = (cb * lm * dt_t[h:h + 1, :]).astype(BF16)
            xh = xs[:, h * SSM_HEAD_DIM:(h + 1) * SSM_HEAD_DIM].astype(BF16)
            ybuf[:, h * SSM_HEAD_DIM:(h + 1) * SSM_HEAD_DIM] = jnp.dot(w, xh, preferred_element_type=F32)

    e = e_ref[...]
    exp_a = _expand_heads(prep["exp_a"], e)
    xdd = (xs * _expand_heads(prep["dd"], e)).astype(BF16)
    decay = exp_a[q - 1:q, :]
    y_off = []
    for g in range(SSM_GROUPS):
        sl = slice(g * GROUP_INNER, (g + 1) * GROUP_INNER)
        bg = xbc[:, D_INNER + g * D_STATE:D_INNER + (g + 1) * D_STATE]
        cg = xbc[:, D_INNER + GBN + g * D_STATE:D_INNER + GBN + (g + 1) * D_STATE].astype(BF16)
        hg = ht[:, sl]
        y_off.append(jnp.dot(cg, hg.astype(BF16), preferred_element_type=F32))
        ht[:, sl] = hg * decay[:, sl] + jnp.dot(bg.T.astype(BF16), xdd[:, sl], preferred_element_type=F32)
    y = ybuf[...] + exp_a * jnp.concatenate(y_off, axis=-1) + dsk_ref[...] * xs
    y = jnp.concatenate(
        [_gate_norm(y[:, g * GROUP_INNER:(g + 1) * GROUP_INNER], z[:, g * GROUP_INNER:(g + 1) * GROUP_INNER],
                    nw_ref[:, g * GROUP_INNER:(g + 1) * GROUP_INNER]) for g in range(SSM_GROUPS)], axis=-1)
    out = jnp.dot(y.astype(BF16), wout_ref[...], preferred_element_type=F32)
    o_ref[0] = _layer_norm_rows(DEEPNORM_ALPHA * x + out, g_ref[...], b_ref[...])

    @pl.when(j == nj - 1)
    def _emit_state():
        cout_ref[...] = prep["new_conv"]
        sout_ref[0] = ht[...].T


def _ssm_consts(nseq, q):
    rows = nseq * q
    r = np.arange(rows)
    tril = ((r[:, None] >= r[None, :]) & (r[:, None] // q == r[None, :] // q)).astype(np.float32)
    e = np.zeros((LANES, D_INNER), np.float32)
    e[np.arange(D_INNER) // SSM_HEAD_DIM, np.arange(D_INNER)] = 1.0
    return jnp.asarray(tril, BF16), jnp.asarray(e, BF16)


def _ssm_weight_specs(d):
    return [
        pl.BlockSpec((d, D_INNER + CONV_DIM), lambda *_: (0, 0), pipeline_mode=pl.Buffered(1)),
        _resident((d, LANES)), _resident((CONV_WIDTH, CONV_DIM)), _resident((1, CONV_DIM)),
        _resident((1, LANES)), _resident((1, LANES)),
    ]


def _ssm_prompt(x, w, g, b):
    bsz, l, d = x.shape
    q = SSM_ROWS
    assert l % q == 0
    tril, e = _ssm_consts(1, q)
    win_cols = D_INNER + CONV_DIM + LANES
    est = (d * win_cols + D_INNER * d) * 2 + LANES * D_INNER * 4 + 12 * q * D_INNER * 4 + 3 * q * win_cols * 4
    return pl.pallas_call(
        _ssm_prompt_kernel,
        grid=(bsz, l // q),
        in_specs=[pl.BlockSpec((1, q, d), lambda bi, j: (bi, j, 0))] + _ssm_weight_specs(d) + [
            _resident((1, D_INNER)), _resident((1, D_INNER)), _resident((D_INNER, d)),
            _resident((LANES, D_INNER)), _resident((q, q)), _resident((1, d)), _resident((1, d)),
        ],
        out_specs=[
            pl.BlockSpec((1, q, d), lambda bi, j: (bi, j, 0)),
            pl.BlockSpec((1, CONV_WIDTH - 1, CONV_DIM), lambda bi, j: (bi, 0, 0)),
            pl.BlockSpec((1, D_INNER, D_STATE), lambda bi, j: (bi, 0, 0)),
        ],
        out_shape=[
            jax.ShapeDtypeStruct((bsz, l, d), F32),
            jax.ShapeDtypeStruct((bsz, CONV_WIDTH - 1, CONV_DIM), F32),
            jax.ShapeDtypeStruct((bsz, D_INNER, D_STATE), F32),
        ],
        scratch_shapes=[
            pltpu.VMEM((1, CONV_PAD + q, CONV_DIM), F32),
            pltpu.VMEM((D_STATE, D_INNER), F32),
            pltpu.VMEM((q, D_INNER), F32),
        ],
        compiler_params=pltpu.CompilerParams(
            dimension_semantics=("arbitrary", "arbitrary"), vmem_limit_bytes=_vmem_limit(est)),
        name="ssm_prompt",
    )(x, w["win"], w["wdt"], w["cw"], w["cb"], w["dtb"], w["alog"], w["dsk"], w["nw"], w["wout"], e, tril, g, b)


def _ssm_sample_front_kernel(x_ref, cst_ref, win_ref, wdt_ref, cw_ref, cb_ref, dtb_ref, alog_ref, dsk_ref, e_ref, tril_ref,
                             y_ref, z_ref, ea_ref, xdd_ref, c_ref, bt_ref, cout_ref, xp_buf, ybuf, *, nseq, q):
    xb = x_ref[...].astype(BF16)

    def proj(c0, c1):
        return _ssm_in_proj(xb, win_ref, wdt_ref, c0, c1)

    prep = _ssm_prep(proj, xp_buf, cst_ref[...], dtb_ref, alog_ref, tril_ref, nseq=nseq, q=q)
    cout_ref[...] = prep["new_conv"]
    for g in range(SSM_GROUPS):
        sl = slice(g * GROUP_INNER, (g + 1) * GROUP_INNER)
        f = {}
        _interleave(_ssm_group_diag(g, f, proj, prep, xp_buf, cw_ref, cb_ref, ybuf, nseq=nseq, q=q))
        _interleave(_ssm_group_expand(g, f, prep, e_ref))
        y_ref[:, sl] = ybuf[:, sl] + dsk_ref[:, sl] * f["xs"]
        z_ref[:, sl] = f["z"]
        ea_ref[:, sl] = f["exp_a"]
        xdd_ref[:, sl] = f["xdd"]
        c_ref[:, g * D_STATE:(g + 1) * D_STATE] = f["cm"]
        bt_ref[g * D_STATE:(g + 1) * D_STATE, :] = f["bm"].T.astype(BF16)


def _ssm_sample_state_kernel(st_ref, y_ref, ea_ref, c_ref, xdd_ref, bt_ref, o_ref, nst_ref, *, bb, q, nseq):
    ea = ea_ref[...]
    xdd = xdd_ref[...]
    first_seq = (pl.program_id(0) % (nseq // bb)) * bb
    lane_seq = lax.broadcasted_iota(jnp.int32, (D_STATE, nseq * q), 1) // q
    for i in range(bb):
        r0 = i * q
        ht = st_ref[i].T
        decay = ea[r0 + q - 1:r0 + q, :]
        y_parts = []
        new_parts = []
        for g in range(SSM_GROUPS):
            sl = slice(g * GROUP_INNER, (g + 1) * GROUP_INNER)
            cg = c_ref[r0:r0 + q, g * D_STATE:(g + 1) * D_STATE].astype(BF16)
            hg = ht[:, sl]
            y_parts.append(jnp.dot(cg, hg.astype(BF16), preferred_element_type=F32))
            btg = bt_ref[g * D_STATE:(g + 1) * D_STATE, :]
            btg = jnp.where(lane_seq == first_seq + i, btg, jnp.zeros_like(btg))
            new_parts.append(hg * decay[:, sl] + jnp.dot(btg, xdd[:, sl], preferred_element_type=F32))
        o_ref[r0:r0 + q, :] = y_ref[r0:r0 + q, :] + ea[r0:r0 + q, :] * jnp.concatenate(y_parts, axis=-1)
        nst_ref[i] = jnp.concatenate(new_parts, axis=-1).T


def _ssm_sample_finish_kernel(x_ref, y_ref, z_ref, nw_ref, wout_ref, g_ref, b_ref, o_ref):
    x = x_ref[...]
    out = jnp.zeros(x.shape, F32)
    for g in range(SSM_GROUPS):
        sl = slice(g * GROUP_INNER, (g + 1) * GROUP_INNER)
        y = _gate_norm(y_ref[:, sl], z_ref[:, sl], nw_ref[:, sl])
        out = out + jnp.dot(y.astype(BF16), wout_ref[sl, :], preferred_element_type=F32)
    o_ref[...] = _layer_norm_rows(DEEPNORM_ALPHA * x + out, g_ref[...], b_ref[...])


def _ssm_sample(x, conv_state, ssm_state, w, g, b, *, bb_state=8):
    bsz, q, d = x.shape
    t = bsz * q
    nseq = SSM_ROWS // q
    rows = SSM_ROWS
    assert q == SUBLANES and bsz % nseq == 0 and nseq % bb_state == 0
    tril, e = _ssm_consts(nseq, q)
    win_cols = D_INNER + CONV_DIM + LANES
    x2 = x.reshape(t, d)
    row_spec = lambda c: pl.BlockSpec((rows, c), lambda i: (i, 0))
    est = d * win_cols * 2 + LANES * D_INNER * 4 + 16 * rows * D_INNER * 4 + 3 * rows * win_cols * 4
    ydiag, z, exp_a, xdd, cmat, bt, new_conv = pl.pallas_call(
        functools.partial(_ssm_sample_front_kernel, nseq=nseq, q=q),
        grid=(t // rows,),
        in_specs=[row_spec(d), pl.BlockSpec((nseq, CONV_WIDTH - 1, CONV_DIM), lambda i: (i, 0, 0))]
        + _ssm_weight_specs(d) + [_resident((1, D_INNER)), _resident((LANES, D_INNER)), _resident((rows, rows))],
        out_specs=[row_spec(D_INNER), row_spec(D_INNER), row_spec(D_INNER), row_spec(D_INNER), row_spec(GBN),
                   pl.BlockSpec((GBN, rows), lambda i: (i, 0)),
                   pl.BlockSpec((nseq, CONV_WIDTH - 1, CONV_DIM), lambda i: (i, 0, 0))],
        out_shape=[
            jax.ShapeDtypeStruct((t, D_INNER), F32), jax.ShapeDtypeStruct((t, D_INNER), F32),
            jax.ShapeDtypeStruct((t, D_INNER), F32), jax.ShapeDtypeStruct((t, D_INNER), BF16),
            jax.ShapeDtypeStruct((t, GBN), F32),
            jax.ShapeDtypeStruct((t // rows * GBN, rows), BF16),
            jax.ShapeDtypeStruct((bsz, CONV_WIDTH - 1, CONV_DIM), F32),
        ],
        scratch_shapes=[pltpu.VMEM((nseq, CONV_PAD + q, CONV_DIM), F32), pltpu.VMEM((rows, D_INNER), F32)],
        compiler_params=pltpu.CompilerParams(
            dimension_semantics=("arbitrary",), vmem_limit_bytes=_vmem_limit(est)),
        name="ssm_sample_front",
    )(x2, conv_state, w["win"], w["wdt"], w["cw"], w["cb"], w["dtb"], w["alog"], w["dsk"], e, tril)

    srows = bb_state * q
    srow_spec = lambda c: pl.BlockSpec((srows, c), lambda i: (i, 0))
    st_spec = pl.BlockSpec((bb_state, D_INNER, D_STATE), lambda i: (i, 0, 0))
    est = 4 * bb_state * D_INNER * D_STATE * 4 + 8 * D_INNER * D_STATE * 4
    per_block = nseq // bb_state
    y, new_state = pl.pallas_call(
        functools.partial(_ssm_sample_state_kernel, bb=bb_state, q=q, nseq=nseq),
        grid=(bsz // bb_state,),
        in_specs=[st_spec, srow_spec(D_INNER), srow_spec(D_INNER), srow_spec(GBN),
                  pl.BlockSpec((rows, D_INNER), lambda i: (i // per_block, 0)),
                  pl.BlockSpec((GBN, rows), lambda i: (i // per_block, 0))],
        out_specs=[srow_spec(D_INNER), st_spec],
        out_shape=[jax.ShapeDtypeStruct((t, D_INNER), F32),
                   jax.ShapeDtypeStruct((bsz, D_INNER, D_STATE), F32)],
        compiler_params=pltpu.CompilerParams(
            dimension_semantics=("arbitrary",), vmem_limit_bytes=_vmem_limit(est)),
        name="ssm_sample_state",
    )(ssm_state, ydiag, exp_a, cmat, xdd, bt)

    est = D_INNER * d * 2 + 8 * rows * D_INNER * 4
    out = pl.pallas_call(
        _ssm_sample_finish_kernel,
        grid=(t // rows,),
        in_specs=[row_spec(d), row_spec(D_INNER), row_spec(D_INNER),
                  _resident((1, D_INNER)), _resident((D_INNER, d)), _resident((1, d)), _resident((1, d))],
        out_specs=row_spec(d),
        out_shape=jax.ShapeDtypeStruct((t, d), F32),
        compiler_params=pltpu.CompilerParams(
            dimension_semantics=("arbitrary",), vmem_limit_bytes=_vmem_limit(est)),
        name="ssm_sample_finish",
    )(x2, y, z, w["nw"], w["wout"], g, b)
    return out.reshape(bsz, q, d), new_conv, new_state


def _ssm_weights(w_in, conv_w, conv_b, dt_bias, a_log, d_skip, norm_w, w_out, win_bf16, wout_bf16):
    pad = LANES - SSM_HEADS
    wdt = jnp.pad(w_in[:, D_INNER + CONV_DIM:], ((0, 0), (0, pad))).astype(BF16)
    return dict(
        win=win_bf16, wdt=wdt, cw=conv_w, cb=conv_b.reshape(1, CONV_DIM),
        dtb=jnp.pad(dt_bias, (0, pad)).reshape(1, LANES),
        alog=jnp.pad(a_log, (0, pad)).reshape(1, LANES),
        dsk=jnp.repeat(d_skip, SSM_HEAD_DIM).reshape(1, D_INNER),
        nw=norm_w.reshape(1, D_INNER), wout=wout_bf16)


def kernel(x_prompt, x_sample, cache_k, cache_v, state_conv, state_ssm, state_pool, rel_bias, attn_w_qkv, attn_b_qkv, attn_w_o, attn_b_o, attn_sinks, ssm_w_in, ssm_conv_w, ssm_conv_b, ssm_dt_bias, ssm_a_log, ssm_d, ssm_norm_w, ssm_w_out, pool_w, pool_scale, ffn_w_gate, ffn_w_up, ffn_w_down, ln_g, ln_b):
    xp, xs = x_prompt, x_sample
    d = xp.shape[-1]
    ffn_w32 = (ffn_w_gate, ffn_w_up, ffn_w_down)
    mixer_w32 = {0: (attn_w_qkv, attn_w_o), 1: (ssm_w_out,)}

    def layer_jobs(i):
        if i >= DEPTH:
            return []
        jobs = [(w, i) for w in ffn_w32]
        return jobs + [(w, i // N_MIXERS) for w in mixer_w32.get(i % N_MIXERS, ())]

    def split_cast(results):
        ffn = tuple(w[None] for w in results[:len(ffn_w32)])
        return ffn, tuple(results[len(ffn_w32):])

    mixer_w = tuple(w[0:1].astype(BF16) for w in mixer_w32[0])
    ffn_w = None
    cache_k_t = jnp.transpose(cache_k, (0, 1, 3, 4, 2))
    cache_v_t = jnp.transpose(cache_v, (0, 1, 3, 4, 2))
    new_cache_t = None
    nk_p, nv_p, nc_p, nh_p, npool_p = [], [], [], [], []
    nc_s, nh_s, npool_s = [], [], []
    for i in range(DEPTH):
        j = i // N_MIXERS
        kind = i % N_MIXERS
        g1 = ln_g[i, 0].reshape(1, d)
        b1 = ln_b[i, 0].reshape(1, d)
        if kind == 0:
            bqkv = attn_b_qkv[j].reshape(1, -1)
            bo = attn_b_o[j].reshape(1, d)
            wqkv, wo = (w if w.ndim == 3 else w[None] for w in mixer_w)
            jobs = [(w, i) for w in ffn_w32] if ffn_w is None else []
            xp, kp, vp, *cast = _attn_prompt(xp, wqkv, bqkv, wo, bo, g1, b1, rel_bias, attn_sinks[j], layer=0,
                                             cast_jobs=jobs)
            if jobs:
                ffn_w, _ = split_cast(cast)
            xs, *new_cache_t = _attn_sample(xs, cache_k_t, cache_v_t, wqkv, bqkv, wo, bo, g1, b1, rel_bias,
                                            attn_sinks[j], layer=0, cache_layer=j, prev=new_cache_t)
            xs = xs.reshape(x_sample.shape)
            kv_shape = (-1, WINDOW, N_KV_HEADS, HEAD_DIM)
            nk_p.append(kp.reshape(kv_shape)); nv_p.append(vp.reshape(kv_shape))
        elif kind == 1:
            w = _ssm_weights(ssm_w_in[j], ssm_conv_w[j], ssm_conv_b[j], ssm_dt_bias[j], ssm_a_log[j],
                             ssm_d[j], ssm_norm_w[j], ssm_w_out[j], ssm_w_in[j].astype(BF16), *mixer_w)
            xp, cp, hp = _ssm_prompt(xp, w, g1, b1)
            xs, cs_, hs_ = _ssm_sample(xs, state_conv[j], state_ssm[j].reshape(-1, D_INNER, D_STATE), w, g1, b1)
            st_shape = (-1, SSM_HEADS, SSM_HEAD_DIM, D_STATE)
            nc_p.append(cp); nh_p.append(hp.reshape(st_shape))
            nc_s.append(cs_); nh_s.append(hs_.reshape(st_shape))
        else:
            pw = pool_w[j].astype(BF16)
            psc = pool_scale[j].reshape(1, d)
            xp, pp = _pool_prompt(xp, pw, psc, g1, b1)
            xs, ps_ = _pool_sample(xs, state_pool[j], pw, psc, g1, b1, start=PAST_LEN)
            npool_p.append(pp); npool_s.append(ps_)
        g2 = ln_g[i, 1].reshape(1, d)
        b2 = ln_b[i, 1].reshape(1, d)
        xp2, cast = _ffn_ln(xp.reshape(-1, d), *ffn_w, g2, b2, layer=0, cast_jobs=layer_jobs(i + 1))
        xs2, _ = _ffn_ln(xs.reshape(-1, d), *ffn_w, g2, b2, layer=0)
        xp, xs = xp2.reshape(xp.shape), xs2.reshape(xs.shape)
        if cast:
            ffn_w, mixer_w = split_cast(cast)
    nk_s, nv_s = (jnp.transpose(t, (0, 1, 4, 2, 3)) for t in new_cache_t)
    return (xp, xs,
            jnp.stack(nk_p), jnp.stack(nv_p), jnp.stack(nc_p), jnp.stack(nh_p), jnp.stack(npool_p),
            nk_s, nv_s, jnp.stack(nc_s), jnp.stack(nh_s), jnp.stack(npool_s))
```

```python
import functools
import math

import jax
import jax.numpy as jnp
import numpy as np
from jax import lax
from jax.experimental import pallas as pl
from jax.experimental.pallas import tpu as pltpu

DEPTH = 4
N_MIXERS = 3
PAST_LEN = 8192
DEEPNORM_ALPHA = (2 * DEPTH) ** 0.25
LN_EPS = 1e-5

V7X_VMEM_BYTES = 64 * 1024 * 1024
LANES = 128
SUBLANES = 8
MXU_DIM = 256

BF16 = jnp.bfloat16
F32 = jnp.float32


def _vmem_limit(estimate_bytes):
    return int(min(V7X_VMEM_BYTES - 8 * 1024 * 1024, max(32 * 1024 * 1024, estimate_bytes * 3 // 2)))


def _layer_norm_rows(v, g, b):
    mu = jnp.mean(v, axis=-1, keepdims=True)
    d = v - mu
    var = jnp.mean(d * d, axis=-1, keepdims=True)
    return d * lax.rsqrt(var + LN_EPS) * g + b


def _resident(shape):
    nd = len(shape)
    return pl.BlockSpec(shape, lambda *_: (0,) * nd, pipeline_mode=pl.Buffered(1))


def _layer_resident(shape, layer):
    nd = len(shape)
    return pl.BlockSpec((1,) + tuple(shape), lambda *_: (layer,) + (0,) * nd, pipeline_mode=pl.Buffered(1))


def _ffn_chunks(d_ff):
    step = 2 * MXU_DIM
    edges = list(range(0, d_ff, step)) + [d_ff]
    return tuple(zip(edges[:-1], edges[1:]))


def _ffn_kernel(x_ref, wg_ref, wu_ref, wd_ref, g_ref, b_ref, *rest, chunks, n_cast):
    casts, (o_ref, *cast_outs) = rest[:n_cast], rest[n_cast:]
    _run_cast_jobs(casts, cast_outs)
    x = x_ref[...]
    xb = x.astype(BF16)
    acc = jnp.zeros(x.shape, F32)
    for c0, c1 in chunks:
        gate = jnp.dot(xb, wg_ref[0, :, c0:c1], preferred_element_type=F32)
        up = jnp.dot(xb, wu_ref[0, :, c0:c1], preferred_element_type=F32)
        h = (gate * jax.nn.sigmoid(gate)) * up
        acc = acc + jnp.dot(h.astype(BF16), wd_ref[0, c0:c1, :], preferred_element_type=F32)
    o_ref[...] = _layer_norm_rows(DEEPNORM_ALPHA * x + acc, g_ref[...], b_ref[...])


def _cast_slab_specs(w32, layer, steps, step_of=lambda i: i):
    _, rows, cols = w32.shape
    if rows % (steps * 2 * SUBLANES) == 0:
        slab = rows // steps
    else:
        slab = next(s for s in range(LANES, rows + 1, LANES) if rows % s == 0 and rows // s <= steps)
    last = rows // slab - 1
    return (pl.BlockSpec((1, slab, cols), lambda *idx: (layer, jnp.minimum(step_of(*idx), last), 0)),
            pl.BlockSpec((slab, cols), lambda *idx: (jnp.minimum(step_of(*idx), last), 0)),
            jax.ShapeDtypeStruct((rows, cols), BF16))


def _cast_jobs_specs(cast_jobs, steps, step_of=lambda i: i):
    specs = [_cast_slab_specs(w32, layer, steps, step_of) for w32, layer in cast_jobs]
    return ([s[0] for s in specs], [s[1] for s in specs], [s[2] for s in specs], [w32 for w32, _ in cast_jobs])


def _run_cast_jobs(srcs, dsts):
    for src, dst in zip(srcs, dsts):
        dst[...] = src[0].astype(BF16)


def _ffn_ln(x, wg, wu, wd, g, b, *, layer, cast_jobs=(), tm=512):
    t, d = x.shape
    d_ff = wg.shape[-1]
    tm = min(tm, t)
    assert t % tm == 0
    steps = t // tm
    est = 3 * d * d_ff * 2 + 4 * tm * d * 4 + 4 * tm * d_ff * 4
    in_specs = [
        pl.BlockSpec((tm, d), lambda i: (i, 0)),
        _layer_resident((d, d_ff), layer),
        _layer_resident((d, d_ff), layer),
        _layer_resident((d_ff, d), layer),
        _resident((1, d)),
        _resident((1, d)),
    ]
    cast_in, cast_out, cast_shape, cast_args = _cast_jobs_specs(cast_jobs, steps)
    outs = pl.pallas_call(
        functools.partial(_ffn_kernel, chunks=_ffn_chunks(d_ff), n_cast=len(cast_jobs)),
        grid=(steps,),
        in_specs=in_specs + cast_in,
        out_specs=[pl.BlockSpec((tm, d), lambda i: (i, 0))] + cast_out,
        out_shape=[jax.ShapeDtypeStruct((t, d), F32)] + cast_shape,
        compiler_params=pltpu.CompilerParams(
            dimension_semantics=("arbitrary",), vmem_limit_bytes=_vmem_limit(est)),
        name="ffn_ln",
    )(x, wg, wu, wd, g, b, *cast_args)
    return outs[0], tuple(outs[1:])


HEAD_DIM = 64
N_HEADS = 16
N_KV_HEADS = 4
GQA_GROUP = N_HEADS // N_KV_HEADS
WINDOW = 128
REL_BUCKETS = 32
REL_MAX_DIST = 128
Q_DIM = N_HEADS * HEAD_DIM
KV_DIM = N_KV_HEADS * HEAD_DIM
NEG_INF = float("-inf")
CHAIN_BATCH = 2


def _t5_bucket_table(dist):
    n = np.maximum(dist, 0)
    max_exact = REL_BUCKETS // 2
    nf = np.maximum(n, 1).astype(np.float32)
    large = max_exact + (np.log(nf / np.float32(max_exact)) / np.float32(math.log(REL_MAX_DIST / max_exact))
                         * np.float32(REL_BUCKETS - max_exact)).astype(np.int32)
    large = np.minimum(large, REL_BUCKETS - 1)
    bucket = np.where(n < max_exact, n, large)
    valid = (dist >= 0) & (dist < WINDOW)
    return np.where(valid, bucket, -1).astype(np.int32)


def _bias_from_buckets(bucket, relb_ref, head):
    acc = jnp.full(bucket.shape, NEG_INF, F32)
    for bkt in range(REL_BUCKETS):
        acc = jnp.where(bucket == bkt, relb_ref[bkt, head], acc)
    return acc


def _attn_prompt_kernel(x_ref, wqkv_ref, bqkv_ref, wo_ref, bo_ref, g_ref, b_ref, bucket_ref,
                        relb_ref, sink_ref, *rest, tq, n_cast):
    casts = rest[:n_cast]
    o_ref, kout_ref, vout_ref = rest[n_cast:n_cast + 3]
    cast_outs = rest[n_cast + 3:2 * n_cast + 3]
    ka_lo, ka_hi, kb_lo, kb_hi, vt, ot, bias_scr, sink_scr = rest[2 * n_cast + 3:]
    _run_cast_jobs(casts, cast_outs)
    bi = pl.program_id(0)
    j = pl.program_id(1)
    nj = pl.num_programs(1)
    blk = WINDOW
    half_heads = GQA_GROUP // 2
    kbufs = (ka_lo, ka_hi, kb_lo, kb_hi)

    @pl.when((bi == 0) & (j == 0))
    def _build_tables():
        bucket = bucket_ref[...]
        lane = lax.broadcasted_iota(jnp.int32, (1, 2 * blk), 1)
        for kv in range(N_KV_HEADS):
            for half in range(2):
                ha = kv * GQA_GROUP + half
                hb = ha + half_heads
                pair = kv * 2 + half
                bias_scr[pair, :, 0:blk] = _bias_from_buckets(bucket, relb_ref, ha)
                bias_scr[pair, :, blk:2 * blk] = _bias_from_buckets(bucket, relb_ref, hb)
                sink_scr[pair] = jnp.where(lane < blk, sink_ref[ha], sink_ref[hb])

    @pl.when(j == 0)
    def _no_past():
        for buf in kbufs:
            buf[0:blk, :] = jnp.zeros((blk, KV_DIM), BF16)
        vt[:, 0:blk] = jnp.zeros((KV_DIM, blk), BF16)

    @pl.when(j > 0)
    def _carry():
        for buf in kbufs:
            buf[0:blk, :] = buf[tq:tq + blk, :]
        vt[:, 0:blk] = vt[:, tq:tq + blk]

    x = x_ref[0]
    qkv = jnp.dot(x.astype(BF16), wqkv_ref[0], preferred_element_type=F32) + bqkv_ref[...]
    q = (qkv[:, :Q_DIM] * (HEAD_DIM ** -0.5)).astype(BF16)
    k = qkv[:, Q_DIM:Q_DIM + KV_DIM]
    v = qkv[:, Q_DIM + KV_DIM:]
    k_sw = jnp.concatenate(
        [pltpu.roll(k[:, c * LANES:(c + 1) * LANES], HEAD_DIM, axis=1) for c in range(KV_DIM // LANES)], axis=1)
    lo = (lax.broadcasted_iota(jnp.int32, (1, KV_DIM), 1) % LANES) < HEAD_DIM
    ka_lo[blk:blk + tq, :] = jnp.where(lo, k, 0.0).astype(BF16)
    ka_hi[blk:blk + tq, :] = jnp.where(lo, 0.0, k).astype(BF16)
    kb_lo[blk:blk + tq, :] = jnp.where(lo, k_sw, 0.0).astype(BF16)
    kb_hi[blk:blk + tq, :] = jnp.where(lo, 0.0, k_sw).astype(BF16)
    vt[:, blk:blk + tq] = v.T.astype(BF16)

    @pl.when(j == nj - 1)
    def _emit_cache():
        kout_ref[0] = k[tq - WINDOW:, :]
        vout_ref[0] = v[tq - WINDOW:, :]

    ks = lax.broadcasted_iota(jnp.int32, (blk, 2 * blk), 0)
    qt = lax.broadcasted_iota(jnp.int32, (blk, 2 * blk), 1) % blk
    own = ks <= qt
    chains = [(i, kv, half) for i in range(tq // blk) for kv in range(N_KV_HEADS) for half in range(2)]

    def scores(i, kv, half):
        r0 = i * blk
        c0 = (kv // 2) * LANES
        in_lo = kv % 2 == 0
        if half == 0:
            kk = (ka_lo if in_lo else kb_lo)[r0:r0 + 2 * blk, c0:c0 + LANES]
        else:
            kk = (kb_hi if in_lo else ka_hi)[r0:r0 + 2 * blk, c0:c0 + LANES]
        qa = q[r0:r0 + blk, (2 * kv) * LANES:(2 * kv + 1) * LANES]
        qb = q[r0:r0 + blk, (2 * kv + 1) * LANES:(2 * kv + 2) * LANES]
        q_pair = jnp.concatenate([qa, qb], axis=0)
        return lax.dot_general(kk, q_pair, (((1,), (1,)), ((), ())), preferred_element_type=F32)

    def fold(i, kv, half, s2):
        s_prev = s2[0:blk, :]
        if i == 0:
            s_prev = jnp.where(j == 0, NEG_INF, s_prev)
        return jnp.where(own, s2[blk:2 * blk, :], s_prev) + bias_scr[kv * 2 + half]

    def softmax_batch(batch, s2s):
        sinks = [sink_scr[kv * 2 + half] for _, kv, half in batch]
        ss = [fold(*ch, s2) for ch, s2 in zip(batch, s2s)]
        ms = [jnp.maximum(jnp.max(s, axis=0, keepdims=True), sink) for s, sink in zip(ss, sinks)]
        ps = [jnp.exp(s - m) for s, m in zip(ss, ms)]
        invs = [1.0 / (jnp.sum(p, axis=0, keepdims=True) + jnp.exp(sink - m)) for p, m, sink in zip(ps, ms, sinks)]
        p2s = [jnp.concatenate([jnp.where(own, 0.0, p), jnp.where(own, p, 0.0)], axis=0).astype(BF16) for p in ps]
        return list(zip(p2s, invs))

    def weighted_values(i, kv, half, p2, inv):
        r0 = i * blk
        ha = kv * GQA_GROUP + half
        hb = ha + half_heads
        v_t = vt[kv * HEAD_DIM:(kv + 1) * HEAD_DIM, r0:r0 + 2 * blk]
        o_t = jnp.dot(v_t, p2, preferred_element_type=F32) * inv
        ot[ha * HEAD_DIM:(ha + 1) * HEAD_DIM, r0:r0 + blk] = o_t[:, 0:blk]
        ot[hb * HEAD_DIM:(hb + 1) * HEAD_DIM, r0:r0 + blk] = o_t[:, blk:2 * blk]

    batches = [chains[c:c + CHAIN_BATCH] for c in range(0, len(chains), CHAIN_BATCH)]
    s_next = [scores(*ch) for ch in batches[0]]
    prev = []
    for bi, batch in enumerate(batches):
        s_cur = s_next
        if bi + 1 < len(batches):
            s_next = [scores(*ch) for ch in batches[bi + 1]]
        probs = softmax_batch(batch, s_cur)
        for ch, pr in prev:
            weighted_values(*ch, *pr)
        prev = list(zip(batch, probs))
    for ch, pr in prev:
        weighted_values(*ch, *pr)

    o = ot[...].T.astype(BF16)
    y = jnp.dot(o, wo_ref[0], preferred_element_type=F32) + bo_ref[...]
    o_ref[0] = _layer_norm_rows(DEEPNORM_ALPHA * x + y, g_ref[...], b_ref[...])


def _attn_prompt(x, wqkv, bqkv, wo, bo, g, b, rel_bias, sinks, *, layer, cast_jobs=(), tq=1024):
    bsz, l, d = x.shape
    tq = min(tq, l)
    assert l % tq == 0 and tq % WINDOW == 0
    nj = l // tq
    cast_in, cast_out, cast_shape, cast_args = _cast_jobs_specs(cast_jobs, bsz * nj, lambda bi, j: bi * nj + j)
    qi = np.arange(WINDOW)[None, :]
    si = np.arange(WINDOW)[:, None]
    bucket = jnp.asarray(_t5_bucket_table(np.where(si <= qi, qi - si, qi + WINDOW - si)))
    qkv_dim = wqkv.shape[-1]
    smem = pl.BlockSpec(memory_space=pltpu.SMEM)
    est = (d * qkv_dim + Q_DIM * d) * 2 + 5 * tq * d * 4 + N_HEADS * WINDOW * 2 * WINDOW * 4 + 3 * tq * qkv_dim * 4
    return pl.pallas_call(
        functools.partial(_attn_prompt_kernel, tq=tq, n_cast=len(cast_jobs)),
        grid=(bsz, nj),
        in_specs=[
            pl.BlockSpec((1, tq, d), lambda bi, j: (bi, j, 0)),
            _layer_resident((d, qkv_dim), layer), _resident((1, qkv_dim)),
            _layer_resident((Q_DIM, d), layer), _resident((1, d)),
            _resident((1, d)), _resident((1, d)),
            _resident((WINDOW, WINDOW)),
            smem, smem,
        ] + cast_in,
        out_specs=[
            pl.BlockSpec((1, tq, d), lambda bi, j: (bi, j, 0)),
            pl.BlockSpec((1, WINDOW, KV_DIM), lambda bi, j: (bi, 0, 0)),
            pl.BlockSpec((1, WINDOW, KV_DIM), lambda bi, j: (bi, 0, 0)),
        ] + cast_out,
        out_shape=[
            jax.ShapeDtypeStruct((bsz, l, d), F32),
            jax.ShapeDtypeStruct((bsz, WINDOW, KV_DIM), F32),
            jax.ShapeDtypeStruct((bsz, WINDOW, KV_DIM), F32),
        ] + cast_shape,
        scratch_shapes=[pltpu.VMEM((WINDOW + tq, KV_DIM), BF16)] * 4 + [
            pltpu.VMEM((KV_DIM, WINDOW + tq), BF16),
            pltpu.VMEM((Q_DIM, tq), F32),
            pltpu.VMEM((N_HEADS // 2, WINDOW, 2 * WINDOW), F32),
            pltpu.VMEM((N_HEADS // 2, 1, 2 * WINDOW), F32),
        ],
        compiler_params=pltpu.CompilerParams(
            dimension_semantics=("arbitrary", "arbitrary"), vmem_limit_bytes=_vmem_limit(est)),
        name="attn_prompt",
    )(x, wqkv, bqkv, wo, bo, g, b, bucket, rel_bias, sinks, *cast_args)


SAMPLE_KEYS = 2 * WINDOW
SAMPLE_CHAIN_BATCH = 16


def _attn_sample_kernel(x_ref, ck_ref, cv_ref, wqkv_ref, bqkv_ref, wo_ref, bo_ref, g_ref, b_ref, bucket_ref,
                        relb_ref, sink_ref, *rest, bb, l, n_prev, out_layer):
    o_ref, nk_ref, nv_ref, qbuf, obuf, bias_scr = rest[n_prev:]
    for other in range(nk_ref.shape[0]):
        if other != out_layer:
            nk_ref[other] = jnp.zeros(nk_ref.shape[1:], F32)
            nv_ref[other] = jnp.zeros(nv_ref.shape[1:], F32)

    @pl.when(pl.program_id(0) == 0)
    def _init():
        bucket = bucket_ref[...]
        for h in range(N_HEADS):
            kv, g_ = divmod(h, GQA_GROUP)
            bias_scr[kv, g_ * l:(g_ + 1) * l, :] = _bias_from_buckets(bucket, relb_ref, h)

    x = x_ref[...]
    qkv = jnp.dot(x.astype(BF16), wqkv_ref[0], preferred_element_type=F32) + bqkv_ref[...]
    qbuf[...] = qkv[:, :Q_DIM] * (HEAD_DIM ** -0.5)
    k_new = qkv[:, Q_DIM:Q_DIM + KV_DIM]
    v_new = qkv[:, Q_DIM + KV_DIM:]

    sink_cols = []
    for kv in range(N_KV_HEADS):
        sink_cols.append(jnp.concatenate(
            [jnp.full((l, 1), sink_ref[kv * GQA_GROUP + g_], F32) for g_ in range(GQA_GROUP)], axis=0))

    kn_t = k_new.T
    vn_t = v_new.T
    lane = lax.broadcasted_iota(jnp.int32, (HEAD_DIM, WINDOW), 1)
    pad = jnp.zeros((WINDOW - l, KV_DIM), F32)

    def shifted(old_t, new_t, i):
        kept = pltpu.roll(old_t, WINDOW - l, axis=1)
        new = pltpu.roll(new_t, (WINDOW - l - i * l) % WINDOW, axis=1)
        return jnp.where(lane >= WINDOW - l, new, kept)

    def keys_values(i):
        r0 = i * l
        for kv in range(N_KV_HEADS):
            hs = slice(kv * HEAD_DIM, (kv + 1) * HEAD_DIM)
            nk_ref[out_layer, i, kv] = shifted(ck_ref[0, i, kv], kn_t[hs, :], i)
            nv_ref[out_layer, i, kv] = shifted(cv_ref[0, i, kv], vn_t[hs, :], i)
        kn = jnp.concatenate([k_new[r0:r0 + l, :], pad], axis=0).astype(BF16)
        vn = jnp.concatenate([v_new[r0:r0 + l, :], pad], axis=0).astype(BF16)
        return kn, vn

    def scores(i, kv, kn):
        r0 = i * l
        c0 = kv * HEAD_DIM
        qs = jnp.concatenate(
            [qbuf[r0:r0 + l, (kv * GQA_GROUP + g_) * HEAD_DIM:(kv * GQA_GROUP + g_ + 1) * HEAD_DIM]
             for g_ in range(GQA_GROUP)], axis=0).astype(BF16)
        s_cache = jnp.dot(qs, ck_ref[0, i, kv].astype(BF16), preferred_element_type=F32)
        s_new = lax.dot_general(qs, kn[:, c0:c0 + HEAD_DIM], (((1,), (1,)), ((), ())), preferred_element_type=F32)
        return jnp.concatenate([s_cache, s_new], axis=1)

    def weighted_values(i, kv, p, vn):
        c0 = kv * HEAD_DIM
        pb = p.astype(BF16)
        o_cache = lax.dot_general(pb[:, :WINDOW], cv_ref[0, i, kv].astype(BF16), (((1,), (1,)), ((), ())),
                                  preferred_element_type=F32)
        return o_cache + jnp.dot(pb[:, WINDOW:], vn[:, c0:c0 + HEAD_DIM], preferred_element_type=F32)

    for i0 in range(0, bb, SAMPLE_CHAIN_BATCH):
        items = range(i0, min(bb, i0 + SAMPLE_CHAIN_BATCH))
        kvs = {i: keys_values(i) for i in items}
        chains = [(i, kv) for i in items for kv in range(N_KV_HEADS)]
        s_all = [scores(i, kv, kvs[i][0]) + bias_scr[kv] for i, kv in chains]
        m_all = [jnp.maximum(jnp.max(s, axis=-1, keepdims=True), sink_cols[kv]) for s, (i, kv) in zip(s_all, chains)]
        p_all = [jnp.exp(s - m) for s, m in zip(s_all, m_all)]
        den_all = [jnp.sum(p, axis=-1, keepdims=True) + jnp.exp(sink_cols[kv] - m)
                   for p, m, (i, kv) in zip(p_all, m_all, chains)]
        o_all = [weighted_values(i, kv, p, kvs[i][1]) for p, (i, kv) in zip(p_all, chains)]
        for o, den, (i, kv) in zip(o_all, den_all, chains):
            o = o / den
            for g_ in range(GQA_GROUP):
                h = kv * GQA_GROUP + g_
                obuf[i * l:(i + 1) * l, h * HEAD_DIM:(h + 1) * HEAD_DIM] = o[g_ * l:(g_ + 1) * l, :]

    y = jnp.dot(obuf[...].astype(BF16), wo_ref[0], preferred_element_type=F32) + bo_ref[...]
    o_ref[...] = _layer_norm_rows(DEEPNORM_ALPHA * x + y, g_ref[...], b_ref[...])


def _attn_sample(x, cache_k_t, cache_v_t, wqkv, bqkv, wo, bo, g, b, rel_bias, sinks, *, layer, cache_layer, prev=None,
                 bb=16):
    bsz, l, d = x.shape
    assert bsz % bb == 0 and l == SUBLANES and bb * l == WINDOW
    ti = np.arange(l)[:, None]
    ci = np.arange(SAMPLE_KEYS)[None, :]
    table = _t5_bucket_table(ti + WINDOW - ci)
    table = np.where(ci < WINDOW + l, table, -1).astype(np.int32)
    bucket = jnp.asarray(table)
    qkv_dim = wqkv.shape[-1]
    rows = bb * l
    smem = pl.BlockSpec(memory_space=pltpu.SMEM)
    cache_spec = pl.BlockSpec((1, bb, N_KV_HEADS, HEAD_DIM, WINDOW), lambda i: (cache_layer, i, 0, 0, 0))
    est = (d * qkv_dim + Q_DIM * d) * 2 + 8 * bb * WINDOW * KV_DIM * 4 + 8 * rows * d * 4
    in_specs = [
        pl.BlockSpec((rows, d), lambda i: (i, 0)),
        cache_spec, cache_spec,
        _layer_resident((d, qkv_dim), layer), _resident((1, qkv_dim)),
        _layer_resident((Q_DIM, d), layer), _resident((1, d)),
        _resident((1, d)), _resident((1, d)),
        _resident((l, SAMPLE_KEYS)),
        smem, smem,
    ]
    args = [x.reshape(bsz * l, d), cache_k_t, cache_v_t, wqkv, bqkv, wo, bo, g, b, bucket, rel_bias, sinks]
    n_layers = cache_k_t.shape[0]
    if prev is None:
        aliases, out_layer = {}, cache_layer
        out_cache_spec = pl.BlockSpec((n_layers, bb, N_KV_HEADS, HEAD_DIM, WINDOW), lambda i: (0, i, 0, 0, 0))
    else:
        aliases, out_layer = {len(args): 1, len(args) + 1: 2}, 0
        out_cache_spec = cache_spec
        in_specs += [pl.BlockSpec(memory_space=pl.ANY)] * 2
        args += list(prev)
    return pl.pallas_call(
        functools.partial(_attn_sample_kernel, bb=bb, l=l, n_prev=len(aliases), out_layer=out_layer),
        grid=(bsz // bb,),
        in_specs=in_specs,
        out_specs=[pl.BlockSpec((rows, d), lambda i: (i, 0)), out_cache_spec, out_cache_spec],
        out_shape=[
            jax.ShapeDtypeStruct((bsz * l, d), F32),
            jax.ShapeDtypeStruct(cache_k_t.shape, F32),
            jax.ShapeDtypeStruct(cache_v_t.shape, F32),
        ],
        scratch_shapes=[
            pltpu.VMEM((rows, Q_DIM), F32),
            pltpu.VMEM((rows, Q_DIM), F32),
            pltpu.VMEM((N_KV_HEADS, GQA_GROUP * l, SAMPLE_KEYS), F32),
        ],
        input_output_aliases=aliases,
        compiler_params=pltpu.CompilerParams(
            dimension_semantics=("arbitrary",), vmem_limit_bytes=_vmem_limit(est)),
        name="attn_sample",
    )(*args)


POOL_WINDOWS = (2, 4, 8, 16)
POOL_PAD = max(POOL_WINDOWS)
POOL_STATE_LEN = POOL_PAD - 1
POOL_HIST = len(POOL_WINDOWS) * SUBLANES


def _pool_mix(window_sum, x, cnt_of, pw_ref, scale):
    gd = x.shape[-1] // len(POOL_WINDOWS)
    outs = []
    for g, w in enumerate(POOL_WINDOWS):
        diff = window_sum(g, w) / cnt_of(w) - x[:, g * gd:(g + 1) * gd]
        outs.append(jnp.dot(diff.astype(BF16), pw_ref[g], preferred_element_type=F32))
    return jnp.concatenate(outs, axis=-1) * scale


def _pool_prompt_kernel(x_ref, pw_ref, sc_ref, g_ref, b_ref, o_ref, st_ref, buf, lvl, *, tm):
    j = pl.program_id(1)
    nj = pl.num_programs(1)
    d = x_ref.shape[-1]
    gd = d // len(POOL_WINDOWS)
    hist, rows = POOL_HIST, POOL_HIST + tm

    @pl.when(j == 0)
    def _no_past():
        buf[0:hist, :] = jnp.zeros((hist, d), F32)

    @pl.when(j > 0)
    def _carry():
        buf[0:hist, :] = buf[tm:tm + hist, :]

    x = x_ref[0]
    buf[hist:rows, :] = x

    sums = {}
    src = buf
    for k, w in enumerate(POOL_WINDOWS):
        step, c0, r0 = w // 2, k * gd, (k + 1) * SUBLANES
        cur = src[r0:rows, c0:d] + src[r0 - step:rows - step, c0:d]
        sums[k] = cur[hist - r0:, 0:gd]
        if k + 1 < len(POOL_WINDOWS):
            lvl[k, r0:rows, c0:d] = cur
            src = lvl.at[k]

    def window_sum(g, w):
        return sums[g]

    pos1 = j * tm + lax.broadcasted_iota(jnp.int32, (tm, 1), 0) + 1

    def cnt_of(w):
        return jnp.minimum(pos1, w).astype(F32)

    y = _pool_mix(window_sum, x, cnt_of, pw_ref, sc_ref[...])
    o_ref[0] = _layer_norm_rows(DEEPNORM_ALPHA * x + y, g_ref[...], b_ref[...])

    @pl.when(j == nj - 1)
    def _emit_state():
        st_ref[0] = buf[rows - POOL_STATE_LEN:rows, :]


def _pool_prompt(x, pw, scale, g, b, *, tm=512):
    bsz, l, d = x.shape
    tm = min(tm, l)
    assert l % tm == 0 and tm >= POOL_HIST >= POOL_STATE_LEN and POOL_WINDOWS == (2, 4, 8, 16)
    ng, gd = pw.shape[0], pw.shape[1]
    est = 10 * tm * d * 4 + ng * gd * gd * 2
    return pl.pallas_call(
        functools.partial(_pool_prompt_kernel, tm=tm),
        grid=(bsz, l // tm),
        in_specs=[
            pl.BlockSpec((1, tm, d), lambda bi, j: (bi, j, 0)),
            _resident((ng, gd, gd)), _resident((1, d)), _resident((1, d)), _resident((1, d)),
        ],
        out_specs=[
            pl.BlockSpec((1, tm, d), lambda bi, j: (bi, j, 0)),
            pl.BlockSpec((1, POOL_STATE_LEN, d), lambda bi, j: (bi, 0, 0)),
        ],
        out_shape=[
            jax.ShapeDtypeStruct((bsz, l, d), F32),
            jax.ShapeDtypeStruct((bsz, POOL_STATE_LEN, d), F32),
        ],
        scratch_shapes=[pltpu.VMEM((POOL_HIST + tm, d), F32),
                        pltpu.VMEM((len(POOL_WINDOWS) - 1, POOL_HIST + tm, d), F32)],
        compiler_params=pltpu.CompilerParams(
            dimension_semantics=("arbitrary", "arbitrary"), vmem_limit_bytes=_vmem_limit(est)),
        name="pool_prompt",
    )(x, pw, scale, g, b)


def _pool_sample_kernel(x_ref, st_ref, pw_ref, sc_ref, g_ref, b_ref, o_ref, nst_ref, buf, *, bb, l):
    d = x_ref.shape[-1]
    gd = d // len(POOL_WINDOWS)
    x3 = x_ref[...]
    buf[:, 1:POOL_PAD, :] = st_ref[...]
    buf[:, POOL_PAD:POOL_PAD + l, :] = x3
    x = x3.reshape(bb * l, d)

    def window_sum(g, w):
        acc = x3[:, :, g * gd:(g + 1) * gd]
        for s in range(1, w):
            acc = acc + buf[:, POOL_PAD - s:POOL_PAD - s + l, g * gd:(g + 1) * gd]
        return acc.reshape(bb * l, gd)

    y = _pool_mix(window_sum, x, lambda w: float(w), pw_ref, sc_ref[...])
    o_ref[...] = _layer_norm_rows(DEEPNORM_ALPHA * x + y, g_ref[...], b_ref[...]).reshape(bb, l, d)
    nst_ref[...] = buf[:, l + 1:l + POOL_PAD, :]


def _pool_sample(x, state, pw, scale, g, b, *, start, bb=16):
    bsz, l, d = x.shape
    assert bsz % bb == 0 and l == SUBLANES and start + 1 >= POOL_PAD
    ng, gd = pw.shape[0], pw.shape[1]
    est = 8 * bb * (POOL_PAD + l) * d * 4 + ng * gd * gd * 2
    return pl.pallas_call(
        functools.partial(_pool_sample_kernel, bb=bb, l=l),
        grid=(bsz // bb,),
        in_specs=[
            pl.BlockSpec((bb, l, d), lambda i: (i, 0, 0)),
            pl.BlockSpec((bb, POOL_STATE_LEN, d), lambda i: (i, 0, 0)),
            _resident((ng, gd, gd)), _resident((1, d)), _resident((1, d)), _resident((1, d)),
        ],
        out_specs=[
            pl.BlockSpec((bb, l, d), lambda i: (i, 0, 0)),
            pl.BlockSpec((bb, POOL_STATE_LEN, d), lambda i: (i, 0, 0)),
        ],
        out_shape=[
            jax.ShapeDtypeStruct((bsz, l, d), F32),
            jax.ShapeDtypeStruct((bsz, POOL_STATE_LEN, d), F32),
        ],
        scratch_shapes=[pltpu.VMEM((bb, POOL_PAD + l, d), F32)],
        compiler_params=pltpu.CompilerParams(
            dimension_semantics=("arbitrary",), vmem_limit_bytes=_vmem_limit(est)),
        name="pool_sample",
    )(x, state, pw, scale, g, b)


D_INNER = 2048
SSM_HEAD_DIM = 64
SSM_HEADS = D_INNER // SSM_HEAD_DIM
SSM_GROUPS = 4
SSM_HPG = SSM_HEADS // SSM_GROUPS
D_STATE = 128
CONV_WIDTH = 4
GBN = SSM_GROUPS * D_STATE
CONV_DIM = D_INNER + 2 * GBN
GROUP_INNER = D_INNER // SSM_GROUPS
RMS_EPS = 1e-5
CONV_PAD = SUBLANES
SSM_ROWS = 128


def _split_bf16(v, parts=3):
    out = []
    r = v
    for _ in range(parts):
        p = r.astype(BF16)
        out.append(p)
        r = r - p.astype(F32)
    return out


def _expand_heads(v, e):
    return sum(jnp.dot(p, e, preferred_element_type=F32) for p in _split_bf16(v, parts=2))


def _ssm_in_proj(xb, win_ref, wdt_ref, c0, c1):
    main = win_ref.shape[1]
    if c0 >= main:
        return jnp.dot(xb, wdt_ref[:, c0 - main:c1 - main], preferred_element_type=F32)
    assert c1 <= main
    return jnp.dot(xb, win_ref[:, c0:c1], preferred_element_type=F32)


def _ssm_prep(proj, xp_buf, conv_state, dtb_ref, alog_ref, tril_ref, *, nseq, q):
    rows = nseq * q
    xbc_pre = proj(D_INNER, D_INNER + CONV_DIM)
    dt_pre = proj(D_INNER + CONV_DIM, D_INNER + CONV_DIM + LANES)
    if conv_state is not None:
        xp_buf[:, CONV_PAD - (CONV_WIDTH - 1):CONV_PAD, :] = conv_state
    xp_buf[:, CONV_PAD:CONV_PAD + q, :] = xbc_pre.reshape(nseq, q, CONV_DIM)
    new_conv = xp_buf[:, q + CONV_PAD - (CONV_WIDTH - 1):q + CONV_PAD, :]

    dtv = dt_pre + dtb_ref[...]
    dt = jnp.maximum(dtv, 0.0) + jnp.log1p(jnp.exp(-jnp.abs(dtv)))
    a = -jnp.exp(alog_ref[...])
    tril = tril_ref[...]
    acum = sum(jnp.dot(tril, p, preferred_element_type=F32) for p in _split_bf16(dt * a))
    a3 = acum.reshape(nseq, q, LANES)
    alast = jnp.broadcast_to(a3[:, q - 1:q, :], (nseq, q, LANES)).reshape(rows, LANES)
    return dict(acum=acum, acum_t=acum.T, dt_t=dt.T, causal=tril > 0.5,
                exp_a=jnp.exp(acum), dd=jnp.exp(alast - acum) * dt, new_conv=new_conv)


def _conv_silu(xp_buf, cw_ref, cb_ref, c0, c1, *, nseq, q):
    acc = cb_ref[:, c0:c1].reshape(1, 1, c1 - c0)
    for jj in range(CONV_WIDTH):
        off = CONV_PAD - (CONV_WIDTH - 1) + jj
        acc = acc + xp_buf[:, off:off + q, c0:c1] * cw_ref[jj:jj + 1, c0:c1].reshape(1, 1, c1 - c0)
    acc = acc.reshape(nseq * q, c1 - c0)
    return acc * jax.nn.sigmoid(acc)


def _ssm_group_diag(g, f, proj, prep, xp_buf, cw_ref, cb_ref, ybuf, *, nseq, q):
    f["z"] = proj(g * GROUP_INNER, (g + 1) * GROUP_INNER)
    conv = functools.partial(_conv_silu, xp_buf, cw_ref, cb_ref, nseq=nseq, q=q)
    half = GROUP_INNER // 2
    xs_lo = conv(g * GROUP_INNER, g * GROUP_INNER + half)
    yield
    xs_hi = conv(g * GROUP_INNER + half, (g + 1) * GROUP_INNER)
    yield
    xs = f["xs"] = jnp.concatenate([xs_lo, xs_hi], axis=-1)
    bm = f["bm"] = conv(D_INNER + g * D_STATE, D_INNER + (g + 1) * D_STATE)
    cm = f["cm"] = conv(D_INNER + GBN + g * D_STATE, D_INNER + GBN + (g + 1) * D_STATE)
    yield
    cb = lax.dot_general(cm.astype(BF16), bm.astype(BF16), (((1,), (1,)), ((), ())), preferred_element_type=F32)
    acum, acum_t, dt_t, causal = prep["acum"], prep["acum_t"], prep["dt_t"], prep["causal"]
    for r in range(SSM_HPG):
        h = g * SSM_HPG + r
        seg = acum[:, h:h + 1] - acum_t[h:h + 1, :]
        lm = jnp.exp(jnp.where(causal, seg, NEG_INF))
        w = (cb * lm * dt_t[h:h + 1, :]).astype(BF16)
        xh = xs[:, r * SSM_HEAD_DIM:(r + 1) * SSM_HEAD_DIM].astype(BF16)
        ybuf[:, h * SSM_HEAD_DIM:(h + 1) * SSM_HEAD_DIM] = jnp.dot(w, xh, preferred_element_type=F32)
        yield


def _ssm_group_expand(g, f, prep, e_ref):
    e = e_ref[:, g * GROUP_INNER:(g + 1) * GROUP_INNER]
    f["exp_a"] = _expand_heads(prep["exp_a"], e)
    yield
    f["xdd"] = (f["xs"] * _expand_heads(prep["dd"], e)).astype(BF16)
    yield


def _interleave(*gens):
    live = list(gens)
    while live:
        for gen in list(live):
            try:
                next(gen)
            except StopIteration:
                live.remove(gen)


def _gate_norm(y, z, nw):
    y = y * (z * jax.nn.sigmoid(z))
    return y * lax.rsqrt(jnp.mean(y * y, axis=-1, keepdims=True) + RMS_EPS) * nw


def _ssm_prompt_kernel(x_ref, win_ref, wdt_ref, cw_ref, cb_ref, dtb_ref, alog_ref, dsk_ref, nw_ref, wout_ref, e_ref,
                       tril_ref, g_ref, b_ref, o_ref, cout_ref, sout_ref, xp_buf, ht, ybuf):
    j = pl.program_id(1)
    nj = pl.num_programs(1)
    q = SSM_ROWS

    @pl.when(j == 0)
    def _no_past():
        xp_buf[:, 0:CONV_PAD, :] = jnp.zeros((1, CONV_PAD, CONV_DIM), F32)
        ht[...] = jnp.zeros(ht.shape, F32)

    @pl.when(j > 0)
    def _carry():
        xp_buf[:, 0:CONV_PAD, :] = xp_buf[:, q:q + CONV_PAD, :]

    x = x_ref[0]
    xb = x.astype(BF16)
    zx = jnp.dot(xb, win_ref[...], preferred_element_type=F32)
    dt_pre = jnp.dot(xb, wdt_ref[...], preferred_element_type=F32)
    z = zx[:, :D_INNER]

    def proj(c0, c1):
        return dt_pre if c0 >= D_INNER + CONV_DIM else zx[:, c0:c1]

    prep = _ssm_prep(proj, xp_buf, None, dtb_ref, alog_ref, tril_ref, nseq=1, q=q)
    xbc = _conv_silu(xp_buf, cw_ref, cb_ref, 0, CONV_DIM, nseq=1, q=q)
    xs = xbc[:, :D_INNER]
    acum, acum_t, dt_t, causal = prep["acum"], prep["acum_t"], prep["dt_t"], prep["causal"]
    for g in range(SSM_GROUPS):
        bg = xbc[:, D_INNER + g * D_STATE:D_INNER + (g + 1) * D_STATE].astype(BF16)
        cg = xbc[:, D_INNER + GBN + g * D_STATE:D_INNER + GBN + (g + 1) * D_STATE].astype(BF16)
        cb = lax.dot_general(cg, bg, (((1,), (1,)), ((), ())), preferred_element_type=F32)
        for r in range(SSM_HPG):
            h = g * SSM_HPG + r
            seg = acum[:, h:h + 1] - acum_t[h:h + 1, :]
            lm = jnp.exp(jnp.where(causal, seg, NEG_INF))
            w = (cb * lm * dt_t[h:h + 1, :]).astype(BF16)
            xh = xs[:, h * SSM_HEAD_DIM:(h + 1) * SSM_HEAD_DIM].astype(BF16)
            ybuf[:, h * SSM_HEAD_DIM:(h + 1) * SSM_HEAD_DIM] = jnp.dot(w, xh, preferred_element_type=F32)

    e = e_ref[...]
    exp_a = _expand_heads(prep["exp_a"], e)
    xdd = (xs * _expand_heads(prep["dd"], e)).astype(BF16)
    decay = exp_a[q - 1:q, :]
    y_off = []
    for g in range(SSM_GROUPS):
        sl = slice(g * GROUP_INNER, (g + 1) * GROUP_INNER)
        bg = xbc[:, D_INNER + g * D_STATE:D_INNER + (g + 1) * D_STATE]
        cg = xbc[:, D_INNER + GBN + g * D_STATE:D_INNER + GBN + (g + 1) * D_STATE].astype(BF16)
        hg = ht[:, sl]
        y_off.append(jnp.dot(cg, hg.astype(BF16), preferred_element_type=F32))
        ht[:, sl] = hg * decay[:, sl] + jnp.dot(bg.T.astype(BF16), xdd[:, sl], preferred_element_type=F32)
    y = ybuf[...] + exp_a * jnp.concatenate(y_off, axis=-1) + dsk_ref[...] * xs
    y = jnp.concatenate(
        [_gate_norm(y[:, g * GROUP_INNER:(g + 1) * GROUP_INNER], z[:, g * GROUP_INNER:(g + 1) * GROUP_INNER],
                    nw_ref[:, g * GROUP_INNER:(g + 1) * GROUP_INNER]) for g in range(SSM_GROUPS)], axis=-1)
    out = jnp.dot(y.astype(BF16), wout_ref[...], preferred_element_type=F32)
    o_ref[0] = _layer_norm_rows(DEEPNORM_ALPHA * x + out, g_ref[...], b_ref[...])

    @pl.when(j == nj - 1)
    def _emit_state():
        cout_ref[...] = prep["new_conv"]
        sout_ref[0] = ht[...].T


def _ssm_consts(nseq, q):
    rows = nseq * q
    r = np.arange(rows)
    tril = ((r[:, None] >= r[None, :]) & (r[:, None] // q == r[None, :] // q)).astype(np.float32)
    e = np.zeros((LANES, D_INNER), np.float32)
    e[np.arange(D_INNER) // SSM_HEAD_DIM, np.arange(D_INNER)] = 1.0
    return jnp.asarray(tril, BF16), jnp.asarray(e, BF16)


def _ssm_weight_specs(d):
    return [
        pl.BlockSpec((d, D_INNER + CONV_DIM), lambda *_: (0, 0), pipeline_mode=pl.Buffered(1)),
        _resident((d, LANES)), _resident((CONV_WIDTH, CONV_DIM)), _resident((1, CONV_DIM)),
        _resident((1, LANES)), _resident((1, LANES)),
    ]


def _ssm_prompt(x, w, g, b):
    bsz, l, d = x.shape
    q = SSM_ROWS
    assert l % q == 0
    tril, e = _ssm_consts(1, q)
    win_cols = D_INNER + CONV_DIM + LANES
    est = (d * win_cols + D_INNER * d) * 2 + LANES * D_INNER * 4 + 12 * q * D_INNER * 4 + 3 * q * win_cols * 4
    return pl.pallas_call(
        _ssm_prompt_kernel,
        grid=(bsz, l // q),
        in_specs=[pl.BlockSpec((1, q, d), lambda bi, j: (bi, j, 0))] + _ssm_weight_specs(d) + [
            _resident((1, D_INNER)), _resident((1, D_INNER)), _resident((D_INNER, d)),
            _resident((LANES, D_INNER)), _resident((q, q)), _resident((1, d)), _resident((1, d)),
        ],
        out_specs=[
            pl.BlockSpec((1, q, d), lambda bi, j: (bi, j, 0)),
            pl.BlockSpec((1, CONV_WIDTH - 1, CONV_DIM), lambda bi, j: (bi, 0, 0)),
            pl.BlockSpec((1, D_INNER, D_STATE), lambda bi, j: (bi, 0, 0)),
        ],
        out_shape=[
            jax.ShapeDtypeStruct((bsz, l, d), F32),
            jax.ShapeDtypeStruct((bsz, CONV_WIDTH - 1, CONV_DIM), F32),
            jax.ShapeDtypeStruct((bsz, D_INNER, D_STATE), F32),
        ],
        scratch_shapes=[
            pltpu.VMEM((1, CONV_PAD + q, CONV_DIM), F32),
            pltpu.VMEM((D_STATE, D_INNER), F32),
            pltpu.VMEM((q, D_INNER), F32),
        ],
        compiler_params=pltpu.CompilerParams(
            dimension_semantics=("arbitrary", "arbitrary"), vmem_limit_bytes=_vmem_limit(est)),
        name="ssm_prompt",
    )(x, w["win"], w["wdt"], w["cw"], w["cb"], w["dtb"], w["alog"], w["dsk"], w["nw"], w["wout"], e, tril, g, b)


def _ssm_sample_front_kernel(x_ref, cst_ref, win_ref, wdt_ref, cw_ref, cb_ref, dtb_ref, alog_ref, dsk_ref, e_ref, tril_ref,
                             y_ref, z_ref, ea_ref, xdd_ref, c_ref, bt_ref, cout_ref, xp_buf, ybuf, *, nseq, q):
    xb = x_ref[...].astype(BF16)

    def proj(c0, c1):
        return _ssm_in_proj(xb, win_ref, wdt_ref, c0, c1)

    prep = _ssm_prep(proj, xp_buf, cst_ref[...], dtb_ref, alog_ref, tril_ref, nseq=nseq, q=q)
    cout_ref[...] = prep["new_conv"]
    for g in range(SSM_GROUPS):
        sl = slice(g * GROUP_INNER, (g + 1) * GROUP_INNER)
        f = {}
        _interleave(_ssm_group_diag(g, f, proj, prep, xp_buf, cw_ref, cb_ref, ybuf, nseq=nseq, q=q))
        _interleave(_ssm_group_expand(g, f, prep, e_ref))
        y_ref[:, sl] = ybuf[:, sl] + dsk_ref[:, sl] * f["xs"]
        z_ref[:, sl] = f["z"]
        ea_ref[:, sl] = f["exp_a"]
        xdd_ref[:, sl] = f["xdd"]
        c_ref[:, g * D_STATE:(g + 1) * D_STATE] = f["cm"]
        bt_ref[g * D_STATE:(g + 1) * D_STATE, :] = f["bm"].T.astype(BF16)


def _ssm_sample_state_kernel(st_ref, y_ref, ea_ref, c_ref, xdd_ref, bt_ref, o_ref, nst_ref, *, bb, q, nseq):
    ea = ea_ref[...]
    xdd = xdd_ref[...]
    first_seq = (pl.program_id(0) % (nseq // bb)) * bb
    lane_seq = lax.broadcasted_iota(jnp.int32, (D_STATE, nseq * q), 1) // q
    for i in range(bb):
        r0 = i * q
        ht = st_ref[i].T
        decay = ea[r0 + q - 1:r0 + q, :]
        y_parts = []
        new_parts = []
        for g in range(SSM_GROUPS):
            sl = slice(g * GROUP_INNER, (g + 1) * GROUP_INNER)
            cg = c_ref[r0:r0 + q, g * D_STATE:(g + 1) * D_STATE].astype(BF16)
            hg = ht[:, sl]
            y_parts.append(jnp.dot(cg, hg.astype(BF16), preferred_element_type=F32))
            btg = bt_ref[g * D_STATE:(g + 1) * D_STATE, :]
            btg = jnp.where(lane_seq == first_seq + i, btg, jnp.zeros_like(btg))
            new_parts.append(hg * decay[:, sl] + jnp.dot(btg, xdd[:, sl], preferred_element_type=F32))
        o_ref[r0:r0 + q, :] = y_ref[r0:r0 + q, :] + ea[r0:r0 + q, :] * jnp.concatenate(y_parts, axis=-1)
        nst_ref[i] = jnp.concatenate(new_parts, axis=-1).T


def _ssm_sample_finish_kernel(x_ref, y_ref, z_ref, nw_ref, wout_ref, g_ref, b_ref, o_ref):
    x = x_ref[...]
    out = jnp.zeros(x.shape, F32)
    for g in range(SSM_GROUPS):
        sl = slice(g * GROUP_INNER, (g + 1) * GROUP_INNER)
        y = _gate_norm(y_ref[:, sl], z_ref[:, sl], nw_ref[:, sl])
        out = out + jnp.dot(y.astype(BF16), wout_ref[sl, :], preferred_element_type=F32)
    o_ref[...] = _layer_norm_rows(DEEPNORM_ALPHA * x + out, g_ref[...], b_ref[...])


def _ssm_sample(x, conv_state, ssm_state, w, g, b, *, bb_state=8):
    bsz, q, d = x.shape
    t = bsz * q
    nseq = SSM_ROWS // q
    rows = SSM_ROWS
    assert q == SUBLANES and bsz % nseq == 0 and nseq % bb_state == 0
    tril, e = _ssm_consts(nseq, q)
    win_cols = D_INNER + CONV_DIM + LANES
    x2 = x.reshape(t, d)
    row_spec = lambda c: pl.BlockSpec((rows, c), lambda i: (i, 0))
    est = d * win_cols * 2 + LANES * D_INNER * 4 + 16 * rows * D_INNER * 4 + 3 * rows * win_cols * 4
    ydiag, z, exp_a, xdd, cmat, bt, new_conv = pl.pallas_call(
        functools.partial(_ssm_sample_front_kernel, nseq=nseq, q=q),
        grid=(t // rows,),
        in_specs=[row_spec(d), pl.BlockSpec((nseq, CONV_WIDTH - 1, CONV_DIM), lambda i: (i, 0, 0))]
        + _ssm_weight_specs(d) + [_resident((1, D_INNER)), _resident((LANES, D_INNER)), _resident((rows, rows))],
        out_specs=[row_spec(D_INNER), row_spec(D_INNER), row_spec(D_INNER), row_spec(D_INNER), row_spec(GBN),
                   pl.BlockSpec((GBN, rows), lambda i: (i, 0)),
                   pl.BlockSpec((nseq, CONV_WIDTH - 1, CONV_DIM), lambda i: (i, 0, 0))],
        out_shape=[
            jax.ShapeDtypeStruct((t, D_INNER), F32), jax.ShapeDtypeStruct((t, D_INNER), F32),
            jax.ShapeDtypeStruct((t, D_INNER), F32), jax.ShapeDtypeStruct((t, D_INNER), BF16),
            jax.ShapeDtypeStruct((t, GBN), F32),
            jax.ShapeDtypeStruct((t // rows * GBN, rows), BF16),
            jax.ShapeDtypeStruct((bsz, CONV_WIDTH - 1, CONV_DIM), F32),
        ],
        scratch_shapes=[pltpu.VMEM((nseq, CONV_PAD + q, CONV_DIM), F32), pltpu.VMEM((rows, D_INNER), F32)],
        compiler_params=pltpu.CompilerParams(
            dimension_semantics=("arbitrary",), vmem_limit_bytes=_vmem_limit(est)),
        name="ssm_sample_front",
    )(x2, conv_state, w["win"], w["wdt"], w["cw"], w["cb"], w["dtb"], w["alog"], w["dsk"], e, tril)

    srows = bb_state * q
    srow_spec = lambda c: pl.BlockSpec((srows, c), lambda i: (i, 0))
    st_spec = pl.BlockSpec((bb_state, D_INNER, D_STATE), lambda i: (i, 0, 0))
    est = 4 * bb_state * D_INNER * D_STATE * 4 + 8 * D_INNER * D_STATE * 4
    per_block = nseq // bb_state
    y, new_state = pl.pallas_call(
        functools.partial(_ssm_sample_state_kernel, bb=bb_state, q=q, nseq=nseq),
        grid=(bsz // bb_state,),
        in_specs=[st_spec, srow_spec(D_INNER), srow_spec(D_INNER), srow_spec(GBN),
                  pl.BlockSpec((rows, D_INNER), lambda i: (i // per_block, 0)),
                  pl.BlockSpec((GBN, rows), lambda i: (i // per_block, 0))],
        out_specs=[srow_spec(D_INNER), st_spec],
        out_shape=[jax.ShapeDtypeStruct((t, D_INNER), F32),
                   jax.ShapeDtypeStruct((bsz, D_INNER, D_STATE), F32)],
        compiler_params=pltpu.CompilerParams(
            dimension_semantics=("arbitrary",), vmem_limit_bytes=_vmem_limit(est)),
        name="ssm_sample_state",
    )(ssm_state, ydiag, exp_a, cmat, xdd, bt)

    est = D_INNER * d * 2 + 8 * rows * D_INNER * 4
    out = pl.pallas_call(
        _ssm_sample_finish_kernel,
        grid=(t // rows,),
        in_specs=[row_spec(d), row_spec(D_INNER), row_spec(D_INNER),
                  _resident((1, D_INNER)), _resident((D_INNER, d)), _resident((1, d)), _resident((1, d))],
        out_specs=row_spec(d),
        out_shape=jax.ShapeDtypeStruct((t, d), F32),
        compiler_params=pltpu.CompilerParams(
            dimension_semantics=("arbitrary",), vmem_limit_bytes=_vmem_limit(est)),
        name="ssm_sample_finish",
    )(x2, y, z, w["nw"], w["wout"], g, b)
    return out.reshape(bsz, q, d), new_conv, new_state


def _ssm_weights(w_in, conv_w, conv_b, dt_bias, a_log, d_skip, norm_w, w_out, win_bf16, wout_bf16):
    pad = LANES - SSM_HEADS
    wdt = jnp.pad(w_in[:, D_INNER + CONV_DIM:], ((0, 0), (0, pad))).astype(BF16)
    return dict(
        win=win_bf16, wdt=wdt, cw=conv_w, cb=conv_b.reshape(1, CONV_DIM),
        dtb=jnp.pad(dt_bias, (0, pad)).reshape(1, LANES),
        alog=jnp.pad(a_log, (0, pad)).reshape(1, LANES),
        dsk=jnp.repeat(d_skip, SSM_HEAD_DIM).reshape(1, D_INNER),
        nw=norm_w.reshape(1, D_INNER), wout=wout_bf16)


def kernel(x_prompt, x_sample, cache_k, cache_v, state_conv, state_ssm, state_pool, rel_bias, attn_w_qkv, attn_b_qkv, attn_w_o, attn_b_o, attn_sinks, ssm_w_in, ssm_conv_w, ssm_conv_b, ssm_dt_bias, ssm_a_log, ssm_d, ssm_norm_w, ssm_w_out, pool_w, pool_scale, ffn_w_gate, ffn_w_up, ffn_w_down, ln_g, ln_b):
    xp, xs = x_prompt, x_sample
    d = xp.shape[-1]
    ffn_w32 = (ffn_w_gate, ffn_w_up, ffn_w_down)
    mixer_w32 = {0: (attn_w_qkv, attn_w_o), 1: (ssm_w_out,)}

    def layer_jobs(i):
        if i >= DEPTH:
            return []
        jobs = [(w, i) for w in ffn_w32]
        return jobs + [(w, i // N_MIXERS) for w in mixer_w32.get(i % N_MIXERS, ())]

    def split_cast(results):
        ffn = tuple(w[None] for w in results[:len(ffn_w32)])
        return ffn, tuple(results[len(ffn_w32):])

    mixer_w = tuple(w[0:1].astype(BF16) for w in mixer_w32[0])
    ffn_w = None
    cache_k_t = jnp.transpose(cache_k, (0, 1, 3, 4, 2))
    cache_v_t = jnp.transpose(cache_v, (0, 1, 3, 4, 2))
    new_cache_t = None
    nk_p, nv_p, nc_p, nh_p, npool_p = [], [], [], [], []
    nc_s, nh_s, npool_s = [], [], []
    for i in range(DEPTH):
        j = i // N_MIXERS
        kind = i % N_MIXERS
        g1 = ln_g[i, 0].reshape(1, d)
        b1 = ln_b[i, 0].reshape(1, d)
        if kind == 0:
            bqkv = attn_b_qkv[j].reshape(1, -1)
            bo = attn_b_o[j].reshape(1, d)
            wqkv, wo = (w if w.ndim == 3 else w[None] for w in mixer_w)
            jobs = [(w, i) for w in ffn_w32] if ffn_w is None else []
            xp, kp, vp, *cast = _attn_prompt(xp, wqkv, bqkv, wo, bo, g1, b1, rel_bias, attn_sinks[j], layer=0,
                                             cast_jobs=jobs)
            if jobs:
                ffn_w, _ = split_cast(cast)
            xs, *new_cache_t = _attn_sample(xs, cache_k_t, cache_v_t, wqkv, bqkv, wo, bo, g1, b1, rel_bias,
                                            attn_sinks[j], layer=0, cache_layer=j, prev=new_cache_t)
            xs = xs.reshape(x_sample.shape)
            kv_shape = (-1, WINDOW, N_KV_HEADS, HEAD_DIM)
            nk_p.append(kp.reshape(kv_shape)); nv_p.append(vp.reshape(kv_shape))
        elif kind == 1:
            w = _ssm_weights(ssm_w_in[j], ssm_conv_w[j], ssm_conv_b[j], ssm_dt_bias[j], ssm_a_log[j],
                             ssm_d[j], ssm_norm_w[j], ssm_w_out[j], ssm_w_in[j].astype(BF16), *mixer_w)
            xp, cp, hp = _ssm_prompt(xp, w, g1, b1)
            xs, cs_, hs_ = _ssm_sample(xs, state_conv[j], state_ssm[j].reshape(-1, D_INNER, D_STATE), w, g1, b1)
            st_shape = (-1, SSM_HEADS, SSM_HEAD_DIM, D_STATE)
            nc_p.append(cp); nh_p.append(hp.reshape(st_shape))
            nc_s.append(cs_); nh_s.append(hs_.reshape(st_shape))
        else:
            pw = pool_w[j].astype(BF16)
            psc = pool_scale[j].reshape(1, d)
            xp, pp = _pool_prompt(xp, pw, psc, g1, b1)
            xs, ps_ = _pool_sample(xs, state_pool[j], pw, psc, g1, b1, start=PAST_LEN)
            npool_p.append(pp); npool_s.append(ps_)
        g2 = ln_g[i, 1].reshape(1, d)
        b2 = ln_b[i, 1].reshape(1, d)
        xp2, cast = _ffn_ln(xp.reshape(-1, d), *ffn_w, g2, b2, layer=0, cast_jobs=layer_jobs(i + 1))
        xs2, _ = _ffn_ln(xs.reshape(-1, d), *ffn_w, g2, b2, layer=0)
        xp, xs = xp2.reshape(xp.shape), xs2.reshape(xs.shape)
        if cast:
            ffn_w, mixer_w = split_cast(cast)
    nk_s, nv_s = (jnp.transpose(t, (0, 1, 4, 2, 3)) for t in new_cache_t)
    return (xp, xs,
            jnp.stack(nk_p), jnp.stack(nv_p), jnp.stack(nc_p), jnp.stack(nh_p), jnp.stack(npool_p),
            nk_s, nv_s, jnp.stack(nc_s), jnp.stack(nh_s), jnp.stack(npool_s))
```
